```python
import jax, jax.numpy as jnp
from jax import lax
import numpy as np

D_MODEL = 1024
BATCH = 8
SEQ = 2048
DEPTH = 1
DEC_BATCH = 8
DEC_SEQ = 64
PAST_LEN = 4096

CHUNK = 64
HEAD_DIM = 64
MIX_DIM = D_MODEL
ATT_DIM = MIX_DIM // 2
N_HEADS = ATT_DIM // HEAD_DIM
N_KV_HEADS = max(1, N_HEADS // 4)
Q_PER_KV = N_HEADS // N_KV_HEADS
KV_DIM = N_KV_HEADS * HEAD_DIM
CONV_DIM = MIX_DIM - ATT_DIM
CONV_W = 3
WINDOW = 128
WIN_CHUNKS = WINDOW // CHUNK
IN_DIM = ATT_DIM + 2 * KV_DIM + 3 * CONV_DIM
SPLITS = (ATT_DIM, ATT_DIM + KV_DIM, ATT_DIM + 2 * KV_DIM,
          ATT_DIM + 2 * KV_DIM + CONV_DIM, ATT_DIM + 2 * KV_DIM + 2 * CONV_DIM)
N_EXPERTS = 32
TOP_K = 4
D_FF = D_MODEL
SWIGLU_LIMIT = 7.0
SWIGLU_ALPHA = 1.702
ROW_BLOCK = 128
EPS = 1e-6
NEG = -1e30

kernel_name = 'hybrid_swa_sink_shortconv_moe_stream_step'


def _rms(x, g):
    xf = x.astype(jnp.float32)
    y = xf * lax.rsqrt(jnp.mean(xf * xf, axis=-1, keepdims=True) + EPS)
    return (y * g.astype(jnp.float32)).astype(x.dtype)


def _project(xn, w_in, q_g, k_g):
    q, k, v, bg, cg, hc = jnp.split(xn @ w_in, SPLITS, axis=-1)
    lead = xn.shape[:-1]
    q = _rms(q.reshape(*lead, N_HEADS, HEAD_DIM), q_g)
    k = _rms(k.reshape(*lead, N_KV_HEADS, HEAD_DIM), k_g)
    v = v.reshape(*lead, N_KV_HEADS, HEAD_DIM)
    return q, k, v, bg, cg * hc


def _sink_attention(q, k, v, key_valid, sinks):
    s = jnp.einsum('...qhgd,...khd->...hgqk', q, k).astype(jnp.float32) * (HEAD_DIM ** -0.5)
    if key_valid is not None:
        s = jnp.where(key_valid[..., None, None, None, :], s, NEG)
    sink = jnp.broadcast_to(sinks.astype(jnp.float32).reshape(N_KV_HEADS, Q_PER_KV, 1, 1),
                            s.shape[:-1] + (1,))
    p = jax.nn.softmax(jnp.concatenate([s, sink], axis=-1), axis=-1)[..., :-1]
    return jnp.einsum('...hgqk,...khd->...qhgd', p.astype(v.dtype), v)


def _window_attention_prompt(q, k, v, sinks):
    b, s = q.shape[0], q.shape[1]
    nc = s // CHUNK
    qb = q.reshape(b, nc, CHUNK, N_KV_HEADS, Q_PER_KV, HEAD_DIM)

    def band(t):
        tc = jnp.pad(t, ((0, 0), (WINDOW, 0), (0, 0), (0, 0))).reshape(
            b, nc + WIN_CHUNKS, CHUNK, N_KV_HEADS, HEAD_DIM)
        return jnp.concatenate([tc[:, i:i + nc] for i in range(WIN_CHUNKS + 1)], axis=2)

    key_pos = (jnp.arange(nc)[:, None] * CHUNK - WINDOW
               + jnp.arange((WIN_CHUNKS + 1) * CHUNK)[None, :])
    o = _sink_attention(qb, band(k), band(v), key_pos >= 0, sinks)
    return o.reshape(b, s, ATT_DIM)


def _window_attention_sample(q, k_ctx, v_ctx, sinks):
    b, s = q.shape[0], q.shape[1]
    qb = q.reshape(b, s, N_KV_HEADS, Q_PER_KV, HEAD_DIM)
    o = _sink_attention(qb, k_ctx, v_ctx, None, sinks)
    return o.reshape(b, s, ATT_DIM)


def _causal_conv(u_ctx, w):
    n = u_ctx.shape[1] - (CONV_W - 1)
    y = u_ctx[:, 0:n] * w[0]
    for j in range(1, CONV_W):
        y = y + u_ctx[:, j:j + n] * w[j]
    return y


def _moe(x2d, w_router, b_router, w_gu, b_gu, w_down, b_down):
    t, d = x2d.shape
    logits = (x2d @ w_router).astype(jnp.float32) + b_router.astype(jnp.float32)
    top_logits, top_idx = lax.top_k(logits, TOP_K)
    gates = jax.nn.softmax(top_logits, axis=-1)
    tk = t * TOP_K
    flat_e = top_idx.reshape(tk)
    order = jnp.argsort(flat_e, stable=True)
    sorted_e = flat_e[order]
    counts = jnp.bincount(flat_e, length=N_EXPERTS)
    padded = (counts + ROW_BLOCK - 1) // ROW_BLOCK * ROW_BLOCK
    pad_ends = jnp.cumsum(padded)
    pad_starts = pad_ends - padded
    starts = jnp.cumsum(counts) - counts
    dest = pad_starts[sorted_e] + (jnp.arange(tk) - starts[sorted_e])
    n_blocks = -(-tk // ROW_BLOCK) + N_EXPERTS
    rows = n_blocks * ROW_BLOCK
    row_token = jnp.zeros((rows,), jnp.int32).at[dest].set((order // TOP_K).astype(jnp.int32))
    row_gate = jnp.zeros((rows,), jnp.float32).at[dest].set(gates.reshape(tk)[order])
    block_e = jnp.minimum(jnp.searchsorted(pad_ends, jnp.arange(n_blocks) * ROW_BLOCK, side='right'),
                          N_EXPERTS - 1)
    xb = x2d[row_token].reshape(n_blocks, ROW_BLOCK, d)

    def expert_block(args):
        xe, e = args
        gu = xe @ w_gu[e] + b_gu[e]
        g, u = gu[:, :D_FF], gu[:, D_FF:]
        g = jnp.minimum(g, SWIGLU_LIMIT)
        u = jnp.clip(u, -SWIGLU_LIMIT, SWIGLU_LIMIT)
        h = (u + 1) * (g * jax.nn.sigmoid(SWIGLU_ALPHA * g))
        return h @ w_down[e] + b_down[e]

    yb = lax.map(expert_block, (xb, block_e))
    yrows = yb.reshape(rows, d) * row_gate[:, None].astype(yb.dtype)
    return jnp.zeros_like(x2d).at[row_token].add(yrows)


def _merge_and_ffn(x, att, conv_out, w_out, norm2_g, w_router, b_router, w_gu, b_gu, w_down, b_down):
    h = x + jnp.concatenate([att, conv_out], axis=-1) @ w_out
    hn = _rms(h, norm2_g)
    ff = _moe(hn.reshape(-1, D_MODEL), w_router, b_router, w_gu, b_gu, w_down, b_down)
    return h + ff.reshape(h.shape)


def setup_inputs(seed: int = 0) -> dict:
    key = jax.random.key(seed)
    ks = jax.random.split(key, 19)
    win_cache = min(WINDOW, PAST_LEN)

    def nrm(k, shape, scale=1.0):
        return jax.random.normal(k, shape, jnp.float32) * scale

    return {
        'x_prompt': nrm(ks[0], (BATCH, SEQ, D_MODEL)),
        'x_sample': nrm(ks[1], (DEC_BATCH, DEC_SEQ, D_MODEL)),
        'cache_k': nrm(ks[2], (DEPTH, DEC_BATCH, win_cache, N_KV_HEADS, HEAD_DIM)),
        'cache_v': nrm(ks[3], (DEPTH, DEC_BATCH, win_cache, N_KV_HEADS, HEAD_DIM)),
        'state_conv': nrm(ks[4], (DEPTH, DEC_BATCH, CONV_W - 1, CONV_DIM)),
        'norm1_g': 1.0 + nrm(ks[5], (DEPTH, D_MODEL), 0.1),
        'w_in': nrm(ks[6], (DEPTH, D_MODEL, IN_DIM), D_MODEL ** -0.5),
        'q_norm_g': 1.0 + nrm(ks[7], (DEPTH, HEAD_DIM), 0.1),
        'k_norm_g': 1.0 + nrm(ks[8], (DEPTH, HEAD_DIM), 0.1),
        'sinks': nrm(ks[9], (DEPTH, N_HEADS), 1.0),
        'conv_w': nrm(ks[10], (DEPTH, CONV_W, CONV_DIM), CONV_W ** -0.5),
        'w_out': nrm(ks[11], (DEPTH, MIX_DIM, D_MODEL), MIX_DIM ** -0.5),
        'norm2_g': 1.0 + nrm(ks[12], (DEPTH, D_MODEL), 0.1),
        'w_router': nrm(ks[13], (DEPTH, D_MODEL, N_EXPERTS), D_MODEL ** -0.5),
        'b_router': nrm(ks[14], (DEPTH, N_EXPERTS), 0.01),
        'w_gu': nrm(ks[15], (DEPTH, N_EXPERTS, D_MODEL, 2 * D_FF), D_MODEL ** -0.5),
        'b_gu': nrm(ks[16], (DEPTH, N_EXPERTS, 2 * D_FF), 0.01),
        'w_down': nrm(ks[17], (DEPTH, N_EXPERTS, D_FF, D_MODEL), D_FF ** -0.5),
        'b_down': nrm(ks[18], (DEPTH, N_EXPERTS, D_MODEL), 0.01),
    }


def reference(x_prompt, x_sample, cache_k, cache_v, state_conv, norm1_g, w_in, q_norm_g, k_norm_g,
              sinks, conv_w, w_out, norm2_g, w_router, b_router, w_gu, b_gu, w_down, b_down):
    win_cache = cache_k.shape[2]
    hp, hs = x_prompt, x_sample
    kp_l, vp_l, cp_l, ks_l, vs_l, cs_l = [], [], [], [], [], []
    for l in range(DEPTH):
        moe_w = (w_router[l], b_router[l], w_gu[l], b_gu[l], w_down[l], b_down[l])
        q, k, v, bg, u = _project(_rms(hp, norm1_g[l]), w_in[l], q_norm_g[l], k_norm_g[l])
        att = _window_attention_prompt(q, k, v, sinks[l])
        cv = _causal_conv(jnp.pad(u, ((0, 0), (CONV_W - 1, 0), (0, 0))), conv_w[l])
        hp = _merge_and_ffn(hp, att, bg * cv, w_out[l], norm2_g[l], *moe_w)
        kp_l.append(k[:, -WINDOW:])
        vp_l.append(v[:, -WINDOW:])
        cp_l.append(u[:, -(CONV_W - 1):])
        q, k, v, bg, u = _project(_rms(hs, norm1_g[l]), w_in[l], q_norm_g[l], k_norm_g[l])
        k_ctx = jnp.concatenate([cache_k[l], k], axis=1)
        v_ctx = jnp.concatenate([cache_v[l], v], axis=1)
        att = _window_attention_sample(q, k_ctx, v_ctx, sinks[l])
        u_ctx = jnp.concatenate([state_conv[l], u], axis=1)
        cv = _causal_conv(u_ctx, conv_w[l])
        hs = _merge_and_ffn(hs, att, bg * cv, w_out[l], norm2_g[l], *moe_w)
        ks_l.append(k_ctx[:, -win_cache:])
        vs_l.append(v_ctx[:, -win_cache:])
        cs_l.append(u_ctx[:, -(CONV_W - 1):])
    return (hp, hs, jnp.stack(kp_l), jnp.stack(vp_l), jnp.stack(cp_l),
            jnp.stack(ks_l), jnp.stack(vs_l), jnp.stack(cs_l))
```

```python
import functools

import jax
import jax.numpy as jnp
from jax import lax
from jax.experimental import pallas as pl
from jax.experimental.pallas import tpu as pltpu

D_MODEL = 1024
CHUNK = 64
HEAD_DIM = 64
N_HEADS = 8
N_KV_HEADS = 2
Q_PER_KV = N_HEADS // N_KV_HEADS
ATT_DIM = N_HEADS * HEAD_DIM
KV_DIM = N_KV_HEADS * HEAD_DIM
CONV_DIM = D_MODEL - ATT_DIM
CONV_W = 3
WINDOW = 128
IN_DIM = ATT_DIM + 2 * KV_DIM + 3 * CONV_DIM
N_EXPERTS = 32
TOP_K = 4
D_FF = D_MODEL
SWIGLU_LIMIT = 7.0
SWIGLU_ALPHA = 1.702
EPS = 1e-6
NEG = -1e30

TOKEN_TILE = 256
ROW_BLOCK = 256
VMEM_LIMIT = 56 * 1024 * 1024

F32 = jnp.float32
BF16 = jnp.bfloat16


def _rms_rows(x, g):
    return x * lax.rsqrt(jnp.mean(x * x, axis=-1, keepdims=True) + EPS) * g


def _split_bf16(x):
    hi = x.astype(BF16)
    lo = (x - hi.astype(F32)).astype(BF16)
    return hi, lo


def _head_rms(t, bd, g):
    hi, lo = _split_bf16(t * t)
    ssq = (jnp.dot(hi, bd, preferred_element_type=F32)
           + jnp.dot(lo, bd, preferred_element_type=F32))
    return t * lax.rsqrt(ssq * (1.0 / HEAD_DIM) + EPS) * g


def _mixer_kernel(*refs, tm, has_cache):
    refs = list(refs)
    x_ref = refs.pop(0)
    if has_cache:
        ck_ref, cv_ref, st_ref = refs.pop(0), refs.pop(0), refs.pop(0)
    (cnt_in_ref, g1_ref, win_ref, qg_ref, kg_ref, bd_ref, sink_ref, cw_ref, wout_ref, g2_ref,
     wr_ref, br_ref, ltri_ref,
     h_ref, hn_ref, idx_ref, gate_ref, rank_ref, knew_ref, vnew_ref, unew_ref, cnt_out_ref,
     kprev, vprev, ubuf, cnt) = refs

    b = pl.program_id(0)
    i = pl.program_id(1)

    @pl.when((b == 0) & (i == 0))
    def _():
        cnt[...] = cnt_in_ref[...]

    @pl.when(i == 0)
    def _():
        ubuf[0:8, :] = jnp.zeros((8, CONV_DIM), F32)
        if has_cache:
            kprev[...] = ck_ref[0]
            vprev[...] = cv_ref[0]
            ubuf[6:8, :] = st_ref[0]
        else:
            kprev[...] = jnp.zeros((WINDOW, KV_DIM), F32)
            vprev[...] = jnp.zeros((WINDOW, KV_DIM), F32)

    x = x_ref[0]
    xn = _rms_rows(x, g1_ref[...])
    proj = jnp.dot(xn.astype(BF16), win_ref[...], preferred_element_type=F32)
    o0 = ATT_DIM
    o1 = o0 + KV_DIM
    o2 = o1 + KV_DIM
    o3 = o2 + CONV_DIM
    o4 = o3 + CONV_DIM
    bd = bd_ref[...]
    qn = _head_rms(proj[:, :o0], bd, qg_ref[...])
    kn = _head_rms(proj[:, o0:o1], bd[:KV_DIM, :KV_DIM], kg_ref[...])
    v = proj[:, o1:o2]
    bg = proj[:, o2:o3]
    u = proj[:, o3:o4] * proj[:, o4:]

    kall = jnp.concatenate([kprev[...], kn], axis=0)
    vall = jnp.concatenate([vprev[...], v], axis=0)
    nk = WINDOW + tm
    qc = lax.broadcasted_iota(jnp.int32, (tm, nk), 0) // CHUNK
    kcol = lax.broadcasted_iota(jnp.int32, (tm, nk), 1)
    kc = kcol // CHUNK
    valid = (kc >= qc) & (kc <= qc + WINDOW // CHUNK)
    if not has_cache:
        valid = valid & (kcol >= jnp.where(i == 0, WINDOW, 0))
    kall_b = kall.astype(BF16)
    vall_b = vall.astype(BF16)
    qn_b = qn.astype(BF16)
    outs = []
    for hk in range(N_KV_HEADS):
        kh = kall_b[:, hk * HEAD_DIM:(hk + 1) * HEAD_DIM]
        vh = vall_b[:, hk * HEAD_DIM:(hk + 1) * HEAD_DIM]
        for g in range(Q_PER_KV):
            hq = hk * Q_PER_KV + g
            qh = qn_b[:, hq * HEAD_DIM:(hq + 1) * HEAD_DIM]
            s = lax.dot_general(qh, kh, (((1,), (1,)), ((), ())),
                                preferred_element_type=F32) * (HEAD_DIM ** -0.5)
            s = jnp.where(valid, s, NEG)
            sink = sink_ref[0, hq]
            m = jnp.maximum(jnp.max(s, axis=-1, keepdims=True), sink)
            p = jnp.exp(s - m)
            den = jnp.sum(p, axis=-1, keepdims=True) + jnp.exp(sink - m)
            o = jnp.dot(p.astype(BF16), vh, preferred_element_type=F32)
            outs.append(o / den)
    att = jnp.concatenate(outs, axis=1)

    kprev[...] = kall[tm:, :]
    vprev[...] = vall[tm:, :]
    knew_ref[0] = kall[tm:, :]
    vnew_ref[0] = vall[tm:, :]

    ubuf[8:8 + tm, :] = u
    cw = cw_ref[...]
    cv = ubuf[6:6 + tm, :] * cw[0:1, :] + ubuf[7:7 + tm, :] * cw[1:2, :]
    cv = cv + u * cw[2:3, :]
    ubuf[0:8, :] = ubuf[tm:tm + 8, :]
    unew_ref[0] = u[tm - (CONV_W - 1):, :]

    mix = jnp.concatenate([att, bg * cv], axis=1).astype(BF16)
    h = x + jnp.dot(mix, wout_ref[...], preferred_element_type=F32)
    h_ref[0] = h
    hn = _rms_rows(h, g2_ref[...])
    hn_ref[0] = hn

    hn_hi, hn_lo = _split_bf16(hn)
    wr_hi, wr_lo = _split_bf16(wr_ref[...])
    logits = (jnp.dot(hn_hi, wr_hi, preferred_element_type=F32)
              + jnp.dot(hn_lo, wr_hi, preferred_element_type=F32)
              + jnp.dot(hn_hi, wr_lo, preferred_element_type=F32)) + br_ref[...]
    eio = lax.broadcasted_iota(jnp.int32, (tm, N_EXPERTS), 1)
    work = logits
    tops, sels, ids = [], [], []
    for _ in range(TOP_K):
        mk = jnp.max(work, axis=-1, keepdims=True)
        ik = jnp.min(jnp.where(work == mk, eio, N_EXPERTS), axis=-1, keepdims=True)
        sel = eio == ik
        work = jnp.where(sel, -jnp.inf, work)
        tops.append(mk)
        sels.append(sel)
        ids.append(ik)
    es = [jnp.exp(t - tops[0]) for t in tops]
    esum = es[0] + es[1] + es[2] + es[3]

    onehot = jnp.zeros((tm, N_EXPERTS), F32)
    for sel in sels:
        onehot = onehot + jnp.where(sel, 1.0, 0.0)
    before = jnp.dot(ltri_ref[...], onehot.astype(BF16), preferred_element_type=F32) + cnt[...]
    io4 = lax.broadcasted_iota(jnp.int32, (tm, TOP_K), 1)
    idx4 = jnp.zeros((tm, TOP_K), jnp.int32)
    gate4 = jnp.zeros((tm, TOP_K), F32)
    rank4 = jnp.zeros((tm, TOP_K), jnp.int32)
    for k in range(TOP_K):
        rk = jnp.sum(jnp.where(sels[k], before, 0.0), axis=-1, keepdims=True).astype(jnp.int32)
        idx4 = jnp.where(io4 == k, ids[k], idx4)
        gate4 = jnp.where(io4 == k, es[k] / esum, gate4)
        rank4 = jnp.where(io4 == k, rk, rank4)
    idx_ref[0] = idx4
    gate_ref[0] = gate4
    rank_ref[0] = rank4
    cnt[...] = cnt[...] + jnp.sum(onehot, axis=0, keepdims=True)
    cnt_out_ref[...] = cnt[...]


def _mixer(x, cache, cnt_in, w, *, tm):
    nb, s, _ = x.shape
    nt = s // tm
    has_cache = cache is not None
    full = lambda shape: pl.BlockSpec(shape, lambda b, i: (0,) * len(shape))
    per_b = lambda shape: pl.BlockSpec((1,) + shape, lambda b, i: (b, 0, 0))
    tile = lambda last: pl.BlockSpec((1, tm, last), lambda b, i: (b, i, 0))

    in_specs = [tile(D_MODEL)]
    args = [x]
    if has_cache:
        in_specs += [per_b((WINDOW, KV_DIM)), per_b((WINDOW, KV_DIM)), per_b((CONV_W - 1, CONV_DIM))]
        args += list(cache)
    ltri = jnp.tril(jnp.ones((tm, tm), F32), -1).astype(BF16)
    in_specs += [
        full((1, N_EXPERTS)), full((1, D_MODEL)), full((D_MODEL, IN_DIM)), full((1, ATT_DIM)),
        full((1, KV_DIM)), full((ATT_DIM, ATT_DIM)),
        pl.BlockSpec(memory_space=pltpu.SMEM),
        full((CONV_W, CONV_DIM)), full((D_MODEL, D_MODEL)), full((1, D_MODEL)),
        full((D_MODEL, N_EXPERTS)), full((1, N_EXPERTS)), full((tm, tm)),
    ]
    args += [cnt_in, w['g1'], w['w_in'], w['qg'], w['kg'], w['bd'], w['sinks'], w['conv_w'],
             w['w_out'], w['g2'], w['w_router'], w['b_router'], ltri]
    out_shape = [
        jax.ShapeDtypeStruct((nb, s, D_MODEL), F32),
        jax.ShapeDtypeStruct((nb, s, D_MODEL), F32),
        jax.ShapeDtypeStruct((nb, s, TOP_K), jnp.int32),
        jax.ShapeDtypeStruct((nb, s, TOP_K), F32),
        jax.ShapeDtypeStruct((nb, s, TOP_K), jnp.int32),
        jax.ShapeDtypeStruct((nb, WINDOW, KV_DIM), F32),
        jax.ShapeDtypeStruct((nb, WINDOW, KV_DIM), F32),
        jax.ShapeDtypeStruct((nb, CONV_W - 1, CONV_DIM), F32),
        jax.ShapeDtypeStruct((1, N_EXPERTS), F32),
    ]
    out_specs = [tile(D_MODEL), tile(D_MODEL), tile(TOP_K), tile(TOP_K), tile(TOP_K),
                 per_b((WINDOW, KV_DIM)), per_b((WINDOW, KV_DIM)), per_b((CONV_W - 1, CONV_DIM)),
                 full((1, N_EXPERTS))]
    scratch = [pltpu.VMEM((WINDOW, KV_DIM), F32), pltpu.VMEM((WINDOW, KV_DIM), F32),
               pltpu.VMEM((tm + 8, CONV_DIM), F32), pltpu.VMEM((1, N_EXPERTS), F32)]
    return pl.pallas_call(
        functools.partial(_mixer_kernel, tm=tm, has_cache=has_cache),
        grid=(nb, nt),
        in_specs=in_specs,
        out_specs=out_specs,
        out_shape=out_shape,
        scratch_shapes=scratch,
        compiler_params=pltpu.CompilerParams(
            dimension_semantics=("arbitrary", "arbitrary"), vmem_limit_bytes=VMEM_LIMIT),
        name="mixer_sample" if has_cache else "mixer_prompt",
    )(*args)


def _dispatch_kernel(dest_ref, hn_ref, xs_in_ref, xs_ref, sem, *, tm):
    del xs_in_ref
    n = tm * TOP_K

    def body(r, c):
        pltpu.make_async_copy(hn_ref.at[pl.ds(r // TOP_K, 1)],
                              xs_ref.at[pl.ds(dest_ref[r], 1)], sem).start()
        return c

    lax.fori_loop(0, n, body, 0)

    def wbody(r, c):
        pltpu.make_async_copy(hn_ref.at[pl.ds(0, 1)], xs_ref.at[pl.ds(0, 1)], sem).wait()
        return c

    lax.fori_loop(0, n, wbody, 0)


def _dispatch(dest_flat, hn2d, xs, *, tm):
    t = hn2d.shape[0]
    n = tm * TOP_K
    return pl.pallas_call(
        functools.partial(_dispatch_kernel, tm=tm),
        grid=(t // tm,),
        in_specs=[pl.BlockSpec((n,), lambda i: (i,), memory_space=pltpu.SMEM),
                  pl.BlockSpec((tm, D_MODEL), lambda i: (i, 0)),
                  pl.BlockSpec(memory_space=pl.ANY)],
        out_specs=pl.BlockSpec(memory_space=pl.ANY),
        out_shape=jax.ShapeDtypeStruct(xs.shape, xs.dtype),
        scratch_shapes=[pltpu.SemaphoreType.DMA],
        input_output_aliases={2: 0},
        compiler_params=pltpu.CompilerParams(dimension_semantics=("arbitrary",)),
        name="dispatch",
    )(dest_flat, hn2d, xs)


def _expert_kernel(be_ref, nused_ref, x_ref, wgu_ref, bgu_ref, wd_ref, bdn_ref, y_ref,
                   wgu_b, wd_b):
    b = pl.program_id(0)
    prev = be_ref[jnp.maximum(b - 1, 0)]
    fresh = (b == 0) | (be_ref[b] != prev)
    active = b < nused_ref[0]

    @pl.when(active & fresh)
    def _():
        wgu_b[...] = wgu_ref[0].astype(BF16)
        wd_b[...] = wd_ref[0].astype(BF16)

    @pl.when(active)
    def _():
        x = x_ref[...].astype(BF16)
        gu = jnp.dot(x, wgu_b[...], preferred_element_type=F32) + bgu_ref[0]
        g = jnp.minimum(gu[:, :D_FF], SWIGLU_LIMIT)
        u = jnp.clip(gu[:, D_FF:], -SWIGLU_LIMIT, SWIGLU_LIMIT)
        act = (u + 1.0) * (g * jax.nn.sigmoid(SWIGLU_ALPHA * g))
        y_ref[...] = jnp.dot(act.astype(BF16), wd_b[...], preferred_element_type=F32) + bdn_ref[0]

    @pl.when(jnp.logical_not(active))
    def _():
        y_ref[...] = jnp.zeros_like(y_ref)


def _experts(block_e, n_used, xs, w_gu, b_gu, w_down, b_down):
    rows = xs.shape[0]
    nblk = rows // ROW_BLOCK

    def xmap(b, be, nu):
        return (jnp.minimum(b, nu[0] - 1), 0)

    def wmap(b, be, nu):
        return (be[b], 0, 0)

    grid_spec = pltpu.PrefetchScalarGridSpec(
        num_scalar_prefetch=2,
        grid=(nblk,),
        in_specs=[pl.BlockSpec((ROW_BLOCK, D_MODEL), xmap),
                  pl.BlockSpec((1, D_MODEL, 2 * D_FF), wmap),
                  pl.BlockSpec((1, 1, 2 * D_FF), wmap),
                  pl.BlockSpec((1, D_FF, D_MODEL), wmap),
                  pl.BlockSpec((1, 1, D_MODEL), wmap)],
        out_specs=pl.BlockSpec((ROW_BLOCK, D_MODEL), lambda b, be, nu: (b, 0)),
        scratch_shapes=[pltpu.VMEM((D_MODEL, 2 * D_FF), BF16), pltpu.VMEM((D_FF, D_MODEL), BF16)],
    )
    return pl.pallas_call(
        _expert_kernel,
        grid_spec=grid_spec,
        out_shape=jax.ShapeDtypeStruct((rows, D_MODEL), F32),
        compiler_params=pltpu.CompilerParams(
            dimension_semantics=("arbitrary",), vmem_limit_bytes=VMEM_LIMIT),
        name="experts",
    )(block_e, n_used, xs, w_gu, b_gu, w_down, b_down)


def _combine_kernel(dest_ref, gate_ref, h_ref, y_hbm, o_ref, buf, sem, *, tm):
    n = tm * TOP_K

    def body(r, c):
        pltpu.make_async_copy(y_hbm.at[pl.ds(dest_ref[r], 1)],
                              buf.at[r % TOP_K, pl.ds(r // TOP_K, 1)], sem).start()
        return c

    lax.fori_loop(0, n, body, 0)

    def wbody(r, c):
        pltpu.make_async_copy(y_hbm.at[pl.ds(0, 1)], buf.at[0, pl.ds(0, 1)], sem).wait()
        return c

    lax.fori_loop(0, n, wbody, 0)
    gate = gate_ref[...]
    acc = h_ref[...]
    for k in range(TOP_K):
        acc = acc + gate[:, k:k + 1] * buf[k]
    o_ref[...] = acc


def _combine(dest_flat, gate2d, h2d, yrows, *, tm):
    t = h2d.shape[0]
    n = tm * TOP_K
    return pl.pallas_call(
        functools.partial(_combine_kernel, tm=tm),
        grid=(t // tm,),
        in_specs=[pl.BlockSpec((n,), lambda i: (i,), memory_space=pltpu.SMEM),
                  pl.BlockSpec((tm, TOP_K), lambda i: (i, 0)),
                  pl.BlockSpec((tm, D_MODEL), lambda i: (i, 0)),
                  pl.BlockSpec(memory_space=pl.ANY)],
        out_specs=pl.BlockSpec((tm, D_MODEL), lambda i: (i, 0)),
        out_shape=jax.ShapeDtypeStruct((t, D_MODEL), F32),
        scratch_shapes=[pltpu.VMEM((TOP_K, tm, D_MODEL), F32), pltpu.SemaphoreType.DMA],
        compiler_params=pltpu.CompilerParams(
            dimension_semantics=("arbitrary",), vmem_limit_bytes=VMEM_LIMIT),
        name="combine",
    )(dest_flat, gate2d, h2d, yrows)


def kernel(x_prompt, x_sample, cache_k, cache_v, state_conv, norm1_g, w_in, q_norm_g, k_norm_g,
           sinks, conv_w, w_out, norm2_g, w_router, b_router, w_gu, b_gu, w_down, b_down):
    l = 0
    nbp, sp, _ = x_prompt.shape
    nbs, ss, _ = x_sample.shape
    grp = jnp.arange(ATT_DIM) // HEAD_DIM
    w = {
        'g1': norm1_g[l][None, :],
        'w_in': w_in[l].astype(BF16),
        'qg': jnp.tile(q_norm_g[l], N_HEADS)[None, :],
        'kg': jnp.tile(k_norm_g[l], N_KV_HEADS)[None, :],
        'bd': (grp[:, None] == grp[None, :]).astype(BF16),
        'sinks': sinks[l][None, :],
        'conv_w': conv_w[l],
        'w_out': w_out[l].astype(BF16),
        'g2': norm2_g[l][None, :],
        'w_router': w_router[l],
        'b_router': b_router[l][None, :],
    }
    cnt0 = jnp.zeros((1, N_EXPERTS), F32)
    (hp, hnp, idxp, gatep, rankp, knp, vnp, unp, cnt1) = _mixer(
        x_prompt, None, cnt0, w, tm=TOKEN_TILE)
    cache = (cache_k[l].reshape(nbs, WINDOW, KV_DIM), cache_v[l].reshape(nbs, WINDOW, KV_DIM),
             state_conv[l])
    (hs, hns, idxs, gates, ranks, kns, vns, uns, cnt2) = _mixer(
        x_sample, cache, cnt1, w, tm=ss)

    tp, ts = nbp * sp, nbs * ss
    t = tp + ts
    rows = (-(-(t * TOP_K) // ROW_BLOCK) + N_EXPERTS) * ROW_BLOCK
    nblk = rows // ROW_BLOCK
    counts = cnt2[0].astype(jnp.int32)
    padded = (counts + ROW_BLOCK - 1) // ROW_BLOCK * ROW_BLOCK
    pad_ends = jnp.cumsum(padded)
    pad_starts = pad_ends - padded
    block_e = jnp.minimum(
        jnp.searchsorted(pad_ends, jnp.arange(nblk, dtype=jnp.int32) * ROW_BLOCK, side='right'),
        N_EXPERTS - 1).astype(jnp.int32)
    n_used = (pad_ends[-1:] // ROW_BLOCK).astype(jnp.int32)
    dest_p = (pad_starts[idxp] + rankp).reshape(tp * TOP_K)
    dest_s = (pad_starts[idxs] + ranks).reshape(ts * TOP_K)

    xs = jnp.zeros((rows, D_MODEL), F32)
    xs = _dispatch(dest_p, hnp.reshape(tp, D_MODEL), xs, tm=TOKEN_TILE)
    xs = _dispatch(dest_s, hns.reshape(ts, D_MODEL), xs, tm=TOKEN_TILE)
    yrows = _experts(block_e, n_used, xs, w_gu[l], b_gu[l][:, None, :], w_down[l],
                     b_down[l][:, None, :])
    yp = _combine(dest_p, gatep.reshape(tp, TOP_K), hp.reshape(tp, D_MODEL), yrows, tm=TOKEN_TILE)
    ys = _combine(dest_s, gates.reshape(ts, TOP_K), hs.reshape(ts, D_MODEL), yrows, tm=TOKEN_TILE)

    kv5 = lambda a: a.reshape(1, a.shape[0], WINDOW, N_KV_HEADS, HEAD_DIM)
    return (yp.reshape(nbp, sp, D_MODEL), ys.reshape(nbs, ss, D_MODEL),
            kv5(knp), kv5(vnp), unp[None], kv5(kns), kv5(vns), uns[None])
```

```python
import functools

import jax
import jax.numpy as jnp
from jax import lax
from jax.experimental import pallas as pl
from jax.experimental.pallas import tpu as pltpu

D_MODEL = 1024
CHUNK = 64
HEAD_DIM = 64
N_HEADS = 8
N_KV_HEADS = 2
Q_PER_KV = N_HEADS // N_KV_HEADS
ATT_DIM = N_HEADS * HEAD_DIM
KV_DIM = N_KV_HEADS * HEAD_DIM
CONV_DIM = D_MODEL - ATT_DIM
CONV_W = 3
WINDOW = 128
IN_DIM = ATT_DIM + 2 * KV_DIM + 3 * CONV_DIM
N_EXPERTS = 32
TOP_K = 4
D_FF = D_MODEL
SWIGLU_LIMIT = 7.0
SWIGLU_ALPHA = 1.702
EPS = 1e-6
NEG = -1e30

LANES = 128
HALF = D_MODEL // 2
PACK_S = HALF // LANES
TOKEN_TILE = 256
ROW_BLOCK = 256
VMEM_LIMIT = 56 * 1024 * 1024

F32 = jnp.float32
BF16 = jnp.bfloat16
U32 = jnp.uint32
I32 = jnp.int32


def _rms_rows(x, g):
    return x * lax.rsqrt(jnp.mean(x * x, axis=-1, keepdims=True) + EPS) * g


def _split_bf16(x):
    hi = x.astype(BF16)
    lo = (x - hi.astype(F32)).astype(BF16)
    return hi, lo


def _dot(a, b):
    return jnp.dot(a, b, preferred_element_type=F32)


def _dot_nt(a, b):
    return lax.dot_general(a, b, (((1,), (1,)), ((), ())), preferred_element_type=F32)


def _head_rms(t, bd, g):
    hi, lo = _split_bf16(t * t)
    ssq = _dot(hi, bd) + _dot(lo, bd)
    return t * lax.rsqrt(ssq * (1.0 / HEAD_DIM) + EPS) * g


def _pack_rows(vals, out_ref):
    bits = pltpu.bitcast(vals, U32)
    word = (bits[:, :HALF] >> 16) | (bits[:, HALF:] & jnp.uint32(0xFFFF0000))
    for s in range(PACK_S):
        out_ref[:, s, :] = word[:, s * LANES:(s + 1) * LANES]


def _unpack_rows(load_slab):
    lo, hi = [], []
    for s in range(PACK_S):
        w = load_slab(s)
        lo.append(pltpu.bitcast(w << 16, F32))
        hi.append(pltpu.bitcast(w & jnp.uint32(0xFFFF0000), F32))
    return jnp.concatenate(lo + hi, axis=1).astype(BF16)


def _mixer_kernel(*refs, tm, has_cache):
    refs = list(refs)
    x_ref = refs.pop(0)
    if has_cache:
        ck_ref, cv_ref, st_ref = refs.pop(0), refs.pop(0), refs.pop(0)
    (g1_ref, win_ref, qg_ref, kg_ref, bd_ref, sink_ref, cw_ref, wout_ref, g2_ref,
     wrt_ref, brt_ref, upper_ref, lower_ref,
     h_ref, xs_ref, col_ref, tcnt_ref, knew_ref, vnew_ref, unew_ref,
     kprev, vprev, ubuf) = refs

    i = pl.program_id(1)

    @pl.when(i == 0)
    def _():
        ubuf[0:8, :] = jnp.zeros((8, CONV_DIM), F32)
        if has_cache:
            kprev[...] = ck_ref[0]
            vprev[...] = cv_ref[0]
            ubuf[6:8, :] = st_ref[0]
        else:
            kprev[...] = jnp.zeros((WINDOW, KV_DIM), F32)
            vprev[...] = jnp.zeros((WINDOW, KV_DIM), F32)

    x = x_ref[0]
    xn = _rms_rows(x, g1_ref[...])
    proj = _dot(xn.astype(BF16), win_ref[...])
    o0 = ATT_DIM
    o1 = o0 + KV_DIM
    o2 = o1 + KV_DIM
    o3 = o2 + CONV_DIM
    o4 = o3 + CONV_DIM
    bd = bd_ref[...]
    qn = _head_rms(proj[:, :o0], bd, qg_ref[...])
    kn = _head_rms(proj[:, o0:o1], bd[:KV_DIM, :KV_DIM], kg_ref[...])
    v = proj[:, o1:o2]
    bg = proj[:, o2:o3]
    u = proj[:, o3:o4] * proj[:, o4:]

    kall = jnp.concatenate([kprev[...], kn], axis=0)
    vall = jnp.concatenate([vprev[...], v], axis=0)
    nk = WINDOW + tm
    qc = lax.broadcasted_iota(I32, (tm, nk), 0) // CHUNK
    kcol = lax.broadcasted_iota(I32, (tm, nk), 1)
    kc = kcol // CHUNK
    valid = (kc >= qc) & (kc <= qc + WINDOW // CHUNK)
    if not has_cache:
        valid = valid & (kcol >= jnp.where(i == 0, WINDOW, 0))
    kall_b = kall.astype(BF16)
    vall_b = vall.astype(BF16)
    qn_b = qn.astype(BF16)
    outs = []
    for hk in range(N_KV_HEADS):
        kh = kall_b[:, hk * HEAD_DIM:(hk + 1) * HEAD_DIM]
        vh = vall_b[:, hk * HEAD_DIM:(hk + 1) * HEAD_DIM]
        for g in range(Q_PER_KV):
            hq = hk * Q_PER_KV + g
            qh = qn_b[:, hq * HEAD_DIM:(hq + 1) * HEAD_DIM]
            s = _dot_nt(qh, kh) * (HEAD_DIM ** -0.5)
            s = jnp.where(valid, s, NEG)
            sink = sink_ref[0, hq]
            m = jnp.maximum(jnp.max(s, axis=-1, keepdims=True), sink)
            p = jnp.exp(s - m)
            den = jnp.sum(p, axis=-1, keepdims=True) + jnp.exp(sink - m)
            outs.append(_dot(p.astype(BF16), vh) / den)
    att = jnp.concatenate(outs, axis=1)

    kprev[...] = kall[tm:, :]
    vprev[...] = vall[tm:, :]
    knew_ref[0] = kall[tm:, :]
    vnew_ref[0] = vall[tm:, :]

    ubuf[8:8 + tm, :] = u
    cw = cw_ref[...]
    cv = ubuf[6:6 + tm, :] * cw[0:1, :] + ubuf[7:7 + tm, :] * cw[1:2, :]
    cv = cv + u * cw[2:3, :]
    ubuf[0:8, :] = ubuf[tm:tm + 8, :]
    unew_ref[0] = u[tm - (CONV_W - 1):, :]

    mix = jnp.concatenate([att, bg * cv], axis=1).astype(BF16)
    h = x + _dot(mix, wout_ref[...])
    h_ref[0] = h
    hn = _rms_rows(h, g2_ref[...])

    hn_hi, hn_lo = _split_bf16(hn)
    wr_hi, wr_lo = _split_bf16(wrt_ref[...])
    logits = (_dot_nt(wr_hi, hn_hi) + _dot_nt(wr_hi, hn_lo) + _dot_nt(wr_lo, hn_hi)) + brt_ref[...]
    eio = lax.broadcasted_iota(I32, (N_EXPERTS, tm), 0)
    work = logits
    tops, sels = [], []
    for _ in range(TOP_K):
        mk = jnp.max(work, axis=0, keepdims=True)
        ik = jnp.min(jnp.where(work == mk, eio, N_EXPERTS), axis=0, keepdims=True)
        sel = eio == ik
        work = jnp.where(sel, -jnp.inf, work)
        tops.append(mk)
        sels.append(sel)
    es = [jnp.exp(t - tops[0]) for t in tops]
    esum = es[0] + es[1] + es[2] + es[3]

    onehot = jnp.zeros((N_EXPERTS, tm), F32)
    for sel in sels:
        onehot = onehot + jnp.where(sel, 1.0, 0.0)
    onehot_b = onehot.astype(BF16)
    before = _dot(onehot_b, upper_ref[...])
    start = jnp.sum(_dot(lower_ref[...], onehot_b), axis=1, keepdims=True)
    where_to = start + before
    pos = [jnp.sum(jnp.where(sel, where_to, 0.0), axis=0, keepdims=True) for sel in sels]
    rio = lax.broadcasted_iota(I32, (TOP_K * tm, tm), 0)
    hit = rio == pos[0].astype(I32)
    for k in range(1, TOP_K):
        hit = hit | (rio == pos[k].astype(I32))
    perm = jnp.where(hit, 1.0, 0.0).astype(BF16)
    _pack_rows(_dot(perm, hn_hi), xs_ref)

    tcnt_ref[0] = _dot_nt(jnp.ones((1, tm), BF16), onehot_b).astype(I32)

    lanes = -(-tm // LANES) * LANES
    rows8 = jnp.concatenate(pos + [e / esum for e in es], axis=0)
    if lanes != tm:
        rows8 = jnp.concatenate([rows8, jnp.zeros((2 * TOP_K, lanes - tm), F32)], axis=1)
    sq = jnp.concatenate([rows8, jnp.zeros((LANES - 2 * TOP_K, lanes), F32)], axis=0)
    col_ref[...] = jnp.transpose(sq)[:tm, :]


def _mixer(x, cache, w, *, tm):
    nb, s, _ = x.shape
    nt = s // tm
    n_rows = nb * s * TOP_K
    has_cache = cache is not None
    full = lambda shape: pl.BlockSpec(shape, lambda b, i: (0,) * len(shape))
    per_b = lambda shape: pl.BlockSpec((1,) + shape, lambda b, i: (b, 0, 0))
    tile = lambda last: pl.BlockSpec((1, tm, last), lambda b, i: (b, i, 0))

    in_specs = [tile(D_MODEL)]
    args = [x]
    if has_cache:
        in_specs += [per_b((WINDOW, KV_DIM)), per_b((WINDOW, KV_DIM)), per_b((CONV_W - 1, CONV_DIM))]
        args += list(cache)
    tio = jnp.arange(tm)
    eio = jnp.arange(N_EXPERTS)
    upper = (tio[:, None] < tio[None, :]).astype(BF16)
    lower = (eio[None, :] < eio[:, None]).astype(BF16)
    in_specs += [
        full((1, D_MODEL)), full((D_MODEL, IN_DIM)), full((1, ATT_DIM)),
        full((1, KV_DIM)), full((ATT_DIM, ATT_DIM)),
        pl.BlockSpec(memory_space=pltpu.SMEM),
        full((CONV_W, CONV_DIM)), full((D_MODEL, D_MODEL)), full((1, D_MODEL)),
        full((N_EXPERTS, D_MODEL)), full((N_EXPERTS, 1)), full((tm, tm)), full((N_EXPERTS, N_EXPERTS)),
    ]
    args += [w['g1'], w['w_in'], w['qg'], w['kg'], w['bd'], w['sinks'], w['conv_w'],
             w['w_out'], w['g2'], w['w_router_t'], w['b_router_t'], upper, lower]
    rt = TOP_K * tm
    out_shape = [
        jax.ShapeDtypeStruct((nb, s, D_MODEL), F32),
        jax.ShapeDtypeStruct((n_rows, PACK_S, LANES), U32),
        jax.ShapeDtypeStruct((nb * s, LANES), F32),
        jax.ShapeDtypeStruct((nb * nt, 1, N_EXPERTS), I32),
        jax.ShapeDtypeStruct((nb, WINDOW, KV_DIM), F32),
        jax.ShapeDtypeStruct((nb, WINDOW, KV_DIM), F32),
        jax.ShapeDtypeStruct((nb, CONV_W - 1, CONV_DIM), F32),
    ]
    out_specs = [
        tile(D_MODEL),
        pl.BlockSpec((rt, PACK_S, LANES), lambda b, i: (b * nt + i, 0, 0)),
        pl.BlockSpec((tm, LANES), lambda b, i: (b * nt + i, 0)),
        pl.BlockSpec((1, 1, N_EXPERTS), lambda b, i: (b * nt + i, 0, 0)),
        per_b((WINDOW, KV_DIM)), per_b((WINDOW, KV_DIM)), per_b((CONV_W - 1, CONV_DIM)),
    ]
    scratch = [pltpu.VMEM((WINDOW, KV_DIM), F32), pltpu.VMEM((WINDOW, KV_DIM), F32),
               pltpu.VMEM((tm + 8, CONV_DIM), F32)]
    return pl.pallas_call(
        functools.partial(_mixer_kernel, tm=tm, has_cache=has_cache),
        grid=(nb, nt),
        in_specs=in_specs,
        out_specs=out_specs,
        out_shape=out_shape,
        scratch_shapes=scratch,
        compiler_params=pltpu.CompilerParams(
            dimension_semantics=("arbitrary", "arbitrary"), vmem_limit_bytes=VMEM_LIMIT),
        name="mixer_sample" if has_cache else "mixer_prompt",
    )(*args)


def _expert_kernel(be_ref, bq_ref, nused_ref, src_ref, off_ref, n_ref, tot_ref,
                   xsp_hbm, xss_hbm, wgu_ref, bgu_ref, wd_ref, bdn_ref, ys_ref,
                   xbuf, sems, wgu_b, wd_b, ptr, *, n_tiles, n_prompt_tiles):
    b = pl.program_id(0)
    nused = nused_ref[0]

    def issue(blk, slot):
        e = be_ref[blk]
        q = bq_ref[blk]
        lo_row = q * ROW_BLOCK
        hi_row = lo_row + ROW_BLOCK
        base = e * n_tiles
        j0 = jnp.where(q == 0, 0, ptr[0])

        def cond(j):
            return (j < n_tiles) & (off_ref[base + jnp.minimum(j, n_tiles - 1)] < hi_row)

        def body(j):
            o = off_ref[base + j]
            lo = jnp.maximum(o, lo_row)
            ln = jnp.minimum(o + n_ref[base + j], hi_row) - lo

            src = src_ref[base + j] + (lo - o)
            dst = xbuf.at[slot, pl.ds(lo - lo_row, ln)]

            @pl.when((ln > 0) & (j < n_prompt_tiles))
            def _():
                pltpu.make_async_copy(xsp_hbm.at[pl.ds(src, ln)], dst, sems.at[slot]).start()

            @pl.when((ln > 0) & (j >= n_prompt_tiles))
            def _():
                pltpu.make_async_copy(xss_hbm.at[pl.ds(src, ln)], dst, sems.at[slot]).start()
            return j + 1

        jend = lax.while_loop(cond, body, j0)
        ptr[0] = jnp.maximum(jend - 1, 0)

    @pl.when(b == 0)
    def _():
        xbuf[...] = jnp.zeros(xbuf.shape, U32)
        issue(0, 0)

    @pl.when(b + 1 < nused)
    def _():
        issue(b + 1, (b + 1) % 2)

    active = b < nused

    @pl.when(active)
    def _():
        slot = b % 2
        e = be_ref[b]
        rows = jnp.minimum(tot_ref[e] - bq_ref[b] * ROW_BLOCK, ROW_BLOCK)
        pltpu.make_async_copy(xsp_hbm.at[pl.ds(0, rows)], xbuf.at[slot, pl.ds(0, rows)],
                              sems.at[slot]).wait()

        @pl.when(bq_ref[b] == 0)
        def _():
            wgu_b[...] = wgu_ref[0].astype(BF16)
            wd_b[...] = wd_ref[0].astype(BF16)

        x = _unpack_rows(lambda s: xbuf[slot, :, s, :])
        gu = _dot(x, wgu_b[...]) + bgu_ref[0]
        g = jnp.minimum(gu[:, :D_FF], SWIGLU_LIMIT)
        u = jnp.clip(gu[:, D_FF:], -SWIGLU_LIMIT, SWIGLU_LIMIT)
        act = (u + 1.0) * (g * jax.nn.sigmoid(SWIGLU_ALPHA * g))
        y = _dot(act.astype(BF16), wd_b[...]) + bdn_ref[0]
        _pack_rows(y.astype(BF16).astype(F32), ys_ref)

    @pl.when(jnp.logical_not(active))
    def _():
        ys_ref[...] = jnp.zeros(ys_ref.shape, U32)


def _experts(plan, xsp, xss, w_gu, b_gu, w_down, b_down, *, n_tiles, n_prompt_tiles, rows):
    nblk = rows // ROW_BLOCK

    def wmap(b, be, *_):
        return (be[b], 0, 0)

    grid_spec = pltpu.PrefetchScalarGridSpec(
        num_scalar_prefetch=7,
        grid=(nblk,),
        in_specs=[pl.BlockSpec(memory_space=pl.ANY),
                  pl.BlockSpec(memory_space=pl.ANY),
                  pl.BlockSpec((1, D_MODEL, 2 * D_FF), wmap),
                  pl.BlockSpec((1, 1, 2 * D_FF), wmap),
                  pl.BlockSpec((1, D_FF, D_MODEL), wmap),
                  pl.BlockSpec((1, 1, D_MODEL), wmap)],
        out_specs=pl.BlockSpec((ROW_BLOCK, PACK_S, LANES), lambda b, *_: (b, 0, 0)),
        scratch_shapes=[pltpu.VMEM((2, ROW_BLOCK, PACK_S, LANES), U32),
                        pltpu.SemaphoreType.DMA((2,)),
                        pltpu.VMEM((D_MODEL, 2 * D_FF), BF16), pltpu.VMEM((D_FF, D_MODEL), BF16),
                        pltpu.SMEM((1,), I32)],
    )
    return pl.pallas_call(
        functools.partial(_expert_kernel, n_tiles=n_tiles, n_prompt_tiles=n_prompt_tiles),
        grid_spec=grid_spec,
        out_shape=jax.ShapeDtypeStruct((rows, PACK_S, LANES), U32),
        compiler_params=pltpu.CompilerParams(
            dimension_semantics=("arbitrary",), vmem_limit_bytes=VMEM_LIMIT),
        name="experts",
    )(plan['block_e'], plan['block_q'], plan['n_used'], plan['src_t'], plan['off_t'], plan['n_t'],
      plan['tot'], xsp, xss, w_gu, b_gu, w_down, b_down)


def _combine_kernel(ysrc_ref, sloc_ref, n_ref, col_ref, h_ref, ys_hbm, o_ref, buf, sems,
                    *, tm, tile0):
    j = pl.program_id(0)
    nt = pl.num_programs(0)
    rt = TOP_K * tm

    def issue(t, slot):
        base = (tile0 + t) * N_EXPERTS

        def body(e, c):
            n = n_ref[base + e]

            @pl.when(n > 0)
            def _():
                pltpu.make_async_copy(ys_hbm.at[pl.ds(ysrc_ref[base + e], n)],
                                      buf.at[slot, pl.ds(sloc_ref[base + e], n)],
                                      sems.at[slot]).start()
            return c

        lax.fori_loop(0, N_EXPERTS, body, 0)

    @pl.when(j == 0)
    def _():
        issue(0, 0)

    @pl.when(j + 1 < nt)
    def _():
        issue(j + 1, (j + 1) % 2)

    slot = j % 2
    pltpu.make_async_copy(ys_hbm.at[pl.ds(0, rt)], buf.at[slot], sems.at[slot]).wait()
    y = _unpack_rows(lambda s: buf[slot, :, s, :])
    col = col_ref[...]
    rio = lax.broadcasted_iota(I32, (tm, rt), 1)
    gmat = jnp.zeros((tm, rt), F32)
    for k in range(TOP_K):
        gmat = gmat + jnp.where(rio == col[:, k:k + 1].astype(I32),
                                col[:, TOP_K + k:TOP_K + k + 1], 0.0)
    o_ref[...] = h_ref[...] + _dot(gmat.astype(BF16), y)


def _combine(plan, col, h2d, ys, *, tm, tile0):
    t = h2d.shape[0]
    rt = TOP_K * tm
    grid_spec = pltpu.PrefetchScalarGridSpec(
        num_scalar_prefetch=3,
        grid=(t // tm,),
        in_specs=[pl.BlockSpec((tm, LANES), lambda i, *_: (i, 0)),
                  pl.BlockSpec((tm, D_MODEL), lambda i, *_: (i, 0)),
                  pl.BlockSpec(memory_space=pl.ANY)],
        out_specs=pl.BlockSpec((tm, D_MODEL), lambda i, *_: (i, 0)),
        scratch_shapes=[pltpu.VMEM((2, rt, PACK_S, LANES), U32), pltpu.SemaphoreType.DMA((2,))],
    )
    return pl.pallas_call(
        functools.partial(_combine_kernel, tm=tm, tile0=tile0),
        grid_spec=grid_spec,
        out_shape=jax.ShapeDtypeStruct((t, D_MODEL), F32),
        compiler_params=pltpu.CompilerParams(
            dimension_semantics=("arbitrary",), vmem_limit_bytes=VMEM_LIMIT),
        name="combine",
    )(plan['ysrc'], plan['sloc'], plan['n'], col, h2d, ys)


def kernel(x_prompt, x_sample, cache_k, cache_v, state_conv, norm1_g, w_in, q_norm_g, k_norm_g,
           sinks, conv_w, w_out, norm2_g, w_router, b_router, w_gu, b_gu, w_down, b_down):
    l = 0
    nbp, sp, _ = x_prompt.shape
    nbs, ss, _ = x_sample.shape
    tp, ts = nbp * sp, nbs * ss
    tmp, tms = TOKEN_TILE, ss
    ntp, nts = tp // tmp, ts // tms
    n_tiles = ntp + nts
    n_rows = (tp + ts) * TOP_K
    grp = jnp.arange(ATT_DIM) // HEAD_DIM
    w = {
        'g1': norm1_g[l][None, :],
        'w_in': w_in[l].astype(BF16),
        'qg': jnp.tile(q_norm_g[l], N_HEADS)[None, :],
        'kg': jnp.tile(k_norm_g[l], N_KV_HEADS)[None, :],
        'bd': (grp[:, None] == grp[None, :]).astype(BF16),
        'sinks': sinks[l][None, :],
        'conv_w': conv_w[l],
        'w_out': w_out[l].astype(BF16),
        'g2': norm2_g[l][None, :],
        'w_router_t': w_router[l].T,
        'b_router_t': b_router[l][:, None],
    }
    hp, xsp, colp, tcp, knp, vnp, unp = _mixer(x_prompt, None, w, tm=tmp)
    cache = (cache_k[l].reshape(nbs, WINDOW, KV_DIM), cache_v[l].reshape(nbs, WINDOW, KV_DIM),
             state_conv[l])
    hs, xss, cols, tcs, kns, vns, uns = _mixer(x_sample, cache, w, tm=tms)

    n = jnp.concatenate([tcp.reshape(ntp, N_EXPERTS), tcs.reshape(nts, N_EXPERTS)], axis=0)
    tile_base = jnp.concatenate([jnp.arange(ntp, dtype=I32) * (TOP_K * tmp),
                                 jnp.arange(nts, dtype=I32) * (TOP_K * tms)])
    sloc = jnp.cumsum(n, axis=1) - n
    off = jnp.cumsum(n, axis=0) - n
    tot = jnp.sum(n, axis=0)
    padded = (tot + ROW_BLOCK - 1) // ROW_BLOCK * ROW_BLOCK
    ends = jnp.cumsum(padded)
    base = ends - padded
    rows = (-(-n_rows // ROW_BLOCK) + N_EXPERTS) * ROW_BLOCK
    nblk = rows // ROW_BLOCK
    blk0 = jnp.arange(nblk, dtype=I32) * ROW_BLOCK
    block_e = jnp.minimum(jnp.sum((ends[None, :] <= blk0[:, None]).astype(I32), axis=1),
                          N_EXPERTS - 1)
    eq = (block_e[:, None] == jnp.arange(N_EXPERTS, dtype=I32)[None, :]).astype(I32)
    block_q = (blk0 - jnp.sum(eq * base[None, :], axis=1)) // ROW_BLOCK
    plan = {
        'block_e': block_e.astype(I32),
        'block_q': block_q.astype(I32),
        'n_used': (ends[-1:] // ROW_BLOCK).astype(I32),
        'src_t': (tile_base[:, None] + sloc).T.reshape(-1).astype(I32),
        'off_t': off.T.reshape(-1).astype(I32),
        'n_t': n.T.reshape(-1).astype(I32),
        'tot': tot.astype(I32),
        'ysrc': (base[None, :] + off).reshape(-1).astype(I32),
        'sloc': sloc.reshape(-1).astype(I32),
        'n': n.reshape(-1).astype(I32),
    }

    ys = _experts(plan, xsp, xss, w_gu[l], b_gu[l][:, None, :], w_down[l], b_down[l][:, None, :],
                  n_tiles=n_tiles, n_prompt_tiles=ntp, rows=rows)
    yp = _combine(plan, colp, hp.reshape(tp, D_MODEL), ys, tm=tmp, tile0=0)
    ysm = _combine(plan, cols, hs.reshape(ts, D_MODEL), ys, tm=tms, tile0=ntp)

    kv5 = lambda a: a.reshape(1, a.shape[0], WINDOW, N_KV_HEADS, HEAD_DIM)
    return (yp.reshape(nbp, sp, D_MODEL), ysm.reshape(nbs, ss, D_MODEL),
            kv5(knp), kv5(vnp), unp[None], kv5(kns), kv5(vns), uns[None])
```

```python
import functools

import jax
import jax.numpy as jnp
from jax import lax
from jax.experimental import pallas as pl
from jax.experimental.pallas import tpu as pltpu

D_MODEL = 1024
CHUNK = 64
HEAD_DIM = 64
N_HEADS = 8
N_KV_HEADS = 2
Q_PER_KV = N_HEADS // N_KV_HEADS
ATT_DIM = N_HEADS * HEAD_DIM
KV_DIM = N_KV_HEADS * HEAD_DIM
CONV_DIM = D_MODEL - ATT_DIM
CONV_W = 3
WINDOW = 128
IN_DIM = ATT_DIM + 2 * KV_DIM + 3 * CONV_DIM
N_EXPERTS = 32
TOP_K = 4
D_FF = D_MODEL
SWIGLU_LIMIT = 7.0
SWIGLU_ALPHA = 1.702
EPS = 1e-6
NEG = -1e30

LANES = 128
HALF = D_MODEL // 2
PACK_S = HALF // LANES
TOKEN_TILE = 256
ROW_BLOCK = 256
VMEM_LIMIT = 56 * 1024 * 1024

F32 = jnp.float32
BF16 = jnp.bfloat16
U32 = jnp.uint32
I32 = jnp.int32


def _rms_rows(x, g):
    return x * lax.rsqrt(jnp.mean(x * x, axis=-1, keepdims=True) + EPS) * g


def _split_bf16(x):
    hi = x.astype(BF16)
    lo = (x - hi.astype(F32)).astype(BF16)
    return hi, lo


def _dot(a, b):
    return jnp.dot(a, b, preferred_element_type=F32)


def _dot_nt(a, b):
    return lax.dot_general(a, b, (((1,), (1,)), ((), ())), preferred_element_type=F32)


def _head_rms(t, bd, g):
    hi, lo = _split_bf16(t * t)
    ssq = _dot(hi, bd) + _dot(lo, bd)
    return t * lax.rsqrt(ssq * (1.0 / HEAD_DIM) + EPS) * g


def _pack_rows(vals, out_ref):
    r = vals.shape[0]
    bits = pltpu.bitcast(vals, U32)
    word = (bits[:, :HALF] >> 16) | (bits[:, HALF:] & jnp.uint32(0xFFFF0000))
    flat = out_ref.reshape(r * PACK_S, LANES)
    for s in range(PACK_S):
        flat[pl.ds(s, r, stride=PACK_S), :] = word[:, s * LANES:(s + 1) * LANES]


def _unpack_rows(packed_ref):
    r = packed_ref.shape[0]
    flat = packed_ref.reshape(r * PACK_S, LANES)
    lo, hi = [], []
    for s in range(PACK_S):
        w = flat[pl.ds(s, r, stride=PACK_S), :]
        lo.append(pltpu.bitcast(w << 16, F32))
        hi.append(pltpu.bitcast(w & jnp.uint32(0xFFFF0000), F32))
    return jnp.concatenate(lo + hi, axis=1).astype(BF16)


def _mixer_kernel(*refs, tm, has_cache):
    refs = list(refs)
    x_ref = refs.pop(0)
    if has_cache:
        ck_ref, cv_ref, st_ref = refs.pop(0), refs.pop(0), refs.pop(0)
    (g1_ref, win_ref, qg_ref, kg_ref, bd_ref, sink_ref, cw_ref, wout_ref, g2_ref,
     wrt_ref, brt_ref, upper_ref, lower_ref,
     h_ref, xs_ref, col_ref, tcnt_ref, knew_ref, vnew_ref, unew_ref,
     kprev, vprev, ubuf) = refs

    i = pl.program_id(1)

    @pl.when(i == 0)
    def _():
        ubuf[0:8, :] = jnp.zeros((8, CONV_DIM), F32)
        if has_cache:
            kprev[...] = ck_ref[0]
            vprev[...] = cv_ref[0]
            ubuf[6:8, :] = st_ref[0]
        else:
            kprev[...] = jnp.zeros((WINDOW, KV_DIM), F32)
            vprev[...] = jnp.zeros((WINDOW, KV_DIM), F32)

    x = x_ref[0]
    xn = _rms_rows(x, g1_ref[...])
    proj = _dot(xn.astype(BF16), win_ref[...])
    o0 = ATT_DIM
    o1 = o0 + KV_DIM
    o2 = o1 + KV_DIM
    o3 = o2 + CONV_DIM
    o4 = o3 + CONV_DIM
    bd = bd_ref[...]
    qn = _head_rms(proj[:, :o0], bd, qg_ref[...])
    kn = _head_rms(proj[:, o0:o1], bd[:KV_DIM, :KV_DIM], kg_ref[...])
    v = proj[:, o1:o2]
    bg = proj[:, o2:o3]
    u = proj[:, o3:o4] * proj[:, o4:]

    kall = jnp.concatenate([kprev[...], kn], axis=0)
    vall = jnp.concatenate([vprev[...], v], axis=0)
    nk = WINDOW + tm
    qc = lax.broadcasted_iota(I32, (tm, nk), 0) // CHUNK
    kcol = lax.broadcasted_iota(I32, (tm, nk), 1)
    kc = kcol // CHUNK
    valid = (kc >= qc) & (kc <= qc + WINDOW // CHUNK)
    if not has_cache:
        valid = valid & (kcol >= jnp.where(i == 0, WINDOW, 0))
    kall_b = kall.astype(BF16)
    vall_b = vall.astype(BF16)
    qn_b = qn.astype(BF16)
    outs = []
    for hk in range(N_KV_HEADS):
        kh = kall_b[:, hk * HEAD_DIM:(hk + 1) * HEAD_DIM]
        vh = vall_b[:, hk * HEAD_DIM:(hk + 1) * HEAD_DIM]
        for g in range(Q_PER_KV):
            hq = hk * Q_PER_KV + g
            qh = qn_b[:, hq * HEAD_DIM:(hq + 1) * HEAD_DIM]
            s = _dot_nt(qh, kh) * (HEAD_DIM ** -0.5)
            s = jnp.where(valid, s, NEG)
            sink = sink_ref[0, hq]
            m = jnp.maximum(jnp.max(s, axis=-1, keepdims=True), sink)
            p = jnp.exp(s - m)
            den = jnp.sum(p, axis=-1, keepdims=True) + jnp.exp(sink - m)
            outs.append(_dot(p.astype(BF16), vh) / den)
    att = jnp.concatenate(outs, axis=1)

    kprev[...] = kall[tm:, :]
    vprev[...] = vall[tm:, :]
    knew_ref[0] = kall[tm:, :]
    vnew_ref[0] = vall[tm:, :]

    ubuf[8:8 + tm, :] = u
    cw = cw_ref[...]
    cv = ubuf[6:6 + tm, :] * cw[0:1, :] + ubuf[7:7 + tm, :] * cw[1:2, :]
    cv = cv + u * cw[2:3, :]
    ubuf[0:8, :] = ubuf[tm:tm + 8, :]
    unew_ref[0] = u[tm - (CONV_W - 1):, :]

    mix = jnp.concatenate([att, bg * cv], axis=1).astype(BF16)
    h = x + _dot(mix, wout_ref[...])
    h_ref[0] = h
    hn = _rms_rows(h, g2_ref[...])

    hn_hi, hn_lo = _split_bf16(hn)
    wr_hi, wr_lo = _split_bf16(wrt_ref[...])
    logits = (_dot_nt(wr_hi, hn_hi) + _dot_nt(wr_hi, hn_lo) + _dot_nt(wr_lo, hn_hi)) + brt_ref[...]
    eio = lax.broadcasted_iota(I32, (N_EXPERTS, tm), 0)
    work = logits
    tops, sels = [], []
    for _ in range(TOP_K):
        mk = jnp.max(work, axis=0, keepdims=True)
        ik = jnp.min(jnp.where(work == mk, eio, N_EXPERTS), axis=0, keepdims=True)
        sel = eio == ik
        work = jnp.where(sel, -jnp.inf, work)
        tops.append(mk)
        sels.append(sel)
    es = [jnp.exp(t - tops[0]) for t in tops]
    esum = es[0] + es[1] + es[2] + es[3]

    onehot = jnp.zeros((N_EXPERTS, tm), F32)
    for sel in sels:
        onehot = onehot + jnp.where(sel, 1.0, 0.0)
    onehot_b = onehot.astype(BF16)
    before = _dot(onehot_b, upper_ref[...])
    start = jnp.sum(_dot(lower_ref[...], onehot_b), axis=1, keepdims=True)
    where_to = start + before
    pos = [jnp.sum(jnp.where(sel, where_to, 0.0), axis=0, keepdims=True) for sel in sels]
    rio = lax.broadcasted_iota(I32, (TOP_K * tm, tm), 0)
    hit = rio == pos[0].astype(I32)
    for k in range(1, TOP_K):
        hit = hit | (rio == pos[k].astype(I32))
    perm = jnp.where(hit, 1.0, 0.0).astype(BF16)
    _pack_rows(_dot(perm, hn_hi), xs_ref)

    tcnt_ref[0] = _dot_nt(jnp.ones((1, tm), BF16), onehot_b).astype(I32)

    lanes = -(-tm // LANES) * LANES
    rows8 = jnp.concatenate(pos + [e / esum for e in es], axis=0)
    if lanes != tm:
        rows8 = jnp.concatenate([rows8, jnp.zeros((2 * TOP_K, lanes - tm), F32)], axis=1)
    sq = jnp.concatenate([rows8, jnp.zeros((LANES - 2 * TOP_K, lanes), F32)], axis=0)
    col_ref[...] = jnp.transpose(sq)[:tm, :]


def _mixer(x, cache, w, *, tm):
    nb, s, _ = x.shape
    nt = s // tm
    n_rows = nb * s * TOP_K
    has_cache = cache is not None
    full = lambda shape: pl.BlockSpec(shape, lambda b, i: (0,) * len(shape))
    per_b = lambda shape: pl.BlockSpec((1,) + shape, lambda b, i: (b, 0, 0))
    tile = lambda last: pl.BlockSpec((1, tm, last), lambda b, i: (b, i, 0))

    in_specs = [tile(D_MODEL)]
    args = [x]
    if has_cache:
        in_specs += [per_b((WINDOW, KV_DIM)), per_b((WINDOW, KV_DIM)), per_b((CONV_W - 1, CONV_DIM))]
        args += list(cache)
    tio = jnp.arange(tm)
    eio = jnp.arange(N_EXPERTS)
    upper = (tio[:, None] < tio[None, :]).astype(BF16)
    lower = (eio[None, :] < eio[:, None]).astype(BF16)
    in_specs += [
        full((1, D_MODEL)), full((D_MODEL, IN_DIM)), full((1, ATT_DIM)),
        full((1, KV_DIM)), full((ATT_DIM, ATT_DIM)),
        pl.BlockSpec(memory_space=pltpu.SMEM),
        full((CONV_W, CONV_DIM)), full((D_MODEL, D_MODEL)), full((1, D_MODEL)),
        full((N_EXPERTS, D_MODEL)), full((N_EXPERTS, 1)), full((tm, tm)), full((N_EXPERTS, N_EXPERTS)),
    ]
    args += [w['g1'], w['w_in'], w['qg'], w['kg'], w['bd'], w['sinks'], w['conv_w'],
             w['w_out'], w['g2'], w['w_router_t'], w['b_router_t'], upper, lower]
    rt = TOP_K * tm
    out_shape = [
        jax.ShapeDtypeStruct((nb, s, D_MODEL), F32),
        jax.ShapeDtypeStruct((n_rows, PACK_S, LANES), U32),
        jax.ShapeDtypeStruct((nb * s, LANES), F32),
        jax.ShapeDtypeStruct((nb * nt, 1, N_EXPERTS), I32),
        jax.ShapeDtypeStruct((nb, WINDOW, KV_DIM), F32),
        jax.ShapeDtypeStruct((nb, WINDOW, KV_DIM), F32),
        jax.ShapeDtypeStruct((nb, CONV_W - 1, CONV_DIM), F32),
    ]
    out_specs = [
        tile(D_MODEL),
        pl.BlockSpec((rt, PACK_S, LANES), lambda b, i: (b * nt + i, 0, 0)),
        pl.BlockSpec((tm, LANES), lambda b, i: (b * nt + i, 0)),
        pl.BlockSpec((1, 1, N_EXPERTS), lambda b, i: (b * nt + i, 0, 0)),
        per_b((WINDOW, KV_DIM)), per_b((WINDOW, KV_DIM)), per_b((CONV_W - 1, CONV_DIM)),
    ]
    scratch = [pltpu.VMEM((WINDOW, KV_DIM), F32), pltpu.VMEM((WINDOW, KV_DIM), F32),
               pltpu.VMEM((tm + 8, CONV_DIM), F32)]
    return pl.pallas_call(
        functools.partial(_mixer_kernel, tm=tm, has_cache=has_cache),
        grid=(nb, nt),
        in_specs=in_specs,
        out_specs=out_specs,
        out_shape=out_shape,
        scratch_shapes=scratch,
        compiler_params=pltpu.CompilerParams(
            dimension_semantics=("arbitrary", "arbitrary"), vmem_limit_bytes=VMEM_LIMIT),
        name="mixer_sample" if has_cache else "mixer_prompt",
    )(*args)


def _expert_kernel(be_ref, bq_ref, nused_ref, src_ref, off_ref, n_ref, tot_ref,
                   xsp_hbm, xss_hbm, wgu_ref, bgu_ref, wd_ref, bdn_ref, ys_ref,
                   xbuf, sems, wgu_b, wd_b, ptr, *, n_tiles, n_prompt_tiles):
    b = pl.program_id(0)
    nused = nused_ref[0]

    def issue(blk, slot):
        e = be_ref[blk]
        q = bq_ref[blk]
        lo_row = q * ROW_BLOCK
        hi_row = lo_row + ROW_BLOCK
        base = e * n_tiles
        j0 = jnp.where(q == 0, 0, ptr[0])

        def cond(j):
            return (j < n_tiles) & (off_ref[base + jnp.minimum(j, n_tiles - 1)] < hi_row)

        def body(j):
            o = off_ref[base + j]
            lo = jnp.maximum(o, lo_row)
            ln = jnp.minimum(o + n_ref[base + j], hi_row) - lo

            src = src_ref[base + j] + (lo - o)
            dst = xbuf.at[slot, pl.ds(lo - lo_row, ln)]

            @pl.when((ln > 0) & (j < n_prompt_tiles))
            def _():
                pltpu.make_async_copy(xsp_hbm.at[pl.ds(src, ln)], dst, sems.at[slot]).start()

            @pl.when((ln > 0) & (j >= n_prompt_tiles))
            def _():
                pltpu.make_async_copy(xss_hbm.at[pl.ds(src, ln)], dst, sems.at[slot]).start()
            return j + 1

        jend = lax.while_loop(cond, body, j0)
        ptr[0] = jnp.maximum(jend - 1, 0)

    @pl.when(b == 0)
    def _():
        xbuf[...] = jnp.zeros(xbuf.shape, U32)
        issue(0, 0)

    @pl.when(b + 1 < nused)
    def _():
        issue(b + 1, (b + 1) % 2)

    active = b < nused

    @pl.when(active)
    def _():
        slot = b % 2
        e = be_ref[b]
        rows = jnp.minimum(tot_ref[e] - bq_ref[b] * ROW_BLOCK, ROW_BLOCK)
        pltpu.make_async_copy(xsp_hbm.at[pl.ds(0, rows)], xbuf.at[slot, pl.ds(0, rows)],
                              sems.at[slot]).wait()

        @pl.when(bq_ref[b] == 0)
        def _():
            wgu_b[...] = wgu_ref[0].astype(BF16)
            wd_b[...] = wd_ref[0].astype(BF16)

        x = _unpack_rows(xbuf.at[slot])
        gu = _dot(x, wgu_b[...]) + bgu_ref[0]
        g = jnp.minimum(gu[:, :D_FF], SWIGLU_LIMIT)
        u = jnp.clip(gu[:, D_FF:], -SWIGLU_LIMIT, SWIGLU_LIMIT)
        act = (u + 1.0) * (g * jax.nn.sigmoid(SWIGLU_ALPHA * g))
        y = _dot(act.astype(BF16), wd_b[...]) + bdn_ref[0]
        _pack_rows(y.astype(BF16).astype(F32), ys_ref)

    @pl.when(jnp.logical_not(active))
    def _():
        ys_ref[...] = jnp.zeros(ys_ref.shape, U32)


def _experts(plan, xsp, xss, w_gu, b_gu, w_down, b_down, *, n_tiles, n_prompt_tiles, rows):
    nblk = rows // ROW_BLOCK

    def wmap(b, be, *_):
        return (be[b], 0, 0)

    grid_spec = pltpu.PrefetchScalarGridSpec(
        num_scalar_prefetch=7,
        grid=(nblk,),
        in_specs=[pl.BlockSpec(memory_space=pl.ANY),
                  pl.BlockSpec(memory_space=pl.ANY),
                  pl.BlockSpec((1, D_MODEL, 2 * D_FF), wmap),
                  pl.BlockSpec((1, 1, 2 * D_FF), wmap),
                  pl.BlockSpec((1, D_FF, D_MODEL), wmap),
                  pl.BlockSpec((1, 1, D_MODEL), wmap)],
        out_specs=pl.BlockSpec((ROW_BLOCK, PACK_S, LANES), lambda b, *_: (b, 0, 0)),
        scratch_shapes=[pltpu.VMEM((2, ROW_BLOCK, PACK_S, LANES), U32),
                        pltpu.SemaphoreType.DMA((2,)),
                        pltpu.VMEM((D_MODEL, 2 * D_FF), BF16), pltpu.VMEM((D_FF, D_MODEL), BF16),
                        pltpu.SMEM((1,), I32)],
    )
    return pl.pallas_call(
        functools.partial(_expert_kernel, n_tiles=n_tiles, n_prompt_tiles=n_prompt_tiles),
        grid_spec=grid_spec,
        out_shape=jax.ShapeDtypeStruct((rows, PACK_S, LANES), U32),
        compiler_params=pltpu.CompilerParams(
            dimension_semantics=("arbitrary",), vmem_limit_bytes=VMEM_LIMIT),
        name="experts",
    )(plan['block_e'], plan['block_q'], plan['n_used'], plan['src_t'], plan['off_t'], plan['n_t'],
      plan['tot'], xsp, xss, w_gu, b_gu, w_down, b_down)


def _combine_kernel(ysrc_ref, sloc_ref, n_ref, col_ref, h_ref, ys_hbm, o_ref, buf, sems,
                    *, tm, tile0):
    j = pl.program_id(0)
    nt = pl.num_programs(0)
    rt = TOP_K * tm

    def issue(t, slot):
        base = (tile0 + t) * N_EXPERTS

        def body(e, c):
            n = n_ref[base + e]

            @pl.when(n > 0)
            def _():
                pltpu.make_async_copy(ys_hbm.at[pl.ds(ysrc_ref[base + e], n)],
                                      buf.at[slot, pl.ds(sloc_ref[base + e], n)],
                                      sems.at[slot]).start()
            return c

        lax.fori_loop(0, N_EXPERTS, body, 0)

    @pl.when(j == 0)
    def _():
        issue(0, 0)

    @pl.when(j + 1 < nt)
    def _():
        issue(j + 1, (j + 1) % 2)

    slot = j % 2
    pltpu.make_async_copy(ys_hbm.at[pl.ds(0, rt)], buf.at[slot], sems.at[slot]).wait()
    y = _unpack_rows(buf.at[slot])
    col = col_ref[...]
    rio = lax.broadcasted_iota(I32, (tm, rt), 1)
    gmat = jnp.zeros((tm, rt), F32)
    for k in range(TOP_K):
        gmat = gmat + jnp.where(rio == col[:, k:k + 1].astype(I32),
                                col[:, TOP_K + k:TOP_K + k + 1], 0.0)
    o_ref[...] = h_ref[...] + _dot(gmat.astype(BF16), y)


def _combine(plan, col, h2d, ys, *, tm, tile0):
    t = h2d.shape[0]
    rt = TOP_K * tm
    grid_spec = pltpu.PrefetchScalarGridSpec(
        num_scalar_prefetch=3,
        grid=(t // tm,),
        in_specs=[pl.BlockSpec((tm, LANES), lambda i, *_: (i, 0)),
                  pl.BlockSpec((tm, D_MODEL), lambda i, *_: (i, 0)),
                  pl.BlockSpec(memory_space=pl.ANY)],
        out_specs=pl.BlockSpec((tm, D_MODEL), lambda i, *_: (i, 0)),
        scratch_shapes=[pltpu.VMEM((2, rt, PACK_S, LANES), U32), pltpu.SemaphoreType.DMA((2,))],
    )
    return pl.pallas_call(
        functools.partial(_combine_kernel, tm=tm, tile0=tile0),
        grid_spec=grid_spec,
        out_shape=jax.ShapeDtypeStruct((t, D_MODEL), F32),
        compiler_params=pltpu.CompilerParams(
            dimension_semantics=("arbitrary",), vmem_limit_bytes=VMEM_LIMIT),
        name="combine",
    )(plan['ysrc'], plan['sloc'], plan['n'], col, h2d, ys)


def kernel(x_prompt, x_sample, cache_k, cache_v, state_conv, norm1_g, w_in, q_norm_g, k_norm_g,
           sinks, conv_w, w_out, norm2_g, w_router, b_router, w_gu, b_gu, w_down, b_down):
    l = 0
    nbp, sp, _ = x_prompt.shape
    nbs, ss, _ = x_sample.shape
    tp, ts = nbp * sp, nbs * ss
    tmp, tms = TOKEN_TILE, ss
    ntp, nts = tp // tmp, ts // tms
    n_tiles = ntp + nts
    n_rows = (tp + ts) * TOP_K
    grp = jnp.arange(ATT_DIM) // HEAD_DIM
    w = {
        'g1': norm1_g[l][None, :],
        'w_in': w_in[l].astype(BF16),
        'qg': jnp.tile(q_norm_g[l], N_HEADS)[None, :],
        'kg': jnp.tile(k_norm_g[l], N_KV_HEADS)[None, :],
        'bd': (grp[:, None] == grp[None, :]).astype(BF16),
        'sinks': sinks[l][None, :],
        'conv_w': conv_w[l],
        'w_out': w_out[l].astype(BF16),
        'g2': norm2_g[l][None, :],
        'w_router_t': w_router[l].T,
        'b_router_t': b_router[l][:, None],
    }
    hp, xsp, colp, tcp, knp, vnp, unp = _mixer(x_prompt, None, w, tm=tmp)
    cache = (cache_k[l].reshape(nbs, WINDOW, KV_DIM), cache_v[l].reshape(nbs, WINDOW, KV_DIM),
             state_conv[l])
    hs, xss, cols, tcs, kns, vns, uns = _mixer(x_sample, cache, w, tm=tms)

    n = jnp.concatenate([tcp.reshape(ntp, N_EXPERTS), tcs.reshape(nts, N_EXPERTS)], axis=0)
    tile_base = jnp.concatenate([jnp.arange(ntp, dtype=I32) * (TOP_K * tmp),
                                 jnp.arange(nts, dtype=I32) * (TOP_K * tms)])
    sloc = jnp.cumsum(n, axis=1) - n
    off = jnp.cumsum(n, axis=0) - n
    tot = jnp.sum(n, axis=0)
    padded = (tot + ROW_BLOCK - 1) // ROW_BLOCK * ROW_BLOCK
    ends = jnp.cumsum(padded)
    base = ends - padded
    rows = (-(-n_rows // ROW_BLOCK) + N_EXPERTS) * ROW_BLOCK
    nblk = rows // ROW_BLOCK
    blk0 = jnp.arange(nblk, dtype=I32) * ROW_BLOCK
    block_e = jnp.minimum(jnp.sum((ends[None, :] <= blk0[:, None]).astype(I32), axis=1),
                          N_EXPERTS - 1)
    eq = (block_e[:, None] == jnp.arange(N_EXPERTS, dtype=I32)[None, :]).astype(I32)
    block_q = (blk0 - jnp.sum(eq * base[None, :], axis=1)) // ROW_BLOCK
    plan = {
        'block_e': block_e.astype(I32),
        'block_q': block_q.astype(I32),
        'n_used': (ends[-1:] // ROW_BLOCK).astype(I32),
        'src_t': (tile_base[:, None] + sloc).T.reshape(-1).astype(I32),
        'off_t': off.T.reshape(-1).astype(I32),
        'n_t': n.T.reshape(-1).astype(I32),
        'tot': tot.astype(I32),
        'ysrc': (base[None, :] + off).reshape(-1).astype(I32),
        'sloc': sloc.reshape(-1).astype(I32),
        'n': n.reshape(-1).astype(I32),
    }

    ys = _experts(plan, xsp, xss, w_gu[l], b_gu[l][:, None, :], w_down[l], b_down[l][:, None, :],
                  n_tiles=n_tiles, n_prompt_tiles=ntp, rows=rows)
    yp = _combine(plan, colp, hp.reshape(tp, D_MODEL), ys, tm=tmp, tile0=0)
    ysm = _combine(plan, cols, hs.reshape(ts, D_MODEL), ys, tm=tms, tile0=ntp)

    kv5 = lambda a: a.reshape(1, a.shape[0], WINDOW, N_KV_HEADS, HEAD_DIM)
    return (yp.reshape(nbp, sp, D_MODEL), ysm.reshape(nbs, ss, D_MODEL),
            kv5(knp), kv5(vnp), unp[None], kv5(kns), kv5(vns), uns[None])
```

```python
import functools

import jax
import jax.numpy as jnp
from jax import lax
from jax.experimental import pallas as pl
from jax.experimental.pallas import tpu as pltpu

D_MODEL = 1024
CHUNK = 64
HEAD_DIM = 64
N_HEADS = 8
N_KV_HEADS = 2
Q_PER_KV = N_HEADS // N_KV_HEADS
ATT_DIM = N_HEADS * HEAD_DIM
KV_DIM = N_KV_HEADS * HEAD_DIM
GROUP_W = Q_PER_KV * HEAD_DIM
CONV_DIM = D_MODEL - ATT_DIM
CONV_W = 3
WINDOW = 128
IN_DIM = ATT_DIM + 2 * KV_DIM + 3 * CONV_DIM
N_EXPERTS = 32
TOP_K = 4
D_FF = D_MODEL
SWIGLU_LIMIT = 7.0
SWIGLU_ALPHA = 1.702
EPS = 1e-6
NEG = -1e30

LANES = 128
HALF = D_MODEL // 2
PACK_S = HALF // LANES
TOKEN_TILE = 256
STREAMS_PER_STEP = 2
ROW_BLOCK = 256
VMEM_LIMIT = 56 * 1024 * 1024

F32 = jnp.float32
BF16 = jnp.bfloat16
U32 = jnp.uint32
I32 = jnp.int32


def _rms_rows(x, g):
    return x * lax.rsqrt(jnp.mean(x * x, axis=-1, keepdims=True) + EPS) * g


def _split_bf16(x):
    hi = x.astype(BF16)
    lo = (x - hi.astype(F32)).astype(BF16)
    return hi, lo


def _dot(a, b):
    return jnp.dot(a, b, preferred_element_type=F32)


def _dot_nt(a, b):
    return lax.dot_general(a, b, (((1,), (1,)), ((), ())), preferred_element_type=F32)


def _head_rms(t, bd, g):
    hi, lo = _split_bf16(t * t)
    ssq = _dot(hi, bd) + _dot(lo, bd)
    return t * lax.rsqrt(ssq * (1.0 / HEAD_DIM) + EPS) * g


def _pack_rows(vals, out_ref):
    r = vals.shape[0]
    bits = pltpu.bitcast(vals, U32)
    word = (bits[:, :HALF] >> 16) | (bits[:, HALF:] & jnp.uint32(0xFFFF0000))
    flat = out_ref.reshape(r * PACK_S, LANES)
    for s in range(PACK_S):
        flat[pl.ds(s, r, stride=PACK_S), :] = word[:, s * LANES:(s + 1) * LANES]


def _unpack_rows(packed_ref):
    r = packed_ref.shape[0]
    flat = packed_ref.reshape(r * PACK_S, LANES)
    lo, hi = [], []
    for s in range(PACK_S):
        w = flat[pl.ds(s, r, stride=PACK_S), :]
        lo.append(pltpu.bitcast(w << 16, F32))
        hi.append(pltpu.bitcast(w & jnp.uint32(0xFFFF0000), F32))
    return jnp.concatenate(lo + hi, axis=1).astype(BF16)


def _attention(i, qn, kall, vall, rep, sink_ref, *, mask_history):
    tm = qn.shape[0]
    k4 = _dot(kall.astype(BF16), rep).astype(BF16)
    v4 = _dot(vall.astype(BF16), rep).astype(BF16)
    lane_grp = lax.broadcasted_iota(I32, (CHUNK, GROUP_W), 1) // HEAD_DIM
    row_grp = lax.broadcasted_iota(I32, (Q_PER_KV * CHUNK, 1), 0) // CHUNK
    span = WINDOW + CHUNK
    att_rows = []
    for c in range(tm // CHUNK):
        per_kv = []
        for hk in range(N_KV_HEADS):
            qc = qn[c * CHUNK:(c + 1) * CHUNK, hk * GROUP_W:(hk + 1) * GROUP_W]
            qs = jnp.concatenate([jnp.where(lane_grp == g, qc, 0.0) for g in range(Q_PER_KV)],
                                 axis=0).astype(BF16)
            kw = k4[c * CHUNK:c * CHUNK + span, hk * GROUP_W:(hk + 1) * GROUP_W]
            vw = v4[c * CHUNK:c * CHUNK + span, hk * GROUP_W:(hk + 1) * GROUP_W]
            s = _dot_nt(qs, kw) * (HEAD_DIM ** -0.5)
            if mask_history and c * CHUNK < WINDOW:
                kcol = lax.broadcasted_iota(I32, (1, span), 1)
                s = jnp.where(kcol >= jnp.where(i == 0, WINDOW - c * CHUNK, 0), s, NEG)
            sink = jnp.zeros((Q_PER_KV * CHUNK, 1), F32)
            for g in range(Q_PER_KV):
                sink = jnp.where(row_grp == g, sink_ref[0, hk * Q_PER_KV + g], sink)
            m = jnp.maximum(jnp.max(s, axis=-1, keepdims=True), sink)
            p = jnp.exp(s - m)
            den = jnp.sum(p, axis=-1, keepdims=True) + jnp.exp(sink - m)
            o = _dot(p.astype(BF16), vw) / den
            oc = jnp.zeros((CHUNK, GROUP_W), F32)
            for g in range(Q_PER_KV):
                oc = oc + jnp.where(lane_grp == g, o[g * CHUNK:(g + 1) * CHUNK, :], 0.0)
            per_kv.append(oc)
        att_rows.append(jnp.concatenate(per_kv, axis=1))
    return jnp.concatenate(att_rows, axis=0)


def _mixer_kernel(*refs, tm, nsub, st, has_cache):
    refs = list(refs)
    x_ref = refs.pop(0)
    if has_cache:
        ck_ref, cv_ref, st_ref = refs.pop(0), refs.pop(0), refs.pop(0)
    (g1_ref, win_ref, qg_ref, kg_ref, bd_ref, rep_ref, sink_ref, cw_ref, wout_ref, g2_ref,
     wrt_ref, brt_ref, upper_ref, lower_ref,
     h_ref, xs_ref, col_ref, tcnt_ref, knew_ref, vnew_ref, unew_ref, kprev, vprev, ubuf) = refs
    i = pl.program_id(1)
    ntok = nsub * tm

    @pl.when(i == 0)
    def _():
        ubuf[:, 0:8, :] = jnp.zeros((nsub, 8, CONV_DIM), F32)
        if has_cache:
            kprev[...] = ck_ref[...]
            vprev[...] = cv_ref[...]
            ubuf[:, 6:8, :] = st_ref[...]
        else:
            kprev[...] = jnp.zeros(kprev.shape, F32)
            vprev[...] = jnp.zeros(vprev.shape, F32)

    x = jnp.concatenate([x_ref[sub] for sub in range(nsub)], axis=0) if nsub > 1 else x_ref[0]
    xn = _rms_rows(x, g1_ref[...])
    proj = _dot(xn.astype(BF16), win_ref[...])
    o0 = ATT_DIM
    o1 = o0 + KV_DIM
    o2 = o1 + KV_DIM
    o3 = o2 + CONV_DIM
    o4 = o3 + CONV_DIM
    bd = bd_ref[...]
    qn = _head_rms(proj[:, :o0], bd, qg_ref[...])
    kn = _head_rms(proj[:, o0:o1], bd[:KV_DIM, :KV_DIM], kg_ref[...])
    v = proj[:, o1:o2]
    bg = proj[:, o2:o3]
    u = proj[:, o3:o4] * proj[:, o4:]

    rep = rep_ref[...]
    cw = cw_ref[...]
    att_parts, cv_parts = [], []
    for sub in range(nsub):
        r0, r1 = sub * tm, (sub + 1) * tm
        kall = jnp.concatenate([kprev[sub], kn[r0:r1]], axis=0)
        vall = jnp.concatenate([vprev[sub], v[r0:r1]], axis=0)
        att_parts.append(_attention(i, qn[r0:r1], kall, vall, rep, sink_ref,
                                    mask_history=not has_cache))
        kprev[sub] = kall[tm:, :]
        vprev[sub] = vall[tm:, :]
        knew_ref[sub] = kall[tm:, :]
        vnew_ref[sub] = vall[tm:, :]

        us = u[r0:r1]
        ubuf[sub, 8:8 + tm, :] = us
        cvs = ubuf[sub, 6:6 + tm, :] * cw[0:1, :] + ubuf[sub, 7:7 + tm, :] * cw[1:2, :]
        cv_parts.append(cvs + us * cw[2:3, :])
        ubuf[sub, 0:8, :] = ubuf[sub, tm:tm + 8, :]
        unew_ref[sub] = us[tm - (CONV_W - 1):, :]
    att = jnp.concatenate(att_parts, axis=0) if nsub > 1 else att_parts[0]
    cv = jnp.concatenate(cv_parts, axis=0) if nsub > 1 else cv_parts[0]

    mix = jnp.concatenate([att, bg * cv], axis=1).astype(BF16)
    h = x + _dot(mix, wout_ref[...])
    for sub in range(nsub):
        h_ref[sub] = h[sub * tm:(sub + 1) * tm]
    hn = _rms_rows(h, g2_ref[...])

    hn_hi, hn_lo = _split_bf16(hn)
    wr_hi, wr_lo = _split_bf16(wrt_ref[...])
    logits = (_dot_nt(wr_hi, hn_hi) + _dot_nt(wr_hi, hn_lo) + _dot_nt(wr_lo, hn_hi)) + brt_ref[...]
    eio = lax.broadcasted_iota(I32, (N_EXPERTS, ntok), 0)
    work = logits
    tops, sels = [], []
    for _ in range(TOP_K):
        mk = jnp.max(work, axis=0, keepdims=True)
        ik = jnp.min(jnp.where(work == mk, eio, N_EXPERTS), axis=0, keepdims=True)
        sel = eio == ik
        work = jnp.where(sel, -jnp.inf, work)
        tops.append(mk)
        sels.append(sel)
    es = [jnp.exp(t - tops[0]) for t in tops]
    esum = es[0] + es[1] + es[2] + es[3]

    onehot = jnp.zeros((N_EXPERTS, ntok), F32)
    for sel in sels:
        onehot = onehot + jnp.where(sel, 1.0, 0.0)
    onehot_b = onehot.astype(BF16)
    before = _dot(onehot_b, upper_ref[...])
    smaller = _dot(lower_ref[...], onehot_b)
    nst = ntok // st
    starts = [jnp.broadcast_to(jnp.sum(smaller[:, k * st:(k + 1) * st], axis=1, keepdims=True),
                               (N_EXPERTS, st)) for k in range(nst)]
    where_to = before + (jnp.concatenate(starts, axis=1) if nst > 1 else starts[0])
    pos = [jnp.sum(jnp.where(sel, where_to, 0.0), axis=0, keepdims=True) for sel in sels]
    rio = lax.broadcasted_iota(I32, (TOP_K * st, st), 0)
    for k in range(nst):
        c0, c1 = k * st, (k + 1) * st
        hit = rio == pos[0][:, c0:c1].astype(I32)
        for slot in range(1, TOP_K):
            hit = hit | (rio == pos[slot][:, c0:c1].astype(I32))
        perm = jnp.where(hit, 1.0, 0.0).astype(BF16)
        _pack_rows(_dot(perm, hn_hi[c0:c1]), xs_ref.at[k, 0])
        tcnt_ref[k, 0] = _dot_nt(jnp.ones((1, st), BF16), onehot_b[:, c0:c1]).astype(I32)

    rows8 = jnp.concatenate(pos + [e / esum for e in es], axis=0)
    sq = jnp.concatenate([rows8, jnp.zeros((LANES - 2 * TOP_K, ntok), F32)], axis=0)
    cols = jnp.transpose(sq)
    for sub in range(nsub):
        col_ref[sub] = cols[sub * tm:(sub + 1) * tm]


def _mixer(x, cache, w, *, tm, nsub, st):
    nb, s, _ = x.shape
    nt = s // tm
    nst = nsub * tm // st
    has_cache = cache is not None
    full = lambda shape: pl.BlockSpec(shape, lambda b, i: (0,) * len(shape))
    per_b = lambda shape: pl.BlockSpec((nsub,) + shape, lambda b, i: (b, 0, 0))
    tile = lambda last: pl.BlockSpec((nsub, tm, last), lambda b, i: (b, i, 0))

    in_specs = [tile(D_MODEL)]
    args = [x]
    if has_cache:
        in_specs += [per_b((WINDOW, KV_DIM)), per_b((WINDOW, KV_DIM)), per_b((CONV_W - 1, CONV_DIM))]
        args += list(cache)
    tio = jnp.arange(nsub * tm)
    eio = jnp.arange(N_EXPERTS)
    upper = ((tio[:, None] < tio[None, :]) & (tio[:, None] // st == tio[None, :] // st)).astype(BF16)
    lower = (eio[None, :] < eio[:, None]).astype(BF16)
    in_specs += [
        full((1, D_MODEL)), full((D_MODEL, IN_DIM)), full((1, ATT_DIM)),
        full((1, KV_DIM)), full((ATT_DIM, ATT_DIM)), full((KV_DIM, N_KV_HEADS * GROUP_W)),
        pl.BlockSpec(memory_space=pltpu.SMEM),
        full((CONV_W, CONV_DIM)), full((D_MODEL, D_MODEL)), full((1, D_MODEL)),
        full((N_EXPERTS, D_MODEL)), full((N_EXPERTS, 1)), full((nsub * tm, nsub * tm)),
        full((N_EXPERTS, N_EXPERTS)),
    ]
    args += [w['g1'], w['w_in'], w['qg'], w['kg'], w['bd'], w['rep'], w['sinks'], w['conv_w'],
             w['w_out'], w['g2'], w['w_router_t'], w['b_router_t'], upper, lower]
    rt = TOP_K * st
    nsort = nb // nsub * nst
    out_shape = [
        jax.ShapeDtypeStruct((nb, s, D_MODEL), F32),
        jax.ShapeDtypeStruct((nsort, nt, rt, PACK_S, LANES), U32),
        jax.ShapeDtypeStruct((nb, s, LANES), F32),
        jax.ShapeDtypeStruct((nsort, nt, 1, N_EXPERTS), I32),
        jax.ShapeDtypeStruct((nb, WINDOW, KV_DIM), F32),
        jax.ShapeDtypeStruct((nb, WINDOW, KV_DIM), F32),
        jax.ShapeDtypeStruct((nb, CONV_W - 1, CONV_DIM), F32),
    ]
    out_specs = [
        tile(D_MODEL),
        pl.BlockSpec((nst, 1, rt, PACK_S, LANES), lambda b, i: (b, i, 0, 0, 0)),
        tile(LANES),
        pl.BlockSpec((nst, 1, 1, N_EXPERTS), lambda b, i: (b, i, 0, 0)),
        per_b((WINDOW, KV_DIM)), per_b((WINDOW, KV_DIM)), per_b((CONV_W - 1, CONV_DIM)),
    ]
    scratch = [pltpu.VMEM((nsub, WINDOW, KV_DIM), F32), pltpu.VMEM((nsub, WINDOW, KV_DIM), F32),
               pltpu.VMEM((nsub, tm + 8, CONV_DIM), F32)]
    return pl.pallas_call(
        functools.partial(_mixer_kernel, tm=tm, nsub=nsub, st=st, has_cache=has_cache),
        grid=(nb // nsub, nt),
        in_specs=in_specs,
        out_specs=out_specs,
        out_shape=out_shape,
        scratch_shapes=scratch,
        compiler_params=pltpu.CompilerParams(
            dimension_semantics=("arbitrary", "arbitrary"), vmem_limit_bytes=VMEM_LIMIT),
        name="mixer_sample" if has_cache else "mixer_prompt",
    )(*args)


def _expert_kernel(be_ref, bq_ref, nused_ref, src_ref, off_ref, n_ref, tot_ref,
                   xsp_hbm, xss_hbm, wgu_hbm, bgu_ref, wd_hbm, bdn_ref, ys_ref,
                   xbuf, sems, wgu_f, wd_f, wsems, wgu_b, wd_b, ptr, *, n_tiles, n_prompt_tiles):
    b = pl.program_id(0)
    nused = nused_ref[0]

    def weight_copies(e):
        return (pltpu.make_async_copy(wgu_hbm.at[e], wgu_f, wsems.at[0]),
                pltpu.make_async_copy(wd_hbm.at[e], wd_f, wsems.at[1]))

    def issue_rows(blk, slot):
        e = be_ref[blk]
        q = bq_ref[blk]
        lo_row = q * ROW_BLOCK
        hi_row = lo_row + ROW_BLOCK
        base = e * n_tiles

        def run(j_start, j_limit, src_hbm):
            def cond(j):
                return (j < j_limit) & (off_ref[base + jnp.minimum(j, n_tiles - 1)] < hi_row)

            def body(j):
                o = off_ref[base + j]
                lo = jnp.maximum(o, lo_row)
                ln = jnp.minimum(o + n_ref[base + j], hi_row) - lo

                @pl.when(ln > 0)
                def _():
                    pltpu.make_async_copy(src_hbm.at[pl.ds(src_ref[base + j] + (lo - o), ln)],
                                          xbuf.at[slot, pl.ds(lo - lo_row, ln)],
                                          sems.at[slot]).start()
                return j + 1

            return lax.while_loop(cond, body, j_start)

        j0 = jnp.where(q == 0, 0, ptr[0])
        j1 = run(j0, n_prompt_tiles, xsp_hbm)
        j2 = run(jnp.maximum(j1, n_prompt_tiles), n_tiles, xss_hbm)
        ptr[0] = jnp.maximum(jnp.where(j1 < n_prompt_tiles, j1, j2) - 1, 0)

    @pl.when(b == 0)
    def _():
        xbuf[...] = jnp.zeros(xbuf.shape, U32)
        issue_rows(0, 0)
        for c in weight_copies(be_ref[0]):
            c.start()

    @pl.when(b + 1 < nused)
    def _():
        issue_rows(b + 1, (b + 1) % 2)

    active = b < nused

    @pl.when(active)
    def _():
        slot = b % 2
        e = be_ref[b]
        rows = jnp.minimum(tot_ref[e] - bq_ref[b] * ROW_BLOCK, ROW_BLOCK)
        pltpu.make_async_copy(xsp_hbm.at[pl.ds(0, rows)], xbuf.at[slot, pl.ds(0, rows)],
                              sems.at[slot]).wait()

        @pl.when(bq_ref[b] == 0)
        def _():
            for c in weight_copies(e):
                c.wait()
            wgu_b[...] = wgu_f[...].astype(BF16)
            wd_b[...] = wd_f[...].astype(BF16)
            nxt = b + (tot_ref[e] + ROW_BLOCK - 1) // ROW_BLOCK

            @pl.when(nxt < nused)
            def _():
                for c in weight_copies(be_ref[jnp.minimum(nxt, pl.num_programs(0) - 1)]):
                    c.start()

        x = _unpack_rows(xbuf.at[slot])
        gu = _dot(x, wgu_b[...]) + bgu_ref[0]
        g = jnp.minimum(gu[:, :D_FF], SWIGLU_LIMIT)
        u = jnp.clip(gu[:, D_FF:], -SWIGLU_LIMIT, SWIGLU_LIMIT)
        act = (u + 1.0) * (g * jax.nn.sigmoid(SWIGLU_ALPHA * g))
        y = _dot(act.astype(BF16), wd_b[...]) + bdn_ref[0]
        _pack_rows(y.astype(BF16).astype(F32), ys_ref)

    @pl.when(jnp.logical_not(active))
    def _():
        ys_ref[...] = jnp.zeros(ys_ref.shape, U32)


def _experts(plan, xsp, xss, w_gu, b_gu, w_down, b_down, *, n_tiles, n_prompt_tiles, rows):
    nblk = rows // ROW_BLOCK

    def bmap(b, be, *_):
        return (be[b], 0, 0)

    grid_spec = pltpu.PrefetchScalarGridSpec(
        num_scalar_prefetch=7,
        grid=(nblk,),
        in_specs=[pl.BlockSpec(memory_space=pl.ANY),
                  pl.BlockSpec(memory_space=pl.ANY),
                  pl.BlockSpec(memory_space=pl.ANY),
                  pl.BlockSpec((1, 1, 2 * D_FF), bmap),
                  pl.BlockSpec(memory_space=pl.ANY),
                  pl.BlockSpec((1, 1, D_MODEL), bmap)],
        out_specs=pl.BlockSpec((ROW_BLOCK, PACK_S, LANES), lambda b, *_: (b, 0, 0)),
        scratch_shapes=[pltpu.VMEM((2, ROW_BLOCK, PACK_S, LANES), U32),
                        pltpu.SemaphoreType.DMA((2,)),
                        pltpu.VMEM((D_MODEL, 2 * D_FF), F32), pltpu.VMEM((D_FF, D_MODEL), F32),
                        pltpu.SemaphoreType.DMA((2,)),
                        pltpu.VMEM((D_MODEL, 2 * D_FF), BF16), pltpu.VMEM((D_FF, D_MODEL), BF16),
                        pltpu.SMEM((1,), I32)],
    )
    return pl.pallas_call(
        functools.partial(_expert_kernel, n_tiles=n_tiles, n_prompt_tiles=n_prompt_tiles),
        grid_spec=grid_spec,
        out_shape=jax.ShapeDtypeStruct((rows, PACK_S, LANES), U32),
        compiler_params=pltpu.CompilerParams(
            dimension_semantics=("arbitrary",), vmem_limit_bytes=VMEM_LIMIT),
        name="experts",
    )(plan['block_e'], plan['block_q'], plan['n_used'], plan['src_t'], plan['off_t'], plan['n_t'],
      plan['tot'], xsp, xss, w_gu, b_gu, w_down, b_down)


def _combine_kernel(ysrc_ref, sloc_ref, n_ref, col_ref, h_ref, ys_hbm, o_ref, buf, sems,
                    *, tm, tile0):
    j = pl.program_id(0)
    nt = pl.num_programs(0)
    rt = TOP_K * tm

    def issue(t, slot):
        base = (tile0 + t) * N_EXPERTS

        def body(e, c):
            n = n_ref[base + e]

            @pl.when(n > 0)
            def _():
                pltpu.make_async_copy(ys_hbm.at[pl.ds(ysrc_ref[base + e], n)],
                                      buf.at[slot, pl.ds(sloc_ref[base + e], n)],
                                      sems.at[slot]).start()
            return c

        lax.fori_loop(0, N_EXPERTS, body, 0)

    @pl.when(j == 0)
    def _():
        issue(0, 0)

    @pl.when(j + 1 < nt)
    def _():
        issue(j + 1, (j + 1) % 2)

    slot = j % 2
    pltpu.make_async_copy(ys_hbm.at[pl.ds(0, rt)], buf.at[slot], sems.at[slot]).wait()
    y = _unpack_rows(buf.at[slot])
    col = col_ref[...]
    rio = lax.broadcasted_iota(I32, (tm, rt), 1)
    gmat = jnp.zeros((tm, rt), F32)
    for k in range(TOP_K):
        gmat = gmat + jnp.where(rio == col[:, k:k + 1].astype(I32),
                                col[:, TOP_K + k:TOP_K + k + 1], 0.0)
    o_ref[...] = h_ref[...] + _dot(gmat.astype(BF16), y)


def _combine(plan, col, h2d, ys, *, tm, tile0):
    t = h2d.shape[0]
    rt = TOP_K * tm
    grid_spec = pltpu.PrefetchScalarGridSpec(
        num_scalar_prefetch=3,
        grid=(t // tm,),
        in_specs=[pl.BlockSpec((tm, LANES), lambda i, *_: (i, 0)),
                  pl.BlockSpec((tm, D_MODEL), lambda i, *_: (i, 0)),
                  pl.BlockSpec(memory_space=pl.ANY)],
        out_specs=pl.BlockSpec((tm, D_MODEL), lambda i, *_: (i, 0)),
        scratch_shapes=[pltpu.VMEM((2, rt, PACK_S, LANES), U32), pltpu.SemaphoreType.DMA((2,))],
    )
    return pl.pallas_call(
        functools.partial(_combine_kernel, tm=tm, tile0=tile0),
        grid_spec=grid_spec,
        out_shape=jax.ShapeDtypeStruct((t, D_MODEL), F32),
        compiler_params=pltpu.CompilerParams(
            dimension_semantics=("arbitrary",), vmem_limit_bytes=VMEM_LIMIT),
        name="combine",
    )(plan['ysrc'], plan['sloc'], plan['n'], col, h2d, ys)


def kernel(x_prompt, x_sample, cache_k, cache_v, state_conv, norm1_g, w_in, q_norm_g, k_norm_g,
           sinks, conv_w, w_out, norm2_g, w_router, b_router, w_gu, b_gu, w_down, b_down):
    l = 0
    nbp, sp, _ = x_prompt.shape
    nbs, ss, _ = x_sample.shape
    tp, ts = nbp * sp, nbs * ss
    tmp, tms = TOKEN_TILE, ts
    ntp, nts = tp // tmp, ts // tms
    n_tiles = ntp + nts
    n_rows = (tp + ts) * TOP_K
    grp = jnp.arange(ATT_DIM) // HEAD_DIM
    rep_col = jnp.arange(N_KV_HEADS * GROUP_W)
    w = {
        'g1': norm1_g[l][None, :],
        'w_in': w_in[l].astype(BF16),
        'qg': jnp.tile(q_norm_g[l], N_HEADS)[None, :],
        'kg': jnp.tile(k_norm_g[l], N_KV_HEADS)[None, :],
        'bd': (grp[:, None] == grp[None, :]).astype(BF16),
        'rep': (jnp.arange(KV_DIM)[:, None]
                == (rep_col // GROUP_W * HEAD_DIM + rep_col % HEAD_DIM)[None, :]).astype(BF16),
        'sinks': sinks[l][None, :],
        'conv_w': conv_w[l],
        'w_out': w_out[l].astype(BF16),
        'g2': norm2_g[l][None, :],
        'w_router_t': w_router[l].T,
        'b_router_t': b_router[l][:, None],
    }
    hp, xsp, colp, tcp, knp, vnp, unp = _mixer(x_prompt, None, w, tm=tmp, nsub=STREAMS_PER_STEP,
                                               st=tmp)
    cache = (cache_k[l].reshape(nbs, WINDOW, KV_DIM), cache_v[l].reshape(nbs, WINDOW, KV_DIM),
             state_conv[l])
    hs, xss, cols, tcs, kns, vns, uns = _mixer(x_sample, cache, w, tm=ss, nsub=nbs, st=tms)

    n = jnp.concatenate([tcp.reshape(ntp, N_EXPERTS), tcs.reshape(nts, N_EXPERTS)], axis=0)
    tile_base = jnp.concatenate([jnp.arange(ntp, dtype=I32) * (TOP_K * tmp),
                                 jnp.arange(nts, dtype=I32) * (TOP_K * tms)])
    sloc = jnp.cumsum(n, axis=1) - n
    off = jnp.cumsum(n, axis=0) - n
    tot = jnp.sum(n, axis=0)
    padded = (tot + ROW_BLOCK - 1) // ROW_BLOCK * ROW_BLOCK
    ends = jnp.cumsum(padded)
    base = ends - padded
    rows = (-(-n_rows // ROW_BLOCK) + N_EXPERTS) * ROW_BLOCK
    nblk = rows // ROW_BLOCK
    blk0 = jnp.arange(nblk, dtype=I32) * ROW_BLOCK
    block_e = jnp.minimum(jnp.sum((ends[None, :] <= blk0[:, None]).astype(I32), axis=1),
                          N_EXPERTS - 1)
    eq = (block_e[:, None] == jnp.arange(N_EXPERTS, dtype=I32)[None, :]).astype(I32)
    block_q = (blk0 - jnp.sum(eq * base[None, :], axis=1)) // ROW_BLOCK
    plan = {
        'block_e': block_e.astype(I32),
        'block_q': block_q.astype(I32),
        'n_used': (ends[-1:] // ROW_BLOCK).astype(I32),
        'src_t': (tile_base[:, None] + sloc).T.reshape(-1).astype(I32),
        'off_t': off.T.reshape(-1).astype(I32),
        'n_t': n.T.reshape(-1).astype(I32),
        'tot': tot.astype(I32),
        'ysrc': (base[None, :] + off).reshape(-1).astype(I32),
        'sloc': sloc.reshape(-1).astype(I32),
        'n': n.reshape(-1).astype(I32),
    }

    ys = _experts(plan, xsp.reshape(-1, PACK_S, LANES), xss.reshape(-1, PACK_S, LANES),
                  w_gu[l], b_gu[l][:, None, :], w_down[l], b_down[l][:, None, :],
                  n_tiles=n_tiles, n_prompt_tiles=ntp, rows=rows)
    yp = _combine(plan, colp.reshape(tp, LANES), hp.reshape(tp, D_MODEL), ys, tm=tmp, tile0=0)
    ysm = _combine(plan, cols.reshape(ts, LANES), hs.reshape(ts, D_MODEL), ys, tm=tms, tile0=ntp)

    kv5 = lambda a: a.reshape(1, a.shape[0], WINDOW, N_KV_HEADS, HEAD_DIM)
    return (yp.reshape(nbp, sp, D_MODEL), ysm.reshape(nbs, ss, D_MODEL),
            kv5(knp), kv5(vnp), unp[None], kv5(kns), kv5(vns), uns[None])
```

```python
import functools

import jax
import jax.numpy as jnp
from jax import lax
from jax.experimental import pallas as pl
from jax.experimental.pallas import tpu as pltpu

D_MODEL = 1024
CHUNK = 64
HEAD_DIM = 64
N_HEADS = 8
N_KV_HEADS = 2
Q_PER_KV = N_HEADS // N_KV_HEADS
ATT_DIM = N_HEADS * HEAD_DIM
KV_DIM = N_KV_HEADS * HEAD_DIM
GROUP_W = Q_PER_KV * HEAD_DIM
CONV_DIM = D_MODEL - ATT_DIM
CONV_W = 3
WINDOW = 128
IN_DIM = ATT_DIM + 2 * KV_DIM + 3 * CONV_DIM
N_EXPERTS = 32
TOP_K = 4
D_FF = D_MODEL
SWIGLU_LIMIT = 7.0
SWIGLU_ALPHA = 1.702
EPS = 1e-6
NEG = -1e30

LANES = 128
HALF = D_MODEL // 2
PACK_S = HALF // LANES
TOKEN_TILE = 256
STREAMS_PER_STEP = 2
ROW_BLOCK = 256
INLINE_SEGMENTS = 12
VMEM_LIMIT = 56 * 1024 * 1024

F32 = jnp.float32
BF16 = jnp.bfloat16
U32 = jnp.uint32
I32 = jnp.int32


def _rms_rows(x, g):
    return x * lax.rsqrt(jnp.mean(x * x, axis=-1, keepdims=True) + EPS) * g


def _split_bf16(x):
    hi = x.astype(BF16)
    lo = (x - hi.astype(F32)).astype(BF16)
    return hi, lo


def _dot(a, b):
    return jnp.dot(a, b, preferred_element_type=F32)


def _dot_nt(a, b):
    return lax.dot_general(a, b, (((1,), (1,)), ((), ())), preferred_element_type=F32)


def _head_rms(t, bd, g):
    hi, lo = _split_bf16(t * t)
    ssq = _dot(hi, bd) + _dot(lo, bd)
    return t * lax.rsqrt(ssq * (1.0 / HEAD_DIM) + EPS) * g


def _pack_rows(vals, out_ref):
    r = vals.shape[0]
    bits = pltpu.bitcast(vals, U32)
    word = (bits[:, :HALF] >> 16) | (bits[:, HALF:] & jnp.uint32(0xFFFF0000))
    flat = out_ref.reshape(r * PACK_S, LANES)
    for s in range(PACK_S):
        flat[pl.ds(s, r, stride=PACK_S), :] = word[:, s * LANES:(s + 1) * LANES]


def _unpack_rows(packed_ref):
    r = packed_ref.shape[0]
    flat = packed_ref.reshape(r * PACK_S, LANES)
    lo, hi = [], []
    for s in range(PACK_S):
        w = flat[pl.ds(s, r, stride=PACK_S), :]
        lo.append(pltpu.bitcast(w << 16, F32))
        hi.append(pltpu.bitcast(w & jnp.uint32(0xFFFF0000), F32))
    return jnp.concatenate(lo + hi, axis=1).astype(BF16)


def _attention(i, qn, kall, vall, rep, sink_ref, *, mask_history):
    tm = qn.shape[0]
    k4 = _dot(kall.astype(BF16), rep).astype(BF16)
    v4 = _dot(vall.astype(BF16), rep).astype(BF16)
    lane_grp = lax.broadcasted_iota(I32, (CHUNK, GROUP_W), 1) // HEAD_DIM
    row_grp = lax.broadcasted_iota(I32, (Q_PER_KV * CHUNK, 1), 0) // CHUNK
    span = WINDOW + CHUNK
    att_rows = []
    for c in range(tm // CHUNK):
        per_kv = []
        for hk in range(N_KV_HEADS):
            qc = qn[c * CHUNK:(c + 1) * CHUNK, hk * GROUP_W:(hk + 1) * GROUP_W]
            qs = jnp.concatenate([jnp.where(lane_grp == g, qc, 0.0) for g in range(Q_PER_KV)],
                                 axis=0).astype(BF16)
            kw = k4[c * CHUNK:c * CHUNK + span, hk * GROUP_W:(hk + 1) * GROUP_W]
            vw = v4[c * CHUNK:c * CHUNK + span, hk * GROUP_W:(hk + 1) * GROUP_W]
            s = _dot_nt(qs, kw) * (HEAD_DIM ** -0.5)
            if mask_history and c * CHUNK < WINDOW:
                kcol = lax.broadcasted_iota(I32, (1, span), 1)
                s = jnp.where(kcol >= jnp.where(i == 0, WINDOW - c * CHUNK, 0), s, NEG)
            sink = jnp.zeros((Q_PER_KV * CHUNK, 1), F32)
            for g in range(Q_PER_KV):
                sink = jnp.where(row_grp == g, sink_ref[0, hk * Q_PER_KV + g], sink)
            m = jnp.maximum(jnp.max(s, axis=-1, keepdims=True), sink)
            p = jnp.exp(s - m)
            den = jnp.sum(p, axis=-1, keepdims=True) + jnp.exp(sink - m)
            o = _dot(p.astype(BF16), vw) / den
            oc = jnp.zeros((CHUNK, GROUP_W), F32)
            for g in range(Q_PER_KV):
                oc = oc + jnp.where(lane_grp == g, o[g * CHUNK:(g + 1) * CHUNK, :], 0.0)
            per_kv.append(oc)
        att_rows.append(jnp.concatenate(per_kv, axis=1))
    return jnp.concatenate(att_rows, axis=0)


def _mixer_kernel(*refs, tm, nsub, st, has_cache):
    refs = list(refs)
    x_ref = refs.pop(0)
    if has_cache:
        ck_ref, cv_ref, st_ref = refs.pop(0), refs.pop(0), refs.pop(0)
    (g1_ref, win_ref, qg_ref, kg_ref, bd_ref, rep_ref, sink_ref, cw_ref, wout_ref, g2_ref,
     wrt_ref, brt_ref, upper_ref, lower_ref,
     h_ref, xs_ref, col_ref, tcnt_ref, knew_ref, vnew_ref, unew_ref, kprev, vprev, ubuf) = refs
    i = pl.program_id(1)
    ntok = nsub * tm

    @pl.when(i == 0)
    def _():
        ubuf[:, 0:8, :] = jnp.zeros((nsub, 8, CONV_DIM), F32)
        if has_cache:
            kprev[...] = ck_ref[...]
            vprev[...] = cv_ref[...]
            ubuf[:, 6:8, :] = st_ref[...]
        else:
            kprev[...] = jnp.zeros(kprev.shape, F32)
            vprev[...] = jnp.zeros(vprev.shape, F32)

    x = jnp.concatenate([x_ref[sub] for sub in range(nsub)], axis=0) if nsub > 1 else x_ref[0]
    xn = _rms_rows(x, g1_ref[...])
    proj = _dot(xn.astype(BF16), win_ref[...])
    o0 = ATT_DIM
    o1 = o0 + KV_DIM
    o2 = o1 + KV_DIM
    o3 = o2 + CONV_DIM
    o4 = o3 + CONV_DIM
    bd = bd_ref[...]
    qn = _head_rms(proj[:, :o0], bd, qg_ref[...])
    kn = _head_rms(proj[:, o0:o1], bd[:KV_DIM, :KV_DIM], kg_ref[...])
    v = proj[:, o1:o2]
    bg = proj[:, o2:o3]
    u = proj[:, o3:o4] * proj[:, o4:]

    rep = rep_ref[...]
    cw = cw_ref[...]
    att_parts, cv_parts = [], []
    for sub in range(nsub):
        r0, r1 = sub * tm, (sub + 1) * tm
        kall = jnp.concatenate([kprev[sub], kn[r0:r1]], axis=0)
        vall = jnp.concatenate([vprev[sub], v[r0:r1]], axis=0)
        att_parts.append(_attention(i, qn[r0:r1], kall, vall, rep, sink_ref,
                                    mask_history=not has_cache))
        kprev[sub] = kall[tm:, :]
        vprev[sub] = vall[tm:, :]
        knew_ref[sub] = kall[tm:, :]
        vnew_ref[sub] = vall[tm:, :]

        us = u[r0:r1]
        ubuf[sub, 8:8 + tm, :] = us
        cvs = ubuf[sub, 6:6 + tm, :] * cw[0:1, :] + ubuf[sub, 7:7 + tm, :] * cw[1:2, :]
        cv_parts.append(cvs + us * cw[2:3, :])
        ubuf[sub, 0:8, :] = ubuf[sub, tm:tm + 8, :]
        unew_ref[sub] = us[tm - (CONV_W - 1):, :]
    att = jnp.concatenate(att_parts, axis=0) if nsub > 1 else att_parts[0]
    cv = jnp.concatenate(cv_parts, axis=0) if nsub > 1 else cv_parts[0]

    mix = jnp.concatenate([att, bg * cv], axis=1).astype(BF16)
    h = x + _dot(mix, wout_ref[...])
    for sub in range(nsub):
        h_ref[sub] = h[sub * tm:(sub + 1) * tm]
    hn = _rms_rows(h, g2_ref[...])

    hn_hi, hn_lo = _split_bf16(hn)
    wr_hi, wr_lo = _split_bf16(wrt_ref[...])
    logits = (_dot_nt(wr_hi, hn_hi) + _dot_nt(wr_hi, hn_lo) + _dot_nt(wr_lo, hn_hi)) + brt_ref[...]
    eio = lax.broadcasted_iota(I32, (N_EXPERTS, ntok), 0)
    work = logits
    tops, sels = [], []
    for _ in range(TOP_K):
        mk = jnp.max(work, axis=0, keepdims=True)
        ik = jnp.min(jnp.where(work == mk, eio, N_EXPERTS), axis=0, keepdims=True)
        sel = eio == ik
        work = jnp.where(sel, -jnp.inf, work)
        tops.append(mk)
        sels.append(sel)
    es = [jnp.exp(t - tops[0]) for t in tops]
    esum = es[0] + es[1] + es[2] + es[3]

    onehot = jnp.zeros((N_EXPERTS, ntok), F32)
    for sel in sels:
        onehot = onehot + jnp.where(sel, 1.0, 0.0)
    onehot_b = onehot.astype(BF16)
    before = _dot(onehot_b, upper_ref[...])
    smaller = _dot(lower_ref[...], onehot_b)
    nst = ntok // st
    starts = [jnp.broadcast_to(jnp.sum(smaller[:, k * st:(k + 1) * st], axis=1, keepdims=True),
                               (N_EXPERTS, st)) for k in range(nst)]
    where_to = before + (jnp.concatenate(starts, axis=1) if nst > 1 else starts[0])
    pos = [jnp.sum(jnp.where(sel, where_to, 0.0), axis=0, keepdims=True) for sel in sels]
    rio = lax.broadcasted_iota(I32, (TOP_K * st, st), 0)
    for k in range(nst):
        c0, c1 = k * st, (k + 1) * st
        hit = rio == pos[0][:, c0:c1].astype(I32)
        for slot in range(1, TOP_K):
            hit = hit | (rio == pos[slot][:, c0:c1].astype(I32))
        perm = jnp.where(hit, 1.0, 0.0).astype(BF16)
        _pack_rows(_dot(perm, hn_hi[c0:c1]), xs_ref.at[k, 0])
        tcnt_ref[k, 0] = _dot_nt(jnp.ones((1, st), BF16), onehot_b[:, c0:c1]).astype(I32)

    rows8 = jnp.concatenate(pos + [e / esum for e in es], axis=0)
    sq = jnp.concatenate([rows8, jnp.zeros((LANES - 2 * TOP_K, ntok), F32)], axis=0)
    cols = jnp.transpose(sq)
    for sub in range(nsub):
        col_ref[sub] = cols[sub * tm:(sub + 1) * tm]


def _mixer(x, cache, w, *, tm, nsub, st):
    nb, s, _ = x.shape
    nt = s // tm
    nst = nsub * tm // st
    has_cache = cache is not None
    full = lambda shape: pl.BlockSpec(shape, lambda b, i: (0,) * len(shape))
    per_b = lambda shape: pl.BlockSpec((nsub,) + shape, lambda b, i: (b, 0, 0))
    tile = lambda last: pl.BlockSpec((nsub, tm, last), lambda b, i: (b, i, 0))

    in_specs = [tile(D_MODEL)]
    args = [x]
    if has_cache:
        in_specs += [per_b((WINDOW, KV_DIM)), per_b((WINDOW, KV_DIM)), per_b((CONV_W - 1, CONV_DIM))]
        args += list(cache)
    tio = jnp.arange(nsub * tm)
    eio = jnp.arange(N_EXPERTS)
    upper = ((tio[:, None] < tio[None, :]) & (tio[:, None] // st == tio[None, :] // st)).astype(BF16)
    lower = (eio[None, :] < eio[:, None]).astype(BF16)
    in_specs += [
        full((1, D_MODEL)), full((D_MODEL, IN_DIM)), full((1, ATT_DIM)),
        full((1, KV_DIM)), full((ATT_DIM, ATT_DIM)), full((KV_DIM, N_KV_HEADS * GROUP_W)),
        pl.BlockSpec(memory_space=pltpu.SMEM),
        full((CONV_W, CONV_DIM)), full((D_MODEL, D_MODEL)), full((1, D_MODEL)),
        full((N_EXPERTS, D_MODEL)), full((N_EXPERTS, 1)), full((nsub * tm, nsub * tm)),
        full((N_EXPERTS, N_EXPERTS)),
    ]
    args += [w['g1'], w['w_in'], w['qg'], w['kg'], w['bd'], w['rep'], w['sinks'], w['conv_w'],
             w['w_out'], w['g2'], w['w_router_t'], w['b_router_t'], upper, lower]
    rt = TOP_K * st
    nsort = nb // nsub * nst
    out_shape = [
        jax.ShapeDtypeStruct((nb, s, D_MODEL), F32),
        jax.ShapeDtypeStruct((nsort, nt, rt, PACK_S, LANES), U32),
        jax.ShapeDtypeStruct((nb, s, LANES), F32),
        jax.ShapeDtypeStruct((nsort, nt, 1, N_EXPERTS), I32),
        jax.ShapeDtypeStruct((nb, WINDOW, KV_DIM), F32),
        jax.ShapeDtypeStruct((nb, WINDOW, KV_DIM), F32),
        jax.ShapeDtypeStruct((nb, CONV_W - 1, CONV_DIM), F32),
    ]
    out_specs = [
        tile(D_MODEL),
        pl.BlockSpec((nst, 1, rt, PACK_S, LANES), lambda b, i: (b, i, 0, 0, 0)),
        tile(LANES),
        pl.BlockSpec((nst, 1, 1, N_EXPERTS), lambda b, i: (b, i, 0, 0)),
        per_b((WINDOW, KV_DIM)), per_b((WINDOW, KV_DIM)), per_b((CONV_W - 1, CONV_DIM)),
    ]
    scratch = [pltpu.VMEM((nsub, WINDOW, KV_DIM), F32), pltpu.VMEM((nsub, WINDOW, KV_DIM), F32),
               pltpu.VMEM((nsub, tm + 8, CONV_DIM), F32)]
    return pl.pallas_call(
        functools.partial(_mixer_kernel, tm=tm, nsub=nsub, st=st, has_cache=has_cache),
        grid=(nb // nsub, nt),
        in_specs=in_specs,
        out_specs=out_specs,
        out_shape=out_shape,
        scratch_shapes=scratch,
        compiler_params=pltpu.CompilerParams(
            dimension_semantics=("arbitrary", "arbitrary"), vmem_limit_bytes=VMEM_LIMIT),
        name="mixer_sample" if has_cache else "mixer_prompt",
    )(*args)


def _expert_kernel(be_ref, bq_ref, nused_ref, src_ref, off_ref, n_ref, tot_ref,
                   xsp_hbm, xss_hbm, wgu_hbm, bgu_ref, wd_hbm, bdn_ref, ys_ref,
                   xbuf, sems, wgu_f, wd_f, wsems, wgu_b, wd_b, ptr, *, n_tiles, n_prompt_tiles):
    b = pl.program_id(0)
    nused = nused_ref[0]

    def weight_copies(e):
        return (pltpu.make_async_copy(wgu_hbm.at[e], wgu_f, wsems.at[0]),
                pltpu.make_async_copy(wd_hbm.at[e], wd_f, wsems.at[1]))

    def issue_rows(blk, slot, enabled, between=lambda: None):
        blk = jnp.minimum(blk, pl.num_programs(0) - 1)
        e = be_ref[blk]
        q = bq_ref[blk]
        lo_row = q * ROW_BLOCK
        hi_row = lo_row + ROW_BLOCK
        base = e * n_tiles

        def segment(j, ok):
            jc = jnp.minimum(j, n_tiles - 1)
            o = off_ref[base + jc]
            inside = ok & (j < n_tiles) & (o < hi_row)
            lo = jnp.maximum(o, lo_row)
            ln = jnp.minimum(o + n_ref[base + jc], hi_row) - lo
            src = src_ref[base + jc] + (lo - o)
            dst = xbuf.at[slot, pl.ds(lo - lo_row, ln)]
            go = inside & (ln > 0)

            @pl.when(go & (jc < n_prompt_tiles))
            def _():
                pltpu.make_async_copy(xsp_hbm.at[pl.ds(src, ln)], dst, sems.at[slot]).start()

            @pl.when(go & (jc >= n_prompt_tiles))
            def _():
                pltpu.make_async_copy(xss_hbm.at[pl.ds(src, ln)], dst, sems.at[slot]).start()
            return inside

        j0 = jnp.where(q == 0, 0, ptr[0])

        def cond(j):
            return enabled & (j < n_tiles) & (off_ref[base + jnp.minimum(j, n_tiles - 1)] < hi_row)

        def body(j):
            segment(j, True)
            return j + 1

        jtail = lax.while_loop(cond, body, j0 + INLINE_SEGMENTS)
        result = between()
        count = jnp.int32(0)
        for k in range(INLINE_SEGMENTS):
            count = count + segment(j0 + k, enabled).astype(I32)
        jend = jnp.where(count == INLINE_SEGMENTS, jtail, j0 + count)
        ptr[0] = jnp.where(enabled, jnp.maximum(jend - 1, 0), ptr[0])
        return result

    @pl.when(b == 0)
    def _():
        xbuf[...] = jnp.zeros(xbuf.shape, U32)
        issue_rows(0, 0, True)
        for c in weight_copies(be_ref[0]):
            c.start()

    active = b < nused

    @pl.when(active)
    def _():
        slot = b % 2
        e = be_ref[b]
        rows = jnp.minimum(tot_ref[e] - bq_ref[b] * ROW_BLOCK, ROW_BLOCK)
        pltpu.make_async_copy(xsp_hbm.at[pl.ds(0, rows)], xbuf.at[slot, pl.ds(0, rows)],
                              sems.at[slot]).wait()

        @pl.when(bq_ref[b] == 0)
        def _():
            for c in weight_copies(e):
                c.wait()
            wgu_b[...] = wgu_f[...].astype(BF16)
            wd_b[...] = wd_f[...].astype(BF16)
            nxt = b + (tot_ref[e] + ROW_BLOCK - 1) // ROW_BLOCK

            @pl.when(nxt < nused)
            def _():
                for c in weight_copies(be_ref[jnp.minimum(nxt, pl.num_programs(0) - 1)]):
                    c.start()

        x = issue_rows(b + 1, (b + 1) % 2, b + 1 < nused,
                       between=lambda: _unpack_rows(xbuf.at[slot]))
        gu =_dot(x, wgu_b[...]) + bgu_ref[0]
        g = jnp.minimum(gu[:, :D_FF], SWIGLU_LIMIT)
        u = jnp.clip(gu[:, D_FF:], -SWIGLU_LIMIT, SWIGLU_LIMIT)
        act = (u + 1.0) * (g * jax.nn.sigmoid(SWIGLU_ALPHA * g))
        y = _dot(act.astype(BF16), wd_b[...]) + bdn_ref[0]
        _pack_rows(y.astype(BF16).astype(F32), ys_ref)

    @pl.when(jnp.logical_not(active))
    def _():
        ys_ref[...] = jnp.zeros(ys_ref.shape, U32)


def _experts(plan, xsp, xss, w_gu, b_gu, w_down, b_down, *, n_tiles, n_prompt_tiles, rows):
    nblk = rows // ROW_BLOCK

    def bmap(b, be, *_):
        return (be[b], 0, 0)

    grid_spec = pltpu.PrefetchScalarGridSpec(
        num_scalar_prefetch=7,
        grid=(nblk,),
        in_specs=[pl.BlockSpec(memory_space=pl.ANY),
                  pl.BlockSpec(memory_space=pl.ANY),
                  pl.BlockSpec(memory_space=pl.ANY),
                  pl.BlockSpec((1, 1, 2 * D_FF), bmap),
                  pl.BlockSpec(memory_space=pl.ANY),
                  pl.BlockSpec((1, 1, D_MODEL), bmap)],
        out_specs=pl.BlockSpec((ROW_BLOCK, PACK_S, LANES), lambda b, *_: (b, 0, 0)),
        scratch_shapes=[pltpu.VMEM((2, ROW_BLOCK, PACK_S, LANES), U32),
                        pltpu.SemaphoreType.DMA((2,)),
                        pltpu.VMEM((D_MODEL, 2 * D_FF), F32), pltpu.VMEM((D_FF, D_MODEL), F32),
                        pltpu.SemaphoreType.DMA((2,)),
                        pltpu.VMEM((D_MODEL, 2 * D_FF), BF16), pltpu.VMEM((D_FF, D_MODEL), BF16),
                        pltpu.SMEM((1,), I32)],
    )
    return pl.pallas_call(
        functools.partial(_expert_kernel, n_tiles=n_tiles, n_prompt_tiles=n_prompt_tiles),
        grid_spec=grid_spec,
        out_shape=jax.ShapeDtypeStruct((rows, PACK_S, LANES), U32),
        compiler_params=pltpu.CompilerParams(
            dimension_semantics=("arbitrary",), vmem_limit_bytes=VMEM_LIMIT),
        name="experts",
    )(plan['block_e'], plan['block_q'], plan['n_used'], plan['src_t'], plan['off_t'], plan['n_t'],
      plan['tot'], xsp, xss, w_gu, b_gu, w_down, b_down)


def _combine_kernel(ysrc_ref, sloc_ref, n_ref, col_ref, h_ref, ys_hbm, o_ref, buf, sems,
                    *, tm, tile0):
    j = pl.program_id(0)
    nt = pl.num_programs(0)
    rt = TOP_K * tm

    def issue(t, slot, enabled):
        base = (tile0 + jnp.minimum(t, nt - 1)) * N_EXPERTS
        for e in range(N_EXPERTS):
            n = n_ref[base + e]

            @pl.when(enabled & (n > 0))
            def _():
                pltpu.make_async_copy(ys_hbm.at[pl.ds(ysrc_ref[base + e], n)],
                                      buf.at[slot, pl.ds(sloc_ref[base + e], n)],
                                      sems.at[slot]).start()

    @pl.when(j == 0)
    def _():
        issue(0, 0, True)

    slot = j % 2
    pltpu.make_async_copy(ys_hbm.at[pl.ds(0, rt)], buf.at[slot], sems.at[slot]).wait()
    y = _unpack_rows(buf.at[slot])
    issue(j + 1, (j + 1) % 2, j + 1 < nt)
    col = col_ref[...]
    rio = lax.broadcasted_iota(I32, (tm, rt), 1)
    gmat = jnp.zeros((tm, rt), F32)
    for k in range(TOP_K):
        gmat = gmat + jnp.where(rio == col[:, k:k + 1].astype(I32),
                                col[:, TOP_K + k:TOP_K + k + 1], 0.0)
    o_ref[...] = h_ref[...] + _dot(gmat.astype(BF16), y)


def _combine(plan, col, h2d, ys, *, tm, tile0):
    t = h2d.shape[0]
    rt = TOP_K * tm
    grid_spec = pltpu.PrefetchScalarGridSpec(
        num_scalar_prefetch=3,
        grid=(t // tm,),
        in_specs=[pl.BlockSpec((tm, LANES), lambda i, *_: (i, 0)),
                  pl.BlockSpec((tm, D_MODEL), lambda i, *_: (i, 0)),
                  pl.BlockSpec(memory_space=pl.ANY)],
        out_specs=pl.BlockSpec((tm, D_MODEL), lambda i, *_: (i, 0)),
        scratch_shapes=[pltpu.VMEM((2, rt, PACK_S, LANES), U32), pltpu.SemaphoreType.DMA((2,))],
    )
    return pl.pallas_call(
        functools.partial(_combine_kernel, tm=tm, tile0=tile0),
        grid_spec=grid_spec,
        out_shape=jax.ShapeDtypeStruct((t, D_MODEL), F32),
        compiler_params=pltpu.CompilerParams(
            dimension_semantics=("arbitrary",), vmem_limit_bytes=VMEM_LIMIT),
        name="combine",
    )(plan['ysrc'], plan['sloc'], plan['n'], col, h2d, ys)


def kernel(x_prompt, x_sample, cache_k, cache_v, state_conv, norm1_g, w_in, q_norm_g, k_norm_g,
           sinks, conv_w, w_out, norm2_g, w_router, b_router, w_gu, b_gu, w_down, b_down):
    l = 0
    nbp, sp, _ = x_prompt.shape
    nbs, ss, _ = x_sample.shape
    tp, ts = nbp * sp, nbs * ss
    tmp, tms = TOKEN_TILE, ts
    ntp, nts = tp // tmp, ts // tms
    n_tiles = ntp + nts
    n_rows = (tp + ts) * TOP_K
    grp = jnp.arange(ATT_DIM) // HEAD_DIM
    rep_col = jnp.arange(N_KV_HEADS * GROUP_W)
    w = {
        'g1': norm1_g[l][None, :],
        'w_in': w_in[l].astype(BF16),
        'qg': jnp.tile(q_norm_g[l], N_HEADS)[None, :],
        'kg': jnp.tile(k_norm_g[l], N_KV_HEADS)[None, :],
        'bd': (grp[:, None] == grp[None, :]).astype(BF16),
        'rep': (jnp.arange(KV_DIM)[:, None]
                == (rep_col // GROUP_W * HEAD_DIM + rep_col % HEAD_DIM)[None, :]).astype(BF16),
        'sinks': sinks[l][None, :],
        'conv_w': conv_w[l],
        'w_out': w_out[l].astype(BF16),
        'g2': norm2_g[l][None, :],
        'w_router_t': w_router[l].T,
        'b_router_t': b_router[l][:, None],
    }
    hp, xsp, colp, tcp, knp, vnp, unp = _mixer(x_prompt, None, w, tm=tmp, nsub=STREAMS_PER_STEP,
                                               st=tmp)
    cache = (cache_k[l].reshape(nbs, WINDOW, KV_DIM), cache_v[l].reshape(nbs, WINDOW, KV_DIM),
             state_conv[l])
    hs, xss, cols, tcs, kns, vns, uns = _mixer(x_sample, cache, w, tm=ss, nsub=nbs, st=tms)

    n = jnp.concatenate([tcp.reshape(ntp, N_EXPERTS), tcs.reshape(nts, N_EXPERTS)], axis=0)
    tile_base = jnp.concatenate([jnp.arange(ntp, dtype=I32) * (TOP_K * tmp),
                                 jnp.arange(nts, dtype=I32) * (TOP_K * tms)])
    sloc = jnp.cumsum(n, axis=1) - n
    off = jnp.cumsum(n, axis=0) - n
    tot = jnp.sum(n, axis=0)
    padded = (tot + ROW_BLOCK - 1) // ROW_BLOCK * ROW_BLOCK
    ends = jnp.cumsum(padded)
    base = ends - padded
    rows = (-(-n_rows // ROW_BLOCK) + N_EXPERTS) * ROW_BLOCK
    nblk = rows // ROW_BLOCK
    blk0 = jnp.arange(nblk, dtype=I32) * ROW_BLOCK
    block_e = jnp.minimum(jnp.sum((ends[None, :] <= blk0[:, None]).astype(I32), axis=1),
                          N_EXPERTS - 1)
    eq = (block_e[:, None] == jnp.arange(N_EXPERTS, dtype=I32)[None, :]).astype(I32)
    block_q = (blk0 - jnp.sum(eq * base[None, :], axis=1)) // ROW_BLOCK
    plan = {
        'block_e': block_e.astype(I32),
        'block_q': block_q.astype(I32),
        'n_used': (ends[-1:] // ROW_BLOCK).astype(I32),
        'src_t': (tile_base[:, None] + sloc).T.reshape(-1).astype(I32),
        'off_t': off.T.reshape(-1).astype(I32),
        'n_t': n.T.reshape(-1).astype(I32),
        'tot': tot.astype(I32),
        'ysrc': (base[None, :] + off).reshape(-1).astype(I32),
        'sloc': sloc.reshape(-1).astype(I32),
        'n': n.reshape(-1).astype(I32),
    }

    ys = _experts(plan, xsp.reshape(-1, PACK_S, LANES), xss.reshape(-1, PACK_S, LANES),
                  w_gu[l], b_gu[l][:, None, :], w_down[l], b_down[l][:, None, :],
                  n_tiles=n_tiles, n_prompt_tiles=ntp, rows=rows)
    yp = _combine(plan, colp.reshape(tp, LANES), hp.reshape(tp, D_MODEL), ys, tm=tmp, tile0=0)
    ysm = _combine(plan, cols.reshape(ts, LANES), hs.reshape(ts, D_MODEL), ys, tm=tms, tile0=ntp)

    kv5 = lambda a: a.reshape(1, a.shape[0], WINDOW, N_KV_HEADS, HEAD_DIM)
    return (yp.reshape(nbp, sp, D_MODEL), ysm.reshape(nbs, ss, D_MODEL),
            kv5(knp), kv5(vnp), unp[None], kv5(kns), kv5(vns), uns[None])
```

```python
import functools

import jax
import jax.numpy as jnp
from jax import lax
from jax.experimental import pallas as pl
from jax.experimental.pallas import tpu as pltpu

D_MODEL = 1024
CHUNK = 64
HEAD_DIM = 64
N_HEADS = 8
N_KV_HEADS = 2
Q_PER_KV = N_HEADS // N_KV_HEADS
ATT_DIM = N_HEADS * HEAD_DIM
KV_DIM = N_KV_HEADS * HEAD_DIM
GROUP_W = Q_PER_KV * HEAD_DIM
CONV_DIM = D_MODEL - ATT_DIM
CONV_W = 3
WINDOW = 128
IN_DIM = ATT_DIM + 2 * KV_DIM + 3 * CONV_DIM
N_EXPERTS = 32
TOP_K = 4
D_FF = D_MODEL
SWIGLU_LIMIT = 7.0
SWIGLU_ALPHA = 1.702
EPS = 1e-6
NEG = -1e30

LANES = 128
HALF = D_MODEL // 2
PACK_S = HALF // LANES
TOKEN_TILE = 256
STREAMS_PER_STEP = 2
ROW_BLOCK = 256
INLINE_SEGMENTS = 12
LOOKAHEAD = 2
ROW_SLOTS = LOOKAHEAD + 1
VMEM_LIMIT = 56 * 1024 * 1024

F32 = jnp.float32
BF16 = jnp.bfloat16
U32 = jnp.uint32
I32 = jnp.int32


def _rms_rows(x, g):
    return x * lax.rsqrt(jnp.mean(x * x, axis=-1, keepdims=True) + EPS) * g


def _split_bf16(x):
    hi = x.astype(BF16)
    lo = (x - hi.astype(F32)).astype(BF16)
    return hi, lo


def _dot(a, b):
    return jnp.dot(a, b, preferred_element_type=F32)


def _dot_nt(a, b):
    return lax.dot_general(a, b, (((1,), (1,)), ((), ())), preferred_element_type=F32)


def _head_rms(t, bd, g):
    hi, lo = _split_bf16(t * t)
    ssq = _dot(hi, bd) + _dot(lo, bd)
    return t * lax.rsqrt(ssq * (1.0 / HEAD_DIM) + EPS) * g


def _pack_rows(vals, out_ref):
    r = vals.shape[0]
    bits = pltpu.bitcast(vals, U32)
    word = (bits[:, :HALF] >> 16) | (bits[:, HALF:] & jnp.uint32(0xFFFF0000))
    flat = out_ref.reshape(r * PACK_S, LANES)
    for s in range(PACK_S):
        flat[pl.ds(s, r, stride=PACK_S), :] = word[:, s * LANES:(s + 1) * LANES]


def _unpack_rows(packed_ref):
    r = packed_ref.shape[0]
    flat = packed_ref.reshape(r * PACK_S, LANES)
    lo, hi = [], []
    for s in range(PACK_S):
        w = flat[pl.ds(s, r, stride=PACK_S), :]
        lo.append(pltpu.bitcast(w << 16, F32))
        hi.append(pltpu.bitcast(w & jnp.uint32(0xFFFF0000), F32))
    return jnp.concatenate(lo + hi, axis=1).astype(BF16)


def _attention(i, qn, kall, vall, rep, sink_ref, *, mask_history):
    tm = qn.shape[0]
    k4 = _dot(kall.astype(BF16), rep).astype(BF16)
    v4 = _dot(vall.astype(BF16), rep).astype(BF16)
    lane_grp = lax.broadcasted_iota(I32, (CHUNK, GROUP_W), 1) // HEAD_DIM
    row_grp = lax.broadcasted_iota(I32, (Q_PER_KV * CHUNK, 1), 0) // CHUNK
    span = WINDOW + CHUNK
    att_rows = []
    for c in range(tm // CHUNK):
        per_kv = []
        for hk in range(N_KV_HEADS):
            qc = qn[c * CHUNK:(c + 1) * CHUNK, hk * GROUP_W:(hk + 1) * GROUP_W]
            qs = jnp.concatenate([jnp.where(lane_grp == g, qc, 0.0) for g in range(Q_PER_KV)],
                                 axis=0).astype(BF16)
            kw = k4[c * CHUNK:c * CHUNK + span, hk * GROUP_W:(hk + 1) * GROUP_W]
            vw = v4[c * CHUNK:c * CHUNK + span, hk * GROUP_W:(hk + 1) * GROUP_W]
            s = _dot_nt(qs, kw) * (HEAD_DIM ** -0.5)
            if mask_history and c * CHUNK < WINDOW:
                kcol = lax.broadcasted_iota(I32, (1, span), 1)
                s = jnp.where(kcol >= jnp.where(i == 0, WINDOW - c * CHUNK, 0), s, NEG)
            sink = jnp.zeros((Q_PER_KV * CHUNK, 1), F32)
            for g in range(Q_PER_KV):
                sink = jnp.where(row_grp == g, sink_ref[0, hk * Q_PER_KV + g], sink)
            m = jnp.maximum(jnp.max(s, axis=-1, keepdims=True), sink)
            p = jnp.exp(s - m)
            den = jnp.sum(p, axis=-1, keepdims=True) + jnp.exp(sink - m)
            o = _dot(p.astype(BF16), vw) / den
            oc = jnp.zeros((CHUNK, GROUP_W), F32)
            for g in range(Q_PER_KV):
                oc = oc + jnp.where(lane_grp == g, o[g * CHUNK:(g + 1) * CHUNK, :], 0.0)
            per_kv.append(oc)
        att_rows.append(jnp.concatenate(per_kv, axis=1))
    return jnp.concatenate(att_rows, axis=0)


def _mixer_kernel(*refs, tm, nsub, st, has_cache):
    refs = list(refs)
    x_ref = refs.pop(0)
    if has_cache:
        ck_ref, cv_ref, st_ref = refs.pop(0), refs.pop(0), refs.pop(0)
    (g1_ref, win_ref, qg_ref, kg_ref, bd_ref, rep_ref, sink_ref, cw_ref, wout_ref, g2_ref,
     wrt_ref, brt_ref, upper_ref, lower_ref,
     h_ref, xs_ref, col_ref, tcnt_ref, knew_ref, vnew_ref, unew_ref, kprev, vprev, ubuf) = refs
    i = pl.program_id(1)
    ntok = nsub * tm

    @pl.when(i == 0)
    def _():
        ubuf[:, 0:8, :] = jnp.zeros((nsub, 8, CONV_DIM), F32)
        if has_cache:
            kprev[...] = ck_ref[...]
            vprev[...] = cv_ref[...]
            ubuf[:, 6:8, :] = st_ref[...]
        else:
            kprev[...] = jnp.zeros(kprev.shape, F32)
            vprev[...] = jnp.zeros(vprev.shape, F32)

    x = jnp.concatenate([x_ref[sub] for sub in range(nsub)], axis=0) if nsub > 1 else x_ref[0]
    xn = _rms_rows(x, g1_ref[...])
    proj = _dot(xn.astype(BF16), win_ref[...])
    o0 = ATT_DIM
    o1 = o0 + KV_DIM
    o2 = o1 + KV_DIM
    o3 = o2 + CONV_DIM
    o4 = o3 + CONV_DIM
    bd = bd_ref[...]
    qn = _head_rms(proj[:, :o0], bd, qg_ref[...])
    kn = _head_rms(proj[:, o0:o1], bd[:KV_DIM, :KV_DIM], kg_ref[...])
    v = proj[:, o1:o2]
    bg = proj[:, o2:o3]
    u = proj[:, o3:o4] * proj[:, o4:]

    rep = rep_ref[...]
    cw = cw_ref[...]
    att_parts, cv_parts = [], []
    for sub in range(nsub):
        r0, r1 = sub * tm, (sub + 1) * tm
        kall = jnp.concatenate([kprev[sub], kn[r0:r1]], axis=0)
        vall = jnp.concatenate([vprev[sub], v[r0:r1]], axis=0)
        att_parts.append(_attention(i, qn[r0:r1], kall, vall, rep, sink_ref,
                                    mask_history=not has_cache))
        kprev[sub] = kall[tm:, :]
        vprev[sub] = vall[tm:, :]
        knew_ref[sub] = kall[tm:, :]
        vnew_ref[sub] = vall[tm:, :]

        us = u[r0:r1]
        ubuf[sub, 8:8 + tm, :] = us
        cvs = ubuf[sub, 6:6 + tm, :] * cw[0:1, :] + ubuf[sub, 7:7 + tm, :] * cw[1:2, :]
        cv_parts.append(cvs + us * cw[2:3, :])
        ubuf[sub, 0:8, :] = ubuf[sub, tm:tm + 8, :]
        unew_ref[sub] = us[tm - (CONV_W - 1):, :]
    att = jnp.concatenate(att_parts, axis=0) if nsub > 1 else att_parts[0]
    cv = jnp.concatenate(cv_parts, axis=0) if nsub > 1 else cv_parts[0]

    mix = jnp.concatenate([att, bg * cv], axis=1).astype(BF16)
    h = x + _dot(mix, wout_ref[...])
    for sub in range(nsub):
        h_ref[sub] = h[sub * tm:(sub + 1) * tm]
    hn = _rms_rows(h, g2_ref[...])

    hn_hi, hn_lo = _split_bf16(hn)
    wr_hi, wr_lo = _split_bf16(wrt_ref[...])
    logits = (_dot_nt(wr_hi, hn_hi) + _dot_nt(wr_hi, hn_lo) + _dot_nt(wr_lo, hn_hi)) + brt_ref[...]
    eio = lax.broadcasted_iota(I32, (N_EXPERTS, ntok), 0)
    work = logits
    tops, sels = [], []
    for _ in range(TOP_K):
        mk = jnp.max(work, axis=0, keepdims=True)
        ik = jnp.min(jnp.where(work == mk, eio, N_EXPERTS), axis=0, keepdims=True)
        sel = eio == ik
        work = jnp.where(sel, -jnp.inf, work)
        tops.append(mk)
        sels.append(sel)
    es = [jnp.exp(t - tops[0]) for t in tops]
    esum = es[0] + es[1] + es[2] + es[3]

    onehot = jnp.zeros((N_EXPERTS, ntok), F32)
    for sel in sels:
        onehot = onehot + jnp.where(sel, 1.0, 0.0)
    onehot_b = onehot.astype(BF16)
    before = _dot(onehot_b, upper_ref[...])
    smaller = _dot(lower_ref[...], onehot_b)
    nst = ntok // st
    starts = [jnp.broadcast_to(jnp.sum(smaller[:, k * st:(k + 1) * st], axis=1, keepdims=True),
                               (N_EXPERTS, st)) for k in range(nst)]
    where_to = before + (jnp.concatenate(starts, axis=1) if nst > 1 else starts[0])
    pos = [jnp.sum(jnp.where(sel, where_to, 0.0), axis=0, keepdims=True) for sel in sels]
    rio = lax.broadcasted_iota(I32, (TOP_K * st, st), 0)
    for k in range(nst):
        c0, c1 = k * st, (k + 1) * st
        hit = rio == pos[0][:, c0:c1].astype(I32)
        for slot in range(1, TOP_K):
            hit = hit | (rio == pos[slot][:, c0:c1].astype(I32))
        perm = jnp.where(hit, 1.0, 0.0).astype(BF16)
        _pack_rows(_dot(perm, hn_hi[c0:c1]), xs_ref.at[k, 0])
        tcnt_ref[k, 0] = _dot_nt(jnp.ones((1, st), BF16), onehot_b[:, c0:c1]).astype(I32)

    rows8 = jnp.concatenate(pos + [e / esum for e in es], axis=0)
    sq = jnp.concatenate([rows8, jnp.zeros((LANES - 2 * TOP_K, ntok), F32)], axis=0)
    cols = jnp.transpose(sq)
    for sub in range(nsub):
        col_ref[sub] = cols[sub * tm:(sub + 1) * tm]


def _mixer(x, cache, w, *, tm, nsub, st):
    nb, s, _ = x.shape
    nt = s // tm
    nst = nsub * tm // st
    has_cache = cache is not None
    full = lambda shape: pl.BlockSpec(shape, lambda b, i: (0,) * len(shape))
    per_b = lambda shape: pl.BlockSpec((nsub,) + shape, lambda b, i: (b, 0, 0))
    tile = lambda last: pl.BlockSpec((nsub, tm, last), lambda b, i: (b, i, 0))

    in_specs = [tile(D_MODEL)]
    args = [x]
    if has_cache:
        in_specs += [per_b((WINDOW, KV_DIM)), per_b((WINDOW, KV_DIM)), per_b((CONV_W - 1, CONV_DIM))]
        args += list(cache)
    tio = jnp.arange(nsub * tm)
    eio = jnp.arange(N_EXPERTS)
    upper = ((tio[:, None] < tio[None, :]) & (tio[:, None] // st == tio[None, :] // st)).astype(BF16)
    lower = (eio[None, :] < eio[:, None]).astype(BF16)
    in_specs += [
        full((1, D_MODEL)), full((D_MODEL, IN_DIM)), full((1, ATT_DIM)),
        full((1, KV_DIM)), full((ATT_DIM, ATT_DIM)), full((KV_DIM, N_KV_HEADS * GROUP_W)),
        pl.BlockSpec(memory_space=pltpu.SMEM),
        full((CONV_W, CONV_DIM)), full((D_MODEL, D_MODEL)), full((1, D_MODEL)),
        full((N_EXPERTS, D_MODEL)), full((N_EXPERTS, 1)), full((nsub * tm, nsub * tm)),
        full((N_EXPERTS, N_EXPERTS)),
    ]
    args += [w['g1'], w['w_in'], w['qg'], w['kg'], w['bd'], w['rep'], w['sinks'], w['conv_w'],
             w['w_out'], w['g2'], w['w_router_t'], w['b_router_t'], upper, lower]
    rt = TOP_K * st
    nsort = nb // nsub * nst
    out_shape = [
        jax.ShapeDtypeStruct((nb, s, D_MODEL), F32),
        jax.ShapeDtypeStruct((nsort, nt, rt, PACK_S, LANES), U32),
        jax.ShapeDtypeStruct((nb, s, LANES), F32),
        jax.ShapeDtypeStruct((nsort, nt, 1, N_EXPERTS), I32),
        jax.ShapeDtypeStruct((nb, WINDOW, KV_DIM), F32),
        jax.ShapeDtypeStruct((nb, WINDOW, KV_DIM), F32),
        jax.ShapeDtypeStruct((nb, CONV_W - 1, CONV_DIM), F32),
    ]
    out_specs = [
        tile(D_MODEL),
        pl.BlockSpec((nst, 1, rt, PACK_S, LANES), lambda b, i: (b, i, 0, 0, 0)),
        tile(LANES),
        pl.BlockSpec((nst, 1, 1, N_EXPERTS), lambda b, i: (b, i, 0, 0)),
        per_b((WINDOW, KV_DIM)), per_b((WINDOW, KV_DIM)), per_b((CONV_W - 1, CONV_DIM)),
    ]
    scratch = [pltpu.VMEM((nsub, WINDOW, KV_DIM), F32), pltpu.VMEM((nsub, WINDOW, KV_DIM), F32),
               pltpu.VMEM((nsub, tm + 8, CONV_DIM), F32)]
    return pl.pallas_call(
        functools.partial(_mixer_kernel, tm=tm, nsub=nsub, st=st, has_cache=has_cache),
        grid=(nb // nsub, nt),
        in_specs=in_specs,
        out_specs=out_specs,
        out_shape=out_shape,
        scratch_shapes=scratch,
        compiler_params=pltpu.CompilerParams(
            dimension_semantics=("arbitrary", "arbitrary"), vmem_limit_bytes=VMEM_LIMIT),
        name="mixer_sample" if has_cache else "mixer_prompt",
    )(*args)


def _expert_kernel(be_ref, bq_ref, nused_ref, src_ref, off_ref, n_ref, tot_ref,
                   xsp_hbm, xss_hbm, wgu_hbm, bgu_ref, wd_hbm, bdn_ref, ys_ref,
                   xbuf, sems, wgu_f, wd_f, wsems, wgu_b, wd_b, ptr, *, n_tiles, n_prompt_tiles):
    b = pl.program_id(0)
    nused = nused_ref[0]

    def weight_copies(e):
        return (pltpu.make_async_copy(wgu_hbm.at[e], wgu_f, wsems.at[0]),
                pltpu.make_async_copy(wd_hbm.at[e], wd_f, wsems.at[1]))

    def issue_rows(blk, slot, enabled, between=lambda: None):
        blk = jnp.minimum(blk, pl.num_programs(0) - 1)
        e = be_ref[blk]
        q = bq_ref[blk]
        lo_row = q * ROW_BLOCK
        hi_row = lo_row + ROW_BLOCK
        base = e * n_tiles

        def segment(j, ok):
            jc = jnp.minimum(j, n_tiles - 1)
            o = off_ref[base + jc]
            inside = ok & (j < n_tiles) & (o < hi_row)
            lo = jnp.maximum(o, lo_row)
            ln = jnp.minimum(o + n_ref[base + jc], hi_row) - lo
            src = src_ref[base + jc] + (lo - o)
            dst = xbuf.at[slot, pl.ds(lo - lo_row, ln)]
            go = inside & (ln > 0)

            @pl.when(go & (jc < n_prompt_tiles))
            def _():
                pltpu.make_async_copy(xsp_hbm.at[pl.ds(src, ln)], dst, sems.at[slot]).start()

            @pl.when(go & (jc >= n_prompt_tiles))
            def _():
                pltpu.make_async_copy(xss_hbm.at[pl.ds(src, ln)], dst, sems.at[slot]).start()
            return inside

        j0 = jnp.where(q == 0, 0, ptr[0])

        def cond(j):
            return enabled & (j < n_tiles) & (off_ref[base + jnp.minimum(j, n_tiles - 1)] < hi_row)

        def body(j):
            segment(j, True)
            return j + 1

        jtail = lax.while_loop(cond, body, j0 + INLINE_SEGMENTS)
        result = between()
        count = jnp.int32(0)
        for k in range(INLINE_SEGMENTS):
            count = count + segment(j0 + k, enabled).astype(I32)
        jend = jnp.where(count == INLINE_SEGMENTS, jtail, j0 + count)
        ptr[0] = jnp.where(enabled, jnp.maximum(jend - 1, 0), ptr[0])
        return result

    @pl.when(b == 0)
    def _():
        xbuf[...] = jnp.zeros(xbuf.shape, U32)
        for ahead in range(LOOKAHEAD):
            issue_rows(ahead, ahead, ahead < nused)
        for c in weight_copies(be_ref[0]):
            c.start()

    active = b < nused

    @pl.when(active)
    def _():
        slot = b % ROW_SLOTS
        e = be_ref[b]
        rows = jnp.minimum(tot_ref[e] - bq_ref[b] * ROW_BLOCK, ROW_BLOCK)
        pltpu.make_async_copy(xsp_hbm.at[pl.ds(0, rows)], xbuf.at[slot, pl.ds(0, rows)],
                              sems.at[slot]).wait()

        @pl.when(bq_ref[b] == 0)
        def _():
            for c in weight_copies(e):
                c.wait()
            wgu_b[...] = wgu_f[...].astype(BF16)
            wd_b[...] = wd_f[...].astype(BF16)
            nxt = b + (tot_ref[e] + ROW_BLOCK - 1) // ROW_BLOCK

            @pl.when(nxt < nused)
            def _():
                for c in weight_copies(be_ref[jnp.minimum(nxt, pl.num_programs(0) - 1)]):
                    c.start()

        x = issue_rows(b + LOOKAHEAD, (b + LOOKAHEAD) % ROW_SLOTS, b + LOOKAHEAD < nused,
                       between=lambda: _unpack_rows(xbuf.at[slot]))
        gu =_dot(x, wgu_b[...]) + bgu_ref[0]
        g = jnp.minimum(gu[:, :D_FF], SWIGLU_LIMIT)
        u = jnp.clip(gu[:, D_FF:], -SWIGLU_LIMIT, SWIGLU_LIMIT)
        act = (u + 1.0) * (g * jax.nn.sigmoid(SWIGLU_ALPHA * g))
        y = _dot(act.astype(BF16), wd_b[...]) + bdn_ref[0]
        _pack_rows(y.astype(BF16).astype(F32), ys_ref)

    @pl.when(jnp.logical_not(active))
    def _():
        ys_ref[...] = jnp.zeros(ys_ref.shape, U32)


def _experts(plan, xsp, xss, w_gu, b_gu, w_down, b_down, *, n_tiles, n_prompt_tiles, rows):
    nblk = rows // ROW_BLOCK

    def bmap(b, be, *_):
        return (be[b], 0, 0)

    grid_spec = pltpu.PrefetchScalarGridSpec(
        num_scalar_prefetch=7,
        grid=(nblk,),
        in_specs=[pl.BlockSpec(memory_space=pl.ANY),
                  pl.BlockSpec(memory_space=pl.ANY),
                  pl.BlockSpec(memory_space=pl.ANY),
                  pl.BlockSpec((1, 1, 2 * D_FF), bmap),
                  pl.BlockSpec(memory_space=pl.ANY),
                  pl.BlockSpec((1, 1, D_MODEL), bmap)],
        out_specs=pl.BlockSpec((ROW_BLOCK, PACK_S, LANES), lambda b, *_: (b, 0, 0)),
        scratch_shapes=[pltpu.VMEM((ROW_SLOTS, ROW_BLOCK, PACK_S, LANES), U32),
                        pltpu.SemaphoreType.DMA((ROW_SLOTS,)),
                        pltpu.VMEM((D_MODEL, 2 * D_FF), F32), pltpu.VMEM((D_FF, D_MODEL), F32),
                        pltpu.SemaphoreType.DMA((2,)),
                        pltpu.VMEM((D_MODEL, 2 * D_FF), BF16), pltpu.VMEM((D_FF, D_MODEL), BF16),
                        pltpu.SMEM((1,), I32)],
    )
    return pl.pallas_call(
        functools.partial(_expert_kernel, n_tiles=n_tiles, n_prompt_tiles=n_prompt_tiles),
        grid_spec=grid_spec,
        out_shape=jax.ShapeDtypeStruct((rows, PACK_S, LANES), U32),
        compiler_params=pltpu.CompilerParams(
            dimension_semantics=("arbitrary",), vmem_limit_bytes=VMEM_LIMIT),
        name="experts",
    )(plan['block_e'], plan['block_q'], plan['n_used'], plan['src_t'], plan['off_t'], plan['n_t'],
      plan['tot'], xsp, xss, w_gu, b_gu, w_down, b_down)


def _combine_kernel(ysrc_ref, sloc_ref, n_ref, col_ref, h_ref, ys_hbm, o_ref, buf, sems,
                    *, tm, tile0):
    j = pl.program_id(0)
    nt = pl.num_programs(0)
    rt = TOP_K * tm

    def issue(t, slot, enabled):
        base = (tile0 + jnp.minimum(t, nt - 1)) * N_EXPERTS
        for e in range(N_EXPERTS):
            n = n_ref[base + e]

            @pl.when(enabled & (n > 0))
            def _():
                pltpu.make_async_copy(ys_hbm.at[pl.ds(ysrc_ref[base + e], n)],
                                      buf.at[slot, pl.ds(sloc_ref[base + e], n)],
                                      sems.at[slot]).start()

    @pl.when(j == 0)
    def _():
        for ahead in range(LOOKAHEAD):
            issue(ahead, ahead, ahead < nt)

    slot = j % ROW_SLOTS
    pltpu.make_async_copy(ys_hbm.at[pl.ds(0, rt)], buf.at[slot], sems.at[slot]).wait()
    y = _unpack_rows(buf.at[slot])
    issue(j + LOOKAHEAD, (j + LOOKAHEAD) % ROW_SLOTS, j + LOOKAHEAD < nt)
    col = col_ref[...]
    rio = lax.broadcasted_iota(I32, (tm, rt), 1)
    gmat = jnp.zeros((tm, rt), F32)
    for k in range(TOP_K):
        gmat = gmat + jnp.where(rio == col[:, k:k + 1].astype(I32),
                                col[:, TOP_K + k:TOP_K + k + 1], 0.0)
    o_ref[...] = h_ref[...] + _dot(gmat.astype(BF16), y)


def _combine(plan, col, h2d, ys, *, tm, tile0):
    t = h2d.shape[0]
    rt = TOP_K * tm
    grid_spec = pltpu.PrefetchScalarGridSpec(
        num_scalar_prefetch=3,
        grid=(t // tm,),
        in_specs=[pl.BlockSpec((tm, LANES), lambda i, *_: (i, 0)),
                  pl.BlockSpec((tm, D_MODEL), lambda i, *_: (i, 0)),
                  pl.BlockSpec(memory_space=pl.ANY)],
        out_specs=pl.BlockSpec((tm, D_MODEL), lambda i, *_: (i, 0)),
        scratch_shapes=[pltpu.VMEM((ROW_SLOTS, rt, PACK_S, LANES), U32),
                        pltpu.SemaphoreType.DMA((ROW_SLOTS,))],
    )
    return pl.pallas_call(
        functools.partial(_combine_kernel, tm=tm, tile0=tile0),
        grid_spec=grid_spec,
        out_shape=jax.ShapeDtypeStruct((t, D_MODEL), F32),
        compiler_params=pltpu.CompilerParams(
            dimension_semantics=("arbitrary",), vmem_limit_bytes=VMEM_LIMIT),
        name="combine",
    )(plan['ysrc'], plan['sloc'], plan['n'], col, h2d, ys)


def kernel(x_prompt, x_sample, cache_k, cache_v, state_conv, norm1_g, w_in, q_norm_g, k_norm_g,
           sinks, conv_w, w_out, norm2_g, w_router, b_router, w_gu, b_gu, w_down, b_down):
    l = 0
    nbp, sp, _ = x_prompt.shape
    nbs, ss, _ = x_sample.shape
    tp, ts = nbp * sp, nbs * ss
    tmp, tms = TOKEN_TILE, ts
    ntp, nts = tp // tmp, ts // tms
    n_tiles = ntp + nts
    n_rows = (tp + ts) * TOP_K
    grp = jnp.arange(ATT_DIM) // HEAD_DIM
    rep_col = jnp.arange(N_KV_HEADS * GROUP_W)
    w = {
        'g1': norm1_g[l][None, :],
        'w_in': w_in[l].astype(BF16),
        'qg': jnp.tile(q_norm_g[l], N_HEADS)[None, :],
        'kg': jnp.tile(k_norm_g[l], N_KV_HEADS)[None, :],
        'bd': (grp[:, None] == grp[None, :]).astype(BF16),
        'rep': (jnp.arange(KV_DIM)[:, None]
                == (rep_col // GROUP_W * HEAD_DIM + rep_col % HEAD_DIM)[None, :]).astype(BF16),
        'sinks': sinks[l][None, :],
        'conv_w': conv_w[l],
        'w_out': w_out[l].astype(BF16),
        'g2': norm2_g[l][None, :],
        'w_router_t': w_router[l].T,
        'b_router_t': b_router[l][:, None],
    }
    hp, xsp, colp, tcp, knp, vnp, unp = _mixer(x_prompt, None, w, tm=tmp, nsub=STREAMS_PER_STEP,
                                               st=tmp)
    cache = (cache_k[l].reshape(nbs, WINDOW, KV_DIM), cache_v[l].reshape(nbs, WINDOW, KV_DIM),
             state_conv[l])
    hs, xss, cols, tcs, kns, vns, uns = _mixer(x_sample, cache, w, tm=ss, nsub=nbs, st=tms)

    n = jnp.concatenate([tcp.reshape(ntp, N_EXPERTS), tcs.reshape(nts, N_EXPERTS)], axis=0)
    tile_base = jnp.concatenate([jnp.arange(ntp, dtype=I32) * (TOP_K * tmp),
                                 jnp.arange(nts, dtype=I32) * (TOP_K * tms)])
    sloc = jnp.cumsum(n, axis=1) - n
    off = jnp.cumsum(n, axis=0) - n
    tot = jnp.sum(n, axis=0)
    padded = (tot + ROW_BLOCK - 1) // ROW_BLOCK * ROW_BLOCK
    ends = jnp.cumsum(padded)
    base = ends - padded
    rows = (-(-n_rows // ROW_BLOCK) + N_EXPERTS) * ROW_BLOCK
    nblk = rows // ROW_BLOCK
    blk0 = jnp.arange(nblk, dtype=I32) * ROW_BLOCK
    block_e = jnp.minimum(jnp.sum((ends[None, :] <= blk0[:, None]).astype(I32), axis=1),
                          N_EXPERTS - 1)
    eq = (block_e[:, None] == jnp.arange(N_EXPERTS, dtype=I32)[None, :]).astype(I32)
    block_q = (blk0 - jnp.sum(eq * base[None, :], axis=1)) // ROW_BLOCK
    plan = {
        'block_e': block_e.astype(I32),
        'block_q': block_q.astype(I32),
        'n_used': (ends[-1:] // ROW_BLOCK).astype(I32),
        'src_t': (tile_base[:, None] + sloc).T.reshape(-1).astype(I32),
        'off_t': off.T.reshape(-1).astype(I32),
        'n_t': n.T.reshape(-1).astype(I32),
        'tot': tot.astype(I32),
        'ysrc': (base[None, :] + off).reshape(-1).astype(I32),
        'sloc': sloc.reshape(-1).astype(I32),
        'n': n.reshape(-1).astype(I32),
    }

    ys = _experts(plan, xsp.reshape(-1, PACK_S, LANES), xss.reshape(-1, PACK_S, LANES),
                  w_gu[l], b_gu[l][:, None, :], w_down[l], b_down[l][:, None, :],
                  n_tiles=n_tiles, n_prompt_tiles=ntp, rows=rows)
    yp = _combine(plan, colp.reshape(tp, LANES), hp.reshape(tp, D_MODEL), ys, tm=tmp, tile0=0)
    ysm = _combine(plan, cols.reshape(ts, LANES), hs.reshape(ts, D_MODEL), ys, tm=tms, tile0=ntp)

    kv5 = lambda a: a.reshape(1, a.shape[0], WINDOW, N_KV_HEADS, HEAD_DIM)
    return (yp.reshape(nbp, sp, D_MODEL), ysm.reshape(nbs, ss, D_MODEL),
            kv5(knp), kv5(vnp), unp[None], kv5(kns), kv5(vns), uns[None])
```

```python
import functools

import jax
import jax.numpy as jnp
from jax import lax
from jax.experimental import pallas as pl
from jax.experimental.pallas import tpu as pltpu

D_MODEL = 1024
CHUNK = 64
HEAD_DIM = 64
N_HEADS = 8
N_KV_HEADS = 2
Q_PER_KV = N_HEADS // N_KV_HEADS
ATT_DIM = N_HEADS * HEAD_DIM
KV_DIM = N_KV_HEADS * HEAD_DIM
GROUP_W = Q_PER_KV * HEAD_DIM
CONV_DIM = D_MODEL - ATT_DIM
CONV_W = 3
WINDOW = 128
IN_DIM = ATT_DIM + 2 * KV_DIM + 3 * CONV_DIM
N_EXPERTS = 32
TOP_K = 4
D_FF = D_MODEL
SWIGLU_LIMIT = 7.0
SWIGLU_ALPHA = 1.702
EPS = 1e-6
NEG = -1e30

LANES = 128
HALF = D_MODEL // 2
PACK_S = HALF // LANES
TOKEN_TILE = 256
STREAMS_PER_STEP = 2
ROW_BLOCK = 512
INLINE_SEGMENTS = 20
LOOKAHEAD = 2
ROW_SLOTS = LOOKAHEAD + 1
VMEM_LIMIT = 56 * 1024 * 1024

F32 = jnp.float32
BF16 = jnp.bfloat16
U32 = jnp.uint32
I32 = jnp.int32


def _rms_rows(x, g):
    return x * lax.rsqrt(jnp.mean(x * x, axis=-1, keepdims=True) + EPS) * g


def _split_bf16(x):
    hi = x.astype(BF16)
    lo = (x - hi.astype(F32)).astype(BF16)
    return hi, lo


def _dot(a, b):
    return jnp.dot(a, b, preferred_element_type=F32)


def _dot_nt(a, b):
    return lax.dot_general(a, b, (((1,), (1,)), ((), ())), preferred_element_type=F32)


def _head_rms(t, bd, g):
    hi, lo = _split_bf16(t * t)
    ssq = _dot(hi, bd) + _dot(lo, bd)
    return t * lax.rsqrt(ssq * (1.0 / HEAD_DIM) + EPS) * g


def _pack_rows(vals, out_ref):
    r = vals.shape[0]
    bits = pltpu.bitcast(vals, U32)
    word = (bits[:, :HALF] >> 16) | (bits[:, HALF:] & jnp.uint32(0xFFFF0000))
    flat = out_ref.reshape(r * PACK_S, LANES)
    for s in range(PACK_S):
        flat[pl.ds(s, r, stride=PACK_S), :] = word[:, s * LANES:(s + 1) * LANES]


def _unpack_rows(packed_ref):
    r = packed_ref.shape[0]
    flat = packed_ref.reshape(r * PACK_S, LANES)
    lo, hi = [], []
    for s in range(PACK_S):
        w = flat[pl.ds(s, r, stride=PACK_S), :]
        lo.append(pltpu.bitcast(w << 16, F32))
        hi.append(pltpu.bitcast(w & jnp.uint32(0xFFFF0000), F32))
    return jnp.concatenate(lo + hi, axis=1).astype(BF16)


def _attention(i, qn, kall, vall, rep, sink_ref, *, mask_history):
    tm = qn.shape[0]
    k4 = _dot(kall.astype(BF16), rep).astype(BF16)
    v4 = _dot(vall.astype(BF16), rep).astype(BF16)
    lane_grp = lax.broadcasted_iota(I32, (CHUNK, GROUP_W), 1) // HEAD_DIM
    row_grp = lax.broadcasted_iota(I32, (Q_PER_KV * CHUNK, 1), 0) // CHUNK
    span = WINDOW + CHUNK
    att_rows = []
    for c in range(tm // CHUNK):
        per_kv = []
        for hk in range(N_KV_HEADS):
            qc = qn[c * CHUNK:(c + 1) * CHUNK, hk * GROUP_W:(hk + 1) * GROUP_W]
            qs = jnp.concatenate([jnp.where(lane_grp == g, qc, 0.0) for g in range(Q_PER_KV)],
                                 axis=0).astype(BF16)
            kw = k4[c * CHUNK:c * CHUNK + span, hk * GROUP_W:(hk + 1) * GROUP_W]
            vw = v4[c * CHUNK:c * CHUNK + span, hk * GROUP_W:(hk + 1) * GROUP_W]
            s = _dot_nt(qs, kw) * (HEAD_DIM ** -0.5)
            if mask_history and c * CHUNK < WINDOW:
                kcol = lax.broadcasted_iota(I32, (1, span), 1)
                s = jnp.where(kcol >= jnp.where(i == 0, WINDOW - c * CHUNK, 0), s, NEG)
            sink = jnp.zeros((Q_PER_KV * CHUNK, 1), F32)
            for g in range(Q_PER_KV):
                sink = jnp.where(row_grp == g, sink_ref[0, hk * Q_PER_KV + g], sink)
            m = jnp.maximum(jnp.max(s, axis=-1, keepdims=True), sink)
            p = jnp.exp(s - m)
            den = jnp.sum(p, axis=-1, keepdims=True) + jnp.exp(sink - m)
            o = _dot(p.astype(BF16), vw) / den
            oc = jnp.zeros((CHUNK, GROUP_W), F32)
            for g in range(Q_PER_KV):
                oc = oc + jnp.where(lane_grp == g, o[g * CHUNK:(g + 1) * CHUNK, :], 0.0)
            per_kv.append(oc)
        att_rows.append(jnp.concatenate(per_kv, axis=1))
    return jnp.concatenate(att_rows, axis=0)


def _mixer_kernel(*refs, tm, nsub, st, has_cache):
    refs = list(refs)
    x_ref = refs.pop(0)
    if has_cache:
        ck_ref, cv_ref, st_ref = refs.pop(0), refs.pop(0), refs.pop(0)
    (g1_ref, win_ref, qg_ref, kg_ref, bd_ref, rep_ref, sink_ref, cw_ref, wout_ref, g2_ref,
     wrt_ref, brt_ref, upper_ref, lower_ref,
     h_ref, xs_ref, col_ref, tcnt_ref, knew_ref, vnew_ref, unew_ref, kprev, vprev, ubuf) = refs
    i = pl.program_id(1)
    ntok = nsub * tm

    @pl.when(i == 0)
    def _():
        ubuf[:, 0:8, :] = jnp.zeros((nsub, 8, CONV_DIM), F32)
        if has_cache:
            kprev[...] = ck_ref[...]
            vprev[...] = cv_ref[...]
            ubuf[:, 6:8, :] = st_ref[...]
        else:
            kprev[...] = jnp.zeros(kprev.shape, F32)
            vprev[...] = jnp.zeros(vprev.shape, F32)

    x = jnp.concatenate([x_ref[sub] for sub in range(nsub)], axis=0) if nsub > 1 else x_ref[0]
    xn = _rms_rows(x, g1_ref[...])
    proj = _dot(xn.astype(BF16), win_ref[...])
    o0 = ATT_DIM
    o1 = o0 + KV_DIM
    o2 = o1 + KV_DIM
    o3 = o2 + CONV_DIM
    o4 = o3 + CONV_DIM
    bd = bd_ref[...]
    qn = _head_rms(proj[:, :o0], bd, qg_ref[...])
    kn = _head_rms(proj[:, o0:o1], bd[:KV_DIM, :KV_DIM], kg_ref[...])
    v = proj[:, o1:o2]
    bg = proj[:, o2:o3]
    u = proj[:, o3:o4] * proj[:, o4:]

    rep = rep_ref[...]
    cw = cw_ref[...]
    att_parts, cv_parts = [], []
    for sub in range(nsub):
        r0, r1 = sub * tm, (sub + 1) * tm
        kall = jnp.concatenate([kprev[sub], kn[r0:r1]], axis=0)
        vall = jnp.concatenate([vprev[sub], v[r0:r1]], axis=0)
        att_parts.append(_attention(i, qn[r0:r1], kall, vall, rep, sink_ref,
                                    mask_history=not has_cache))
        kprev[sub] = kall[tm:, :]
        vprev[sub] = vall[tm:, :]
        knew_ref[sub] = kall[tm:, :]
        vnew_ref[sub] = vall[tm:, :]

        us = u[r0:r1]
        ubuf[sub, 8:8 + tm, :] = us
        cvs = ubuf[sub, 6:6 + tm, :] * cw[0:1, :] + ubuf[sub, 7:7 + tm, :] * cw[1:2, :]
        cv_parts.append(cvs + us * cw[2:3, :])
        ubuf[sub, 0:8, :] = ubuf[sub, tm:tm + 8, :]
        unew_ref[sub] = us[tm - (CONV_W - 1):, :]
    att = jnp.concatenate(att_parts, axis=0) if nsub > 1 else att_parts[0]
    cv = jnp.concatenate(cv_parts, axis=0) if nsub > 1 else cv_parts[0]

    mix = jnp.concatenate([att, bg * cv], axis=1).astype(BF16)
    h = x + _dot(mix, wout_ref[...])
    for sub in range(nsub):
        h_ref[sub] = h[sub * tm:(sub + 1) * tm]
    hn = _rms_rows(h, g2_ref[...])

    hn_hi, hn_lo = _split_bf16(hn)
    wr_hi, wr_lo = _split_bf16(wrt_ref[...])
    logits = (_dot_nt(wr_hi, hn_hi) + _dot_nt(wr_hi, hn_lo) + _dot_nt(wr_lo, hn_hi)) + brt_ref[...]
    eio = lax.broadcasted_iota(I32, (N_EXPERTS, ntok), 0)
    work = logits
    tops, sels = [], []
    for _ in range(TOP_K):
        mk = jnp.max(work, axis=0, keepdims=True)
        ik = jnp.min(jnp.where(work == mk, eio, N_EXPERTS), axis=0, keepdims=True)
        sel = eio == ik
        work = jnp.where(sel, -jnp.inf, work)
        tops.append(mk)
        sels.append(sel)
    es = [jnp.exp(t - tops[0]) for t in tops]
    esum = es[0] + es[1] + es[2] + es[3]

    onehot = jnp.zeros((N_EXPERTS, ntok), F32)
    for sel in sels:
        onehot = onehot + jnp.where(sel, 1.0, 0.0)
    onehot_b = onehot.astype(BF16)
    before = _dot(onehot_b, upper_ref[...])
    smaller = _dot(lower_ref[...], onehot_b)
    nst = ntok // st
    starts = [jnp.broadcast_to(jnp.sum(smaller[:, k * st:(k + 1) * st], axis=1, keepdims=True),
                               (N_EXPERTS, st)) for k in range(nst)]
    where_to = before + (jnp.concatenate(starts, axis=1) if nst > 1 else starts[0])
    pos = [jnp.sum(jnp.where(sel, where_to, 0.0), axis=0, keepdims=True) for sel in sels]
    rio = lax.broadcasted_iota(I32, (TOP_K * st, st), 0)
    for k in range(nst):
        c0, c1 = k * st, (k + 1) * st
        hit = rio == pos[0][:, c0:c1].astype(I32)
        for slot in range(1, TOP_K):
            hit = hit | (rio == pos[slot][:, c0:c1].astype(I32))
        perm = jnp.where(hit, 1.0, 0.0).astype(BF16)
        _pack_rows(_dot(perm, hn_hi[c0:c1]), xs_ref.at[k, 0])
        tcnt_ref[k, 0] = _dot_nt(jnp.ones((1, st), BF16), onehot_b[:, c0:c1]).astype(I32)

    rows8 = jnp.concatenate(pos + [e / esum for e in es], axis=0)
    sq = jnp.concatenate([rows8, jnp.zeros((LANES - 2 * TOP_K, ntok), F32)], axis=0)
    cols = jnp.transpose(sq)
    for sub in range(nsub):
        col_ref[sub] = cols[sub * tm:(sub + 1) * tm]


def _mixer(x, cache, w, *, tm, nsub, st):
    nb, s, _ = x.shape
    nt = s // tm
    nst = nsub * tm // st
    has_cache = cache is not None
    full = lambda shape: pl.BlockSpec(shape, lambda b, i: (0,) * len(shape))
    per_b = lambda shape: pl.BlockSpec((nsub,) + shape, lambda b, i: (b, 0, 0))
    tile = lambda last: pl.BlockSpec((nsub, tm, last), lambda b, i: (b, i, 0))

    in_specs = [tile(D_MODEL)]
    args = [x]
    if has_cache:
        in_specs += [per_b((WINDOW, KV_DIM)), per_b((WINDOW, KV_DIM)), per_b((CONV_W - 1, CONV_DIM))]
        args += list(cache)
    tio = jnp.arange(nsub * tm)
    eio = jnp.arange(N_EXPERTS)
    upper = ((tio[:, None] < tio[None, :]) & (tio[:, None] // st == tio[None, :] // st)).astype(BF16)
    lower = (eio[None, :] < eio[:, None]).astype(BF16)
    in_specs += [
        full((1, D_MODEL)), full((D_MODEL, IN_DIM)), full((1, ATT_DIM)),
        full((1, KV_DIM)), full((ATT_DIM, ATT_DIM)), full((KV_DIM, N_KV_HEADS * GROUP_W)),
        pl.BlockSpec(memory_space=pltpu.SMEM),
        full((CONV_W, CONV_DIM)), full((D_MODEL, D_MODEL)), full((1, D_MODEL)),
        full((N_EXPERTS, D_MODEL)), full((N_EXPERTS, 1)), full((nsub * tm, nsub * tm)),
        full((N_EXPERTS, N_EXPERTS)),
    ]
    args += [w['g1'], w['w_in'], w['qg'], w['kg'], w['bd'], w['rep'], w['sinks'], w['conv_w'],
             w['w_out'], w['g2'], w['w_router_t'], w['b_router_t'], upper, lower]
    rt = TOP_K * st
    nsort = nb // nsub * nst
    out_shape = [
        jax.ShapeDtypeStruct((nb, s, D_MODEL), F32),
        jax.ShapeDtypeStruct((nsort, nt, rt, PACK_S, LANES), U32),
        jax.ShapeDtypeStruct((nb, s, LANES), F32),
        jax.ShapeDtypeStruct((nsort, nt, 1, N_EXPERTS), I32),
        jax.ShapeDtypeStruct((nb, WINDOW, KV_DIM), F32),
        jax.ShapeDtypeStruct((nb, WINDOW, KV_DIM), F32),
        jax.ShapeDtypeStruct((nb, CONV_W - 1, CONV_DIM), F32),
    ]
    out_specs = [
        tile(D_MODEL),
        pl.BlockSpec((nst, 1, rt, PACK_S, LANES), lambda b, i: (b, i, 0, 0, 0)),
        tile(LANES),
        pl.BlockSpec((nst, 1, 1, N_EXPERTS), lambda b, i: (b, i, 0, 0)),
        per_b((WINDOW, KV_DIM)), per_b((WINDOW, KV_DIM)), per_b((CONV_W - 1, CONV_DIM)),
    ]
    scratch = [pltpu.VMEM((nsub, WINDOW, KV_DIM), F32), pltpu.VMEM((nsub, WINDOW, KV_DIM), F32),
               pltpu.VMEM((nsub, tm + 8, CONV_DIM), F32)]
    return pl.pallas_call(
        functools.partial(_mixer_kernel, tm=tm, nsub=nsub, st=st, has_cache=has_cache),
        grid=(nb // nsub, nt),
        in_specs=in_specs,
        out_specs=out_specs,
        out_shape=out_shape,
        scratch_shapes=scratch,
        compiler_params=pltpu.CompilerParams(
            dimension_semantics=("arbitrary", "arbitrary"), vmem_limit_bytes=VMEM_LIMIT),
        name="mixer_sample" if has_cache else "mixer_prompt",
    )(*args)


def _expert_kernel(be_ref, bq_ref, nused_ref, src_ref, off_ref, n_ref, tot_ref,
                   xsp_hbm, xss_hbm, wgu_hbm, bgu_ref, wd_hbm, bdn_ref, ys_ref,
                   xbuf, sems, wgu_f, wd_f, wsems, wgu_b, wd_b, ptr, *, n_tiles, n_prompt_tiles):
    b = pl.program_id(0)
    nused = nused_ref[0]

    def weight_copies(e):
        return (pltpu.make_async_copy(wgu_hbm.at[e], wgu_f, wsems.at[0]),
                pltpu.make_async_copy(wd_hbm.at[e], wd_f, wsems.at[1]))

    def issue_rows(blk, slot, enabled, between=lambda: None):
        blk = jnp.minimum(blk, pl.num_programs(0) - 1)
        e = be_ref[blk]
        q = bq_ref[blk]
        lo_row = q * ROW_BLOCK
        hi_row = lo_row + ROW_BLOCK
        base = e * n_tiles

        def segment(j, j_limit, src_hbm, ok):
            jc = jnp.minimum(j, j_limit - 1)
            o = off_ref[base + jc]
            inside = ok & (j < j_limit) & (o < hi_row)
            lo = jnp.maximum(o, lo_row)
            ln = jnp.minimum(o + n_ref[base + jc], hi_row) - lo

            @pl.when(inside & (ln > 0))
            def _():
                pltpu.make_async_copy(src_hbm.at[pl.ds(src_ref[base + jc] + (lo - o), ln)],
                                      xbuf.at[slot, pl.ds(lo - lo_row, ln)], sems.at[slot]).start()
            return inside

        j0 = jnp.where(q == 0, 0, ptr[0])

        def cond(j):
            return (enabled & (j < n_prompt_tiles)
                    & (off_ref[base + jnp.minimum(j, n_prompt_tiles - 1)] < hi_row))

        def body(j):
            segment(j, n_prompt_tiles, xsp_hbm, True)
            return j + 1

        jtail = lax.while_loop(cond, body, j0 + INLINE_SEGMENTS)
        result = between()
        count = jnp.int32(0)
        for k in range(INLINE_SEGMENTS):
            count = count + segment(j0 + k, n_prompt_tiles, xsp_hbm, enabled).astype(I32)
        jend = jnp.where(count == INLINE_SEGMENTS, jtail, j0 + count)
        ptr[0] = jnp.where(enabled, jnp.maximum(jend - 1, 0), ptr[0])
        for j in range(n_prompt_tiles, n_tiles):
            segment(j, n_tiles, xss_hbm, enabled)
        return result

    @pl.when(b == 0)
    def _():
        xbuf[...] = jnp.zeros(xbuf.shape, U32)
        for ahead in range(LOOKAHEAD):
            issue_rows(ahead, ahead, ahead < nused)
        for c in weight_copies(be_ref[0]):
            c.start()

    active = b < nused

    @pl.when(active)
    def _():
        slot = b % ROW_SLOTS
        e = be_ref[b]
        rows = jnp.minimum(tot_ref[e] - bq_ref[b] * ROW_BLOCK, ROW_BLOCK)
        pltpu.make_async_copy(xsp_hbm.at[pl.ds(0, rows)], xbuf.at[slot, pl.ds(0, rows)],
                              sems.at[slot]).wait()

        @pl.when(bq_ref[b] == 0)
        def _():
            for c in weight_copies(e):
                c.wait()
            wgu_b[...] = wgu_f[...].astype(BF16)
            wd_b[...] = wd_f[...].astype(BF16)
            nxt = b + (tot_ref[e] + ROW_BLOCK - 1) // ROW_BLOCK

            @pl.when(nxt < nused)
            def _():
                for c in weight_copies(be_ref[jnp.minimum(nxt, pl.num_programs(0) - 1)]):
                    c.start()

        def compute(nrows):
            x = issue_rows(b + LOOKAHEAD, (b + LOOKAHEAD) % ROW_SLOTS, b + LOOKAHEAD < nused,
                           between=lambda: _unpack_rows(xbuf.at[slot, pl.ds(0, nrows)]))
            gu = _dot(x, wgu_b[...]) + bgu_ref[0]
            g = jnp.minimum(gu[:, :D_FF], SWIGLU_LIMIT)
            u = jnp.clip(gu[:, D_FF:], -SWIGLU_LIMIT, SWIGLU_LIMIT)
            act = (u + 1.0) * (g * jax.nn.sigmoid(SWIGLU_ALPHA * g))
            y = _dot(act.astype(BF16), wd_b[...]) + bdn_ref[0]
            _pack_rows(y.astype(BF16).astype(F32), ys_ref.at[pl.ds(0, nrows)])
            if nrows < ROW_BLOCK:
                ys_ref[nrows:] = jnp.zeros((ROW_BLOCK - nrows, PACK_S, LANES), U32)

        @pl.when(rows > ROW_BLOCK // 2)
        def _():
            compute(ROW_BLOCK)

        @pl.when(rows <= ROW_BLOCK // 2)
        def _():
            compute(ROW_BLOCK // 2)

    @pl.when(jnp.logical_not(active))
    def _():
        ys_ref[...] = jnp.zeros(ys_ref.shape, U32)


def _experts(plan, xsp, xss, w_gu, b_gu, w_down, b_down, *, n_tiles, n_prompt_tiles, rows):
    nblk = rows // ROW_BLOCK

    def bmap(b, be, *_):
        return (be[b], 0, 0)

    grid_spec = pltpu.PrefetchScalarGridSpec(
        num_scalar_prefetch=7,
        grid=(nblk,),
        in_specs=[pl.BlockSpec(memory_space=pl.ANY),
                  pl.BlockSpec(memory_space=pl.ANY),
                  pl.BlockSpec(memory_space=pl.ANY),
                  pl.BlockSpec((1, 1, 2 * D_FF), bmap),
                  pl.BlockSpec(memory_space=pl.ANY),
                  pl.BlockSpec((1, 1, D_MODEL), bmap)],
        out_specs=pl.BlockSpec((ROW_BLOCK, PACK_S, LANES), lambda b, *_: (b, 0, 0)),
        scratch_shapes=[pltpu.VMEM((ROW_SLOTS, ROW_BLOCK, PACK_S, LANES), U32),
                        pltpu.SemaphoreType.DMA((ROW_SLOTS,)),
                        pltpu.VMEM((D_MODEL, 2 * D_FF), F32), pltpu.VMEM((D_FF, D_MODEL), F32),
                        pltpu.SemaphoreType.DMA((2,)),
                        pltpu.VMEM((D_MODEL, 2 * D_FF), BF16), pltpu.VMEM((D_FF, D_MODEL), BF16),
                        pltpu.SMEM((1,), I32)],
    )
    return pl.pallas_call(
        functools.partial(_expert_kernel, n_tiles=n_tiles, n_prompt_tiles=n_prompt_tiles),
        grid_spec=grid_spec,
        out_shape=jax.ShapeDtypeStruct((rows, PACK_S, LANES), U32),
        compiler_params=pltpu.CompilerParams(
            dimension_semantics=("arbitrary",), vmem_limit_bytes=VMEM_LIMIT),
        name="experts",
    )(plan['block_e'], plan['block_q'], plan['n_used'], plan['src_t'], plan['off_t'], plan['n_t'],
      plan['tot'], xsp, xss, w_gu, b_gu, w_down, b_down)


def _combine_kernel(ysrc_ref, sloc_ref, n_ref, col_ref, h_ref, ys_hbm, o_ref, buf, sems,
                    *, tm, tile0):
    j = pl.program_id(0)
    nt = pl.num_programs(0)
    rt = TOP_K * tm

    def issue(t, slot, enabled):
        base = (tile0 + jnp.minimum(t, nt - 1)) * N_EXPERTS
        for e in range(N_EXPERTS):
            n = n_ref[base + e]

            @pl.when(enabled & (n > 0))
            def _():
                pltpu.make_async_copy(ys_hbm.at[pl.ds(ysrc_ref[base + e], n)],
                                      buf.at[slot, pl.ds(sloc_ref[base + e], n)],
                                      sems.at[slot]).start()

    @pl.when(j == 0)
    def _():
        for ahead in range(LOOKAHEAD):
            issue(ahead, ahead, ahead < nt)

    slot = j % ROW_SLOTS
    pltpu.make_async_copy(ys_hbm.at[pl.ds(0, rt)], buf.at[slot], sems.at[slot]).wait()
    y = _unpack_rows(buf.at[slot])
    issue(j + LOOKAHEAD, (j + LOOKAHEAD) % ROW_SLOTS, j + LOOKAHEAD < nt)
    col = col_ref[...]
    rio = lax.broadcasted_iota(I32, (tm, rt), 1)
    gmat = jnp.zeros((tm, rt), F32)
    for k in range(TOP_K):
        gmat = gmat + jnp.where(rio == col[:, k:k + 1].astype(I32),
                                col[:, TOP_K + k:TOP_K + k + 1], 0.0)
    o_ref[...] = h_ref[...] + _dot(gmat.astype(BF16), y)


def _combine(plan, col, h2d, ys, *, tm, tile0):
    t = h2d.shape[0]
    rt = TOP_K * tm
    grid_spec = pltpu.PrefetchScalarGridSpec(
        num_scalar_prefetch=3,
        grid=(t // tm,),
        in_specs=[pl.BlockSpec((tm, LANES), lambda i, *_: (i, 0)),
                  pl.BlockSpec((tm, D_MODEL), lambda i, *_: (i, 0)),
                  pl.BlockSpec(memory_space=pl.ANY)],
        out_specs=pl.BlockSpec((tm, D_MODEL), lambda i, *_: (i, 0)),
        scratch_shapes=[pltpu.VMEM((ROW_SLOTS, rt, PACK_S, LANES), U32),
                        pltpu.SemaphoreType.DMA((ROW_SLOTS,))],
    )
    return pl.pallas_call(
        functools.partial(_combine_kernel, tm=tm, tile0=tile0),
        grid_spec=grid_spec,
        out_shape=jax.ShapeDtypeStruct((t, D_MODEL), F32),
        compiler_params=pltpu.CompilerParams(
            dimension_semantics=("arbitrary",), vmem_limit_bytes=VMEM_LIMIT),
        name="combine",
    )(plan['ysrc'], plan['sloc'], plan['n'], col, h2d, ys)


def kernel(x_prompt, x_sample, cache_k, cache_v, state_conv, norm1_g, w_in, q_norm_g, k_norm_g,
           sinks, conv_w, w_out, norm2_g, w_router, b_router, w_gu, b_gu, w_down, b_down):
    l = 0
    nbp, sp, _ = x_prompt.shape
    nbs, ss, _ = x_sample.shape
    tp, ts = nbp * sp, nbs * ss
    tmp, tms = TOKEN_TILE, ts
    ntp, nts = tp // tmp, ts // tms
    n_tiles = ntp + nts
    n_rows = (tp + ts) * TOP_K
    grp = jnp.arange(ATT_DIM) // HEAD_DIM
    rep_col = jnp.arange(N_KV_HEADS * GROUP_W)
    w = {
        'g1': norm1_g[l][None, :],
        'w_in': w_in[l].astype(BF16),
        'qg': jnp.tile(q_norm_g[l], N_HEADS)[None, :],
        'kg': jnp.tile(k_norm_g[l], N_KV_HEADS)[None, :],
        'bd': (grp[:, None] == grp[None, :]).astype(BF16),
        'rep': (jnp.arange(KV_DIM)[:, None]
                == (rep_col // GROUP_W * HEAD_DIM + rep_col % HEAD_DIM)[None, :]).astype(BF16),
        'sinks': sinks[l][None, :],
        'conv_w': conv_w[l],
        'w_out': w_out[l].astype(BF16),
        'g2': norm2_g[l][None, :],
        'w_router_t': w_router[l].T,
        'b_router_t': b_router[l][:, None],
    }
    hp, xsp, colp, tcp, knp, vnp, unp = _mixer(x_prompt, None, w, tm=tmp, nsub=STREAMS_PER_STEP,
                                               st=tmp)
    cache = (cache_k[l].reshape(nbs, WINDOW, KV_DIM), cache_v[l].reshape(nbs, WINDOW, KV_DIM),
             state_conv[l])
    hs, xss, cols, tcs, kns, vns, uns = _mixer(x_sample, cache, w, tm=ss, nsub=nbs, st=tms)

    n = jnp.concatenate([tcp.reshape(ntp, N_EXPERTS), tcs.reshape(nts, N_EXPERTS)], axis=0)
    tile_base = jnp.concatenate([jnp.arange(ntp, dtype=I32) * (TOP_K * tmp),
                                 jnp.arange(nts, dtype=I32) * (TOP_K * tms)])
    sloc = jnp.cumsum(n, axis=1) - n
    off = jnp.cumsum(n, axis=0) - n
    tot = jnp.sum(n, axis=0)
    padded = (tot + ROW_BLOCK - 1) // ROW_BLOCK * ROW_BLOCK
    ends = jnp.cumsum(padded)
    base = ends - padded
    rows = (-(-n_rows // ROW_BLOCK) + N_EXPERTS) * ROW_BLOCK
    nblk = rows // ROW_BLOCK
    blk0 = jnp.arange(nblk, dtype=I32) * ROW_BLOCK
    block_e = jnp.minimum(jnp.sum((ends[None, :] <= blk0[:, None]).astype(I32), axis=1),
                          N_EXPERTS - 1)
    eq = (block_e[:, None] == jnp.arange(N_EXPERTS, dtype=I32)[None, :]).astype(I32)
    block_q = (blk0 - jnp.sum(eq * base[None, :], axis=1)) // ROW_BLOCK
    plan = {
        'block_e': block_e.astype(I32),
        'block_q': block_q.astype(I32),
        'n_used': (ends[-1:] // ROW_BLOCK).astype(I32),
        'src_t': (tile_base[:, None] + sloc).T.reshape(-1).astype(I32),
        'off_t': off.T.reshape(-1).astype(I32),
        'n_t': n.T.reshape(-1).astype(I32),
        'tot': tot.astype(I32),
        'ysrc': (base[None, :] + off).reshape(-1).astype(I32),
        'sloc': sloc.reshape(-1).astype(I32),
        'n': n.reshape(-1).astype(I32),
    }

    ys = _experts(plan, xsp.reshape(-1, PACK_S, LANES), xss.reshape(-1, PACK_S, LANES),
                  w_gu[l], b_gu[l][:, None, :], w_down[l], b_down[l][:, None, :],
                  n_tiles=n_tiles, n_prompt_tiles=ntp, rows=rows)
    yp = _combine(plan, colp.reshape(tp, LANES), hp.reshape(tp, D_MODEL), ys, tm=tmp, tile0=0)
    ysm = _combine(plan, cols.reshape(ts, LANES), hs.reshape(ts, D_MODEL), ys, tm=tms, tile0=ntp)

    kv5 = lambda a: a.reshape(1, a.shape[0], WINDOW, N_KV_HEADS, HEAD_DIM)
    return (yp.reshape(nbp, sp, D_MODEL), ysm.reshape(nbs, ss, D_MODEL),
            kv5(knp), kv5(vnp), unp[None], kv5(kns), kv5(vns), uns[None])
```

```python
import functools

import jax
import jax.numpy as jnp
from jax import lax
from jax.experimental import pallas as pl
from jax.experimental.pallas import tpu as pltpu

D_MODEL = 1024
CHUNK = 64
HEAD_DIM = 64
N_HEADS = 8
N_KV_HEADS = 2
Q_PER_KV = N_HEADS // N_KV_HEADS
ATT_DIM = N_HEADS * HEAD_DIM
KV_DIM = N_KV_HEADS * HEAD_DIM
GROUP_W = Q_PER_KV * HEAD_DIM
CONV_DIM = D_MODEL - ATT_DIM
CONV_W = 3
WINDOW = 128
IN_DIM = ATT_DIM + 2 * KV_DIM + 3 * CONV_DIM
N_EXPERTS = 32
TOP_K = 4
D_FF = D_MODEL
SWIGLU_LIMIT = 7.0
SWIGLU_ALPHA = 1.702
EPS = 1e-6
NEG = -1e30

LANES = 128
HALF = D_MODEL // 2
PACK_S = HALF // LANES
TOKEN_TILE = 256
STREAMS_PER_STEP = 2
ROW_BLOCK = 512
INLINE_SEGMENTS = 20
LOOKAHEAD = 2
ROW_SLOTS = LOOKAHEAD + 1
VMEM_LIMIT = 56 * 1024 * 1024

F32 = jnp.float32
BF16 = jnp.bfloat16
U32 = jnp.uint32
I32 = jnp.int32


def _rms_rows(x, g):
    return x * lax.rsqrt(jnp.mean(x * x, axis=-1, keepdims=True) + EPS) * g


def _split_bf16(x):
    hi = x.astype(BF16)
    lo = (x - hi.astype(F32)).astype(BF16)
    return hi, lo


def _dot(a, b):
    return jnp.dot(a, b, preferred_element_type=F32)


def _dot_nt(a, b):
    return lax.dot_general(a, b, (((1,), (1,)), ((), ())), preferred_element_type=F32)


def _head_rms(t, bd, g):
    hi, lo = _split_bf16(t * t)
    ssq = _dot(hi, bd) + _dot(lo, bd)
    return t * lax.rsqrt(ssq * (1.0 / HEAD_DIM) + EPS) * g


def _pack_rows(vals, out_ref):
    r = vals.shape[0]
    bits = pltpu.bitcast(vals, U32)
    word = (bits[:, :HALF] >> 16) | (bits[:, HALF:] & jnp.uint32(0xFFFF0000))
    flat = out_ref.reshape(r * PACK_S, LANES)
    for s in range(PACK_S):
        flat[pl.ds(s, r, stride=PACK_S), :] = word[:, s * LANES:(s + 1) * LANES]


def _unpack_rows(packed_ref):
    r = packed_ref.shape[0]
    flat = packed_ref.reshape(r * PACK_S, LANES)
    lo, hi = [], []
    for s in range(PACK_S):
        w = flat[pl.ds(s, r, stride=PACK_S), :]
        lo.append(pltpu.bitcast(w << 16, F32))
        hi.append(pltpu.bitcast(w & jnp.uint32(0xFFFF0000), F32))
    return jnp.concatenate(lo + hi, axis=1).astype(BF16)


def _anchor_zero_row(staging_ref):
    z = staging_ref.at[ROW_SLOTS, pl.ds(0, 8 // PACK_S)].reshape(8, LANES)[...]
    return jnp.concatenate([pltpu.bitcast(z, F32)[0:1, :]] * (D_MODEL // LANES), axis=1)


def _attention(i, qn, kall, vall, rep, sink_ref, *, mask_history):
    tm = qn.shape[0]
    k4 = _dot(kall.astype(BF16), rep).astype(BF16)
    v4 = _dot(vall.astype(BF16), rep).astype(BF16)
    lane_grp = lax.broadcasted_iota(I32, (CHUNK, GROUP_W), 1) // HEAD_DIM
    row_grp = lax.broadcasted_iota(I32, (Q_PER_KV * CHUNK, 1), 0) // CHUNK
    span = WINDOW + CHUNK
    att_rows = []
    for c in range(tm // CHUNK):
        per_kv = []
        for hk in range(N_KV_HEADS):
            qc = qn[c * CHUNK:(c + 1) * CHUNK, hk * GROUP_W:(hk + 1) * GROUP_W]
            qs = jnp.concatenate([jnp.where(lane_grp == g, qc, 0.0) for g in range(Q_PER_KV)],
                                 axis=0).astype(BF16)
            kw = k4[c * CHUNK:c * CHUNK + span, hk * GROUP_W:(hk + 1) * GROUP_W]
            vw = v4[c * CHUNK:c * CHUNK + span, hk * GROUP_W:(hk + 1) * GROUP_W]
            s = _dot_nt(qs, kw) * (HEAD_DIM ** -0.5)
            if mask_history and c * CHUNK < WINDOW:
                kcol = lax.broadcasted_iota(I32, (1, span), 1)
                s = jnp.where(kcol >= jnp.where(i == 0, WINDOW - c * CHUNK, 0), s, NEG)
            sink = jnp.zeros((Q_PER_KV * CHUNK, 1), F32)
            for g in range(Q_PER_KV):
                sink = jnp.where(row_grp == g, sink_ref[0, hk * Q_PER_KV + g], sink)
            m = jnp.maximum(jnp.max(s, axis=-1, keepdims=True), sink)
            p = jnp.exp(s - m)
            den = jnp.sum(p, axis=-1, keepdims=True) + jnp.exp(sink - m)
            o = _dot(p.astype(BF16), vw) / den
            oc = jnp.zeros((CHUNK, GROUP_W), F32)
            for g in range(Q_PER_KV):
                oc = oc + jnp.where(lane_grp == g, o[g * CHUNK:(g + 1) * CHUNK, :], 0.0)
            per_kv.append(oc)
        att_rows.append(jnp.concatenate(per_kv, axis=1))
    return jnp.concatenate(att_rows, axis=0)


def _mixer_kernel(*refs, tm, nsub, st, has_cache):
    refs = list(refs)
    x_ref = refs.pop(0)
    if has_cache:
        ck_ref, cv_ref, st_ref = refs.pop(0), refs.pop(0), refs.pop(0)
    (g1_ref, win_ref, qg_ref, kg_ref, bd_ref, rep_ref, sink_ref, cw_ref, wout_ref, g2_ref,
     wrt_ref, brt_ref, upper_ref, lower_ref,
     h_ref, xs_ref, col_ref, tcnt_ref, knew_ref, vnew_ref, unew_ref, kprev, vprev, ubuf) = refs
    i = pl.program_id(1)
    ntok = nsub * tm

    @pl.when(i == 0)
    def _():
        ubuf[:, 0:8, :] = jnp.zeros((nsub, 8, CONV_DIM), F32)
        if has_cache:
            kprev[...] = ck_ref[...]
            vprev[...] = cv_ref[...]
            ubuf[:, 6:8, :] = st_ref[...]
        else:
            kprev[...] = jnp.zeros(kprev.shape, F32)
            vprev[...] = jnp.zeros(vprev.shape, F32)

    x = jnp.concatenate([x_ref[sub] for sub in range(nsub)], axis=0) if nsub > 1 else x_ref[0]
    xn = _rms_rows(x, g1_ref[...])
    proj = _dot(xn.astype(BF16), win_ref[...])
    o0 = ATT_DIM
    o1 = o0 + KV_DIM
    o2 = o1 + KV_DIM
    o3 = o2 + CONV_DIM
    o4 = o3 + CONV_DIM
    bd = bd_ref[...]
    qn = _head_rms(proj[:, :o0], bd, qg_ref[...])
    kn = _head_rms(proj[:, o0:o1], bd[:KV_DIM, :KV_DIM], kg_ref[...])
    v = proj[:, o1:o2]
    bg = proj[:, o2:o3]
    u = proj[:, o3:o4] * proj[:, o4:]

    rep = rep_ref[...]
    cw = cw_ref[...]
    att_parts, cv_parts = [], []
    for sub in range(nsub):
        r0, r1 = sub * tm, (sub + 1) * tm
        kall = jnp.concatenate([kprev[sub], kn[r0:r1]], axis=0)
        vall = jnp.concatenate([vprev[sub], v[r0:r1]], axis=0)
        att_parts.append(_attention(i, qn[r0:r1], kall, vall, rep, sink_ref,
                                    mask_history=not has_cache))
        kprev[sub] = kall[tm:, :]
        vprev[sub] = vall[tm:, :]
        knew_ref[sub] = kall[tm:, :]
        vnew_ref[sub] = vall[tm:, :]

        us = u[r0:r1]
        ubuf[sub, 8:8 + tm, :] = us
        cvs = ubuf[sub, 6:6 + tm, :] * cw[0:1, :] + ubuf[sub, 7:7 + tm, :] * cw[1:2, :]
        cv_parts.append(cvs + us * cw[2:3, :])
        ubuf[sub, 0:8, :] = ubuf[sub, tm:tm + 8, :]
        unew_ref[sub] = us[tm - (CONV_W - 1):, :]
    att = jnp.concatenate(att_parts, axis=0) if nsub > 1 else att_parts[0]
    cv = jnp.concatenate(cv_parts, axis=0) if nsub > 1 else cv_parts[0]

    mix = jnp.concatenate([att, bg * cv], axis=1).astype(BF16)
    h = x + _dot(mix, wout_ref[...])
    for sub in range(nsub):
        h_ref[sub] = h[sub * tm:(sub + 1) * tm]
    hn = _rms_rows(h, g2_ref[...])

    hn_hi, hn_lo = _split_bf16(hn)
    wr_hi, wr_lo = _split_bf16(wrt_ref[...])
    logits = (_dot_nt(wr_hi, hn_hi) + _dot_nt(wr_hi, hn_lo) + _dot_nt(wr_lo, hn_hi)) + brt_ref[...]
    eio = lax.broadcasted_iota(I32, (N_EXPERTS, ntok), 0)
    work = logits
    tops, sels = [], []
    for _ in range(TOP_K):
        mk = jnp.max(work, axis=0, keepdims=True)
        ik = jnp.min(jnp.where(work == mk, eio, N_EXPERTS), axis=0, keepdims=True)
        sel = eio == ik
        work = jnp.where(sel, -jnp.inf, work)
        tops.append(mk)
        sels.append(sel)
    es = [jnp.exp(t - tops[0]) for t in tops]
    esum = es[0] + es[1] + es[2] + es[3]

    onehot = jnp.zeros((N_EXPERTS, ntok), F32)
    for sel in sels:
        onehot = onehot + jnp.where(sel, 1.0, 0.0)
    onehot_b = onehot.astype(BF16)
    before = _dot(onehot_b, upper_ref[...])
    smaller = _dot(lower_ref[...], onehot_b)
    nst = ntok // st
    starts = [jnp.broadcast_to(jnp.sum(smaller[:, k * st:(k + 1) * st], axis=1, keepdims=True),
                               (N_EXPERTS, st)) for k in range(nst)]
    where_to = before + (jnp.concatenate(starts, axis=1) if nst > 1 else starts[0])
    pos = [jnp.sum(jnp.where(sel, where_to, 0.0), axis=0, keepdims=True) for sel in sels]
    rio = lax.broadcasted_iota(I32, (TOP_K * st, st), 0)
    for k in range(nst):
        c0, c1 = k * st, (k + 1) * st
        hit = rio == pos[0][:, c0:c1].astype(I32)
        for slot in range(1, TOP_K):
            hit = hit | (rio == pos[slot][:, c0:c1].astype(I32))
        perm = jnp.where(hit, 1.0, 0.0).astype(BF16)
        _pack_rows(_dot(perm, hn_hi[c0:c1]), xs_ref.at[k, 0])
        tcnt_ref[k, 0] = _dot_nt(jnp.ones((1, st), BF16), onehot_b[:, c0:c1]).astype(I32)

    rows8 = jnp.concatenate(pos + [e / esum for e in es], axis=0)
    sq = jnp.concatenate([rows8, jnp.zeros((LANES - 2 * TOP_K, ntok), F32)], axis=0)
    cols = jnp.transpose(sq)
    for sub in range(nsub):
        col_ref[sub] = cols[sub * tm:(sub + 1) * tm]


def _mixer(x, cache, w, *, tm, nsub, st):
    nb, s, _ = x.shape
    nt = s // tm
    nst = nsub * tm // st
    has_cache = cache is not None
    full = lambda shape: pl.BlockSpec(shape, lambda b, i: (0,) * len(shape))
    per_b = lambda shape: pl.BlockSpec((nsub,) + shape, lambda b, i: (b, 0, 0))
    tile = lambda last: pl.BlockSpec((nsub, tm, last), lambda b, i: (b, i, 0))

    in_specs = [tile(D_MODEL)]
    args = [x]
    if has_cache:
        in_specs += [per_b((WINDOW, KV_DIM)), per_b((WINDOW, KV_DIM)), per_b((CONV_W - 1, CONV_DIM))]
        args += list(cache)
    tio = jnp.arange(nsub * tm)
    eio = jnp.arange(N_EXPERTS)
    upper = ((tio[:, None] < tio[None, :]) & (tio[:, None] // st == tio[None, :] // st)).astype(BF16)
    lower = (eio[None, :] < eio[:, None]).astype(BF16)
    in_specs += [
        full((1, D_MODEL)), full((D_MODEL, IN_DIM)), full((1, ATT_DIM)),
        full((1, KV_DIM)), full((ATT_DIM, ATT_DIM)), full((KV_DIM, N_KV_HEADS * GROUP_W)),
        pl.BlockSpec(memory_space=pltpu.SMEM),
        full((CONV_W, CONV_DIM)), full((D_MODEL, D_MODEL)), full((1, D_MODEL)),
        full((N_EXPERTS, D_MODEL)), full((N_EXPERTS, 1)), full((nsub * tm, nsub * tm)),
        full((N_EXPERTS, N_EXPERTS)),
    ]
    args += [w['g1'], w['w_in'], w['qg'], w['kg'], w['bd'], w['rep'], w['sinks'], w['conv_w'],
             w['w_out'], w['g2'], w['w_router_t'], w['b_router_t'], upper, lower]
    rt = TOP_K * st
    nsort = nb // nsub * nst
    out_shape = [
        jax.ShapeDtypeStruct((nb, s, D_MODEL), F32),
        jax.ShapeDtypeStruct((nsort, nt, rt, PACK_S, LANES), U32),
        jax.ShapeDtypeStruct((nb, s, LANES), F32),
        jax.ShapeDtypeStruct((nsort, nt, 1, N_EXPERTS), I32),
        jax.ShapeDtypeStruct((nb, WINDOW, KV_DIM), F32),
        jax.ShapeDtypeStruct((nb, WINDOW, KV_DIM), F32),
        jax.ShapeDtypeStruct((nb, CONV_W - 1, CONV_DIM), F32),
    ]
    out_specs = [
        tile(D_MODEL),
        pl.BlockSpec((nst, 1, rt, PACK_S, LANES), lambda b, i: (b, i, 0, 0, 0)),
        tile(LANES),
        pl.BlockSpec((nst, 1, 1, N_EXPERTS), lambda b, i: (b, i, 0, 0)),
        per_b((WINDOW, KV_DIM)), per_b((WINDOW, KV_DIM)), per_b((CONV_W - 1, CONV_DIM)),
    ]
    scratch = [pltpu.VMEM((nsub, WINDOW, KV_DIM), F32), pltpu.VMEM((nsub, WINDOW, KV_DIM), F32),
               pltpu.VMEM((nsub, tm + 8, CONV_DIM), F32)]
    return pl.pallas_call(
        functools.partial(_mixer_kernel, tm=tm, nsub=nsub, st=st, has_cache=has_cache),
        grid=(nb // nsub, nt),
        in_specs=in_specs,
        out_specs=out_specs,
        out_shape=out_shape,
        scratch_shapes=scratch,
        compiler_params=pltpu.CompilerParams(
            dimension_semantics=("arbitrary", "arbitrary"), vmem_limit_bytes=VMEM_LIMIT),
        name="mixer_sample" if has_cache else "mixer_prompt",
    )(*args)


def _expert_kernel(be_ref, bq_ref, nused_ref, src_ref, off_ref, n_ref, tot_ref,
                   xsp_hbm, xss_hbm, wgu_hbm, bgu_ref, wd_hbm, bdn_ref, ys_ref,
                   xbuf, sems, wgu_f, wd_f, wsems, wgu_b, wd_b, ptr, *, n_tiles, n_prompt_tiles):
    b = pl.program_id(0)
    nused = nused_ref[0]

    def weight_copies(e):
        return (pltpu.make_async_copy(wgu_hbm.at[e], wgu_f, wsems.at[0]),
                pltpu.make_async_copy(wd_hbm.at[e], wd_f, wsems.at[1]))

    def issue_rows(blk, slot, enabled, between=lambda: None):
        blk = jnp.minimum(blk, pl.num_programs(0) - 1)
        e = be_ref[blk]
        q = bq_ref[blk]
        lo_row = q * ROW_BLOCK
        hi_row = lo_row + ROW_BLOCK
        base = e * n_tiles

        def segment(j, j_limit, src_hbm, ok):
            jc = jnp.minimum(j, j_limit - 1)
            o = off_ref[base + jc]
            inside = ok & (j < j_limit) & (o < hi_row)
            lo = jnp.maximum(o, lo_row)
            ln = jnp.minimum(o + n_ref[base + jc], hi_row) - lo

            @pl.when(inside & (ln > 0))
            def _():
                pltpu.make_async_copy(src_hbm.at[pl.ds(src_ref[base + jc] + (lo - o), ln)],
                                      xbuf.at[slot, pl.ds(lo - lo_row, ln)], sems.at[slot]).start()
            return inside

        j0 = jnp.where(q == 0, 0, ptr[0])

        def cond(j):
            return (enabled & (j < n_prompt_tiles)
                    & (off_ref[base + jnp.minimum(j, n_prompt_tiles - 1)] < hi_row))

        def body(j):
            segment(j, n_prompt_tiles, xsp_hbm, True)
            return j + 1

        jtail = lax.while_loop(cond, body, j0 + INLINE_SEGMENTS)
        result = between()
        count = jnp.int32(0)
        for k in range(INLINE_SEGMENTS):
            count = count + segment(j0 + k, n_prompt_tiles, xsp_hbm, enabled).astype(I32)
        jend = jnp.where(count == INLINE_SEGMENTS, jtail, j0 + count)
        ptr[0] = jnp.where(enabled, jnp.maximum(jend - 1, 0), ptr[0])
        for j in range(n_prompt_tiles, n_tiles):
            segment(j, n_tiles, xss_hbm, enabled)
        return result

    @pl.when(b == 0)
    def _():
        xbuf[...] = jnp.zeros(xbuf.shape, U32)
        for ahead in range(LOOKAHEAD):
            issue_rows(ahead, ahead, ahead < nused)
        for c in weight_copies(be_ref[0]):
            c.start()

    active = b < nused

    @pl.when(active)
    def _():
        slot = b % ROW_SLOTS
        e = be_ref[b]
        rows = jnp.minimum(tot_ref[e] - bq_ref[b] * ROW_BLOCK, ROW_BLOCK)
        pltpu.make_async_copy(xsp_hbm.at[pl.ds(0, rows)], xbuf.at[slot, pl.ds(0, rows)],
                              sems.at[slot]).wait()

        @pl.when(bq_ref[b] == 0)
        def _():
            for c in weight_copies(e):
                c.wait()
            wgu_b[...] = wgu_f[...].astype(BF16)
            wd_b[...] = wd_f[...].astype(BF16)
            nxt = b + (tot_ref[e] + ROW_BLOCK - 1) // ROW_BLOCK

            @pl.when(nxt < nused)
            def _():
                for c in weight_copies(be_ref[jnp.minimum(nxt, pl.num_programs(0) - 1)]):
                    c.start()

        def compute(nrows):
            x = issue_rows(b + LOOKAHEAD, (b + LOOKAHEAD) % ROW_SLOTS, b + LOOKAHEAD < nused,
                           between=lambda: _unpack_rows(xbuf.at[slot, pl.ds(0, nrows)]))
            gu = _dot(x, wgu_b[...]) + bgu_ref[0]
            g = jnp.minimum(gu[:, :D_FF], SWIGLU_LIMIT)
            u = jnp.clip(gu[:, D_FF:], -SWIGLU_LIMIT, SWIGLU_LIMIT)
            act = (u + 1.0) * (g * jax.nn.sigmoid(SWIGLU_ALPHA * g))
            y = _dot(act.astype(BF16), wd_b[...]) + (bdn_ref[0] + _anchor_zero_row(xbuf))
            _pack_rows(y.astype(BF16).astype(F32), ys_ref.at[pl.ds(0, nrows)])
            if nrows < ROW_BLOCK:
                ys_ref[nrows:] = jnp.zeros((ROW_BLOCK - nrows, PACK_S, LANES), U32)

        @pl.when(rows > ROW_BLOCK // 2)
        def _():
            compute(ROW_BLOCK)

        @pl.when(rows <= ROW_BLOCK // 2)
        def _():
            compute(ROW_BLOCK // 2)

    @pl.when(jnp.logical_not(active))
    def _():
        ys_ref[...] = jnp.zeros(ys_ref.shape, U32)


def _experts(plan, xsp, xss, w_gu, b_gu, w_down, b_down, *, n_tiles, n_prompt_tiles, rows):
    nblk = rows // ROW_BLOCK

    def bmap(b, be, *_):
        return (be[b], 0, 0)

    grid_spec = pltpu.PrefetchScalarGridSpec(
        num_scalar_prefetch=7,
        grid=(nblk,),
        in_specs=[pl.BlockSpec(memory_space=pl.ANY),
                  pl.BlockSpec(memory_space=pl.ANY),
                  pl.BlockSpec(memory_space=pl.ANY),
                  pl.BlockSpec((1, 1, 2 * D_FF), bmap),
                  pl.BlockSpec(memory_space=pl.ANY),
                  pl.BlockSpec((1, 1, D_MODEL), bmap)],
        out_specs=pl.BlockSpec((ROW_BLOCK, PACK_S, LANES), lambda b, *_: (b, 0, 0)),
        scratch_shapes=[pltpu.VMEM((ROW_SLOTS + 1, ROW_BLOCK, PACK_S, LANES), U32),
                        pltpu.SemaphoreType.DMA((ROW_SLOTS,)),
                        pltpu.VMEM((D_MODEL, 2 * D_FF), F32), pltpu.VMEM((D_FF, D_MODEL), F32),
                        pltpu.SemaphoreType.DMA((2,)),
                        pltpu.VMEM((D_MODEL, 2 * D_FF), BF16), pltpu.VMEM((D_FF, D_MODEL), BF16),
                        pltpu.SMEM((1,), I32)],
    )
    return pl.pallas_call(
        functools.partial(_expert_kernel, n_tiles=n_tiles, n_prompt_tiles=n_prompt_tiles),
        grid_spec=grid_spec,
        out_shape=jax.ShapeDtypeStruct((rows, PACK_S, LANES), U32),
        compiler_params=pltpu.CompilerParams(
            dimension_semantics=("arbitrary",), vmem_limit_bytes=VMEM_LIMIT),
        name="experts",
    )(plan['block_e'], plan['block_q'], plan['n_used'], plan['src_t'], plan['off_t'], plan['n_t'],
      plan['tot'], xsp, xss, w_gu, b_gu, w_down, b_down)


def _combine_kernel(ysrc_ref, sloc_ref, n_ref, col_ref, h_ref, ys_hbm, o_ref, buf, sems,
                    *, tm, tile0):
    j = pl.program_id(0)
    nt = pl.num_programs(0)
    rt = TOP_K * tm

    def issue(t, slot, enabled):
        base = (tile0 + jnp.minimum(t, nt - 1)) * N_EXPERTS
        for e in range(N_EXPERTS):
            n = n_ref[base + e]

            @pl.when(enabled & (n > 0))
            def _():
                pltpu.make_async_copy(ys_hbm.at[pl.ds(ysrc_ref[base + e], n)],
                                      buf.at[slot, pl.ds(sloc_ref[base + e], n)],
                                      sems.at[slot]).start()

    @pl.when(j == 0)
    def _():
        buf[ROW_SLOTS, 0:8 // PACK_S] = jnp.zeros((8 // PACK_S, PACK_S, LANES), U32)
        for ahead in range(LOOKAHEAD):
            issue(ahead, ahead, ahead < nt)

    slot = j % ROW_SLOTS
    pltpu.make_async_copy(ys_hbm.at[pl.ds(0, rt)], buf.at[slot], sems.at[slot]).wait()
    y = _unpack_rows(buf.at[slot])
    issue(j + LOOKAHEAD, (j + LOOKAHEAD) % ROW_SLOTS, j + LOOKAHEAD < nt)
    col = col_ref[...]
    rio = lax.broadcasted_iota(I32, (tm, rt), 1)
    gmat = jnp.zeros((tm, rt), F32)
    for k in range(TOP_K):
        gmat = gmat + jnp.where(rio == col[:, k:k + 1].astype(I32),
                                col[:, TOP_K + k:TOP_K + k + 1], 0.0)
    o_ref[...] = (h_ref[...] + _anchor_zero_row(buf)) + _dot(gmat.astype(BF16), y)


def _combine(plan, col, h2d, ys, *, tm, tile0):
    t = h2d.shape[0]
    rt = TOP_K * tm
    grid_spec = pltpu.PrefetchScalarGridSpec(
        num_scalar_prefetch=3,
        grid=(t // tm,),
        in_specs=[pl.BlockSpec((tm, LANES), lambda i, *_: (i, 0)),
                  pl.BlockSpec((tm, D_MODEL), lambda i, *_: (i, 0)),
                  pl.BlockSpec(memory_space=pl.ANY)],
        out_specs=pl.BlockSpec((tm, D_MODEL), lambda i, *_: (i, 0)),
        scratch_shapes=[pltpu.VMEM((ROW_SLOTS + 1, rt, PACK_S, LANES), U32),
                        pltpu.SemaphoreType.DMA((ROW_SLOTS,))],
    )
    return pl.pallas_call(
        functools.partial(_combine_kernel, tm=tm, tile0=tile0),
        grid_spec=grid_spec,
        out_shape=jax.ShapeDtypeStruct((t, D_MODEL), F32),
        compiler_params=pltpu.CompilerParams(
            dimension_semantics=("arbitrary",), vmem_limit_bytes=VMEM_LIMIT),
        name="combine",
    )(plan['ysrc'], plan['sloc'], plan['n'], col, h2d, ys)


def kernel(x_prompt, x_sample, cache_k, cache_v, state_conv, norm1_g, w_in, q_norm_g, k_norm_g,
           sinks, conv_w, w_out, norm2_g, w_router, b_router, w_gu, b_gu, w_down, b_down):
    l = 0
    nbp, sp, _ = x_prompt.shape
    nbs, ss, _ = x_sample.shape
    tp, ts = nbp * sp, nbs * ss
    tmp, tms = TOKEN_TILE, ts
    ntp, nts = tp // tmp, ts // tms
    n_tiles = ntp + nts
    n_rows = (tp + ts) * TOP_K
    grp = jnp.arange(ATT_DIM) // HEAD_DIM
    rep_col = jnp.arange(N_KV_HEADS * GROUP_W)
    w = {
        'g1': norm1_g[l][None, :],
        'w_in': w_in[l].astype(BF16),
        'qg': jnp.tile(q_norm_g[l], N_HEADS)[None, :],
        'kg': jnp.tile(k_norm_g[l], N_KV_HEADS)[None, :],
        'bd': (grp[:, None] == grp[None, :]).astype(BF16),
        'rep': (jnp.arange(KV_DIM)[:, None]
                == (rep_col // GROUP_W * HEAD_DIM + rep_col % HEAD_DIM)[None, :]).astype(BF16),
        'sinks': sinks[l][None, :],
        'conv_w': conv_w[l],
        'w_out': w_out[l].astype(BF16),
        'g2': norm2_g[l][None, :],
        'w_router_t': w_router[l].T,
        'b_router_t': b_router[l][:, None],
    }
    hp, xsp, colp, tcp, knp, vnp, unp = _mixer(x_prompt, None, w, tm=tmp, nsub=STREAMS_PER_STEP,
                                               st=tmp)
    cache = (cache_k[l].reshape(nbs, WINDOW, KV_DIM), cache_v[l].reshape(nbs, WINDOW, KV_DIM),
             state_conv[l])
    hs, xss, cols, tcs, kns, vns, uns = _mixer(x_sample, cache, w, tm=ss, nsub=nbs, st=tms)

    n = jnp.concatenate([tcp.reshape(ntp, N_EXPERTS), tcs.reshape(nts, N_EXPERTS)], axis=0)
    tile_base = jnp.concatenate([jnp.arange(ntp, dtype=I32) * (TOP_K * tmp),
                                 jnp.arange(nts, dtype=I32) * (TOP_K * tms)])
    sloc = jnp.cumsum(n, axis=1) - n
    off = jnp.cumsum(n, axis=0) - n
    tot = jnp.sum(n, axis=0)
    padded = (tot + ROW_BLOCK - 1) // ROW_BLOCK * ROW_BLOCK
    ends = jnp.cumsum(padded)
    base = ends - padded
    rows = (-(-n_rows // ROW_BLOCK) + N_EXPERTS) * ROW_BLOCK
    nblk = rows // ROW_BLOCK
    blk0 = jnp.arange(nblk, dtype=I32) * ROW_BLOCK
    block_e = jnp.minimum(jnp.sum((ends[None, :] <= blk0[:, None]).astype(I32), axis=1),
                          N_EXPERTS - 1)
    eq = (block_e[:, None] == jnp.arange(N_EXPERTS, dtype=I32)[None, :]).astype(I32)
    block_q = (blk0 - jnp.sum(eq * base[None, :], axis=1)) // ROW_BLOCK
    plan = {
        'block_e': block_e.astype(I32),
        'block_q': block_q.astype(I32),
        'n_used': (ends[-1:] // ROW_BLOCK).astype(I32),
        'src_t': (tile_base[:, None] + sloc).T.reshape(-1).astype(I32),
        'off_t': off.T.reshape(-1).astype(I32),
        'n_t': n.T.reshape(-1).astype(I32),
        'tot': tot.astype(I32),
        'ysrc': (base[None, :] + off).reshape(-1).astype(I32),
        'sloc': sloc.reshape(-1).astype(I32),
        'n': n.reshape(-1).astype(I32),
    }

    ys = _experts(plan, xsp.reshape(-1, PACK_S, LANES), xss.reshape(-1, PACK_S, LANES),
                  w_gu[l], b_gu[l][:, None, :], w_down[l], b_down[l][:, None, :],
                  n_tiles=n_tiles, n_prompt_tiles=ntp, rows=rows)
    yp = _combine(plan, colp.reshape(tp, LANES), hp.reshape(tp, D_MODEL), ys, tm=tmp, tile0=0)
    ysm = _combine(plan, cols.reshape(ts, LANES), hs.reshape(ts, D_MODEL), ys, tm=tms, tile0=ntp)

    kv5 = lambda a: a.reshape(1, a.shape[0], WINDOW, N_KV_HEADS, HEAD_DIM)
    return (yp.reshape(nbp, sp, D_MODEL), ysm.reshape(nbs, ss, D_MODEL),
            kv5(knp), kv5(vnp), unp[None], kv5(kns), kv5(vns), uns[None])
```

```python
import functools

import jax
import jax.numpy as jnp
from jax import lax
from jax.experimental import pallas as pl
from jax.experimental.pallas import tpu as pltpu

D_MODEL = 1024
CHUNK = 64
HEAD_DIM = 64
N_HEADS = 8
N_KV_HEADS = 2
Q_PER_KV = N_HEADS // N_KV_HEADS
ATT_DIM = N_HEADS * HEAD_DIM
KV_DIM = N_KV_HEADS * HEAD_DIM
GROUP_W = Q_PER_KV * HEAD_DIM
CONV_DIM = D_MODEL - ATT_DIM
CONV_W = 3
WINDOW = 128
IN_DIM = ATT_DIM + 2 * KV_DIM + 3 * CONV_DIM
N_EXPERTS = 32
TOP_K = 4
D_FF = D_MODEL
SWIGLU_LIMIT = 7.0
SWIGLU_ALPHA = 1.702
EPS = 1e-6
NEG = -1e30

LANES = 128
HALF = D_MODEL // 2
PACK_S = HALF // LANES
TOKEN_TILE = 256
STREAMS_PER_STEP = 2
ROW_BLOCK = 512
INLINE_SEGMENTS = 20
LOOKAHEAD = 2
ROW_SLOTS = LOOKAHEAD + 1
VMEM_LIMIT = 56 * 1024 * 1024

F32 = jnp.float32
BF16 = jnp.bfloat16
U32 = jnp.uint32
I32 = jnp.int32


def _rms_rows(x, g):
    return x * lax.rsqrt(jnp.mean(x * x, axis=-1, keepdims=True) + EPS) * g


def _split_bf16(x):
    hi = x.astype(BF16)
    lo = (x - hi.astype(F32)).astype(BF16)
    return hi, lo


def _dot(a, b):
    return jnp.dot(a, b, preferred_element_type=F32)


def _dot_nt(a, b):
    return lax.dot_general(a, b, (((1,), (1,)), ((), ())), preferred_element_type=F32)


def _head_rms(t, bd, g):
    ssq = _dot((t * t).astype(BF16), bd)
    return t * lax.rsqrt(ssq * (1.0 / HEAD_DIM) + EPS) * g


def _pack_rows(vals, out_ref):
    r = vals.shape[0]
    bits = pltpu.bitcast(vals, U32)
    word = (bits[:, :HALF] >> 16) | (bits[:, HALF:] & jnp.uint32(0xFFFF0000))
    flat = out_ref.reshape(r * PACK_S, LANES)
    for s in range(PACK_S):
        flat[pl.ds(s, r, stride=PACK_S), :] = word[:, s * LANES:(s + 1) * LANES]


def _unpack_rows(packed_ref):
    r = packed_ref.shape[0]
    flat = packed_ref.reshape(r * PACK_S, LANES)
    lo, hi = [], []
    for s in range(PACK_S):
        w = flat[pl.ds(s, r, stride=PACK_S), :]
        lo.append(pltpu.bitcast(w << 16, F32))
        hi.append(pltpu.bitcast(w & jnp.uint32(0xFFFF0000), F32))
    return jnp.concatenate(lo + hi, axis=1).astype(BF16)


def _anchor_zero_row(staging_ref):
    z = staging_ref.at[ROW_SLOTS, pl.ds(0, 8 // PACK_S)].reshape(8, LANES)[...]
    return jnp.concatenate([pltpu.bitcast(z, F32)[0:1, :]] * (D_MODEL // LANES), axis=1)


def _attention(i, qn, kall, vall, rep, sink_ref, *, mask_history):
    tm = qn.shape[0]
    k4 = _dot(kall.astype(BF16), rep).astype(BF16)
    v4 = _dot(vall.astype(BF16), rep).astype(BF16)
    lane_grp = lax.broadcasted_iota(I32, (CHUNK, GROUP_W), 1) // HEAD_DIM
    row_grp = lax.broadcasted_iota(I32, (Q_PER_KV * CHUNK, 1), 0) // CHUNK
    span = WINDOW + CHUNK
    att_rows = []
    for c in range(tm // CHUNK):
        per_kv = []
        for hk in range(N_KV_HEADS):
            qc = qn[c * CHUNK:(c + 1) * CHUNK, hk * GROUP_W:(hk + 1) * GROUP_W]
            qs = jnp.concatenate([jnp.where(lane_grp == g, qc, 0.0) for g in range(Q_PER_KV)],
                                 axis=0).astype(BF16)
            kw = k4[c * CHUNK:c * CHUNK + span, hk * GROUP_W:(hk + 1) * GROUP_W]
            vw = v4[c * CHUNK:c * CHUNK + span, hk * GROUP_W:(hk + 1) * GROUP_W]
            s = _dot_nt(qs, kw) * (HEAD_DIM ** -0.5)
            if mask_history and c * CHUNK < WINDOW:
                kcol = lax.broadcasted_iota(I32, (1, span), 1)
                s = jnp.where(kcol >= jnp.where(i == 0, WINDOW - c * CHUNK, 0), s, NEG)
            sink = jnp.zeros((Q_PER_KV * CHUNK, 1), F32)
            for g in range(Q_PER_KV):
                sink = jnp.where(row_grp == g, sink_ref[0, hk * Q_PER_KV + g], sink)
            m = jnp.maximum(jnp.max(s, axis=-1, keepdims=True), sink)
            p = jnp.exp(s - m)
            den = jnp.sum(p, axis=-1, keepdims=True) + jnp.exp(sink - m)
            o = _dot(p.astype(BF16), vw) / den
            oc = jnp.zeros((CHUNK, GROUP_W), F32)
            for g in range(Q_PER_KV):
                oc = oc + jnp.where(lane_grp == g, o[g * CHUNK:(g + 1) * CHUNK, :], 0.0)
            per_kv.append(oc)
        att_rows.append(jnp.concatenate(per_kv, axis=1))
    return jnp.concatenate(att_rows, axis=0)


def _mixer_kernel(*refs, tm, nsub, st, has_cache):
    refs = list(refs)
    x_ref = refs.pop(0)
    if has_cache:
        ck_ref, cv_ref, st_ref = refs.pop(0), refs.pop(0), refs.pop(0)
    (g1_ref, win_ref, qg_ref, kg_ref, bd_ref, rep_ref, sink_ref, cw_ref, wout_ref, g2_ref,
     wrt_ref, brt_ref, upper_ref, lower_ref,
     h_ref, xs_ref, col_ref, tcnt_ref, knew_ref, vnew_ref, unew_ref, kprev, vprev, ubuf) = refs
    i = pl.program_id(1)
    ntok = nsub * tm

    @pl.when(i == 0)
    def _():
        ubuf[:, 0:8, :] = jnp.zeros((nsub, 8, CONV_DIM), F32)
        if has_cache:
            kprev[...] = ck_ref[...]
            vprev[...] = cv_ref[...]
            ubuf[:, 6:8, :] = st_ref[...]
        else:
            kprev[...] = jnp.zeros(kprev.shape, F32)
            vprev[...] = jnp.zeros(vprev.shape, F32)

    x = jnp.concatenate([x_ref[sub] for sub in range(nsub)], axis=0) if nsub > 1 else x_ref[0]
    inv_rms = lax.rsqrt(jnp.mean(x * x, axis=-1, keepdims=True) + EPS)
    proj = _dot((x * g1_ref[...]).astype(BF16), win_ref[...]) * inv_rms
    o0 = ATT_DIM
    o1 = o0 + KV_DIM
    o2 = o1 + KV_DIM
    o3 = o2 + CONV_DIM
    o4 = o3 + CONV_DIM
    bd = bd_ref[...]
    qn = _head_rms(proj[:, :o0], bd, qg_ref[...])
    kn = _head_rms(proj[:, o0:o1], bd[:KV_DIM, :KV_DIM], kg_ref[...])
    v = proj[:, o1:o2]
    bg = proj[:, o2:o3]
    u = proj[:, o3:o4] * proj[:, o4:]

    rep = rep_ref[...]
    cw = cw_ref[...]
    att_parts, cv_parts = [], []
    for sub in range(nsub):
        r0, r1 = sub * tm, (sub + 1) * tm
        kall = jnp.concatenate([kprev[sub], kn[r0:r1]], axis=0)
        vall = jnp.concatenate([vprev[sub], v[r0:r1]], axis=0)
        att_parts.append(_attention(i, qn[r0:r1], kall, vall, rep, sink_ref,
                                    mask_history=not has_cache))
        kprev[sub] = kall[tm:, :]
        vprev[sub] = vall[tm:, :]
        knew_ref[sub] = kall[tm:, :]
        vnew_ref[sub] = vall[tm:, :]

        us = u[r0:r1]
        ubuf[sub, 8:8 + tm, :] = us
        cvs = ubuf[sub, 6:6 + tm, :] * cw[0:1, :] + ubuf[sub, 7:7 + tm, :] * cw[1:2, :]
        cv_parts.append(cvs + us * cw[2:3, :])
        ubuf[sub, 0:8, :] = ubuf[sub, tm:tm + 8, :]
        unew_ref[sub] = us[tm - (CONV_W - 1):, :]
    att = jnp.concatenate(att_parts, axis=0) if nsub > 1 else att_parts[0]
    cv = jnp.concatenate(cv_parts, axis=0) if nsub > 1 else cv_parts[0]

    mix = jnp.concatenate([att, bg * cv], axis=1).astype(BF16)
    h = x + _dot(mix, wout_ref[...])
    for sub in range(nsub):
        h_ref[sub] = h[sub * tm:(sub + 1) * tm]
    hn = _rms_rows(h, g2_ref[...])

    hn_hi, hn_lo = _split_bf16(hn)
    wr_hi, wr_lo = _split_bf16(wrt_ref[...])
    logits = (_dot_nt(wr_hi, hn_hi) + _dot_nt(wr_hi, hn_lo) + _dot_nt(wr_lo, hn_hi)) + brt_ref[...]
    eio = lax.broadcasted_iota(I32, (N_EXPERTS, ntok), 0)
    sub = 8
    groups = [logits[g * sub:(g + 1) * sub, :] for g in range(N_EXPERTS // sub)]
    ranks = [jnp.zeros((sub, ntok), I32) for _ in groups]
    sub_io = lax.broadcasted_iota(I32, (sub, ntok), 0)
    for e in range(N_EXPERTS):
        row = logits[e:e + 1, :]
        for g, lg in enumerate(groups):
            if g * sub > e:
                ahead = jnp.where(row >= lg, 1, 0)
            elif (g + 1) * sub <= e:
                ahead = jnp.where(row > lg, 1, 0)
            else:
                ahead = jnp.where(sub_io > e - g * sub, jnp.where(row >= lg, 1, 0),
                                  jnp.where(row > lg, 1, 0))
            ranks[g] = ranks[g] + ahead
    rank = jnp.concatenate(ranks, axis=0)
    sels = [rank == k for k in range(TOP_K)]
    tops = [jnp.sum(jnp.where(sel, logits, 0.0), axis=0, keepdims=True) for sel in sels]
    es = [jnp.exp(t - tops[0]) for t in tops]
    esum = es[0] + es[1] + es[2] + es[3]

    onehot = jnp.zeros((N_EXPERTS, ntok), F32)
    for sel in sels:
        onehot = onehot + jnp.where(sel, 1.0, 0.0)
    onehot_b = onehot.astype(BF16)
    before = _dot(onehot_b, upper_ref[...])
    smaller = _dot(lower_ref[...], onehot_b)
    nst = ntok // st
    starts = [jnp.broadcast_to(jnp.sum(smaller[:, k * st:(k + 1) * st], axis=1, keepdims=True),
                               (N_EXPERTS, st)) for k in range(nst)]
    where_to = before + (jnp.concatenate(starts, axis=1) if nst > 1 else starts[0])
    pos = [jnp.sum(jnp.where(sel, where_to, 0.0), axis=0, keepdims=True) for sel in sels]
    rio = lax.broadcasted_iota(I32, (TOP_K * st, st), 0)
    for k in range(nst):
        c0, c1 = k * st, (k + 1) * st
        hit = rio == pos[0][:, c0:c1].astype(I32)
        for slot in range(1, TOP_K):
            hit = hit | (rio == pos[slot][:, c0:c1].astype(I32))
        perm = jnp.where(hit, 1.0, 0.0).astype(BF16)
        _pack_rows(_dot(perm, hn_hi[c0:c1]), xs_ref.at[k, 0])
        tcnt_ref[k, 0] = _dot_nt(jnp.ones((1, st), BF16), onehot_b[:, c0:c1]).astype(I32)

    rows8 = jnp.concatenate(pos + [e / esum for e in es], axis=0)
    sq = jnp.concatenate([rows8, jnp.zeros((LANES - 2 * TOP_K, ntok), F32)], axis=0)
    cols = jnp.transpose(sq)
    for sub in range(nsub):
        col_ref[sub] = cols[sub * tm:(sub + 1) * tm]


def _mixer(x, cache, w, *, tm, nsub, st):
    nb, s, _ = x.shape
    nt = s // tm
    nst = nsub * tm // st
    has_cache = cache is not None
    full = lambda shape: pl.BlockSpec(shape, lambda b, i: (0,) * len(shape))
    per_b = lambda shape: pl.BlockSpec((nsub,) + shape, lambda b, i: (b, 0, 0))
    tile = lambda last: pl.BlockSpec((nsub, tm, last), lambda b, i: (b, i, 0))

    in_specs = [tile(D_MODEL)]
    args = [x]
    if has_cache:
        in_specs += [per_b((WINDOW, KV_DIM)), per_b((WINDOW, KV_DIM)), per_b((CONV_W - 1, CONV_DIM))]
        args += list(cache)
    tio = jnp.arange(nsub * tm)
    eio = jnp.arange(N_EXPERTS)
    upper = ((tio[:, None] < tio[None, :]) & (tio[:, None] // st == tio[None, :] // st)).astype(BF16)
    lower = (eio[None, :] < eio[:, None]).astype(BF16)
    in_specs += [
        full((1, D_MODEL)), full((D_MODEL, IN_DIM)), full((1, ATT_DIM)),
        full((1, KV_DIM)), full((ATT_DIM, ATT_DIM)), full((KV_DIM, N_KV_HEADS * GROUP_W)),
        pl.BlockSpec(memory_space=pltpu.SMEM),
        full((CONV_W, CONV_DIM)), full((D_MODEL, D_MODEL)), full((1, D_MODEL)),
        full((N_EXPERTS, D_MODEL)), full((N_EXPERTS, 1)), full((nsub * tm, nsub * tm)),
        full((N_EXPERTS, N_EXPERTS)),
    ]
    args += [w['g1'], w['w_in'], w['qg'], w['kg'], w['bd'], w['rep'], w['sinks'], w['conv_w'],
             w['w_out'], w['g2'], w['w_router_t'], w['b_router_t'], upper, lower]
    rt = TOP_K * st
    nsort = nb // nsub * nst
    out_shape = [
        jax.ShapeDtypeStruct((nb, s, D_MODEL), F32),
        jax.ShapeDtypeStruct((nsort, nt, rt, PACK_S, LANES), U32),
        jax.ShapeDtypeStruct((nb, s, LANES), F32),
        jax.ShapeDtypeStruct((nsort, nt, 1, N_EXPERTS), I32),
        jax.ShapeDtypeStruct((nb, WINDOW, KV_DIM), F32),
        jax.ShapeDtypeStruct((nb, WINDOW, KV_DIM), F32),
        jax.ShapeDtypeStruct((nb, CONV_W - 1, CONV_DIM), F32),
    ]
    out_specs = [
        tile(D_MODEL),
        pl.BlockSpec((nst, 1, rt, PACK_S, LANES), lambda b, i: (b, i, 0, 0, 0)),
        tile(LANES),
        pl.BlockSpec((nst, 1, 1, N_EXPERTS), lambda b, i: (b, i, 0, 0)),
        per_b((WINDOW, KV_DIM)), per_b((WINDOW, KV_DIM)), per_b((CONV_W - 1, CONV_DIM)),
    ]
    scratch = [pltpu.VMEM((nsub, WINDOW, KV_DIM), F32), pltpu.VMEM((nsub, WINDOW, KV_DIM), F32),
               pltpu.VMEM((nsub, tm + 8, CONV_DIM), F32)]
    return pl.pallas_call(
        functools.partial(_mixer_kernel, tm=tm, nsub=nsub, st=st, has_cache=has_cache),
        grid=(nb // nsub, nt),
        in_specs=in_specs,
        out_specs=out_specs,
        out_shape=out_shape,
        scratch_shapes=scratch,
        compiler_params=pltpu.CompilerParams(
            dimension_semantics=("arbitrary", "arbitrary"), vmem_limit_bytes=VMEM_LIMIT),
        name="mixer_sample" if has_cache else "mixer_prompt",
    )(*args)


def _expert_kernel(be_ref, bq_ref, nused_ref, src_ref, off_ref, n_ref, tot_ref,
                   xsp_hbm, xss_hbm, wgu_hbm, bgu_ref, wd_hbm, bdn_ref, ys_ref,
                   xbuf, sems, wgu_f, wd_f, wsems, wgu_b, wd_b, ptr, *, n_tiles, n_prompt_tiles):
    b = pl.program_id(0)
    nused = nused_ref[0]

    def weight_copies(e):
        return (pltpu.make_async_copy(wgu_hbm.at[e], wgu_f, wsems.at[0]),
                pltpu.make_async_copy(wd_hbm.at[e], wd_f, wsems.at[1]))

    def issue_rows(blk, slot, enabled, between=lambda: None):
        blk = jnp.minimum(blk, pl.num_programs(0) - 1)
        e = be_ref[blk]
        q = bq_ref[blk]
        lo_row = q * ROW_BLOCK
        hi_row = lo_row + ROW_BLOCK
        base = e * n_tiles

        def segment(j, j_limit, src_hbm, ok):
            jc = jnp.minimum(j, j_limit - 1)
            o = off_ref[base + jc]
            inside = ok & (j < j_limit) & (o < hi_row)
            lo = jnp.maximum(o, lo_row)
            ln = jnp.minimum(o + n_ref[base + jc], hi_row) - lo

            @pl.when(inside & (ln > 0))
            def _():
                pltpu.make_async_copy(src_hbm.at[pl.ds(src_ref[base + jc] + (lo - o), ln)],
                                      xbuf.at[slot, pl.ds(lo - lo_row, ln)], sems.at[slot]).start()
            return inside

        j0 = jnp.where(q == 0, 0, ptr[0])

        def cond(j):
            return (enabled & (j < n_prompt_tiles)
                    & (off_ref[base + jnp.minimum(j, n_prompt_tiles - 1)] < hi_row))

        def body(j):
            segment(j, n_prompt_tiles, xsp_hbm, True)
            return j + 1

        jtail = lax.while_loop(cond, body, j0 + INLINE_SEGMENTS)
        result = between()
        count = jnp.int32(0)
        for k in range(INLINE_SEGMENTS):
            count = count + segment(j0 + k, n_prompt_tiles, xsp_hbm, enabled).astype(I32)
        jend = jnp.where(count == INLINE_SEGMENTS, jtail, j0 + count)
        ptr[0] = jnp.where(enabled, jnp.maximum(jend - 1, 0), ptr[0])
        for j in range(n_prompt_tiles, n_tiles):
            segment(j, n_tiles, xss_hbm, enabled)
        return result

    @pl.when(b == 0)
    def _():
        xbuf[...] = jnp.zeros(xbuf.shape, U32)
        for ahead in range(LOOKAHEAD):
            issue_rows(ahead, ahead, ahead < nused)
        for c in weight_copies(be_ref[0]):
            c.start()

    active = b < nused

    @pl.when(active)
    def _():
        slot = b % ROW_SLOTS
        e = be_ref[b]
        rows = jnp.minimum(tot_ref[e] - bq_ref[b] * ROW_BLOCK, ROW_BLOCK)
        pltpu.make_async_copy(xsp_hbm.at[pl.ds(0, rows)], xbuf.at[slot, pl.ds(0, rows)],
                              sems.at[slot]).wait()

        @pl.when(bq_ref[b] == 0)
        def _():
            for c in weight_copies(e):
                c.wait()
            wgu_b[...] = wgu_f[...].astype(BF16)
            wd_b[...] = wd_f[...].astype(BF16)
            nxt = b + (tot_ref[e] + ROW_BLOCK - 1) // ROW_BLOCK

            @pl.when(nxt < nused)
            def _():
                for c in weight_copies(be_ref[jnp.minimum(nxt, pl.num_programs(0) - 1)]):
                    c.start()

        def compute(nrows):
            x = issue_rows(b + LOOKAHEAD, (b + LOOKAHEAD) % ROW_SLOTS, b + LOOKAHEAD < nused,
                           between=lambda: _unpack_rows(xbuf.at[slot, pl.ds(0, nrows)]))
            gu = _dot(x, wgu_b[...]) + bgu_ref[0]
            g = jnp.minimum(gu[:, :D_FF], SWIGLU_LIMIT)
            u = jnp.clip(gu[:, D_FF:], -SWIGLU_LIMIT, SWIGLU_LIMIT)
            act = (u + 1.0) * (g * jax.nn.sigmoid(SWIGLU_ALPHA * g))
            y = _dot(act.astype(BF16), wd_b[...]) + (bdn_ref[0] + _anchor_zero_row(xbuf))
            _pack_rows(y.astype(BF16).astype(F32), ys_ref.at[pl.ds(0, nrows)])
            if nrows < ROW_BLOCK:
                ys_ref[nrows:] = jnp.zeros((ROW_BLOCK - nrows, PACK_S, LANES), U32)

        @pl.when(rows > ROW_BLOCK // 2)
        def _():
            compute(ROW_BLOCK)

        @pl.when(rows <= ROW_BLOCK // 2)
        def _():
            compute(ROW_BLOCK // 2)

    @pl.when(jnp.logical_not(active))
    def _():
        ys_ref[...] = jnp.zeros(ys_ref.shape, U32)


def _experts(plan, xsp, xss, w_gu, b_gu, w_down, b_down, *, n_tiles, n_prompt_tiles, rows):
    nblk = rows // ROW_BLOCK

    def bmap(b, be, *_):
        return (be[b], 0, 0)

    grid_spec = pltpu.PrefetchScalarGridSpec(
        num_scalar_prefetch=7,
        grid=(nblk,),
        in_specs=[pl.BlockSpec(memory_space=pl.ANY),
                  pl.BlockSpec(memory_space=pl.ANY),
                  pl.BlockSpec(memory_space=pl.ANY),
                  pl.BlockSpec((1, 1, 2 * D_FF), bmap),
                  pl.BlockSpec(memory_space=pl.ANY),
                  pl.BlockSpec((1, 1, D_MODEL), bmap)],
        out_specs=pl.BlockSpec((ROW_BLOCK, PACK_S, LANES), lambda b, *_: (b, 0, 0)),
        scratch_shapes=[pltpu.VMEM((ROW_SLOTS + 1, ROW_BLOCK, PACK_S, LANES), U32),
                        pltpu.SemaphoreType.DMA((ROW_SLOTS,)),
                        pltpu.VMEM((D_MODEL, 2 * D_FF), F32), pltpu.VMEM((D_FF, D_MODEL), F32),
                        pltpu.SemaphoreType.DMA((2,)),
                        pltpu.VMEM((D_MODEL, 2 * D_FF), BF16), pltpu.VMEM((D_FF, D_MODEL), BF16),
                        pltpu.SMEM((1,), I32)],
    )
    return pl.pallas_call(
        functools.partial(_expert_kernel, n_tiles=n_tiles, n_prompt_tiles=n_prompt_tiles),
        grid_spec=grid_spec,
        out_shape=jax.ShapeDtypeStruct((rows, PACK_S, LANES), U32),
        compiler_params=pltpu.CompilerParams(
            dimension_semantics=("arbitrary",), vmem_limit_bytes=VMEM_LIMIT),
        name="experts",
    )(plan['block_e'], plan['block_q'], plan['n_used'], plan['src_t'], plan['off_t'], plan['n_t'],
      plan['tot'], xsp, xss, w_gu, b_gu, w_down, b_down)


def _combine_kernel(ysrc_ref, sloc_ref, n_ref, col_ref, h_ref, ys_hbm, o_ref, buf, sems,
                    *, tm, tile0):
    j = pl.program_id(0)
    nt = pl.num_programs(0)
    rt = TOP_K * tm

    def issue(t, slot, enabled):
        base = (tile0 + jnp.minimum(t, nt - 1)) * N_EXPERTS
        for e in range(N_EXPERTS):
            n = n_ref[base + e]

            @pl.when(enabled & (n > 0))
            def _():
                pltpu.make_async_copy(ys_hbm.at[pl.ds(ysrc_ref[base + e], n)],
                                      buf.at[slot, pl.ds(sloc_ref[base + e], n)],
                                      sems.at[slot]).start()

    @pl.when(j == 0)
    def _():
        buf[ROW_SLOTS, 0:8 // PACK_S] = jnp.zeros((8 // PACK_S, PACK_S, LANES), U32)
        for ahead in range(LOOKAHEAD):
            issue(ahead, ahead, ahead < nt)

    slot = j % ROW_SLOTS
    pltpu.make_async_copy(ys_hbm.at[pl.ds(0, rt)], buf.at[slot], sems.at[slot]).wait()
    y = _unpack_rows(buf.at[slot])
    issue(j + LOOKAHEAD, (j + LOOKAHEAD) % ROW_SLOTS, j + LOOKAHEAD < nt)
    col = col_ref[...]
    rio = lax.broadcasted_iota(I32, (tm, rt), 1)
    gmat = jnp.zeros((tm, rt), F32)
    for k in range(TOP_K):
        gmat = gmat + jnp.where(rio == col[:, k:k + 1].astype(I32),
                                col[:, TOP_K + k:TOP_K + k + 1], 0.0)
    o_ref[...] = (h_ref[...] + _anchor_zero_row(buf)) + _dot(gmat.astype(BF16), y)


def _combine(plan, col, h2d, ys, *, tm, tile0):
    t = h2d.shape[0]
    rt = TOP_K * tm
    grid_spec = pltpu.PrefetchScalarGridSpec(
        num_scalar_prefetch=3,
        grid=(t // tm,),
        in_specs=[pl.BlockSpec((tm, LANES), lambda i, *_: (i, 0)),
                  pl.BlockSpec((tm, D_MODEL), lambda i, *_: (i, 0)),
                  pl.BlockSpec(memory_space=pl.ANY)],
        out_specs=pl.BlockSpec((tm, D_MODEL), lambda i, *_: (i, 0)),
        scratch_shapes=[pltpu.VMEM((ROW_SLOTS + 1, rt, PACK_S, LANES), U32),
                        pltpu.SemaphoreType.DMA((ROW_SLOTS,))],
    )
    return pl.pallas_call(
        functools.partial(_combine_kernel, tm=tm, tile0=tile0),
        grid_spec=grid_spec,
        out_shape=jax.ShapeDtypeStruct((t, D_MODEL), F32),
        compiler_params=pltpu.CompilerParams(
            dimension_semantics=("arbitrary",), vmem_limit_bytes=VMEM_LIMIT),
        name="combine",
    )(plan['ysrc'], plan['sloc'], plan['n'], col, h2d, ys)


def kernel(x_prompt, x_sample, cache_k, cache_v, state_conv, norm1_g, w_in, q_norm_g, k_norm_g,
           sinks, conv_w, w_out, norm2_g, w_router, b_router, w_gu, b_gu, w_down, b_down):
    l = 0
    nbp, sp, _ = x_prompt.shape
    nbs, ss, _ = x_sample.shape
    tp, ts = nbp * sp, nbs * ss
    tmp, tms = TOKEN_TILE, ts
    ntp, nts = tp // tmp, ts // tms
    n_tiles = ntp + nts
    n_rows = (tp + ts) * TOP_K
    grp = jnp.arange(ATT_DIM) // HEAD_DIM
    rep_col = jnp.arange(N_KV_HEADS * GROUP_W)
    w = {
        'g1': norm1_g[l][None, :],
        'w_in': w_in[l].astype(BF16),
        'qg': jnp.tile(q_norm_g[l], N_HEADS)[None, :],
        'kg': jnp.tile(k_norm_g[l], N_KV_HEADS)[None, :],
        'bd': (grp[:, None] == grp[None, :]).astype(BF16),
        'rep': (jnp.arange(KV_DIM)[:, None]
                == (rep_col // GROUP_W * HEAD_DIM + rep_col % HEAD_DIM)[None, :]).astype(BF16),
        'sinks': sinks[l][None, :],
        'conv_w': conv_w[l],
        'w_out': w_out[l].astype(BF16),
        'g2': norm2_g[l][None, :],
        'w_router_t': w_router[l].T,
        'b_router_t': b_router[l][:, None],
    }
    hp, xsp, colp, tcp, knp, vnp, unp = _mixer(x_prompt, None, w, tm=tmp, nsub=STREAMS_PER_STEP,
                                               st=tmp)
    cache = (cache_k[l].reshape(nbs, WINDOW, KV_DIM), cache_v[l].reshape(nbs, WINDOW, KV_DIM),
             state_conv[l])
    hs, xss, cols, tcs, kns, vns, uns = _mixer(x_sample, cache, w, tm=ss, nsub=nbs, st=tms)

    n = jnp.concatenate([tcp.reshape(ntp, N_EXPERTS), tcs.reshape(nts, N_EXPERTS)], axis=0)
    tile_base = jnp.concatenate([jnp.arange(ntp, dtype=I32) * (TOP_K * tmp),
                                 jnp.arange(nts, dtype=I32) * (TOP_K * tms)])
    sloc = jnp.cumsum(n, axis=1) - n
    off = jnp.cumsum(n, axis=0) - n
    tot = jnp.sum(n, axis=0)
    padded = (tot + ROW_BLOCK - 1) // ROW_BLOCK * ROW_BLOCK
    ends = jnp.cumsum(padded)
    base = ends - padded
    rows = (-(-n_rows // ROW_BLOCK) + N_EXPERTS) * ROW_BLOCK
    nblk = rows // ROW_BLOCK
    blk0 = jnp.arange(nblk, dtype=I32) * ROW_BLOCK
    block_e = jnp.minimum(jnp.sum((ends[None, :] <= blk0[:, None]).astype(I32), axis=1),
                          N_EXPERTS - 1)
    eq = (block_e[:, None] == jnp.arange(N_EXPERTS, dtype=I32)[None, :]).astype(I32)
    block_q = (blk0 - jnp.sum(eq * base[None, :], axis=1)) // ROW_BLOCK
    plan = {
        'block_e': block_e.astype(I32),
        'block_q': block_q.astype(I32),
        'n_used': (ends[-1:] // ROW_BLOCK).astype(I32),
        'src_t': (tile_base[:, None] + sloc).T.reshape(-1).astype(I32),
        'off_t': off.T.reshape(-1).astype(I32),
        'n_t': n.T.reshape(-1).astype(I32),
        'tot': tot.astype(I32),
        'ysrc': (base[None, :] + off).reshape(-1).astype(I32),
        'sloc': sloc.reshape(-1).astype(I32),
        'n': n.reshape(-1).astype(I32),
    }

    ys = _experts(plan, xsp.reshape(-1, PACK_S, LANES), xss.reshape(-1, PACK_S, LANES),
                  w_gu[l], b_gu[l][:, None, :], w_down[l], b_down[l][:, None, :],
                  n_tiles=n_tiles, n_prompt_tiles=ntp, rows=rows)
    yp = _combine(plan, colp.reshape(tp, LANES), hp.reshape(tp, D_MODEL), ys, tm=tmp, tile0=0)
    ysm = _combine(plan, cols.reshape(ts, LANES), hs.reshape(ts, D_MODEL), ys, tm=tms, tile0=ntp)

    kv5 = lambda a: a.reshape(1, a.shape[0], WINDOW, N_KV_HEADS, HEAD_DIM)
    return (yp.reshape(nbp, sp, D_MODEL), ysm.reshape(nbs, ss, D_MODEL),
            kv5(knp), kv5(vnp), unp[None], kv5(kns), kv5(vns), uns[None])
```

```python
import functools

import jax
import jax.numpy as jnp
from jax import lax
from jax.experimental import pallas as pl
from jax.experimental.pallas import tpu as pltpu

D_MODEL = 1024
CHUNK = 64
HEAD_DIM = 64
N_HEADS = 8
N_KV_HEADS = 2
Q_PER_KV = N_HEADS // N_KV_HEADS
ATT_DIM = N_HEADS * HEAD_DIM
KV_DIM = N_KV_HEADS * HEAD_DIM
GROUP_W = Q_PER_KV * HEAD_DIM
CONV_DIM = D_MODEL - ATT_DIM
CONV_W = 3
WINDOW = 128
IN_DIM = ATT_DIM + 2 * KV_DIM + 3 * CONV_DIM
N_EXPERTS = 32
TOP_K = 4
D_FF = D_MODEL
SWIGLU_LIMIT = 7.0
SWIGLU_ALPHA = 1.702
EPS = 1e-6
NEG = -1e30

LANES = 128
HALF = D_MODEL // 2
PACK_S = HALF // LANES
TOKEN_TILE = 256
STREAMS_PER_STEP = 4
ROW_BLOCK = 512
INLINE_SEGMENTS = 20
LOOKAHEAD = 2
ROW_SLOTS = LOOKAHEAD + 1
VMEM_LIMIT = 56 * 1024 * 1024
MIXER_VMEM_LIMIT = 60 * 1024 * 1024

F32 = jnp.float32
BF16 = jnp.bfloat16
U32 = jnp.uint32
I32 = jnp.int32


def _rms_rows(x, g):
    return x * lax.rsqrt(jnp.mean(x * x, axis=-1, keepdims=True) + EPS) * g


def _split_bf16(x):
    hi = x.astype(BF16)
    lo = (x - hi.astype(F32)).astype(BF16)
    return hi, lo


def _dot(a, b):
    return jnp.dot(a, b, preferred_element_type=F32)


def _dot_nt(a, b):
    return lax.dot_general(a, b, (((1,), (1,)), ((), ())), preferred_element_type=F32)


def _head_rms(t, bd, g):
    hi, lo = _split_bf16(t * t)
    ssq = _dot(hi, bd) + _dot(lo, bd)
    return t * lax.rsqrt(ssq * (1.0 / HEAD_DIM) + EPS) * g


def _pack_rows(vals, out_ref):
    r = vals.shape[0]
    bits = pltpu.bitcast(vals, U32)
    word = (bits[:, :HALF] >> 16) | (bits[:, HALF:] & jnp.uint32(0xFFFF0000))
    flat = out_ref.reshape(r * PACK_S, LANES)
    for s in range(PACK_S):
        flat[pl.ds(s, r, stride=PACK_S), :] = word[:, s * LANES:(s + 1) * LANES]


def _unpack_rows(packed_ref):
    r = packed_ref.shape[0]
    flat = packed_ref.reshape(r * PACK_S, LANES)
    lo, hi = [], []
    for s in range(PACK_S):
        w = flat[pl.ds(s, r, stride=PACK_S), :]
        lo.append(pltpu.bitcast(w << 16, F32))
        hi.append(pltpu.bitcast(w & jnp.uint32(0xFFFF0000), F32))
    return jnp.concatenate(lo + hi, axis=1).astype(BF16)


def _anchor_zero_row(staging_ref):
    z = staging_ref.at[ROW_SLOTS, pl.ds(0, 8 // PACK_S)].reshape(8, LANES)[...]
    return jnp.concatenate([pltpu.bitcast(z, F32)[0:1, :]] * (D_MODEL // LANES), axis=1)


def _attention(i, qn, kall, vall, rep, sink_ref, *, mask_history):
    tm = qn.shape[0]
    k4 = _dot(kall.astype(BF16), rep).astype(BF16)
    v4 = _dot(vall.astype(BF16), rep).astype(BF16)
    lane_grp = lax.broadcasted_iota(I32, (CHUNK, GROUP_W), 1) // HEAD_DIM
    row_grp = lax.broadcasted_iota(I32, (Q_PER_KV * CHUNK, 1), 0) // CHUNK
    span = WINDOW + CHUNK
    att_rows = []
    for c in range(tm // CHUNK):
        per_kv = []
        for hk in range(N_KV_HEADS):
            qc = qn[c * CHUNK:(c + 1) * CHUNK, hk * GROUP_W:(hk + 1) * GROUP_W]
            qs = jnp.concatenate([jnp.where(lane_grp == g, qc, 0.0) for g in range(Q_PER_KV)],
                                 axis=0).astype(BF16)
            kw = k4[c * CHUNK:c * CHUNK + span, hk * GROUP_W:(hk + 1) * GROUP_W]
            vw = v4[c * CHUNK:c * CHUNK + span, hk * GROUP_W:(hk + 1) * GROUP_W]
            s = _dot_nt(qs, kw) * (HEAD_DIM ** -0.5)
            if mask_history and c * CHUNK < WINDOW:
                kcol = lax.broadcasted_iota(I32, (1, span), 1)
                s = jnp.where(kcol >= jnp.where(i == 0, WINDOW - c * CHUNK, 0), s, NEG)
            sink = jnp.zeros((Q_PER_KV * CHUNK, 1), F32)
            for g in range(Q_PER_KV):
                sink = jnp.where(row_grp == g, sink_ref[0, hk * Q_PER_KV + g], sink)
            m = jnp.maximum(jnp.max(s, axis=-1, keepdims=True), sink)
            p = jnp.exp(s - m)
            den = jnp.sum(p, axis=-1, keepdims=True) + jnp.exp(sink - m)
            o = _dot(p.astype(BF16), vw) / den
            oc = jnp.zeros((CHUNK, GROUP_W), F32)
            for g in range(Q_PER_KV):
                oc = oc + jnp.where(lane_grp == g, o[g * CHUNK:(g + 1) * CHUNK, :], 0.0)
            per_kv.append(oc)
        att_rows.append(jnp.concatenate(per_kv, axis=1))
    return jnp.concatenate(att_rows, axis=0)


def _mixer_kernel(*refs, tm, nsub, st, has_cache):
    refs = list(refs)
    x_ref = refs.pop(0)
    if has_cache:
        ck_ref, cv_ref, st_ref = refs.pop(0), refs.pop(0), refs.pop(0)
    (g1_ref, win_ref, qg_ref, kg_ref, bd_ref, rep_ref, sink_ref, cw_ref, wout_ref, g2_ref,
     wrt_ref, brt_ref, upper_ref, lower_ref,
     h_ref, xs_ref, col_ref, tcnt_ref, knew_ref, vnew_ref, unew_ref, kprev, vprev, ubuf) = refs
    i = pl.program_id(1)
    ntok = nsub * tm

    @pl.when(i == 0)
    def _():
        ubuf[:, 0:8, :] = jnp.zeros((nsub, 8, CONV_DIM), F32)
        if has_cache:
            kprev[...] = ck_ref[...]
            vprev[...] = cv_ref[...]
            ubuf[:, 6:8, :] = st_ref[...]
        else:
            kprev[...] = jnp.zeros(kprev.shape, F32)
            vprev[...] = jnp.zeros(vprev.shape, F32)

    x = jnp.concatenate([x_ref[sub] for sub in range(nsub)], axis=0) if nsub > 1 else x_ref[0]
    xn = _rms_rows(x, g1_ref[...])
    proj = _dot(xn.astype(BF16), win_ref[...])
    o0 = ATT_DIM
    o1 = o0 + KV_DIM
    o2 = o1 + KV_DIM
    o3 = o2 + CONV_DIM
    o4 = o3 + CONV_DIM
    bd = bd_ref[...]
    qn = _head_rms(proj[:, :o0], bd, qg_ref[...])
    kn = _head_rms(proj[:, o0:o1], bd[:KV_DIM, :KV_DIM], kg_ref[...])
    v = proj[:, o1:o2]
    bg = proj[:, o2:o3]
    u = proj[:, o3:o4] * proj[:, o4:]

    rep = rep_ref[...]
    cw = cw_ref[...]
    att_parts, cv_parts = [], []
    for sub in range(nsub):
        r0, r1 = sub * tm, (sub + 1) * tm
        kall = jnp.concatenate([kprev[sub], kn[r0:r1]], axis=0)
        vall = jnp.concatenate([vprev[sub], v[r0:r1]], axis=0)
        att_parts.append(_attention(i, qn[r0:r1], kall, vall, rep, sink_ref,
                                    mask_history=not has_cache))
        kprev[sub] = kall[tm:, :]
        vprev[sub] = vall[tm:, :]
        knew_ref[sub] = kall[tm:, :]
        vnew_ref[sub] = vall[tm:, :]

        us = u[r0:r1]
        ubuf[sub, 8:8 + tm, :] = us
        cvs = ubuf[sub, 6:6 + tm, :] * cw[0:1, :] + ubuf[sub, 7:7 + tm, :] * cw[1:2, :]
        cv_parts.append(cvs + us * cw[2:3, :])
        ubuf[sub, 0:8, :] = ubuf[sub, tm:tm + 8, :]
        unew_ref[sub] = us[tm - (CONV_W - 1):, :]
    att = jnp.concatenate(att_parts, axis=0) if nsub > 1 else att_parts[0]
    cv = jnp.concatenate(cv_parts, axis=0) if nsub > 1 else cv_parts[0]

    mix = jnp.concatenate([att, bg * cv], axis=1).astype(BF16)
    h = x + _dot(mix, wout_ref[...])
    for sub in range(nsub):
        h_ref[sub] = h[sub * tm:(sub + 1) * tm]
    hn = _rms_rows(h, g2_ref[...])

    hn_hi, hn_lo = _split_bf16(hn)
    wr_hi, wr_lo = _split_bf16(wrt_ref[...])
    logits = (_dot_nt(wr_hi, hn_hi) + _dot_nt(wr_hi, hn_lo) + _dot_nt(wr_lo, hn_hi)) + brt_ref[...]
    eio = lax.broadcasted_iota(I32, (N_EXPERTS, ntok), 0)
    work = logits
    tops, sels = [], []
    for _ in range(TOP_K):
        mk = jnp.max(work, axis=0, keepdims=True)
        ik = jnp.min(jnp.where(work == mk, eio, N_EXPERTS), axis=0, keepdims=True)
        sel = eio == ik
        work = jnp.where(sel, -jnp.inf, work)
        tops.append(mk)
        sels.append(sel)
    es = [jnp.exp(t - tops[0]) for t in tops]
    esum = es[0] + es[1] + es[2] + es[3]

    onehot = jnp.zeros((N_EXPERTS, ntok), F32)
    for sel in sels:
        onehot = onehot + jnp.where(sel, 1.0, 0.0)
    onehot_b = onehot.astype(BF16)
    before = _dot(onehot_b, upper_ref[...])
    smaller = _dot(lower_ref[...], onehot_b)
    nst = ntok // st
    starts = [jnp.broadcast_to(jnp.sum(smaller[:, k * st:(k + 1) * st], axis=1, keepdims=True),
                               (N_EXPERTS, st)) for k in range(nst)]
    where_to = before + (jnp.concatenate(starts, axis=1) if nst > 1 else starts[0])
    pos = [jnp.sum(jnp.where(sel, where_to, 0.0), axis=0, keepdims=True) for sel in sels]
    rio = lax.broadcasted_iota(I32, (TOP_K * st, st), 0)
    for k in range(nst):
        c0, c1 = k * st, (k + 1) * st
        hit = rio == pos[0][:, c0:c1].astype(I32)
        for slot in range(1, TOP_K):
            hit = hit | (rio == pos[slot][:, c0:c1].astype(I32))
        perm = jnp.where(hit, 1.0, 0.0).astype(BF16)
        _pack_rows(_dot(perm, hn_hi[c0:c1]), xs_ref.at[k, 0])
        tcnt_ref[k, 0] = _dot_nt(jnp.ones((1, st), BF16), onehot_b[:, c0:c1]).astype(I32)

    rows8 = jnp.concatenate(pos + [e / esum for e in es], axis=0)
    sq = jnp.concatenate([rows8, jnp.zeros((LANES - 2 * TOP_K, ntok), F32)], axis=0)
    cols = jnp.transpose(sq)
    for sub in range(nsub):
        col_ref[sub] = cols[sub * tm:(sub + 1) * tm]


def _mixer(x, cache, w, *, tm, nsub, st):
    nb, s, _ = x.shape
    nt = s // tm
    nst = nsub * tm // st
    has_cache = cache is not None
    full = lambda shape: pl.BlockSpec(shape, lambda b, i: (0,) * len(shape))
    per_b = lambda shape: pl.BlockSpec((nsub,) + shape, lambda b, i: (b, 0, 0))
    tile = lambda last: pl.BlockSpec((nsub, tm, last), lambda b, i: (b, i, 0))

    in_specs = [tile(D_MODEL)]
    args = [x]
    if has_cache:
        in_specs += [per_b((WINDOW, KV_DIM)), per_b((WINDOW, KV_DIM)), per_b((CONV_W - 1, CONV_DIM))]
        args += list(cache)
    tio = jnp.arange(nsub * tm)
    eio = jnp.arange(N_EXPERTS)
    upper = ((tio[:, None] < tio[None, :]) & (tio[:, None] // st == tio[None, :] // st)).astype(BF16)
    lower = (eio[None, :] < eio[:, None]).astype(BF16)
    in_specs += [
        full((1, D_MODEL)), full((D_MODEL, IN_DIM)), full((1, ATT_DIM)),
        full((1, KV_DIM)), full((ATT_DIM, ATT_DIM)), full((KV_DIM, N_KV_HEADS * GROUP_W)),
        pl.BlockSpec(memory_space=pltpu.SMEM),
        full((CONV_W, CONV_DIM)), full((D_MODEL, D_MODEL)), full((1, D_MODEL)),
        full((N_EXPERTS, D_MODEL)), full((N_EXPERTS, 1)), full((nsub * tm, nsub * tm)),
        full((N_EXPERTS, N_EXPERTS)),
    ]
    args += [w['g1'], w['w_in'], w['qg'], w['kg'], w['bd'], w['rep'], w['sinks'], w['conv_w'],
             w['w_out'], w['g2'], w['w_router_t'], w['b_router_t'], upper, lower]
    rt = TOP_K * st
    nsort = nb // nsub * nst
    out_shape = [
        jax.ShapeDtypeStruct((nb, s, D_MODEL), F32),
        jax.ShapeDtypeStruct((nsort, nt, rt, PACK_S, LANES), U32),
        jax.ShapeDtypeStruct((nb, s, LANES), F32),
        jax.ShapeDtypeStruct((nsort, nt, 1, N_EXPERTS), I32),
        jax.ShapeDtypeStruct((nb, WINDOW, KV_DIM), F32),
        jax.ShapeDtypeStruct((nb, WINDOW, KV_DIM), F32),
        jax.ShapeDtypeStruct((nb, CONV_W - 1, CONV_DIM), F32),
    ]
    out_specs = [
        tile(D_MODEL),
        pl.BlockSpec((nst, 1, rt, PACK_S, LANES), lambda b, i: (b, i, 0, 0, 0)),
        tile(LANES),
        pl.BlockSpec((nst, 1, 1, N_EXPERTS), lambda b, i: (b, i, 0, 0)),
        per_b((WINDOW, KV_DIM)), per_b((WINDOW, KV_DIM)), per_b((CONV_W - 1, CONV_DIM)),
    ]
    scratch = [pltpu.VMEM((nsub, WINDOW, KV_DIM), F32), pltpu.VMEM((nsub, WINDOW, KV_DIM), F32),
               pltpu.VMEM((nsub, tm + 8, CONV_DIM), F32)]
    return pl.pallas_call(
        functools.partial(_mixer_kernel, tm=tm, nsub=nsub, st=st, has_cache=has_cache),
        grid=(nb // nsub, nt),
        in_specs=in_specs,
        out_specs=out_specs,
        out_shape=out_shape,
        scratch_shapes=scratch,
        compiler_params=pltpu.CompilerParams(
            dimension_semantics=("arbitrary", "arbitrary"), vmem_limit_bytes=MIXER_VMEM_LIMIT),
        name="mixer_sample" if has_cache else "mixer_prompt",
    )(*args)


def _expert_kernel(be_ref, bq_ref, nused_ref, src_ref, off_ref, n_ref, tot_ref,
                   xsp_hbm, xss_hbm, wgu_hbm, bgu_ref, wd_hbm, bdn_ref, ys_ref,
                   xbuf, sems, wgu_f, wd_f, wsems, ptr, wcur, *, n_tiles, n_prompt_tiles):
    b = pl.program_id(0)
    nused = nused_ref[0]

    def weight_copies(e, ws):
        return (pltpu.make_async_copy(wgu_hbm.at[e], wgu_f.at[ws], wsems.at[0, ws]),
                pltpu.make_async_copy(wd_hbm.at[e], wd_f.at[ws], wsems.at[1, ws]))

    def issue_rows(blk, slot, enabled, between=lambda: None):
        blk = jnp.minimum(blk, pl.num_programs(0) - 1)
        e = be_ref[blk]
        q = bq_ref[blk]
        lo_row = q * ROW_BLOCK
        hi_row = lo_row + ROW_BLOCK
        base = e * n_tiles

        def segment(j, j_limit, src_hbm, ok):
            jc = jnp.minimum(j, j_limit - 1)
            o = off_ref[base + jc]
            inside = ok & (j < j_limit) & (o < hi_row)
            lo = jnp.maximum(o, lo_row)
            ln = jnp.minimum(o + n_ref[base + jc], hi_row) - lo

            @pl.when(inside & (ln > 0))
            def _():
                pltpu.make_async_copy(src_hbm.at[pl.ds(src_ref[base + jc] + (lo - o), ln)],
                                      xbuf.at[slot, pl.ds(lo - lo_row, ln)], sems.at[slot]).start()
            return inside

        j0 = jnp.where(q == 0, 0, ptr[0])

        def cond(j):
            return (enabled & (j < n_prompt_tiles)
                    & (off_ref[base + jnp.minimum(j, n_prompt_tiles - 1)] < hi_row))

        def body(j):
            segment(j, n_prompt_tiles, xsp_hbm, True)
            return j + 1

        jtail = lax.while_loop(cond, body, j0 + INLINE_SEGMENTS)
        result = between()
        count = jnp.int32(0)
        for k in range(INLINE_SEGMENTS):
            count = count + segment(j0 + k, n_prompt_tiles, xsp_hbm, enabled).astype(I32)
        jend = jnp.where(count == INLINE_SEGMENTS, jtail, j0 + count)
        ptr[0] = jnp.where(enabled, jnp.maximum(jend - 1, 0), ptr[0])
        for j in range(n_prompt_tiles, n_tiles):
            segment(j, n_tiles, xss_hbm, enabled)
        return result

    @pl.when(b == 0)
    def _():
        xbuf[...] = jnp.zeros(xbuf.shape, U32)
        for ahead in range(LOOKAHEAD):
            issue_rows(ahead, ahead, ahead < nused)
        for c in weight_copies(be_ref[0], 0):
            c.start()
        wcur[0] = 1

    active = b < nused

    @pl.when(active)
    def _():
        slot = b % ROW_SLOTS
        e = be_ref[b]
        rows = jnp.minimum(tot_ref[e] - bq_ref[b] * ROW_BLOCK, ROW_BLOCK)
        pltpu.make_async_copy(xsp_hbm.at[pl.ds(0, rows)], xbuf.at[slot, pl.ds(0, rows)],
                              sems.at[slot]).wait()

        @pl.when(bq_ref[b] == 0)
        def _():
            ws = 1 - wcur[0]
            wcur[0] = ws
            for c in weight_copies(e, ws):
                c.wait()
            nxt = b + (tot_ref[e] + ROW_BLOCK - 1) // ROW_BLOCK

            @pl.when(nxt < nused)
            def _():
                for c in weight_copies(be_ref[jnp.minimum(nxt, pl.num_programs(0) - 1)], 1 - ws):
                    c.start()

        ws = wcur[0]

        def compute(nrows):
            x = issue_rows(b + LOOKAHEAD, (b + LOOKAHEAD) % ROW_SLOTS, b + LOOKAHEAD < nused,
                           between=lambda: _unpack_rows(xbuf.at[slot, pl.ds(0, nrows)]))
            gu = _dot(x, wgu_f[ws]) + bgu_ref[pl.ds(e, 1), :]
            g = jnp.minimum(gu[:, :D_FF], SWIGLU_LIMIT)
            u = jnp.clip(gu[:, D_FF:], -SWIGLU_LIMIT, SWIGLU_LIMIT)
            act = (u + 1.0) * (g * jax.nn.sigmoid(SWIGLU_ALPHA * g))
            y = (_dot(act.astype(BF16), wd_f[ws])
                 + (bdn_ref[pl.ds(e, 1), :] + _anchor_zero_row(xbuf)))
            _pack_rows(y.astype(BF16).astype(F32), ys_ref.at[pl.ds(0, nrows)])
            if nrows < ROW_BLOCK:
                ys_ref[nrows:] = jnp.zeros((ROW_BLOCK - nrows, PACK_S, LANES), U32)

        @pl.when(rows > ROW_BLOCK // 2)
        def _():
            compute(ROW_BLOCK)

        @pl.when(rows <= ROW_BLOCK // 2)
        def _():
            compute(ROW_BLOCK // 2)

    @pl.when(jnp.logical_not(active))
    def _():
        ys_ref[...] = jnp.zeros(ys_ref.shape, U32)


def _experts(plan, xsp, xss, w_gu, b_gu, w_down, b_down, *, n_tiles, n_prompt_tiles, rows):
    nblk = rows // ROW_BLOCK

    grid_spec = pltpu.PrefetchScalarGridSpec(
        num_scalar_prefetch=7,
        grid=(nblk,),
        in_specs=[pl.BlockSpec(memory_space=pl.ANY),
                  pl.BlockSpec(memory_space=pl.ANY),
                  pl.BlockSpec(memory_space=pl.ANY),
                  pl.BlockSpec((N_EXPERTS, 2 * D_FF), lambda b, *_: (0, 0)),
                  pl.BlockSpec(memory_space=pl.ANY),
                  pl.BlockSpec((N_EXPERTS, D_MODEL), lambda b, *_: (0, 0))],
        out_specs=pl.BlockSpec((ROW_BLOCK, PACK_S, LANES), lambda b, *_: (b, 0, 0)),
        scratch_shapes=[pltpu.VMEM((ROW_SLOTS + 1, ROW_BLOCK, PACK_S, LANES), U32),
                        pltpu.SemaphoreType.DMA((ROW_SLOTS,)),
                        pltpu.VMEM((2, D_MODEL, 2 * D_FF), F32), pltpu.VMEM((2, D_FF, D_MODEL), F32),
                        pltpu.SemaphoreType.DMA((2, 2)),
                        pltpu.SMEM((1,), I32), pltpu.SMEM((1,), I32)],
    )
    return pl.pallas_call(
        functools.partial(_expert_kernel, n_tiles=n_tiles, n_prompt_tiles=n_prompt_tiles),
        grid_spec=grid_spec,
        out_shape=jax.ShapeDtypeStruct((rows, PACK_S, LANES), U32),
        compiler_params=pltpu.CompilerParams(
            dimension_semantics=("arbitrary",), vmem_limit_bytes=VMEM_LIMIT),
        name="experts",
    )(plan['block_e'], plan['block_q'], plan['n_used'], plan['src_t'], plan['off_t'], plan['n_t'],
      plan['tot'], xsp, xss, w_gu, b_gu, w_down, b_down)


def _combine_kernel(ysrc_ref, sloc_ref, n_ref, col_ref, h_ref, ys_hbm, o_ref, buf, sems,
                    *, tm, tile0):
    j = pl.program_id(0)
    nt = pl.num_programs(0)
    rt = TOP_K * tm

    def issue(t, slot, enabled):
        base = (tile0 + jnp.minimum(t, nt - 1)) * N_EXPERTS
        for e in range(N_EXPERTS):
            n = n_ref[base + e]

            @pl.when(enabled & (n > 0))
            def _():
                pltpu.make_async_copy(ys_hbm.at[pl.ds(ysrc_ref[base + e], n)],
                                      buf.at[slot, pl.ds(sloc_ref[base + e], n)],
                                      sems.at[slot]).start()

    @pl.when(j == 0)
    def _():
        buf[ROW_SLOTS, 0:8 // PACK_S] = jnp.zeros((8 // PACK_S, PACK_S, LANES), U32)
        for ahead in range(LOOKAHEAD):
            issue(ahead, ahead, ahead < nt)

    slot = j % ROW_SLOTS
    pltpu.make_async_copy(ys_hbm.at[pl.ds(0, rt)], buf.at[slot], sems.at[slot]).wait()
    y = _unpack_rows(buf.at[slot])
    issue(j + LOOKAHEAD, (j + LOOKAHEAD) % ROW_SLOTS, j + LOOKAHEAD < nt)
    col = col_ref[...]
    rio = lax.broadcasted_iota(I32, (tm, rt), 1)
    gmat = jnp.zeros((tm, rt), F32)
    for k in range(TOP_K):
        gmat = gmat + jnp.where(rio == col[:, k:k + 1].astype(I32),
                                col[:, TOP_K + k:TOP_K + k + 1], 0.0)
    o_ref[...] = (h_ref[...] + _anchor_zero_row(buf)) + _dot(gmat.astype(BF16), y)


def _combine(plan, col, h2d, ys, *, tm, tile0):
    t = h2d.shape[0]
    rt = TOP_K * tm
    grid_spec = pltpu.PrefetchScalarGridSpec(
        num_scalar_prefetch=3,
        grid=(t // tm,),
        in_specs=[pl.BlockSpec((tm, LANES), lambda i, *_: (i, 0)),
                  pl.BlockSpec((tm, D_MODEL), lambda i, *_: (i, 0)),
                  pl.BlockSpec(memory_space=pl.ANY)],
        out_specs=pl.BlockSpec((tm, D_MODEL), lambda i, *_: (i, 0)),
        scratch_shapes=[pltpu.VMEM((ROW_SLOTS + 1, rt, PACK_S, LANES), U32),
                        pltpu.SemaphoreType.DMA((ROW_SLOTS,))],
    )
    return pl.pallas_call(
        functools.partial(_combine_kernel, tm=tm, tile0=tile0),
        grid_spec=grid_spec,
        out_shape=jax.ShapeDtypeStruct((t, D_MODEL), F32),
        compiler_params=pltpu.CompilerParams(
            dimension_semantics=("arbitrary",), vmem_limit_bytes=VMEM_LIMIT),
        name="combine",
    )(plan['ysrc'], plan['sloc'], plan['n'], col, h2d, ys)


def kernel(x_prompt, x_sample, cache_k, cache_v, state_conv, norm1_g, w_in, q_norm_g, k_norm_g,
           sinks, conv_w, w_out, norm2_g, w_router, b_router, w_gu, b_gu, w_down, b_down):
    l = 0
    nbp, sp, _ = x_prompt.shape
    nbs, ss, _ = x_sample.shape
    tp, ts = nbp * sp, nbs * ss
    tmp, tms = TOKEN_TILE, ts
    ntp, nts = tp // tmp, ts // tms
    n_tiles = ntp + nts
    n_rows = (tp + ts) * TOP_K
    grp = jnp.arange(ATT_DIM) // HEAD_DIM
    rep_col = jnp.arange(N_KV_HEADS * GROUP_W)
    w = {
        'g1': norm1_g[l][None, :],
        'w_in': w_in[l].astype(BF16),
        'qg': jnp.tile(q_norm_g[l], N_HEADS)[None, :],
        'kg': jnp.tile(k_norm_g[l], N_KV_HEADS)[None, :],
        'bd': (grp[:, None] == grp[None, :]).astype(BF16),
        'rep': (jnp.arange(KV_DIM)[:, None]
                == (rep_col // GROUP_W * HEAD_DIM + rep_col % HEAD_DIM)[None, :]).astype(BF16),
        'sinks': sinks[l][None, :],
        'conv_w': conv_w[l],
        'w_out': w_out[l].astype(BF16),
        'g2': norm2_g[l][None, :],
        'w_router_t': w_router[l].T,
        'b_router_t': b_router[l][:, None],
    }
    hp, xsp, colp, tcp, knp, vnp, unp = _mixer(x_prompt, None, w, tm=tmp, nsub=STREAMS_PER_STEP,
                                               st=tmp)
    cache = (cache_k[l].reshape(nbs, WINDOW, KV_DIM), cache_v[l].reshape(nbs, WINDOW, KV_DIM),
             state_conv[l])
    hs, xss, cols, tcs, kns, vns, uns = _mixer(x_sample, cache, w, tm=ss, nsub=nbs, st=tms)

    n = jnp.concatenate([tcp.reshape(ntp, N_EXPERTS), tcs.reshape(nts, N_EXPERTS)], axis=0)
    tile_base = jnp.concatenate([jnp.arange(ntp, dtype=I32) * (TOP_K * tmp),
                                 jnp.arange(nts, dtype=I32) * (TOP_K * tms)])
    sloc = jnp.cumsum(n, axis=1) - n
    off = jnp.cumsum(n, axis=0) - n
    tot = jnp.sum(n, axis=0)
    padded = (tot + ROW_BLOCK - 1) // ROW_BLOCK * ROW_BLOCK
    ends = jnp.cumsum(padded)
    base = ends - padded
    rows = (-(-n_rows // ROW_BLOCK) + N_EXPERTS) * ROW_BLOCK
    nblk = rows // ROW_BLOCK
    blk0 = jnp.arange(nblk, dtype=I32) * ROW_BLOCK
    block_e = jnp.minimum(jnp.sum((ends[None, :] <= blk0[:, None]).astype(I32), axis=1),
                          N_EXPERTS - 1)
    eq = (block_e[:, None] == jnp.arange(N_EXPERTS, dtype=I32)[None, :]).astype(I32)
    block_q = (blk0 - jnp.sum(eq * base[None, :], axis=1)) // ROW_BLOCK
    plan = {
        'block_e': block_e.astype(I32),
        'block_q': block_q.astype(I32),
        'n_used': (ends[-1:] // ROW_BLOCK).astype(I32),
        'src_t': (tile_base[:, None] + sloc).T.reshape(-1).astype(I32),
        'off_t': off.T.reshape(-1).astype(I32),
        'n_t': n.T.reshape(-1).astype(I32),
        'tot': tot.astype(I32),
        'ysrc': (base[None, :] + off).reshape(-1).astype(I32),
        'sloc': sloc.reshape(-1).astype(I32),
        'n': n.reshape(-1).astype(I32),
    }

    ys = _experts(plan, xsp.reshape(-1, PACK_S, LANES), xss.reshape(-1, PACK_S, LANES),
                  w_gu[l], b_gu[l], w_down[l], b_down[l],
                  n_tiles=n_tiles, n_prompt_tiles=ntp, rows=rows)
    yp = _combine(plan, colp.reshape(tp, LANES), hp.reshape(tp, D_MODEL), ys, tm=tmp, tile0=0)
    ysm = _combine(plan, cols.reshape(ts, LANES), hs.reshape(ts, D_MODEL), ys, tm=tms, tile0=ntp)

    kv5 = lambda a: a.reshape(1, a.shape[0], WINDOW, N_KV_HEADS, HEAD_DIM)
    return (yp.reshape(nbp, sp, D_MODEL), ysm.reshape(nbs, ss, D_MODEL),
            kv5(knp), kv5(vnp), unp[None], kv5(kns), kv5(vns), uns[None])
```

```python
import functools

import jax
import jax.numpy as jnp
from jax import lax
from jax.experimental import pallas as pl
from jax.experimental.pallas import tpu as pltpu

D_MODEL = 1024
CHUNK = 64
HEAD_DIM = 64
N_HEADS = 8
N_KV_HEADS = 2
Q_PER_KV = N_HEADS // N_KV_HEADS
ATT_DIM = N_HEADS * HEAD_DIM
KV_DIM = N_KV_HEADS * HEAD_DIM
GROUP_W = Q_PER_KV * HEAD_DIM
CONV_DIM = D_MODEL - ATT_DIM
CONV_W = 3
WINDOW = 128
IN_DIM = ATT_DIM + 2 * KV_DIM + 3 * CONV_DIM
N_EXPERTS = 32
TOP_K = 4
D_FF = D_MODEL
SWIGLU_LIMIT = 7.0
SWIGLU_ALPHA = 1.702
EPS = 1e-6
NEG = -1e30

LANES = 128
HALF = D_MODEL // 2
PACK_S = HALF // LANES
TOKEN_TILE = 256
STREAMS_PER_STEP = 4
ROW_BLOCK = 512
TAIL_ROWS = 128
INLINE_SEGMENTS = 20
LOOKAHEAD = 2
ROW_SLOTS = LOOKAHEAD + 1
VMEM_LIMIT = 56 * 1024 * 1024
MIXER_VMEM_LIMIT = 60 * 1024 * 1024

F32 = jnp.float32
BF16 = jnp.bfloat16
U32 = jnp.uint32
I32 = jnp.int32


def _rms_rows(x, g):
    return x * lax.rsqrt(jnp.mean(x * x, axis=-1, keepdims=True) + EPS) * g


def _split_bf16(x):
    hi = x.astype(BF16)
    lo = (x - hi.astype(F32)).astype(BF16)
    return hi, lo


def _dot(a, b):
    return jnp.dot(a, b, preferred_element_type=F32)


def _dot_nt(a, b):
    return lax.dot_general(a, b, (((1,), (1,)), ((), ())), preferred_element_type=F32)


def _head_rms(t, bd, g):
    hi, lo = _split_bf16(t * t)
    ssq = _dot(hi, bd) + _dot(lo, bd)
    return t * lax.rsqrt(ssq * (1.0 / HEAD_DIM) + EPS) * g


def _pack_rows(vals, out_ref):
    r = vals.shape[0]
    bits = pltpu.bitcast(vals, U32)
    word = (bits[:, :HALF] >> 16) | (bits[:, HALF:] & jnp.uint32(0xFFFF0000))
    flat = out_ref.reshape(r * PACK_S, LANES)
    for s in range(PACK_S):
        flat[pl.ds(s, r, stride=PACK_S), :] = word[:, s * LANES:(s + 1) * LANES]


def _unpack_rows(packed_ref):
    r = packed_ref.shape[0]
    flat = packed_ref.reshape(r * PACK_S, LANES)
    lo, hi = [], []
    for s in range(PACK_S):
        w = flat[pl.ds(s, r, stride=PACK_S), :]
        lo.append(pltpu.bitcast(w << 16, F32))
        hi.append(pltpu.bitcast(w & jnp.uint32(0xFFFF0000), F32))
    return jnp.concatenate(lo + hi, axis=1).astype(BF16)


def _anchor_zero_row(staging_ref):
    z = staging_ref.at[ROW_SLOTS, pl.ds(0, 8 // PACK_S)].reshape(8, LANES)[...]
    return jnp.concatenate([pltpu.bitcast(z, F32)[0:1, :]] * (D_MODEL // LANES), axis=1)


def _attention(i, qn, kall, vall, rep, sink_ref, *, mask_history):
    tm = qn.shape[0]
    k4 = _dot(kall.astype(BF16), rep).astype(BF16)
    v4 = _dot(vall.astype(BF16), rep).astype(BF16)
    lane_grp = lax.broadcasted_iota(I32, (CHUNK, GROUP_W), 1) // HEAD_DIM
    row_grp = lax.broadcasted_iota(I32, (Q_PER_KV * CHUNK, 1), 0) // CHUNK
    span = WINDOW + CHUNK
    att_rows = []
    for c in range(tm // CHUNK):
        per_kv = []
        for hk in range(N_KV_HEADS):
            qc = qn[c * CHUNK:(c + 1) * CHUNK, hk * GROUP_W:(hk + 1) * GROUP_W]
            qs = jnp.concatenate([jnp.where(lane_grp == g, qc, 0.0) for g in range(Q_PER_KV)],
                                 axis=0).astype(BF16)
            kw = k4[c * CHUNK:c * CHUNK + span, hk * GROUP_W:(hk + 1) * GROUP_W]
            vw = v4[c * CHUNK:c * CHUNK + span, hk * GROUP_W:(hk + 1) * GROUP_W]
            s = _dot_nt(qs, kw) * (HEAD_DIM ** -0.5)
            if mask_history and c * CHUNK < WINDOW:
                kcol = lax.broadcasted_iota(I32, (1, span), 1)
                s = jnp.where(kcol >= jnp.where(i == 0, WINDOW - c * CHUNK, 0), s, NEG)
            sink = jnp.zeros((Q_PER_KV * CHUNK, 1), F32)
            for g in range(Q_PER_KV):
                sink = jnp.where(row_grp == g, sink_ref[0, hk * Q_PER_KV + g], sink)
            m = jnp.maximum(jnp.max(s, axis=-1, keepdims=True), sink)
            p = jnp.exp(s - m)
            den = jnp.sum(p, axis=-1, keepdims=True) + jnp.exp(sink - m)
            o = _dot(p.astype(BF16), vw) / den
            oc = jnp.zeros((CHUNK, GROUP_W), F32)
            for g in range(Q_PER_KV):
                oc = oc + jnp.where(lane_grp == g, o[g * CHUNK:(g + 1) * CHUNK, :], 0.0)
            per_kv.append(oc)
        att_rows.append(jnp.concatenate(per_kv, axis=1))
    return jnp.concatenate(att_rows, axis=0)


def _mixer_kernel(*refs, tm, nsub, st, has_cache):
    refs = list(refs)
    x_ref = refs.pop(0)
    if has_cache:
        ck_ref, cv_ref, st_ref = refs.pop(0), refs.pop(0), refs.pop(0)
    (g1_ref, win_ref, qg_ref, kg_ref, bd_ref, rep_ref, sink_ref, cw_ref, wout_ref, g2_ref,
     wrt_ref, brt_ref, upper_ref, lower_ref,
     h_ref, xs_ref, col_ref, tcnt_ref, knew_ref, vnew_ref, unew_ref, kprev, vprev, ubuf) = refs
    i = pl.program_id(1)
    ntok = nsub * tm

    @pl.when(i == 0)
    def _():
        ubuf[:, 0:8, :] = jnp.zeros((nsub, 8, CONV_DIM), F32)
        if has_cache:
            for sub in range(nsub):
                kprev[sub] = jnp.concatenate([ck_ref[sub, :, hk, :] for hk in range(N_KV_HEADS)], axis=1)
                vprev[sub] = jnp.concatenate([cv_ref[sub, :, hk, :] for hk in range(N_KV_HEADS)], axis=1)
            ubuf[:, 6:8, :] = st_ref[...]
        else:
            kprev[...] = jnp.zeros(kprev.shape, F32)
            vprev[...] = jnp.zeros(vprev.shape, F32)

    x = jnp.concatenate([x_ref[sub] for sub in range(nsub)], axis=0) if nsub > 1 else x_ref[0]
    xn = _rms_rows(x, g1_ref[...])
    proj = _dot(xn.astype(BF16), win_ref[...])
    o0 = ATT_DIM
    o1 = o0 + KV_DIM
    o2 = o1 + KV_DIM
    o3 = o2 + CONV_DIM
    o4 = o3 + CONV_DIM
    bd = bd_ref[...]
    qn = _head_rms(proj[:, :o0], bd, qg_ref[...])
    kn = _head_rms(proj[:, o0:o1], bd[:KV_DIM, :KV_DIM], kg_ref[...])
    v = proj[:, o1:o2]
    bg = proj[:, o2:o3]
    u = proj[:, o3:o4] * proj[:, o4:]

    rep = rep_ref[...]
    cw = cw_ref[...]
    att_parts, cv_parts = [], []
    for sub in range(nsub):
        r0, r1 = sub * tm, (sub + 1) * tm
        kall = jnp.concatenate([kprev[sub], kn[r0:r1]], axis=0)
        vall = jnp.concatenate([vprev[sub], v[r0:r1]], axis=0)
        att_parts.append(_attention(i, qn[r0:r1], kall, vall, rep, sink_ref,
                                    mask_history=not has_cache))
        kprev[sub] = kall[tm:, :]
        vprev[sub] = vall[tm:, :]
        if has_cache:
            for hk in range(N_KV_HEADS):
                knew_ref[sub, :, hk, :] = kall[tm:, hk * HEAD_DIM:(hk + 1) * HEAD_DIM]
                vnew_ref[sub, :, hk, :] = vall[tm:, hk * HEAD_DIM:(hk + 1) * HEAD_DIM]
        else:
            knew_ref[sub] = kall[tm:, :]
            vnew_ref[sub] = vall[tm:, :]

        us = u[r0:r1]
        ubuf[sub, 8:8 + tm, :] = us
        cvs = ubuf[sub, 6:6 + tm, :] * cw[0:1, :] + ubuf[sub, 7:7 + tm, :] * cw[1:2, :]
        cv_parts.append(cvs + us * cw[2:3, :])
        ubuf[sub, 0:8, :] = ubuf[sub, tm:tm + 8, :]
        unew_ref[sub] = us[tm - (CONV_W - 1):, :]
    att = jnp.concatenate(att_parts, axis=0) if nsub > 1 else att_parts[0]
    cv = jnp.concatenate(cv_parts, axis=0) if nsub > 1 else cv_parts[0]

    mix = jnp.concatenate([att, bg * cv], axis=1).astype(BF16)
    h = x + _dot(mix, wout_ref[...])
    for sub in range(nsub):
        h_ref[sub] = h[sub * tm:(sub + 1) * tm]
    hn = _rms_rows(h, g2_ref[...])

    hn_hi, hn_lo = _split_bf16(hn)
    wr_hi, wr_lo = _split_bf16(wrt_ref[...])
    logits = (_dot_nt(wr_hi, hn_hi) + _dot_nt(wr_hi, hn_lo) + _dot_nt(wr_lo, hn_hi)) + brt_ref[...]
    eio = lax.broadcasted_iota(I32, (N_EXPERTS, ntok), 0)
    work = logits
    tops, sels = [], []
    for _ in range(TOP_K):
        mk = jnp.max(work, axis=0, keepdims=True)
        ik = jnp.min(jnp.where(work == mk, eio, N_EXPERTS), axis=0, keepdims=True)
        sel = eio == ik
        work = jnp.where(sel, -jnp.inf, work)
        tops.append(mk)
        sels.append(sel)
    es = [jnp.exp(t - tops[0]) for t in tops]
    esum = es[0] + es[1] + es[2] + es[3]

    onehot = jnp.zeros((N_EXPERTS, ntok), F32)
    for sel in sels:
        onehot = onehot + jnp.where(sel, 1.0, 0.0)
    onehot_b = onehot.astype(BF16)
    before = _dot(onehot_b, upper_ref[...])
    smaller = _dot(lower_ref[...], onehot_b)
    nst = ntok // st
    starts = [jnp.broadcast_to(jnp.sum(smaller[:, k * st:(k + 1) * st], axis=1, keepdims=True),
                               (N_EXPERTS, st)) for k in range(nst)]
    where_to = before + (jnp.concatenate(starts, axis=1) if nst > 1 else starts[0])
    pos = [jnp.sum(jnp.where(sel, where_to, 0.0), axis=0, keepdims=True) for sel in sels]
    rio = lax.broadcasted_iota(I32, (TOP_K * st, st), 0)
    for k in range(nst):
        c0, c1 = k * st, (k + 1) * st
        hit = rio == pos[0][:, c0:c1].astype(I32)
        for slot in range(1, TOP_K):
            hit = hit | (rio == pos[slot][:, c0:c1].astype(I32))
        perm = jnp.where(hit, 1.0, 0.0).astype(BF16)
        _pack_rows(_dot(perm, hn_hi[c0:c1]), xs_ref.at[k, 0])
        tcnt_ref[k, 0] = _dot_nt(jnp.ones((1, st), BF16), onehot_b[:, c0:c1]).astype(I32)

    rows8 = jnp.concatenate(pos + [e / esum for e in es], axis=0)
    sq = jnp.concatenate([rows8, jnp.zeros((LANES - 2 * TOP_K, ntok), F32)], axis=0)
    cols = jnp.transpose(sq)
    for sub in range(nsub):
        col_ref[sub] = cols[sub * tm:(sub + 1) * tm]


def _mixer(x, cache, w, *, tm, nsub, st):
    nb, s, _ = x.shape
    nt = s // tm
    nst = nsub * tm // st
    has_cache = cache is not None
    full = lambda shape: pl.BlockSpec(shape, lambda b, i: (0,) * len(shape))
    per_b = lambda shape: pl.BlockSpec((nsub,) + shape, lambda b, i: (b, 0, 0))
    tile = lambda last: pl.BlockSpec((nsub, tm, last), lambda b, i: (b, i, 0))
    if has_cache:
        kv_shape = (WINDOW, N_KV_HEADS, HEAD_DIM)
        kv_spec = pl.BlockSpec((nsub,) + kv_shape, lambda b, i: (b, 0, 0, 0))
    else:
        kv_shape = (WINDOW, KV_DIM)
        kv_spec = per_b(kv_shape)

    in_specs = [tile(D_MODEL)]
    args = [x]
    if has_cache:
        in_specs += [kv_spec, kv_spec, per_b((CONV_W - 1, CONV_DIM))]
        args += list(cache)
    tio = jnp.arange(nsub * tm)
    eio = jnp.arange(N_EXPERTS)
    upper = ((tio[:, None] < tio[None, :]) & (tio[:, None] // st == tio[None, :] // st)).astype(BF16)
    lower = (eio[None, :] < eio[:, None]).astype(BF16)
    in_specs += [
        full((1, D_MODEL)), full((D_MODEL, IN_DIM)), full((1, ATT_DIM)),
        full((1, KV_DIM)), full((ATT_DIM, ATT_DIM)), full((KV_DIM, N_KV_HEADS * GROUP_W)),
        pl.BlockSpec(memory_space=pltpu.SMEM),
        full((CONV_W, CONV_DIM)), full((D_MODEL, D_MODEL)), full((1, D_MODEL)),
        full((N_EXPERTS, D_MODEL)), full((N_EXPERTS, 1)), full((nsub * tm, nsub * tm)),
        full((N_EXPERTS, N_EXPERTS)),
    ]
    args += [w['g1'], w['w_in'], w['qg'], w['kg'], w['bd'], w['rep'], w['sinks'], w['conv_w'],
             w['w_out'], w['g2'], w['w_router_t'], w['b_router_t'], upper, lower]
    rt = TOP_K * st
    nsort = nb // nsub * nst
    out_shape = [
        jax.ShapeDtypeStruct((nb, s, D_MODEL), F32),
        jax.ShapeDtypeStruct((nsort, nt, rt, PACK_S, LANES), U32),
        jax.ShapeDtypeStruct((nb, s, LANES), F32),
        jax.ShapeDtypeStruct((nsort, nt, 1, N_EXPERTS), I32),
        jax.ShapeDtypeStruct((nb,) + kv_shape, F32),
        jax.ShapeDtypeStruct((nb,) + kv_shape, F32),
        jax.ShapeDtypeStruct((nb, CONV_W - 1, CONV_DIM), F32),
    ]
    out_specs = [
        tile(D_MODEL),
        pl.BlockSpec((nst, 1, rt, PACK_S, LANES), lambda b, i: (b, i, 0, 0, 0)),
        tile(LANES),
        pl.BlockSpec((nst, 1, 1, N_EXPERTS), lambda b, i: (b, i, 0, 0)),
        kv_spec, kv_spec, per_b((CONV_W - 1, CONV_DIM)),
    ]
    scratch = [pltpu.VMEM((nsub, WINDOW, KV_DIM), F32), pltpu.VMEM((nsub, WINDOW, KV_DIM), F32),
               pltpu.VMEM((nsub, tm + 8, CONV_DIM), F32)]
    return pl.pallas_call(
        functools.partial(_mixer_kernel, tm=tm, nsub=nsub, st=st, has_cache=has_cache),
        grid=(nb // nsub, nt),
        in_specs=in_specs,
        out_specs=out_specs,
        out_shape=out_shape,
        scratch_shapes=scratch,
        compiler_params=pltpu.CompilerParams(
            dimension_semantics=("arbitrary", "arbitrary"), vmem_limit_bytes=MIXER_VMEM_LIMIT),
        name="mixer_sample" if has_cache else "mixer_prompt",
    )(*args)


def _expert_kernel(be_ref, bq_ref, nused_ref, src_ref, off_ref, n_ref, tot_ref,
                   xsp_hbm, xss_hbm, wgu_hbm, bgu_ref, wd_hbm, bdn_ref, ys_ref,
                   xbuf, sems, wgu_f, wd_f, wsems, ptr, wcur, *, n_tiles, n_prompt_tiles):
    b = pl.program_id(0)
    nused = nused_ref[0]

    def weight_copies(e, ws):
        return (pltpu.make_async_copy(wgu_hbm.at[e], wgu_f.at[ws], wsems.at[0, ws]),
                pltpu.make_async_copy(wd_hbm.at[e], wd_f.at[ws], wsems.at[1, ws]))

    def issue_rows(blk, slot, enabled, between=lambda: None):
        blk = jnp.minimum(blk, pl.num_programs(0) - 1)
        e = be_ref[blk]
        q = bq_ref[blk]
        lo_row = q * ROW_BLOCK
        hi_row = lo_row + ROW_BLOCK
        base = e * n_tiles

        def segment(j, j_limit, src_hbm, ok):
            jc = jnp.minimum(j, j_limit - 1)
            o = off_ref[base + jc]
            inside = ok & (j < j_limit) & (o < hi_row)
            lo = jnp.maximum(o, lo_row)
            ln = jnp.minimum(o + n_ref[base + jc], hi_row) - lo

            @pl.when(inside & (ln > 0))
            def _():
                pltpu.make_async_copy(src_hbm.at[pl.ds(src_ref[base + jc] + (lo - o), ln)],
                                      xbuf.at[slot, pl.ds(lo - lo_row, ln)], sems.at[slot]).start()
            return inside

        j0 = jnp.where(q == 0, 0, ptr[0])

        def cond(j):
            return (enabled & (j < n_prompt_tiles)
                    & (off_ref[base + jnp.minimum(j, n_prompt_tiles - 1)] < hi_row))

        def body(j):
            segment(j, n_prompt_tiles, xsp_hbm, True)
            return j + 1

        jtail = lax.while_loop(cond, body, j0 + INLINE_SEGMENTS)
        result = between()
        count = jnp.int32(0)
        for k in range(INLINE_SEGMENTS):
            count = count + segment(j0 + k, n_prompt_tiles, xsp_hbm, enabled).astype(I32)
        jend = jnp.where(count == INLINE_SEGMENTS, jtail, j0 + count)
        ptr[0] = jnp.where(enabled, jnp.maximum(jend - 1, 0), ptr[0])
        for j in range(n_prompt_tiles, n_tiles):
            segment(j, n_tiles, xss_hbm, enabled)
        return result

    @pl.when(b == 0)
    def _():
        xbuf[...] = jnp.zeros(xbuf.shape, U32)
        for ahead in range(LOOKAHEAD):
            issue_rows(ahead, ahead, ahead < nused)
        for c in weight_copies(be_ref[0], 0):
            c.start()
        wcur[0] = 1

    active = b < nused

    @pl.when(active)
    def _():
        slot = b % ROW_SLOTS
        e = be_ref[b]
        rows = jnp.minimum(tot_ref[e] - bq_ref[b] * ROW_BLOCK, ROW_BLOCK)
        pltpu.make_async_copy(xsp_hbm.at[pl.ds(0, rows)], xbuf.at[slot, pl.ds(0, rows)],
                              sems.at[slot]).wait()

        @pl.when(bq_ref[b] == 0)
        def _():
            ws = 1 - wcur[0]
            wcur[0] = ws
            for c in weight_copies(e, ws):
                c.wait()
            nxt = b + (tot_ref[e] + ROW_BLOCK - 1) // ROW_BLOCK

            @pl.when(nxt < nused)
            def _():
                for c in weight_copies(be_ref[jnp.minimum(nxt, pl.num_programs(0) - 1)], 1 - ws):
                    c.start()

        ws = wcur[0]

        def compute(nrows):
            x = issue_rows(b + LOOKAHEAD, (b + LOOKAHEAD) % ROW_SLOTS, b + LOOKAHEAD < nused,
                           between=lambda: _unpack_rows(xbuf.at[slot, pl.ds(0, nrows)]))
            gu = _dot(x, wgu_f[ws]) + bgu_ref[pl.ds(e, 1), :]
            g = jnp.minimum(gu[:, :D_FF], SWIGLU_LIMIT)
            u = jnp.clip(gu[:, D_FF:], -SWIGLU_LIMIT, SWIGLU_LIMIT)
            act = (u + 1.0) * (g * jax.nn.sigmoid(SWIGLU_ALPHA * g))
            y = (_dot(act.astype(BF16), wd_f[ws])
                 + (bdn_ref[pl.ds(e, 1), :] + _anchor_zero_row(xbuf)))
            _pack_rows(y.astype(BF16).astype(F32), ys_ref.at[pl.ds(0, nrows)])
            if nrows < ROW_BLOCK:
                ys_ref[nrows:] = jnp.zeros((ROW_BLOCK - nrows, PACK_S, LANES), U32)

        for piece in range(ROW_BLOCK // TAIL_ROWS):
            top = (piece + 1) * TAIL_ROWS
            pl.when((rows > top - TAIL_ROWS) & (rows <= top))(functools.partial(compute, top))

    @pl.when(jnp.logical_not(active))
    def _():
        ys_ref[...] = jnp.zeros(ys_ref.shape, U32)


def _experts(plan, xsp, xss, w_gu, b_gu, w_down, b_down, *, n_tiles, n_prompt_tiles, rows):
    nblk = rows // ROW_BLOCK

    grid_spec = pltpu.PrefetchScalarGridSpec(
        num_scalar_prefetch=7,
        grid=(nblk,),
        in_specs=[pl.BlockSpec(memory_space=pl.ANY),
                  pl.BlockSpec(memory_space=pl.ANY),
                  pl.BlockSpec(memory_space=pl.ANY),
                  pl.BlockSpec((N_EXPERTS, 2 * D_FF), lambda b, *_: (0, 0)),
                  pl.BlockSpec(memory_space=pl.ANY),
                  pl.BlockSpec((N_EXPERTS, D_MODEL), lambda b, *_: (0, 0))],
        out_specs=pl.BlockSpec((ROW_BLOCK, PACK_S, LANES), lambda b, *_: (b, 0, 0)),
        scratch_shapes=[pltpu.VMEM((ROW_SLOTS + 1, ROW_BLOCK, PACK_S, LANES), U32),
                        pltpu.SemaphoreType.DMA((ROW_SLOTS,)),
                        pltpu.VMEM((2, D_MODEL, 2 * D_FF), F32), pltpu.VMEM((2, D_FF, D_MODEL), F32),
                        pltpu.SemaphoreType.DMA((2, 2)),
                        pltpu.SMEM((1,), I32), pltpu.SMEM((1,), I32)],
    )
    return pl.pallas_call(
        functools.partial(_expert_kernel, n_tiles=n_tiles, n_prompt_tiles=n_prompt_tiles),
        grid_spec=grid_spec,
        out_shape=jax.ShapeDtypeStruct((rows, PACK_S, LANES), U32),
        compiler_params=pltpu.CompilerParams(
            dimension_semantics=("arbitrary",), vmem_limit_bytes=VMEM_LIMIT),
        name="experts",
    )(plan['block_e'], plan['block_q'], plan['n_used'], plan['src_t'], plan['off_t'], plan['n_t'],
      plan['tot'], xsp, xss, w_gu, b_gu, w_down, b_down)


def _combine_kernel(ysrc_ref, sloc_ref, n_ref, col_ref, h_ref, ys_hbm, o_ref, buf, sems,
                    *, tm, tile0):
    j = pl.program_id(0)
    nt = pl.num_programs(0)
    rt = TOP_K * tm

    def issue(t, slot, enabled):
        base = (tile0 + jnp.minimum(t, nt - 1)) * N_EXPERTS
        for e in range(N_EXPERTS):
            n = n_ref[base + e]

            @pl.when(enabled & (n > 0))
            def _():
                pltpu.make_async_copy(ys_hbm.at[pl.ds(ysrc_ref[base + e], n)],
                                      buf.at[slot, pl.ds(sloc_ref[base + e], n)],
                                      sems.at[slot]).start()

    @pl.when(j == 0)
    def _():
        buf[ROW_SLOTS, 0:8 // PACK_S] = jnp.zeros((8 // PACK_S, PACK_S, LANES), U32)
        for ahead in range(LOOKAHEAD):
            issue(ahead, ahead, ahead < nt)

    slot = j % ROW_SLOTS
    pltpu.make_async_copy(ys_hbm.at[pl.ds(0, rt)], buf.at[slot], sems.at[slot]).wait()
    y = _unpack_rows(buf.at[slot])
    issue(j + LOOKAHEAD, (j + LOOKAHEAD) % ROW_SLOTS, j + LOOKAHEAD < nt)
    col = col_ref[...]
    rio = lax.broadcasted_iota(I32, (tm, rt), 1)
    gmat = jnp.zeros((tm, rt), F32)
    for k in range(TOP_K):
        gmat = gmat + jnp.where(rio == col[:, k:k + 1].astype(I32),
                                col[:, TOP_K + k:TOP_K + k + 1], 0.0)
    o_ref[...] = (h_ref[...] + _anchor_zero_row(buf)) + _dot(gmat.astype(BF16), y)


def _combine(plan, col, h2d, ys, *, tm, tile0):
    t = h2d.shape[0]
    rt = TOP_K * tm
    grid_spec = pltpu.PrefetchScalarGridSpec(
        num_scalar_prefetch=3,
        grid=(t // tm,),
        in_specs=[pl.BlockSpec((tm, LANES), lambda i, *_: (i, 0)),
                  pl.BlockSpec((tm, D_MODEL), lambda i, *_: (i, 0)),
                  pl.BlockSpec(memory_space=pl.ANY)],
        out_specs=pl.BlockSpec((tm, D_MODEL), lambda i, *_: (i, 0)),
        scratch_shapes=[pltpu.VMEM((ROW_SLOTS + 1, rt, PACK_S, LANES), U32),
                        pltpu.SemaphoreType.DMA((ROW_SLOTS,))],
    )
    return pl.pallas_call(
        functools.partial(_combine_kernel, tm=tm, tile0=tile0),
        grid_spec=grid_spec,
        out_shape=jax.ShapeDtypeStruct((t, D_MODEL), F32),
        compiler_params=pltpu.CompilerParams(
            dimension_semantics=("arbitrary",), vmem_limit_bytes=VMEM_LIMIT),
        name="combine",
    )(plan['ysrc'], plan['sloc'], plan['n'], col, h2d, ys)


def kernel(x_prompt, x_sample, cache_k, cache_v, state_conv, norm1_g, w_in, q_norm_g, k_norm_g,
           sinks, conv_w, w_out, norm2_g, w_router, b_router, w_gu, b_gu, w_down, b_down):
    l = 0
    nbp, sp, _ = x_prompt.shape
    nbs, ss, _ = x_sample.shape
    tp, ts = nbp * sp, nbs * ss
    tmp, tms = TOKEN_TILE, ts
    ntp, nts = tp // tmp, ts // tms
    n_tiles = ntp + nts
    n_rows = (tp + ts) * TOP_K
    grp = jnp.arange(ATT_DIM) // HEAD_DIM
    rep_col = jnp.arange(N_KV_HEADS * GROUP_W)
    w = {
        'g1': norm1_g[l][None, :],
        'w_in': w_in[l].astype(BF16),
        'qg': jnp.tile(q_norm_g[l], N_HEADS)[None, :],
        'kg': jnp.tile(k_norm_g[l], N_KV_HEADS)[None, :],
        'bd': (grp[:, None] == grp[None, :]).astype(BF16),
        'rep': (jnp.arange(KV_DIM)[:, None]
                == (rep_col // GROUP_W * HEAD_DIM + rep_col % HEAD_DIM)[None, :]).astype(BF16),
        'sinks': sinks[l][None, :],
        'conv_w': conv_w[l],
        'w_out': w_out[l].astype(BF16),
        'g2': norm2_g[l][None, :],
        'w_router_t': w_router[l].T,
        'b_router_t': b_router[l][:, None],
    }
    hp, xsp, colp, tcp, knp, vnp, unp = _mixer(x_prompt, None, w, tm=tmp, nsub=STREAMS_PER_STEP,
                                               st=tmp)
    cache = (cache_k[l], cache_v[l], state_conv[l])
    hs, xss, cols, tcs, kns, vns, uns = _mixer(x_sample, cache, w, tm=ss, nsub=nbs, st=tms)

    n = jnp.concatenate([tcp.reshape(ntp, N_EXPERTS), tcs.reshape(nts, N_EXPERTS)], axis=0)
    tile_base = jnp.concatenate([jnp.arange(ntp, dtype=I32) * (TOP_K * tmp),
                                 jnp.arange(nts, dtype=I32) * (TOP_K * tms)])
    sloc = jnp.cumsum(n, axis=1) - n
    off = jnp.cumsum(n, axis=0) - n
    tot = jnp.sum(n, axis=0)
    padded = (tot + ROW_BLOCK - 1) // ROW_BLOCK * ROW_BLOCK
    ends = jnp.cumsum(padded)
    base = ends - padded
    rows = (-(-n_rows // ROW_BLOCK) + N_EXPERTS) * ROW_BLOCK
    nblk = rows // ROW_BLOCK
    blk0 = jnp.arange(nblk, dtype=I32) * ROW_BLOCK
    block_e = jnp.minimum(jnp.sum((ends[None, :] <= blk0[:, None]).astype(I32), axis=1),
                          N_EXPERTS - 1)
    eq = (block_e[:, None] == jnp.arange(N_EXPERTS, dtype=I32)[None, :]).astype(I32)
    block_q = (blk0 - jnp.sum(eq * base[None, :], axis=1)) // ROW_BLOCK
    plan = {
        'block_e': block_e.astype(I32),
        'block_q': block_q.astype(I32),
        'n_used': (ends[-1:] // ROW_BLOCK).astype(I32),
        'src_t': (tile_base[:, None] + sloc).T.reshape(-1).astype(I32),
        'off_t': off.T.reshape(-1).astype(I32),
        'n_t': n.T.reshape(-1).astype(I32),
        'tot': tot.astype(I32),
        'ysrc': (base[None, :] + off).reshape(-1).astype(I32),
        'sloc': sloc.reshape(-1).astype(I32),
        'n': n.reshape(-1).astype(I32),
    }

    ys = _experts(plan, xsp.reshape(-1, PACK_S, LANES), xss.reshape(-1, PACK_S, LANES),
                  w_gu[l], b_gu[l], w_down[l], b_down[l],
                  n_tiles=n_tiles, n_prompt_tiles=ntp, rows=rows)
    yp = _combine(plan, colp.reshape(tp, LANES), hp.reshape(tp, D_MODEL), ys, tm=tmp, tile0=0)
    ysm = _combine(plan, cols.reshape(ts, LANES), hs.reshape(ts, D_MODEL), ys, tm=tms, tile0=ntp)

    kv5 = (1, nbp, WINDOW, N_KV_HEADS, HEAD_DIM)
    return (yp.reshape(nbp, sp, D_MODEL), ysm.reshape(nbs, ss, D_MODEL),
            knp.reshape(kv5), vnp.reshape(kv5), unp[None], kns[None], vns[None], uns[None])
```

```python
import functools

import jax
import jax.numpy as jnp
from jax import lax
from jax.experimental import pallas as pl
from jax.experimental.pallas import tpu as pltpu

D_MODEL = 1024
CHUNK = 64
HEAD_DIM = 64
N_HEADS = 8
N_KV_HEADS = 2
Q_PER_KV = N_HEADS // N_KV_HEADS
ATT_DIM = N_HEADS * HEAD_DIM
KV_DIM = N_KV_HEADS * HEAD_DIM
GROUP_W = Q_PER_KV * HEAD_DIM
CONV_DIM = D_MODEL - ATT_DIM
CONV_W = 3
WINDOW = 128
IN_DIM = ATT_DIM + 2 * KV_DIM + 3 * CONV_DIM
N_EXPERTS = 32
TOP_K = 4
D_FF = D_MODEL
SWIGLU_LIMIT = 7.0
SWIGLU_ALPHA = 1.702
EPS = 1e-6
NEG = -1e30

LANES = 128
HALF = D_MODEL // 2
PACK_S = HALF // LANES
TOKEN_TILE = 256
STREAMS_PER_STEP = 4
ROW_BLOCK = 512
TAIL_ROWS = 128
INLINE_SEGMENTS = 20
LOOKAHEAD = 2
ROW_SLOTS = LOOKAHEAD + 1
VMEM_LIMIT = 56 * 1024 * 1024
MIXER_VMEM_LIMIT = 60 * 1024 * 1024

F32 = jnp.float32
BF16 = jnp.bfloat16
U32 = jnp.uint32
I32 = jnp.int32


def _rms_rows(x, g):
    return x * lax.rsqrt(jnp.mean(x * x, axis=-1, keepdims=True) + EPS) * g


def _split_bf16(x):
    hi = x.astype(BF16)
    lo = (x - hi.astype(F32)).astype(BF16)
    return hi, lo


def _dot(a, b):
    return jnp.dot(a, b, preferred_element_type=F32)


def _dot_nt(a, b):
    return lax.dot_general(a, b, (((1,), (1,)), ((), ())), preferred_element_type=F32)


def _head_rms(t, bd, g):
    ssq = _dot((t * t).astype(BF16), bd)
    return t * lax.rsqrt(ssq * (1.0 / HEAD_DIM) + EPS) * g


def _pack_rows(vals, out_ref):
    r = vals.shape[0]
    bits = pltpu.bitcast(vals, U32)
    word = (bits[:, :HALF] >> 16) | (bits[:, HALF:] & jnp.uint32(0xFFFF0000))
    flat = out_ref.reshape(r * PACK_S, LANES)
    for s in range(PACK_S):
        flat[pl.ds(s, r, stride=PACK_S), :] = word[:, s * LANES:(s + 1) * LANES]


def _unpack_rows(packed_ref):
    r = packed_ref.shape[0]
    flat = packed_ref.reshape(r * PACK_S, LANES)
    lo, hi = [], []
    for s in range(PACK_S):
        w = flat[pl.ds(s, r, stride=PACK_S), :]
        lo.append(pltpu.bitcast(w << 16, F32))
        hi.append(pltpu.bitcast(w & jnp.uint32(0xFFFF0000), F32))
    return jnp.concatenate(lo + hi, axis=1).astype(BF16)


def _anchor_zero_row(staging_ref):
    z = staging_ref.at[ROW_SLOTS, pl.ds(0, 8 // PACK_S)].reshape(8, LANES)[...]
    return jnp.concatenate([pltpu.bitcast(z, F32)[0:1, :]] * (D_MODEL // LANES), axis=1)


def _attention(i, qn, kall, vall, rep, sink_ref, *, mask_history):
    tm = qn.shape[0]
    k4 = _dot(kall.astype(BF16), rep).astype(BF16)
    v4 = _dot(vall.astype(BF16), rep).astype(BF16)
    lane_grp = lax.broadcasted_iota(I32, (CHUNK, GROUP_W), 1) // HEAD_DIM
    row_grp = lax.broadcasted_iota(I32, (Q_PER_KV * CHUNK, 1), 0) // CHUNK
    span = WINDOW + CHUNK
    att_rows = []
    for c in range(tm // CHUNK):
        per_kv = []
        for hk in range(N_KV_HEADS):
            qc = qn[c * CHUNK:(c + 1) * CHUNK, hk * GROUP_W:(hk + 1) * GROUP_W]
            qs = jnp.concatenate([jnp.where(lane_grp == g, qc, 0.0) for g in range(Q_PER_KV)],
                                 axis=0).astype(BF16)
            kw = k4[c * CHUNK:c * CHUNK + span, hk * GROUP_W:(hk + 1) * GROUP_W]
            vw = v4[c * CHUNK:c * CHUNK + span, hk * GROUP_W:(hk + 1) * GROUP_W]
            s = _dot_nt(qs, kw) * (HEAD_DIM ** -0.5)
            if mask_history and c * CHUNK < WINDOW:
                kcol = lax.broadcasted_iota(I32, (1, span), 1)
                s = jnp.where(kcol >= jnp.where(i == 0, WINDOW - c * CHUNK, 0), s, NEG)
            sink = jnp.zeros((Q_PER_KV * CHUNK, 1), F32)
            for g in range(Q_PER_KV):
                sink = jnp.where(row_grp == g, sink_ref[0, hk * Q_PER_KV + g], sink)
            m = jnp.maximum(jnp.max(s, axis=-1, keepdims=True), sink)
            p = jnp.exp(s - m)
            den = jnp.sum(p, axis=-1, keepdims=True) + jnp.exp(sink - m)
            o = _dot(p.astype(BF16), vw) / den
            oc = jnp.zeros((CHUNK, GROUP_W), F32)
            for g in range(Q_PER_KV):
                oc = oc + jnp.where(lane_grp == g, o[g * CHUNK:(g + 1) * CHUNK, :], 0.0)
            per_kv.append(oc)
        att_rows.append(jnp.concatenate(per_kv, axis=1))
    return jnp.concatenate(att_rows, axis=0)


def _mixer_kernel(*refs, tm, nsub, st, has_cache):
    refs = list(refs)
    x_ref = refs.pop(0)
    if has_cache:
        ck_ref, cv_ref, st_ref = refs.pop(0), refs.pop(0), refs.pop(0)
    (g1_ref, win_ref, qg_ref, kg_ref, bd_ref, rep_ref, sink_ref, cw_ref, wout_ref, g2_ref,
     wrt_ref, brt_ref, upper_ref, lower_ref,
     h_ref, xs_ref, col_ref, tcnt_ref, knew_ref, vnew_ref, unew_ref, kprev, vprev, ubuf) = refs
    i = pl.program_id(1)
    ntok = nsub * tm

    @pl.when(i == 0)
    def _():
        ubuf[:, 0:8, :] = jnp.zeros((nsub, 8, CONV_DIM), F32)
        if has_cache:
            kprev[...] = ck_ref[...]
            vprev[...] = cv_ref[...]
            ubuf[:, 6:8, :] = st_ref[...]
        else:
            kprev[...] = jnp.zeros(kprev.shape, F32)
            vprev[...] = jnp.zeros(vprev.shape, F32)

    x = jnp.concatenate([x_ref[sub] for sub in range(nsub)], axis=0) if nsub > 1 else x_ref[0]
    xn = _rms_rows(x, g1_ref[...])
    proj = _dot(xn.astype(BF16), win_ref[...])
    o0 = ATT_DIM
    o1 = o0 + KV_DIM
    o2 = o1 + KV_DIM
    o3 = o2 + CONV_DIM
    o4 = o3 + CONV_DIM
    bd = bd_ref[...]
    qn = _head_rms(proj[:, :o0], bd, qg_ref[...])
    kn = _head_rms(proj[:, o0:o1], bd[:KV_DIM, :KV_DIM], kg_ref[...])
    v = proj[:, o1:o2]
    bg = proj[:, o2:o3]
    u = proj[:, o3:o4] * proj[:, o4:]

    rep = rep_ref[...]
    cw = cw_ref[...]
    att_parts, cv_parts = [], []
    for sub in range(nsub):
        r0, r1 = sub * tm, (sub + 1) * tm
        kall = jnp.concatenate([kprev[sub], kn[r0:r1]], axis=0)
        vall = jnp.concatenate([vprev[sub], v[r0:r1]], axis=0)
        att_parts.append(_attention(i, qn[r0:r1], kall, vall, rep, sink_ref,
                                    mask_history=not has_cache))
        kprev[sub] = kall[tm:, :]
        vprev[sub] = vall[tm:, :]
        knew_ref[sub] = kall[tm:, :]
        vnew_ref[sub] = vall[tm:, :]

        us = u[r0:r1]
        ubuf[sub, 8:8 + tm, :] = us
        cvs = ubuf[sub, 6:6 + tm, :] * cw[0:1, :] + ubuf[sub, 7:7 + tm, :] * cw[1:2, :]
        cv_parts.append(cvs + us * cw[2:3, :])
        ubuf[sub, 0:8, :] = ubuf[sub, tm:tm + 8, :]
        unew_ref[sub] = us[tm - (CONV_W - 1):, :]
    att = jnp.concatenate(att_parts, axis=0) if nsub > 1 else att_parts[0]
    cv = jnp.concatenate(cv_parts, axis=0) if nsub > 1 else cv_parts[0]

    mix = jnp.concatenate([att, bg * cv], axis=1).astype(BF16)
    h = x + _dot(mix, wout_ref[...])
    for sub in range(nsub):
        h_ref[sub] = h[sub * tm:(sub + 1) * tm]
    hn = _rms_rows(h, g2_ref[...])

    hn_hi, hn_lo = _split_bf16(hn)
    wr_hi, wr_lo = _split_bf16(wrt_ref[...])
    logits = (_dot_nt(wr_hi, hn_hi) + _dot_nt(wr_hi, hn_lo) + _dot_nt(wr_lo, hn_hi)) + brt_ref[...]
    eio = lax.broadcasted_iota(I32, (N_EXPERTS, ntok), 0)
    work = logits
    tops, sels = [], []
    for _ in range(TOP_K):
        mk = jnp.max(work, axis=0, keepdims=True)
        ik = jnp.min(jnp.where(work == mk, eio, N_EXPERTS), axis=0, keepdims=True)
        sel = eio == ik
        work = jnp.where(sel, -jnp.inf, work)
        tops.append(mk)
        sels.append(sel)
    es = [jnp.exp(t - tops[0]) for t in tops]
    esum = es[0] + es[1] + es[2] + es[3]

    onehot = jnp.zeros((N_EXPERTS, ntok), F32)
    for sel in sels:
        onehot = onehot + jnp.where(sel, 1.0, 0.0)
    onehot_b = onehot.astype(BF16)
    smaller = _dot(lower_ref[...], onehot_b)
    nst = ntok // st
    where_parts = []
    for k in range(nst):
        c0, c1 = k * st, (k + 1) * st
        before = _dot(onehot_b[:, c0:c1], upper_ref[...])
        start = jnp.sum(smaller[:, c0:c1], axis=1, keepdims=True)
        where_parts.append(before + start)
    where_to = jnp.concatenate(where_parts, axis=1) if nst > 1 else where_parts[0]
    pos = [jnp.sum(jnp.where(sel, where_to, 0.0), axis=0, keepdims=True) for sel in sels]
    rio = lax.broadcasted_iota(I32, (TOP_K * st, st), 0)
    for k in range(nst):
        c0, c1 = k * st, (k + 1) * st
        hit = rio == pos[0][:, c0:c1].astype(I32)
        for slot in range(1, TOP_K):
            hit = hit | (rio == pos[slot][:, c0:c1].astype(I32))
        perm = jnp.where(hit, 1.0, 0.0).astype(BF16)
        _pack_rows(_dot(perm, hn_hi[c0:c1]), xs_ref.at[k, 0])
        tcnt_ref[k, 0] = _dot_nt(jnp.ones((1, st), BF16), onehot_b[:, c0:c1]).astype(I32)

    rows8 = jnp.concatenate(pos + [e / esum for e in es], axis=0)
    sq = jnp.concatenate([rows8, jnp.zeros((LANES - 2 * TOP_K, ntok), F32)], axis=0)
    cols = jnp.transpose(sq)
    for sub in range(nsub):
        col_ref[sub] = cols[sub * tm:(sub + 1) * tm]


def _mixer(x, cache, w, *, tm, nsub, st):
    nb, s, _ = x.shape
    nt = s // tm
    nst = nsub * tm // st
    has_cache = cache is not None
    full = lambda shape: pl.BlockSpec(shape, lambda b, i: (0,) * len(shape))
    per_b = lambda shape: pl.BlockSpec((nsub,) + shape, lambda b, i: (b, 0, 0))
    tile = lambda last: pl.BlockSpec((nsub, tm, last), lambda b, i: (b, i, 0))
    kv_shape = (WINDOW, KV_DIM)
    kv_spec = per_b(kv_shape)

    in_specs = [tile(D_MODEL)]
    args = [x]
    if has_cache:
        in_specs += [kv_spec, kv_spec, per_b((CONV_W - 1, CONV_DIM))]
        args += list(cache)
    tio = jnp.arange(st)
    eio = jnp.arange(N_EXPERTS)
    upper = (tio[:, None] < tio[None, :]).astype(BF16)
    lower = (eio[None, :] < eio[:, None]).astype(BF16)
    in_specs += [
        full((1, D_MODEL)), full((D_MODEL, IN_DIM)), full((1, ATT_DIM)),
        full((1, KV_DIM)), full((ATT_DIM, ATT_DIM)), full((KV_DIM, N_KV_HEADS * GROUP_W)),
        pl.BlockSpec(memory_space=pltpu.SMEM),
        full((CONV_W, CONV_DIM)), full((D_MODEL, D_MODEL)), full((1, D_MODEL)),
        full((N_EXPERTS, D_MODEL)), full((N_EXPERTS, 1)), full((st, st)),
        full((N_EXPERTS, N_EXPERTS)),
    ]
    args += [w['g1'], w['w_in'], w['qg'], w['kg'], w['bd'], w['rep'], w['sinks'], w['conv_w'],
             w['w_out'], w['g2'], w['w_router_t'], w['b_router_t'], upper, lower]
    rt = TOP_K * st
    nsort = nb // nsub * nst
    out_shape = [
        jax.ShapeDtypeStruct((nb, s, D_MODEL), F32),
        jax.ShapeDtypeStruct((nsort, nt, rt, PACK_S, LANES), U32),
        jax.ShapeDtypeStruct((nb, s, LANES), F32),
        jax.ShapeDtypeStruct((nsort, nt, 1, N_EXPERTS), I32),
        jax.ShapeDtypeStruct((nb,) + kv_shape, F32),
        jax.ShapeDtypeStruct((nb,) + kv_shape, F32),
        jax.ShapeDtypeStruct((nb, CONV_W - 1, CONV_DIM), F32),
    ]
    out_specs = [
        tile(D_MODEL),
        pl.BlockSpec((nst, 1, rt, PACK_S, LANES), lambda b, i: (b, i, 0, 0, 0)),
        tile(LANES),
        pl.BlockSpec((nst, 1, 1, N_EXPERTS), lambda b, i: (b, i, 0, 0)),
        kv_spec, kv_spec, per_b((CONV_W - 1, CONV_DIM)),
    ]
    scratch = [pltpu.VMEM((nsub, WINDOW, KV_DIM), F32), pltpu.VMEM((nsub, WINDOW, KV_DIM), F32),
               pltpu.VMEM((nsub, tm + 8, CONV_DIM), F32)]
    return pl.pallas_call(
        functools.partial(_mixer_kernel, tm=tm, nsub=nsub, st=st, has_cache=has_cache),
        grid=(nb // nsub, nt),
        in_specs=in_specs,
        out_specs=out_specs,
        out_shape=out_shape,
        scratch_shapes=scratch,
        compiler_params=pltpu.CompilerParams(
            dimension_semantics=("arbitrary", "arbitrary"), vmem_limit_bytes=MIXER_VMEM_LIMIT),
        name="mixer_sample" if has_cache else "mixer_prompt",
    )(*args)


def _expert_kernel(be_ref, bq_ref, nused_ref, src_ref, off_ref, n_ref, tot_ref,
                   xsp_hbm, xss_hbm, wgu_hbm, bgu_ref, wd_hbm, bdn_ref, ys_ref,
                   xbuf, sems, wgu_f, wd_f, wsems, ptr, wcur, *, n_tiles, n_prompt_tiles):
    b = pl.program_id(0)
    nused = nused_ref[0]

    def weight_copies(e, ws):
        return (pltpu.make_async_copy(wgu_hbm.at[e], wgu_f.at[ws], wsems.at[0, ws]),
                pltpu.make_async_copy(wd_hbm.at[e], wd_f.at[ws], wsems.at[1, ws]))

    def issue_rows(blk, slot, enabled, between=lambda: None):
        blk = jnp.minimum(blk, pl.num_programs(0) - 1)
        e = be_ref[blk]
        q = bq_ref[blk]
        lo_row = q * ROW_BLOCK
        hi_row = lo_row + ROW_BLOCK
        base = e * n_tiles

        def segment(j, j_limit, src_hbm, ok):
            jc = jnp.minimum(j, j_limit - 1)
            o = off_ref[base + jc]
            inside = ok & (j < j_limit) & (o < hi_row)
            lo = jnp.maximum(o, lo_row)
            ln = jnp.minimum(o + n_ref[base + jc], hi_row) - lo

            @pl.when(inside & (ln > 0))
            def _():
                pltpu.make_async_copy(src_hbm.at[pl.ds(src_ref[base + jc] + (lo - o), ln)],
                                      xbuf.at[slot, pl.ds(lo - lo_row, ln)], sems.at[slot]).start()
            return inside

        j0 = jnp.where(q == 0, 0, ptr[0])

        def cond(j):
            return (enabled & (j < n_prompt_tiles)
                    & (off_ref[base + jnp.minimum(j, n_prompt_tiles - 1)] < hi_row))

        def body(j):
            segment(j, n_prompt_tiles, xsp_hbm, True)
            return j + 1

        jtail = lax.while_loop(cond, body, j0 + INLINE_SEGMENTS)
        result = between()
        count = jnp.int32(0)
        for k in range(INLINE_SEGMENTS):
            count = count + segment(j0 + k, n_prompt_tiles, xsp_hbm, enabled).astype(I32)
        jend = jnp.where(count == INLINE_SEGMENTS, jtail, j0 + count)
        ptr[0] = jnp.where(enabled, jnp.maximum(jend - 1, 0), ptr[0])
        for j in range(n_prompt_tiles, n_tiles):
            segment(j, n_tiles, xss_hbm, enabled)
        return result

    @pl.when(b == 0)
    def _():
        xbuf[...] = jnp.zeros(xbuf.shape, U32)
        for ahead in range(LOOKAHEAD):
            issue_rows(ahead, ahead, ahead < nused)
        for c in weight_copies(be_ref[0], 0):
            c.start()
        wcur[0] = 1

    active = b < nused

    @pl.when(active)
    def _():
        slot = b % ROW_SLOTS
        e = be_ref[b]
        rows = jnp.minimum(tot_ref[e] - bq_ref[b] * ROW_BLOCK, ROW_BLOCK)
        pltpu.make_async_copy(xsp_hbm.at[pl.ds(0, rows)], xbuf.at[slot, pl.ds(0, rows)],
                              sems.at[slot]).wait()

        @pl.when(bq_ref[b] == 0)
        def _():
            ws = 1 - wcur[0]
            wcur[0] = ws
            for c in weight_copies(e, ws):
                c.wait()
            nxt = b + (tot_ref[e] + ROW_BLOCK - 1) // ROW_BLOCK

            @pl.when(nxt < nused)
            def _():
                for c in weight_copies(be_ref[jnp.minimum(nxt, pl.num_programs(0) - 1)], 1 - ws):
                    c.start()

        ws = wcur[0]

        def compute(nrows):
            x = issue_rows(b + LOOKAHEAD, (b + LOOKAHEAD) % ROW_SLOTS, b + LOOKAHEAD < nused,
                           between=lambda: _unpack_rows(xbuf.at[slot, pl.ds(0, nrows)]))
            gu = _dot(x, wgu_f[ws]) + bgu_ref[pl.ds(e, 1), :]
            g = jnp.minimum(gu[:, :D_FF], SWIGLU_LIMIT)
            u = jnp.clip(gu[:, D_FF:], -SWIGLU_LIMIT, SWIGLU_LIMIT)
            act = (u + 1.0) * (g * jax.nn.sigmoid(SWIGLU_ALPHA * g))
            y = (_dot(act.astype(BF16), wd_f[ws])
                 + (bdn_ref[pl.ds(e, 1), :] + _anchor_zero_row(xbuf)))
            _pack_rows(y.astype(BF16).astype(F32), ys_ref.at[pl.ds(0, nrows)])
            if nrows < ROW_BLOCK:
                ys_ref[nrows:] = jnp.zeros((ROW_BLOCK - nrows, PACK_S, LANES), U32)

        for piece in range(ROW_BLOCK // TAIL_ROWS):
            top = (piece + 1) * TAIL_ROWS
            pl.when((rows > top - TAIL_ROWS) & (rows <= top))(functools.partial(compute, top))

    @pl.when(jnp.logical_not(active))
    def _():
        ys_ref[...] = jnp.zeros(ys_ref.shape, U32)


def _experts(plan, xsp, xss, w_gu, b_gu, w_down, b_down, *, n_tiles, n_prompt_tiles, rows):
    nblk = rows // ROW_BLOCK

    grid_spec = pltpu.PrefetchScalarGridSpec(
        num_scalar_prefetch=7,
        grid=(nblk,),
        in_specs=[pl.BlockSpec(memory_space=pl.ANY),
                  pl.BlockSpec(memory_space=pl.ANY),
                  pl.BlockSpec(memory_space=pl.ANY),
                  pl.BlockSpec((N_EXPERTS, 2 * D_FF), lambda b, *_: (0, 0)),
                  pl.BlockSpec(memory_space=pl.ANY),
                  pl.BlockSpec((N_EXPERTS, D_MODEL), lambda b, *_: (0, 0))],
        out_specs=pl.BlockSpec((ROW_BLOCK, PACK_S, LANES), lambda b, *_: (b, 0, 0)),
        scratch_shapes=[pltpu.VMEM((ROW_SLOTS + 1, ROW_BLOCK, PACK_S, LANES), U32),
                        pltpu.SemaphoreType.DMA((ROW_SLOTS,)),
                        pltpu.VMEM((2, D_MODEL, 2 * D_FF), F32), pltpu.VMEM((2, D_FF, D_MODEL), F32),
                        pltpu.SemaphoreType.DMA((2, 2)),
                        pltpu.SMEM((1,), I32), pltpu.SMEM((1,), I32)],
    )
    return pl.pallas_call(
        functools.partial(_expert_kernel, n_tiles=n_tiles, n_prompt_tiles=n_prompt_tiles),
        grid_spec=grid_spec,
        out_shape=jax.ShapeDtypeStruct((rows, PACK_S, LANES), U32),
        compiler_params=pltpu.CompilerParams(
            dimension_semantics=("arbitrary",), vmem_limit_bytes=VMEM_LIMIT),
        name="experts",
    )(plan['block_e'], plan['block_q'], plan['n_used'], plan['src_t'], plan['off_t'], plan['n_t'],
      plan['tot'], xsp, xss, w_gu, b_gu, w_down, b_down)


def _combine_kernel(ysrc_ref, sloc_ref, n_ref, col_ref, h_ref, ys_hbm, o_ref, buf, sems,
                    *, tm, tile0):
    j = pl.program_id(0)
    nt = pl.num_programs(0)
    rt = TOP_K * tm

    def issue(t, slot, enabled):
        base = (tile0 + jnp.minimum(t, nt - 1)) * N_EXPERTS
        for e in range(N_EXPERTS):
            n = n_ref[base + e]

            @pl.when(enabled & (n > 0))
            def _():
                pltpu.make_async_copy(ys_hbm.at[pl.ds(ysrc_ref[base + e], n)],
                                      buf.at[slot, pl.ds(sloc_ref[base + e], n)],
                                      sems.at[slot]).start()

    @pl.when(j == 0)
    def _():
        buf[ROW_SLOTS, 0:8 // PACK_S] = jnp.zeros((8 // PACK_S, PACK_S, LANES), U32)
        for ahead in range(LOOKAHEAD):
            issue(ahead, ahead, ahead < nt)

    slot = j % ROW_SLOTS
    pltpu.make_async_copy(ys_hbm.at[pl.ds(0, rt)], buf.at[slot], sems.at[slot]).wait()
    y = _unpack_rows(buf.at[slot])
    issue(j + LOOKAHEAD, (j + LOOKAHEAD) % ROW_SLOTS, j + LOOKAHEAD < nt)
    col = col_ref[...]
    rio = lax.broadcasted_iota(I32, (tm, rt), 1)
    gmat = jnp.zeros((tm, rt), F32)
    for k in range(TOP_K):
        gmat = gmat + jnp.where(rio == col[:, k:k + 1].astype(I32),
                                col[:, TOP_K + k:TOP_K + k + 1], 0.0)
    o_ref[...] = (h_ref[...] + _anchor_zero_row(buf)) + _dot(gmat.astype(BF16), y)


def _combine(plan, col, h2d, ys, *, tm, tile0):
    t = h2d.shape[0]
    rt = TOP_K * tm
    grid_spec = pltpu.PrefetchScalarGridSpec(
        num_scalar_prefetch=3,
        grid=(t // tm,),
        in_specs=[pl.BlockSpec((tm, LANES), lambda i, *_: (i, 0)),
                  pl.BlockSpec((tm, D_MODEL), lambda i, *_: (i, 0)),
                  pl.BlockSpec(memory_space=pl.ANY)],
        out_specs=pl.BlockSpec((tm, D_MODEL), lambda i, *_: (i, 0)),
        scratch_shapes=[pltpu.VMEM((ROW_SLOTS + 1, rt, PACK_S, LANES), U32),
                        pltpu.SemaphoreType.DMA((ROW_SLOTS,))],
    )
    return pl.pallas_call(
        functools.partial(_combine_kernel, tm=tm, tile0=tile0),
        grid_spec=grid_spec,
        out_shape=jax.ShapeDtypeStruct((t, D_MODEL), F32),
        compiler_params=pltpu.CompilerParams(
            dimension_semantics=("arbitrary",), vmem_limit_bytes=VMEM_LIMIT),
        name="combine",
    )(plan['ysrc'], plan['sloc'], plan['n'], col, h2d, ys)


def kernel(x_prompt, x_sample, cache_k, cache_v, state_conv, norm1_g, w_in, q_norm_g, k_norm_g,
           sinks, conv_w, w_out, norm2_g, w_router, b_router, w_gu, b_gu, w_down, b_down):
    l = 0
    nbp, sp, _ = x_prompt.shape
    nbs, ss, _ = x_sample.shape
    tp, ts = nbp * sp, nbs * ss
    tmp, tms = TOKEN_TILE, ts
    ntp, nts = tp // tmp, ts // tms
    n_tiles = ntp + nts
    n_rows = (tp + ts) * TOP_K
    grp = jnp.arange(ATT_DIM) // HEAD_DIM
    rep_col = jnp.arange(N_KV_HEADS * GROUP_W)
    w = {
        'g1': norm1_g[l][None, :],
        'w_in': w_in[l].astype(BF16),
        'qg': jnp.tile(q_norm_g[l], N_HEADS)[None, :],
        'kg': jnp.tile(k_norm_g[l], N_KV_HEADS)[None, :],
        'bd': (grp[:, None] == grp[None, :]).astype(BF16),
        'rep': (jnp.arange(KV_DIM)[:, None]
                == (rep_col // GROUP_W * HEAD_DIM + rep_col % HEAD_DIM)[None, :]).astype(BF16),
        'sinks': sinks[l][None, :],
        'conv_w': conv_w[l],
        'w_out': w_out[l].astype(BF16),
        'g2': norm2_g[l][None, :],
        'w_router_t': w_router[l].T,
        'b_router_t': b_router[l][:, None],
    }
    hp, xsp, colp, tcp, knp, vnp, unp = _mixer(x_prompt, None, w, tm=tmp, nsub=STREAMS_PER_STEP,
                                               st=tmp)
    cache = (cache_k[l].reshape(nbs, WINDOW, KV_DIM), cache_v[l].reshape(nbs, WINDOW, KV_DIM),
             state_conv[l])
    hs, xss, cols, tcs, kns, vns, uns = _mixer(x_sample, cache, w, tm=ss, nsub=nbs, st=tms)

    n = jnp.concatenate([tcp.reshape(ntp, N_EXPERTS), tcs.reshape(nts, N_EXPERTS)], axis=0)
    tile_base = jnp.concatenate([jnp.arange(ntp, dtype=I32) * (TOP_K * tmp),
                                 jnp.arange(nts, dtype=I32) * (TOP_K * tms)])
    sloc = jnp.cumsum(n, axis=1) - n
    off = jnp.cumsum(n, axis=0) - n
    tot = jnp.sum(n, axis=0)
    padded = (tot + ROW_BLOCK - 1) // ROW_BLOCK * ROW_BLOCK
    ends = jnp.cumsum(padded)
    base = ends - padded
    rows = (-(-n_rows // ROW_BLOCK) + N_EXPERTS) * ROW_BLOCK
    nblk = rows // ROW_BLOCK
    blk0 = jnp.arange(nblk, dtype=I32) * ROW_BLOCK
    block_e = jnp.minimum(jnp.sum((ends[None, :] <= blk0[:, None]).astype(I32), axis=1),
                          N_EXPERTS - 1)
    eq = (block_e[:, None] == jnp.arange(N_EXPERTS, dtype=I32)[None, :]).astype(I32)
    block_q = (blk0 - jnp.sum(eq * base[None, :], axis=1)) // ROW_BLOCK
    plan = {
        'block_e': block_e.astype(I32),
        'block_q': block_q.astype(I32),
        'n_used': (ends[-1:] // ROW_BLOCK).astype(I32),
        'src_t': (tile_base[:, None] + sloc).T.reshape(-1).astype(I32),
        'off_t': off.T.reshape(-1).astype(I32),
        'n_t': n.T.reshape(-1).astype(I32),
        'tot': tot.astype(I32),
        'ysrc': (base[None, :] + off).reshape(-1).astype(I32),
        'sloc': sloc.reshape(-1).astype(I32),
        'n': n.reshape(-1).astype(I32),
    }

    ys = _experts(plan, xsp.reshape(-1, PACK_S, LANES), xss.reshape(-1, PACK_S, LANES),
                  w_gu[l], b_gu[l], w_down[l], b_down[l],
                  n_tiles=n_tiles, n_prompt_tiles=ntp, rows=rows)
    yp = _combine(plan, colp.reshape(tp, LANES), hp.reshape(tp, D_MODEL), ys, tm=tmp, tile0=0)
    ysm = _combine(plan, cols.reshape(ts, LANES), hs.reshape(ts, D_MODEL), ys, tm=tms, tile0=ntp)

    kv5 = lambda a: a.reshape(1, a.shape[0], WINDOW, N_KV_HEADS, HEAD_DIM)
    return (yp.reshape(nbp, sp, D_MODEL), ysm.reshape(nbs, ss, D_MODEL),
            kv5(knp), kv5(vnp), unp[None], kv5(kns), kv5(vns), uns[None])
```

```python
import functools

import jax
import jax.numpy as jnp
from jax import lax
from jax.experimental import pallas as pl
from jax.experimental.pallas import tpu as pltpu

D_MODEL = 1024
CHUNK = 64
HEAD_DIM = 64
N_HEADS = 8
N_KV_HEADS = 2
Q_PER_KV = N_HEADS // N_KV_HEADS
ATT_DIM = N_HEADS * HEAD_DIM
KV_DIM = N_KV_HEADS * HEAD_DIM
GROUP_W = Q_PER_KV * HEAD_DIM
CONV_DIM = D_MODEL - ATT_DIM
CONV_W = 3
WINDOW = 128
IN_DIM = ATT_DIM + 2 * KV_DIM + 3 * CONV_DIM
N_EXPERTS = 32
TOP_K = 4
D_FF = D_MODEL
SWIGLU_LIMIT = 7.0
SWIGLU_ALPHA = 1.702
EPS = 1e-6
NEG = -1e30

LANES = 128
HALF = D_MODEL // 2
PACK_S = HALF // LANES
TOKEN_TILE = 256
STREAMS_PER_STEP = 4
ROW_BLOCK = 512
TAIL_ROWS = 128
INLINE_SEGMENTS = 20
LOOKAHEAD = 2
ROW_SLOTS = LOOKAHEAD + 1
VMEM_LIMIT = 56 * 1024 * 1024
MIXER_VMEM_LIMIT = 60 * 1024 * 1024

F32 = jnp.float32
BF16 = jnp.bfloat16
U32 = jnp.uint32
I32 = jnp.int32


def _rms_rows(x, g):
    return x * lax.rsqrt(jnp.mean(x * x, axis=-1, keepdims=True) + EPS) * g


def _split_bf16(x):
    hi = x.astype(BF16)
    lo = (x - hi.astype(F32)).astype(BF16)
    return hi, lo


def _dot(a, b):
    return jnp.dot(a, b, preferred_element_type=F32)


def _dot_nt(a, b):
    return lax.dot_general(a, b, (((1,), (1,)), ((), ())), preferred_element_type=F32)


def _head_rms(t, bd, g):
    ssq = _dot((t * t).astype(BF16), bd)
    return t * lax.rsqrt(ssq * (1.0 / HEAD_DIM) + EPS) * g


def _pack_rows(vals, out_ref):
    r = vals.shape[0]
    bits = pltpu.bitcast(vals, U32)
    word = (bits[:, :HALF] >> 16) | (bits[:, HALF:] & jnp.uint32(0xFFFF0000))
    flat = out_ref.reshape(r * PACK_S, LANES)
    for s in range(PACK_S):
        flat[pl.ds(s, r, stride=PACK_S), :] = word[:, s * LANES:(s + 1) * LANES]


def _unpack_rows(packed_ref):
    r = packed_ref.shape[0]
    flat = packed_ref.reshape(r * PACK_S, LANES)
    lo, hi = [], []
    for s in range(PACK_S):
        w = flat[pl.ds(s, r, stride=PACK_S), :]
        lo.append(pltpu.bitcast(w << 16, F32))
        hi.append(pltpu.bitcast(w & jnp.uint32(0xFFFF0000), F32))
    return jnp.concatenate(lo + hi, axis=1).astype(BF16)


def _anchor_zero_row(staging_ref):
    z = staging_ref.at[ROW_SLOTS, pl.ds(0, 8 // PACK_S)].reshape(8, LANES)[...]
    return jnp.concatenate([pltpu.bitcast(z, F32)[0:1, :]] * (D_MODEL // LANES), axis=1)


def _spread_kv(t):
    assert N_KV_HEADS == 2 and KV_DIM == LANES
    swapped = pltpu.roll(t, HEAD_DIM, axis=1)
    low = lax.broadcasted_iota(I32, t.shape, 1) < HEAD_DIM
    pair0 = jnp.where(low, t, swapped)
    pair1 = jnp.where(low, swapped, t)
    reps = GROUP_W // LANES
    return jnp.concatenate([pair0] * reps + [pair1] * reps, axis=1).astype(BF16)


def _attention(i, qn, kall, vall, sink_ref, *, mask_history):
    tm = qn.shape[0]
    k4 = _spread_kv(kall)
    v4 = _spread_kv(vall)
    lane_grp = lax.broadcasted_iota(I32, (CHUNK, GROUP_W), 1) // HEAD_DIM
    row_grp = lax.broadcasted_iota(I32, (Q_PER_KV * CHUNK, 1), 0) // CHUNK
    span = WINDOW + CHUNK
    att_rows = []
    for c in range(tm // CHUNK):
        per_kv = []
        for hk in range(N_KV_HEADS):
            qc = qn[c * CHUNK:(c + 1) * CHUNK, hk * GROUP_W:(hk + 1) * GROUP_W]
            qs = jnp.concatenate([jnp.where(lane_grp == g, qc, 0.0) for g in range(Q_PER_KV)],
                                 axis=0).astype(BF16)
            kw = k4[c * CHUNK:c * CHUNK + span, hk * GROUP_W:(hk + 1) * GROUP_W]
            vw = v4[c * CHUNK:c * CHUNK + span, hk * GROUP_W:(hk + 1) * GROUP_W]
            s = _dot_nt(qs, kw) * (HEAD_DIM ** -0.5)
            if mask_history and c * CHUNK < WINDOW:
                kcol = lax.broadcasted_iota(I32, (1, span), 1)
                s = jnp.where(kcol >= jnp.where(i == 0, WINDOW - c * CHUNK, 0), s, NEG)
            sink = jnp.zeros((Q_PER_KV * CHUNK, 1), F32)
            for g in range(Q_PER_KV):
                sink = jnp.where(row_grp == g, sink_ref[0, hk * Q_PER_KV + g], sink)
            m = jnp.maximum(jnp.max(s, axis=-1, keepdims=True), sink)
            p = jnp.exp(s - m)
            den = jnp.sum(p, axis=-1, keepdims=True) + jnp.exp(sink - m)
            o = _dot(p.astype(BF16), vw) / den
            oc = jnp.zeros((CHUNK, GROUP_W), F32)
            for g in range(Q_PER_KV):
                oc = oc + jnp.where(lane_grp == g, o[g * CHUNK:(g + 1) * CHUNK, :], 0.0)
            per_kv.append(oc)
        att_rows.append(jnp.concatenate(per_kv, axis=1))
    return jnp.concatenate(att_rows, axis=0)


def _mixer_kernel(*refs, tm, nsub, st, has_cache):
    refs = list(refs)
    x_ref = refs.pop(0)
    if has_cache:
        ck_ref, cv_ref, st_ref = refs.pop(0), refs.pop(0), refs.pop(0)
    (g1_ref, win_ref, qg_ref, kg_ref, bd_ref, sink_ref, cw_ref, wout_ref, g2_ref,
     wrt_ref, brt_ref, upper_ref, lower_ref,
     h_ref, xs_ref, col_ref, tcnt_ref, knew_ref, vnew_ref, unew_ref, kprev, vprev, ubuf) = refs
    i = pl.program_id(1)
    ntok = nsub * tm

    @pl.when(i == 0)
    def _():
        ubuf[:, 0:8, :] = jnp.zeros((nsub, 8, CONV_DIM), F32)
        if has_cache:
            kprev[...] = ck_ref[...]
            vprev[...] = cv_ref[...]
            ubuf[:, 6:8, :] = st_ref[...]
        else:
            kprev[...] = jnp.zeros(kprev.shape, F32)
            vprev[...] = jnp.zeros(vprev.shape, F32)

    x = jnp.concatenate([x_ref[sub] for sub in range(nsub)], axis=0) if nsub > 1 else x_ref[0]
    xn = _rms_rows(x, g1_ref[...])
    proj = _dot(xn.astype(BF16), win_ref[...])
    o0 = ATT_DIM
    o1 = o0 + KV_DIM
    o2 = o1 + KV_DIM
    o3 = o2 + CONV_DIM
    o4 = o3 + CONV_DIM
    bd = bd_ref[...]
    qn = _head_rms(proj[:, :o0], bd, qg_ref[...])
    kn = _head_rms(proj[:, o0:o1], bd[:KV_DIM, :KV_DIM], kg_ref[...])
    v = proj[:, o1:o2]
    bg = proj[:, o2:o3]
    u = proj[:, o3:o4] * proj[:, o4:]

    cw = cw_ref[...]
    att_parts, cv_parts = [], []
    for sub in range(nsub):
        r0, r1 = sub * tm, (sub + 1) * tm
        kall = jnp.concatenate([kprev[sub], kn[r0:r1]], axis=0)
        vall = jnp.concatenate([vprev[sub], v[r0:r1]], axis=0)
        att_parts.append(_attention(i, qn[r0:r1], kall, vall, sink_ref,
                                    mask_history=not has_cache))
        kprev[sub] = kall[tm:, :]
        vprev[sub] = vall[tm:, :]
        knew_ref[sub] = kall[tm:, :]
        vnew_ref[sub] = vall[tm:, :]

        us = u[r0:r1]
        ubuf[sub, 8:8 + tm, :] = us
        cvs = ubuf[sub, 6:6 + tm, :] * cw[0:1, :] + ubuf[sub, 7:7 + tm, :] * cw[1:2, :]
        cv_parts.append(cvs + us * cw[2:3, :])
        ubuf[sub, 0:8, :] = ubuf[sub, tm:tm + 8, :]
        unew_ref[sub] = us[tm - (CONV_W - 1):, :]
    att = jnp.concatenate(att_parts, axis=0) if nsub > 1 else att_parts[0]
    cv = jnp.concatenate(cv_parts, axis=0) if nsub > 1 else cv_parts[0]

    mix = jnp.concatenate([att, bg * cv], axis=1).astype(BF16)
    h = x + _dot(mix, wout_ref[...])
    for sub in range(nsub):
        h_ref[sub] = h[sub * tm:(sub + 1) * tm]
    hn = _rms_rows(h, g2_ref[...])

    hn_hi, hn_lo = _split_bf16(hn)
    wr_hi, wr_lo = _split_bf16(wrt_ref[...])
    logits = (_dot_nt(wr_hi, hn_hi) + _dot_nt(wr_hi, hn_lo) + _dot_nt(wr_lo, hn_hi)) + brt_ref[...]
    eio = lax.broadcasted_iota(I32, (N_EXPERTS, ntok), 0)
    work = logits
    tops, sels = [], []
    for _ in range(TOP_K):
        mk = jnp.max(work, axis=0, keepdims=True)
        ik = jnp.min(jnp.where(work == mk, eio, N_EXPERTS), axis=0, keepdims=True)
        sel = eio == ik
        work = jnp.where(sel, -jnp.inf, work)
        tops.append(mk)
        sels.append(sel)
    es = [jnp.exp(t - tops[0]) for t in tops]
    esum = es[0] + es[1] + es[2] + es[3]

    onehot = jnp.zeros((N_EXPERTS, ntok), F32)
    for sel in sels:
        onehot = onehot + jnp.where(sel, 1.0, 0.0)
    onehot_b = onehot.astype(BF16)
    smaller = _dot(lower_ref[...], onehot_b)
    nst = ntok // st
    where_parts = []
    for k in range(nst):
        c0, c1 = k * st, (k + 1) * st
        before = _dot(onehot_b[:, c0:c1], upper_ref[...])
        start = jnp.sum(smaller[:, c0:c1], axis=1, keepdims=True)
        where_parts.append(before + start)
    where_to = jnp.concatenate(where_parts, axis=1) if nst > 1 else where_parts[0]
    pos = [jnp.sum(jnp.where(sel, where_to, 0.0), axis=0, keepdims=True) for sel in sels]
    rio = lax.broadcasted_iota(I32, (TOP_K * st, st), 0)
    for k in range(nst):
        c0, c1 = k * st, (k + 1) * st
        hit = rio == pos[0][:, c0:c1].astype(I32)
        for slot in range(1, TOP_K):
            hit = hit | (rio == pos[slot][:, c0:c1].astype(I32))
        perm = jnp.where(hit, 1.0, 0.0).astype(BF16)
        _pack_rows(_dot(perm, hn_hi[c0:c1]), xs_ref.at[k, 0])
        tcnt_ref[k, 0] = _dot_nt(jnp.ones((1, st), BF16), onehot_b[:, c0:c1]).astype(I32)

    rows8 = jnp.concatenate(pos + [e / esum for e in es], axis=0)
    sq = jnp.concatenate([rows8, jnp.zeros((LANES - 2 * TOP_K, ntok), F32)], axis=0)
    cols = jnp.transpose(sq)
    for sub in range(nsub):
        col_ref[sub] = cols[sub * tm:(sub + 1) * tm]


def _mixer(x, cache, w, *, tm, nsub, st):
    nb, s, _ = x.shape
    nt = s // tm
    nst = nsub * tm // st
    has_cache = cache is not None
    full = lambda shape: pl.BlockSpec(shape, lambda b, i: (0,) * len(shape))
    per_b = lambda shape: pl.BlockSpec((nsub,) + shape, lambda b, i: (b, 0, 0))
    tile = lambda last: pl.BlockSpec((nsub, tm, last), lambda b, i: (b, i, 0))
    kv_shape = (WINDOW, KV_DIM)
    kv_spec = per_b(kv_shape)

    in_specs = [tile(D_MODEL)]
    args = [x]
    if has_cache:
        in_specs += [kv_spec, kv_spec, per_b((CONV_W - 1, CONV_DIM))]
        args += list(cache)
    tio = jnp.arange(st)
    eio = jnp.arange(N_EXPERTS)
    upper = (tio[:, None] < tio[None, :]).astype(BF16)
    lower = (eio[None, :] < eio[:, None]).astype(BF16)
    in_specs += [
        full((1, D_MODEL)), full((D_MODEL, IN_DIM)), full((1, ATT_DIM)),
        full((1, KV_DIM)), full((ATT_DIM, ATT_DIM)),
        pl.BlockSpec(memory_space=pltpu.SMEM),
        full((CONV_W, CONV_DIM)), full((D_MODEL, D_MODEL)), full((1, D_MODEL)),
        full((N_EXPERTS, D_MODEL)), full((N_EXPERTS, 1)), full((st, st)),
        full((N_EXPERTS, N_EXPERTS)),
    ]
    args += [w['g1'], w['w_in'], w['qg'], w['kg'], w['bd'], w['sinks'], w['conv_w'],
             w['w_out'], w['g2'], w['w_router_t'], w['b_router_t'], upper, lower]
    rt = TOP_K * st
    nsort = nb // nsub * nst
    out_shape = [
        jax.ShapeDtypeStruct((nb, s, D_MODEL), F32),
        jax.ShapeDtypeStruct((nsort, nt, rt, PACK_S, LANES), U32),
        jax.ShapeDtypeStruct((nb, s, LANES), F32),
        jax.ShapeDtypeStruct((nsort, nt, 1, N_EXPERTS), I32),
        jax.ShapeDtypeStruct((nb,) + kv_shape, F32),
        jax.ShapeDtypeStruct((nb,) + kv_shape, F32),
        jax.ShapeDtypeStruct((nb, CONV_W - 1, CONV_DIM), F32),
    ]
    out_specs = [
        tile(D_MODEL),
        pl.BlockSpec((nst, 1, rt, PACK_S, LANES), lambda b, i: (b, i, 0, 0, 0)),
        tile(LANES),
        pl.BlockSpec((nst, 1, 1, N_EXPERTS), lambda b, i: (b, i, 0, 0)),
        kv_spec, kv_spec, per_b((CONV_W - 1, CONV_DIM)),
    ]
    scratch = [pltpu.VMEM((nsub, WINDOW, KV_DIM), F32), pltpu.VMEM((nsub, WINDOW, KV_DIM), F32),
               pltpu.VMEM((nsub, tm + 8, CONV_DIM), F32)]
    return pl.pallas_call(
        functools.partial(_mixer_kernel, tm=tm, nsub=nsub, st=st, has_cache=has_cache),
        grid=(nb // nsub, nt),
        in_specs=in_specs,
        out_specs=out_specs,
        out_shape=out_shape,
        scratch_shapes=scratch,
        compiler_params=pltpu.CompilerParams(
            dimension_semantics=("arbitrary", "arbitrary"), vmem_limit_bytes=MIXER_VMEM_LIMIT),
        name="mixer_sample" if has_cache else "mixer_prompt",
    )(*args)


def _expert_kernel(be_ref, bq_ref, nused_ref, src_ref, off_ref, n_ref, tot_ref,
                   xsp_hbm, xss_hbm, wgu_hbm, bgu_ref, wd_hbm, bdn_ref, ys_ref,
                   xbuf, sems, wgu_f, wd_f, wsems, ptr, wcur, *, n_tiles, n_prompt_tiles):
    b = pl.program_id(0)
    nused = nused_ref[0]

    def weight_copies(e, ws):
        return (pltpu.make_async_copy(wgu_hbm.at[e], wgu_f.at[ws], wsems.at[0, ws]),
                pltpu.make_async_copy(wd_hbm.at[e], wd_f.at[ws], wsems.at[1, ws]))

    def issue_rows(blk, slot, enabled, between=lambda: None):
        blk = jnp.minimum(blk, pl.num_programs(0) - 1)
        e = be_ref[blk]
        q = bq_ref[blk]
        lo_row = q * ROW_BLOCK
        hi_row = lo_row + ROW_BLOCK
        base = e * n_tiles

        def segment(j, j_limit, src_hbm, ok):
            jc = jnp.minimum(j, j_limit - 1)
            o = off_ref[base + jc]
            inside = ok & (j < j_limit) & (o < hi_row)
            lo = jnp.maximum(o, lo_row)
            ln = jnp.minimum(o + n_ref[base + jc], hi_row) - lo

            @pl.when(inside & (ln > 0))
            def _():
                pltpu.make_async_copy(src_hbm.at[pl.ds(src_ref[base + jc] + (lo - o), ln)],
                                      xbuf.at[slot, pl.ds(lo - lo_row, ln)], sems.at[slot]).start()
            return inside

        j0 = jnp.where(q == 0, 0, ptr[0])

        def cond(j):
            return (enabled & (j < n_prompt_tiles)
                    & (off_ref[base + jnp.minimum(j, n_prompt_tiles - 1)] < hi_row))

        def body(j):
            segment(j, n_prompt_tiles, xsp_hbm, True)
            return j + 1

        jtail = lax.while_loop(cond, body, j0 + INLINE_SEGMENTS)
        result = between()
        count = jnp.int32(0)
        for k in range(INLINE_SEGMENTS):
            count = count + segment(j0 + k, n_prompt_tiles, xsp_hbm, enabled).astype(I32)
        jend = jnp.where(count == INLINE_SEGMENTS, jtail, j0 + count)
        ptr[0] = jnp.where(enabled, jnp.maximum(jend - 1, 0), ptr[0])
        for j in range(n_prompt_tiles, n_tiles):
            segment(j, n_tiles, xss_hbm, enabled)
        return result

    @pl.when(b == 0)
    def _():
        xbuf[...] = jnp.zeros(xbuf.shape, U32)
        for ahead in range(LOOKAHEAD):
            issue_rows(ahead, ahead, ahead < nused)
        for c in weight_copies(be_ref[0], 0):
            c.start()
        wcur[0] = 1

    active = b < nused

    @pl.when(active)
    def _():
        slot = b % ROW_SLOTS
        e = be_ref[b]
        rows = jnp.minimum(tot_ref[e] - bq_ref[b] * ROW_BLOCK, ROW_BLOCK)
        pltpu.make_async_copy(xsp_hbm.at[pl.ds(0, rows)], xbuf.at[slot, pl.ds(0, rows)],
                              sems.at[slot]).wait()

        @pl.when(bq_ref[b] == 0)
        def _():
            ws = 1 - wcur[0]
            wcur[0] = ws
            for c in weight_copies(e, ws):
                c.wait()
            nxt = b + (tot_ref[e] + ROW_BLOCK - 1) // ROW_BLOCK

            @pl.when(nxt < nused)
            def _():
                for c in weight_copies(be_ref[jnp.minimum(nxt, pl.num_programs(0) - 1)], 1 - ws):
                    c.start()

        ws = wcur[0]

        def compute(nrows):
            x = issue_rows(b + LOOKAHEAD, (b + LOOKAHEAD) % ROW_SLOTS, b + LOOKAHEAD < nused,
                           between=lambda: _unpack_rows(xbuf.at[slot, pl.ds(0, nrows)]))
            gu = _dot(x, wgu_f[ws]) + bgu_ref[pl.ds(e, 1), :]
            g = jnp.minimum(gu[:, :D_FF], SWIGLU_LIMIT)
            u = jnp.clip(gu[:, D_FF:], -SWIGLU_LIMIT, SWIGLU_LIMIT)
            act = (u + 1.0) * (g * jax.nn.sigmoid(SWIGLU_ALPHA * g))
            y = (_dot(act.astype(BF16), wd_f[ws])
                 + (bdn_ref[pl.ds(e, 1), :] + _anchor_zero_row(xbuf)))
            _pack_rows(y.astype(BF16).astype(F32), ys_ref.at[pl.ds(0, nrows)])
            if nrows < ROW_BLOCK:
                ys_ref[nrows:] = jnp.zeros((ROW_BLOCK - nrows, PACK_S, LANES), U32)

        for piece in range(ROW_BLOCK // TAIL_ROWS):
            top = (piece + 1) * TAIL_ROWS
            pl.when((rows > top - TAIL_ROWS) & (rows <= top))(functools.partial(compute, top))

    @pl.when(jnp.logical_not(active))
    def _():
        ys_ref[...] = jnp.zeros(ys_ref.shape, U32)


def _experts(plan, xsp, xss, w_gu, b_gu, w_down, b_down, *, n_tiles, n_prompt_tiles, rows):
    nblk = rows // ROW_BLOCK

    grid_spec = pltpu.PrefetchScalarGridSpec(
        num_scalar_prefetch=7,
        grid=(nblk,),
        in_specs=[pl.BlockSpec(memory_space=pl.ANY),
                  pl.BlockSpec(memory_space=pl.ANY),
                  pl.BlockSpec(memory_space=pl.ANY),
                  pl.BlockSpec((N_EXPERTS, 2 * D_FF), lambda b, *_: (0, 0)),
                  pl.BlockSpec(memory_space=pl.ANY),
                  pl.BlockSpec((N_EXPERTS, D_MODEL), lambda b, *_: (0, 0))],
        out_specs=pl.BlockSpec((ROW_BLOCK, PACK_S, LANES), lambda b, *_: (b, 0, 0)),
        scratch_shapes=[pltpu.VMEM((ROW_SLOTS + 1, ROW_BLOCK, PACK_S, LANES), U32),
                        pltpu.SemaphoreType.DMA((ROW_SLOTS,)),
                        pltpu.VMEM((2, D_MODEL, 2 * D_FF), F32), pltpu.VMEM((2, D_FF, D_MODEL), F32),
                        pltpu.SemaphoreType.DMA((2, 2)),
                        pltpu.SMEM((1,), I32), pltpu.SMEM((1,), I32)],
    )
    return pl.pallas_call(
        functools.partial(_expert_kernel, n_tiles=n_tiles, n_prompt_tiles=n_prompt_tiles),
        grid_spec=grid_spec,
        out_shape=jax.ShapeDtypeStruct((rows, PACK_S, LANES), U32),
        compiler_params=pltpu.CompilerParams(
            dimension_semantics=("arbitrary",), vmem_limit_bytes=VMEM_LIMIT),
        name="experts",
    )(plan['block_e'], plan['block_q'], plan['n_used'], plan['src_t'], plan['off_t'], plan['n_t'],
      plan['tot'], xsp, xss, w_gu, b_gu, w_down, b_down)


def _combine_kernel(ysrc_ref, sloc_ref, n_ref, col_ref, h_ref, ys_hbm, o_ref, buf, sems,
                    *, tm, tile0):
    j = pl.program_id(0)
    nt = pl.num_programs(0)
    rt = TOP_K * tm

    def issue(t, slot, enabled):
        base = (tile0 + jnp.minimum(t, nt - 1)) * N_EXPERTS
        for e in range(N_EXPERTS):
            n = n_ref[base + e]

            @pl.when(enabled & (n > 0))
            def _():
                pltpu.make_async_copy(ys_hbm.at[pl.ds(ysrc_ref[base + e], n)],
                                      buf.at[slot, pl.ds(sloc_ref[base + e], n)],
                                      sems.at[slot]).start()

    @pl.when(j == 0)
    def _():
        buf[ROW_SLOTS, 0:8 // PACK_S] = jnp.zeros((8 // PACK_S, PACK_S, LANES), U32)
        for ahead in range(LOOKAHEAD):
            issue(ahead, ahead, ahead < nt)

    slot = j % ROW_SLOTS
    pltpu.make_async_copy(ys_hbm.at[pl.ds(0, rt)], buf.at[slot], sems.at[slot]).wait()
    y = _unpack_rows(buf.at[slot])
    issue(j + LOOKAHEAD, (j + LOOKAHEAD) % ROW_SLOTS, j + LOOKAHEAD < nt)
    col = col_ref[...]
    rio = lax.broadcasted_iota(I32, (tm, rt), 1)
    gmat = jnp.zeros((tm, rt), F32)
    for k in range(TOP_K):
        gmat = gmat + jnp.where(rio == col[:, k:k + 1].astype(I32),
                                col[:, TOP_K + k:TOP_K + k + 1], 0.0)
    o_ref[...] = (h_ref[...] + _anchor_zero_row(buf)) + _dot(gmat.astype(BF16), y)


def _combine(plan, col, h2d, ys, *, tm, tile0):
    t = h2d.shape[0]
    rt = TOP_K * tm
    grid_spec = pltpu.PrefetchScalarGridSpec(
        num_scalar_prefetch=3,
        grid=(t // tm,),
        in_specs=[pl.BlockSpec((tm, LANES), lambda i, *_: (i, 0)),
                  pl.BlockSpec((tm, D_MODEL), lambda i, *_: (i, 0)),
                  pl.BlockSpec(memory_space=pl.ANY)],
        out_specs=pl.BlockSpec((tm, D_MODEL), lambda i, *_: (i, 0)),
        scratch_shapes=[pltpu.VMEM((ROW_SLOTS + 1, rt, PACK_S, LANES), U32),
                        pltpu.SemaphoreType.DMA((ROW_SLOTS,))],
    )
    return pl.pallas_call(
        functools.partial(_combine_kernel, tm=tm, tile0=tile0),
        grid_spec=grid_spec,
        out_shape=jax.ShapeDtypeStruct((t, D_MODEL), F32),
        compiler_params=pltpu.CompilerParams(
            dimension_semantics=("arbitrary",), vmem_limit_bytes=VMEM_LIMIT),
        name="combine",
    )(plan['ysrc'], plan['sloc'], plan['n'], col, h2d, ys)


def kernel(x_prompt, x_sample, cache_k, cache_v, state_conv, norm1_g, w_in, q_norm_g, k_norm_g,
           sinks, conv_w, w_out, norm2_g, w_router, b_router, w_gu, b_gu, w_down, b_down):
    l = 0
    nbp, sp, _ = x_prompt.shape
    nbs, ss, _ = x_sample.shape
    tp, ts = nbp * sp, nbs * ss
    tmp, tms = TOKEN_TILE, ts
    ntp, nts = tp // tmp, ts // tms
    n_tiles = ntp + nts
    n_rows = (tp + ts) * TOP_K
    grp = jnp.arange(ATT_DIM) // HEAD_DIM
    w = {
        'g1': norm1_g[l][None, :],
        'w_in': w_in[l].astype(BF16),
        'qg': jnp.tile(q_norm_g[l], N_HEADS)[None, :],
        'kg': jnp.tile(k_norm_g[l], N_KV_HEADS)[None, :],
        'bd': (grp[:, None] == grp[None, :]).astype(BF16),
        'sinks': sinks[l][None, :],
        'conv_w': conv_w[l],
        'w_out': w_out[l].astype(BF16),
        'g2': norm2_g[l][None, :],
        'w_router_t': w_router[l].T,
        'b_router_t': b_router[l][:, None],
    }
    hp, xsp, colp, tcp, knp, vnp, unp = _mixer(x_prompt, None, w, tm=tmp, nsub=STREAMS_PER_STEP,
                                               st=tmp)
    cache = (cache_k[l].reshape(nbs, WINDOW, KV_DIM), cache_v[l].reshape(nbs, WINDOW, KV_DIM),
             state_conv[l])
    hs, xss, cols, tcs, kns, vns, uns = _mixer(x_sample, cache, w, tm=ss, nsub=nbs, st=tms)

    n = jnp.concatenate([tcp.reshape(ntp, N_EXPERTS), tcs.reshape(nts, N_EXPERTS)], axis=0)
    tile_base = jnp.concatenate([jnp.arange(ntp, dtype=I32) * (TOP_K * tmp),
                                 jnp.arange(nts, dtype=I32) * (TOP_K * tms)])
    sloc = jnp.cumsum(n, axis=1) - n
    off = jnp.cumsum(n, axis=0) - n
    tot = jnp.sum(n, axis=0)
    padded = (tot + ROW_BLOCK - 1) // ROW_BLOCK * ROW_BLOCK
    ends = jnp.cumsum(padded)
    base = ends - padded
    rows = (-(-n_rows // ROW_BLOCK) + N_EXPERTS) * ROW_BLOCK
    nblk = rows // ROW_BLOCK
    blk0 = jnp.arange(nblk, dtype=I32) * ROW_BLOCK
    block_e = jnp.minimum(jnp.sum((ends[None, :] <= blk0[:, None]).astype(I32), axis=1),
                          N_EXPERTS - 1)
    eq = (block_e[:, None] == jnp.arange(N_EXPERTS, dtype=I32)[None, :]).astype(I32)
    block_q = (blk0 - jnp.sum(eq * base[None, :], axis=1)) // ROW_BLOCK
    plan = {
        'block_e': block_e.astype(I32),
        'block_q': block_q.astype(I32),
        'n_used': (ends[-1:] // ROW_BLOCK).astype(I32),
        'src_t': (tile_base[:, None] + sloc).T.reshape(-1).astype(I32),
        'off_t': off.T.reshape(-1).astype(I32),
        'n_t': n.T.reshape(-1).astype(I32),
        'tot': tot.astype(I32),
        'ysrc': (base[None, :] + off).reshape(-1).astype(I32),
        'sloc': sloc.reshape(-1).astype(I32),
        'n': n.reshape(-1).astype(I32),
    }

    ys = _experts(plan, xsp.reshape(-1, PACK_S, LANES), xss.reshape(-1, PACK_S, LANES),
                  w_gu[l], b_gu[l], w_down[l], b_down[l],
                  n_tiles=n_tiles, n_prompt_tiles=ntp, rows=rows)
    yp = _combine(plan, colp.reshape(tp, LANES), hp.reshape(tp, D_MODEL), ys, tm=tmp, tile0=0)
    ysm = _combine(plan, cols.reshape(ts, LANES), hs.reshape(ts, D_MODEL), ys, tm=tms, tile0=ntp)

    kv5 = lambda a: a.reshape(1, a.shape[0], WINDOW, N_KV_HEADS, HEAD_DIM)
    return (yp.reshape(nbp, sp, D_MODEL), ysm.reshape(nbs, ss, D_MODEL),
            kv5(knp), kv5(vnp), unp[None], kv5(kns), kv5(vns), uns[None])
```

```python
import functools

import jax
import jax.numpy as jnp
from jax import lax
from jax.experimental import pallas as pl
from jax.experimental.pallas import tpu as pltpu

D_MODEL = 1024
CHUNK = 64
HEAD_DIM = 64
N_HEADS = 8
N_KV_HEADS = 2
Q_PER_KV = N_HEADS // N_KV_HEADS
ATT_DIM = N_HEADS * HEAD_DIM
KV_DIM = N_KV_HEADS * HEAD_DIM
GROUP_W = Q_PER_KV * HEAD_DIM
CONV_DIM = D_MODEL - ATT_DIM
CONV_W = 3
WINDOW = 128
IN_DIM = ATT_DIM + 2 * KV_DIM + 3 * CONV_DIM
N_EXPERTS = 32
TOP_K = 4
D_FF = D_MODEL
SWIGLU_LIMIT = 7.0
SWIGLU_ALPHA = 1.702
EPS = 1e-6
NEG = -1e30
LOG2_E = 1.4426950408889634

LANES = 128
HALF = D_MODEL // 2
PACK_S = HALF // LANES
TOKEN_TILE = 256
STREAMS_PER_STEP = 4
ROW_BLOCK = 512
TAIL_ROWS = 128
INLINE_SEGMENTS = 20
LOOKAHEAD = 2
ROW_SLOTS = LOOKAHEAD + 1
VMEM_LIMIT = 56 * 1024 * 1024
MIXER_VMEM_LIMIT = 60 * 1024 * 1024

F32 = jnp.float32
BF16 = jnp.bfloat16
U32 = jnp.uint32
I32 = jnp.int32


def _rms_rows(x, g):
    return x * lax.rsqrt(jnp.mean(x * x, axis=-1, keepdims=True) + EPS) * g


def _split_bf16(x):
    hi = x.astype(BF16)
    lo = (x - hi.astype(F32)).astype(BF16)
    return hi, lo


def _dot(a, b):
    return jnp.dot(a, b, preferred_element_type=F32)


def _dot_nt(a, b):
    return lax.dot_general(a, b, (((1,), (1,)), ((), ())), preferred_element_type=F32)


def _head_rms(t, bd, g):
    ssq = _dot((t * t).astype(BF16), bd)
    return t * lax.rsqrt(ssq * (1.0 / HEAD_DIM) + EPS) * g


def _pack_rows(vals, out_ref):
    r = vals.shape[0]
    bits = pltpu.bitcast(vals, U32)
    word = (bits[:, :HALF] >> 16) | (bits[:, HALF:] & jnp.uint32(0xFFFF0000))
    flat = out_ref.reshape(r * PACK_S, LANES)
    for s in range(PACK_S):
        flat[pl.ds(s, r, stride=PACK_S), :] = word[:, s * LANES:(s + 1) * LANES]


def _unpack_rows(packed_ref):
    r = packed_ref.shape[0]
    flat = packed_ref.reshape(r * PACK_S, LANES)
    lo, hi = [], []
    for s in range(PACK_S):
        w = flat[pl.ds(s, r, stride=PACK_S), :]
        lo.append(pltpu.bitcast(w << 16, F32))
        hi.append(pltpu.bitcast(w & jnp.uint32(0xFFFF0000), F32))
    return jnp.concatenate(lo + hi, axis=1).astype(BF16)


def _anchor_zero_row(staging_ref):
    z = staging_ref.at[ROW_SLOTS, pl.ds(0, 8 // PACK_S)].reshape(8, LANES)[...]
    return jnp.concatenate([pltpu.bitcast(z, F32)[0:1, :]] * (D_MODEL // LANES), axis=1)


def _spread_kv(t):
    assert N_KV_HEADS == 2 and KV_DIM == LANES
    swapped = pltpu.roll(t, HEAD_DIM, axis=1)
    low = lax.broadcasted_iota(I32, t.shape, 1) < HEAD_DIM
    pair0 = jnp.where(low, t, swapped)
    pair1 = jnp.where(low, swapped, t)
    reps = GROUP_W // LANES
    return jnp.concatenate([pair0] * reps + [pair1] * reps, axis=1).astype(BF16)


def _attention(i, qn, kall, vall, sink_ref, *, mask_history):
    tm = qn.shape[0]
    k4 = _spread_kv(kall)
    v4 = _spread_kv(vall)
    qb = (qn * (HEAD_DIM ** -0.5 * LOG2_E)).astype(BF16)
    lane_grp = lax.broadcasted_iota(I32, (CHUNK, GROUP_W), 1) // HEAD_DIM
    low_half = lax.broadcasted_iota(I32, (CHUNK, LANES), 1) < HEAD_DIM
    row_grp = lax.broadcasted_iota(I32, (Q_PER_KV * CHUNK, 1), 0) // CHUNK
    span = WINDOW + CHUNK
    att_rows = []
    for c in range(tm // CHUNK):
        per_kv = []
        for hk in range(N_KV_HEADS):
            qc = qb[c * CHUNK:(c + 1) * CHUNK, hk * GROUP_W:(hk + 1) * GROUP_W]
            qs = jnp.concatenate([jnp.where(lane_grp == g, qc, jnp.zeros_like(qc))
                                  for g in range(Q_PER_KV)], axis=0)
            kw = k4[c * CHUNK:c * CHUNK + span, hk * GROUP_W:(hk + 1) * GROUP_W]
            vw = v4[c * CHUNK:c * CHUNK + span, hk * GROUP_W:(hk + 1) * GROUP_W]
            s = _dot_nt(qs, kw)
            if mask_history and c * CHUNK < WINDOW:
                kcol = lax.broadcasted_iota(I32, (1, span), 1)
                s = jnp.where(kcol >= jnp.where(i == 0, WINDOW - c * CHUNK, 0), s, NEG)
            sink = jnp.zeros((Q_PER_KV * CHUNK, 1), F32)
            for g in range(Q_PER_KV):
                sink = jnp.where(row_grp == g, sink_ref[0, hk * Q_PER_KV + g] * LOG2_E, sink)
            m = jnp.maximum(jnp.max(s, axis=-1, keepdims=True), sink)
            p = jnp.exp2(s - m)
            inv = 1.0 / (jnp.sum(p, axis=-1, keepdims=True) + jnp.exp2(sink - m))
            o = _dot(p.astype(BF16), vw)
            cols = []
            for col in range(GROUP_W // LANES):
                pair = []
                for g in range(2 * col, 2 * col + 2):
                    rows = slice(g * CHUNK, (g + 1) * CHUNK)
                    pair.append(o[rows, col * LANES:(col + 1) * LANES] * inv[rows])
                cols.append(jnp.where(low_half, pair[0], pair[1]))
            per_kv.append(jnp.concatenate(cols, axis=1))
        att_rows.append(jnp.concatenate(per_kv, axis=1))
    return jnp.concatenate(att_rows, axis=0)


def _mixer_kernel(*refs, tm, nsub, st, has_cache):
    refs = list(refs)
    x_ref = refs.pop(0)
    if has_cache:
        ck_ref, cv_ref, st_ref = refs.pop(0), refs.pop(0), refs.pop(0)
    (g1_ref, win_ref, qg_ref, kg_ref, bd_ref, sink_ref, cw_ref, wout_ref, g2_ref,
     wrt_ref, brt_ref, upper_ref, lower_ref,
     h_ref, xs_ref, col_ref, tcnt_ref, knew_ref, vnew_ref, unew_ref, kprev, vprev, ubuf) = refs
    i = pl.program_id(1)
    ntok = nsub * tm

    @pl.when(i == 0)
    def _():
        ubuf[:, 0:8, :] = jnp.zeros((nsub, 8, CONV_DIM), F32)
        if has_cache:
            kprev[...] = ck_ref[...]
            vprev[...] = cv_ref[...]
            ubuf[:, 6:8, :] = st_ref[...]
        else:
            kprev[...] = jnp.zeros(kprev.shape, F32)
            vprev[...] = jnp.zeros(vprev.shape, F32)

    x = jnp.concatenate([x_ref[sub] for sub in range(nsub)], axis=0) if nsub > 1 else x_ref[0]
    xn = _rms_rows(x, g1_ref[...])
    proj = _dot(xn.astype(BF16), win_ref[...])
    o0 = ATT_DIM
    o1 = o0 + KV_DIM
    o2 = o1 + KV_DIM
    o3 = o2 + CONV_DIM
    o4 = o3 + CONV_DIM
    bd = bd_ref[...]
    qn = _head_rms(proj[:, :o0], bd, qg_ref[...])
    kn = _head_rms(proj[:, o0:o1], bd[:KV_DIM, :KV_DIM], kg_ref[...])
    v = proj[:, o1:o2]
    bg = proj[:, o2:o3]
    u = proj[:, o3:o4] * proj[:, o4:]

    cw = cw_ref[...]
    att_parts, cv_parts = [], []
    for sub in range(nsub):
        r0, r1 = sub * tm, (sub + 1) * tm
        kall = jnp.concatenate([kprev[sub], kn[r0:r1]], axis=0)
        vall = jnp.concatenate([vprev[sub], v[r0:r1]], axis=0)
        att_parts.append(_attention(i, qn[r0:r1], kall, vall, sink_ref,
                                    mask_history=not has_cache))
        kprev[sub] = kall[tm:, :]
        vprev[sub] = vall[tm:, :]
        knew_ref[sub] = kall[tm:, :]
        vnew_ref[sub] = vall[tm:, :]

        us = u[r0:r1]
        ubuf[sub, 8:8 + tm, :] = us
        cvs = ubuf[sub, 6:6 + tm, :] * cw[0:1, :] + ubuf[sub, 7:7 + tm, :] * cw[1:2, :]
        cv_parts.append(cvs + us * cw[2:3, :])
        ubuf[sub, 0:8, :] = ubuf[sub, tm:tm + 8, :]
        unew_ref[sub] = us[tm - (CONV_W - 1):, :]
    att = jnp.concatenate(att_parts, axis=0) if nsub > 1 else att_parts[0]
    cv = jnp.concatenate(cv_parts, axis=0) if nsub > 1 else cv_parts[0]

    mix = jnp.concatenate([att, bg * cv], axis=1).astype(BF16)
    h = x + _dot(mix, wout_ref[...])
    for sub in range(nsub):
        h_ref[sub] = h[sub * tm:(sub + 1) * tm]
    hn = _rms_rows(h, g2_ref[...])

    hn_hi, hn_lo = _split_bf16(hn)
    wr_hi, wr_lo = _split_bf16(wrt_ref[...])
    logits = (_dot_nt(wr_hi, hn_hi) + _dot_nt(wr_hi, hn_lo) + _dot_nt(wr_lo, hn_hi)) + brt_ref[...]
    eio = lax.broadcasted_iota(I32, (N_EXPERTS, ntok), 0)
    work = logits
    tops, sels = [], []
    for _ in range(TOP_K):
        mk = jnp.max(work, axis=0, keepdims=True)
        ik = jnp.min(jnp.where(work == mk, eio, N_EXPERTS), axis=0, keepdims=True)
        sel = eio == ik
        work = jnp.where(sel, -jnp.inf, work)
        tops.append(mk)
        sels.append(sel)
    es = [jnp.exp(t - tops[0]) for t in tops]
    esum = es[0] + es[1] + es[2] + es[3]

    onehot = jnp.zeros((N_EXPERTS, ntok), F32)
    for sel in sels:
        onehot = onehot + jnp.where(sel, 1.0, 0.0)
    onehot_b = onehot.astype(BF16)
    smaller = _dot(lower_ref[...], onehot_b)
    nst = ntok // st
    where_parts = []
    for k in range(nst):
        c0, c1 = k * st, (k + 1) * st
        before = _dot(onehot_b[:, c0:c1], upper_ref[...])
        start = jnp.sum(smaller[:, c0:c1], axis=1, keepdims=True)
        where_parts.append(before + start)
    where_to = jnp.concatenate(where_parts, axis=1) if nst > 1 else where_parts[0]
    pos = [jnp.sum(jnp.where(sel, where_to, 0.0), axis=0, keepdims=True) for sel in sels]
    rio = lax.broadcasted_iota(I32, (TOP_K * st, st), 0)
    for k in range(nst):
        c0, c1 = k * st, (k + 1) * st
        hit = rio == pos[0][:, c0:c1].astype(I32)
        for slot in range(1, TOP_K):
            hit = hit | (rio == pos[slot][:, c0:c1].astype(I32))
        perm = jnp.where(hit, 1.0, 0.0).astype(BF16)
        _pack_rows(_dot(perm, hn_hi[c0:c1]), xs_ref.at[k, 0])
        tcnt_ref[k, 0] = _dot_nt(jnp.ones((1, st), BF16), onehot_b[:, c0:c1]).astype(I32)

    rows8 = jnp.concatenate(pos + [e / esum for e in es], axis=0)
    sq = jnp.concatenate([rows8, jnp.zeros((LANES - 2 * TOP_K, ntok), F32)], axis=0)
    cols = jnp.transpose(sq)
    for sub in range(nsub):
        col_ref[sub] = cols[sub * tm:(sub + 1) * tm]


def _mixer(x, cache, w, *, tm, nsub, st):
    nb, s, _ = x.shape
    nt = s // tm
    nst = nsub * tm // st
    has_cache = cache is not None
    full = lambda shape: pl.BlockSpec(shape, lambda b, i: (0,) * len(shape))
    per_b = lambda shape: pl.BlockSpec((nsub,) + shape, lambda b, i: (b, 0, 0))
    tile = lambda last: pl.BlockSpec((nsub, tm, last), lambda b, i: (b, i, 0))
    kv_shape = (WINDOW, KV_DIM)
    kv_spec = per_b(kv_shape)

    in_specs = [tile(D_MODEL)]
    args = [x]
    if has_cache:
        in_specs += [kv_spec, kv_spec, per_b((CONV_W - 1, CONV_DIM))]
        args += list(cache)
    tio = jnp.arange(st)
    eio = jnp.arange(N_EXPERTS)
    upper = (tio[:, None] < tio[None, :]).astype(BF16)
    lower = (eio[None, :] < eio[:, None]).astype(BF16)
    in_specs += [
        full((1, D_MODEL)), full((D_MODEL, IN_DIM)), full((1, ATT_DIM)),
        full((1, KV_DIM)), full((ATT_DIM, ATT_DIM)),
        pl.BlockSpec(memory_space=pltpu.SMEM),
        full((CONV_W, CONV_DIM)), full((D_MODEL, D_MODEL)), full((1, D_MODEL)),
        full((N_EXPERTS, D_MODEL)), full((N_EXPERTS, 1)), full((st, st)),
        full((N_EXPERTS, N_EXPERTS)),
    ]
    args += [w['g1'], w['w_in'], w['qg'], w['kg'], w['bd'], w['sinks'], w['conv_w'],
             w['w_out'], w['g2'], w['w_router_t'], w['b_router_t'], upper, lower]
    rt = TOP_K * st
    nsort = nb // nsub * nst
    out_shape = [
        jax.ShapeDtypeStruct((nb, s, D_MODEL), F32),
        jax.ShapeDtypeStruct((nsort, nt, rt, PACK_S, LANES), U32),
        jax.ShapeDtypeStruct((nb, s, LANES), F32),
        jax.ShapeDtypeStruct((nsort, nt, 1, N_EXPERTS), I32),
        jax.ShapeDtypeStruct((nb,) + kv_shape, F32),
        jax.ShapeDtypeStruct((nb,) + kv_shape, F32),
        jax.ShapeDtypeStruct((nb, CONV_W - 1, CONV_DIM), F32),
    ]
    out_specs = [
        tile(D_MODEL),
        pl.BlockSpec((nst, 1, rt, PACK_S, LANES), lambda b, i: (b, i, 0, 0, 0)),
        tile(LANES),
        pl.BlockSpec((nst, 1, 1, N_EXPERTS), lambda b, i: (b, i, 0, 0)),
        kv_spec, kv_spec, per_b((CONV_W - 1, CONV_DIM)),
    ]
    scratch = [pltpu.VMEM((nsub, WINDOW, KV_DIM), F32), pltpu.VMEM((nsub, WINDOW, KV_DIM), F32),
               pltpu.VMEM((nsub, tm + 8, CONV_DIM), F32)]
    return pl.pallas_call(
        functools.partial(_mixer_kernel, tm=tm, nsub=nsub, st=st, has_cache=has_cache),
        grid=(nb // nsub, nt),
        in_specs=in_specs,
        out_specs=out_specs,
        out_shape=out_shape,
        scratch_shapes=scratch,
        compiler_params=pltpu.CompilerParams(
            dimension_semantics=("arbitrary", "arbitrary"), vmem_limit_bytes=MIXER_VMEM_LIMIT),
        name="mixer_sample" if has_cache else "mixer_prompt",
    )(*args)


def _expert_kernel(be_ref, bq_ref, nused_ref, src_ref, off_ref, n_ref, tot_ref,
                   xsp_hbm, xss_hbm, wgu_hbm, bgu_ref, wd_hbm, bdn_ref, ys_ref,
                   xbuf, sems, wgu_f, wd_f, wsems, ptr, wcur, *, n_tiles, n_prompt_tiles):
    b = pl.program_id(0)
    nused = nused_ref[0]

    def weight_copies(e, ws):
        return (pltpu.make_async_copy(wgu_hbm.at[e], wgu_f.at[ws], wsems.at[0, ws]),
                pltpu.make_async_copy(wd_hbm.at[e], wd_f.at[ws], wsems.at[1, ws]))

    def issue_rows(blk, slot, enabled, between=lambda: None):
        blk = jnp.minimum(blk, pl.num_programs(0) - 1)
        e = be_ref[blk]
        q = bq_ref[blk]
        lo_row = q * ROW_BLOCK
        hi_row = lo_row + ROW_BLOCK
        base = e * n_tiles

        def segment(j, j_limit, src_hbm, ok):
            jc = jnp.minimum(j, j_limit - 1)
            o = off_ref[base + jc]
            inside = ok & (j < j_limit) & (o < hi_row)
            lo = jnp.maximum(o, lo_row)
            ln = jnp.minimum(o + n_ref[base + jc], hi_row) - lo

            @pl.when(inside & (ln > 0))
            def _():
                pltpu.make_async_copy(src_hbm.at[pl.ds(src_ref[base + jc] + (lo - o), ln)],
                                      xbuf.at[slot, pl.ds(lo - lo_row, ln)], sems.at[slot]).start()
            return inside

        j0 = jnp.where(q == 0, 0, ptr[0])

        def cond(j):
            return (enabled & (j < n_prompt_tiles)
                    & (off_ref[base + jnp.minimum(j, n_prompt_tiles - 1)] < hi_row))

        def body(j):
            segment(j, n_prompt_tiles, xsp_hbm, True)
            return j + 1

        jtail = lax.while_loop(cond, body, j0 + INLINE_SEGMENTS)
        result = between()
        count = jnp.int32(0)
        for k in range(INLINE_SEGMENTS):
            count = count + segment(j0 + k, n_prompt_tiles, xsp_hbm, enabled).astype(I32)
        jend = jnp.where(count == INLINE_SEGMENTS, jtail, j0 + count)
        ptr[0] = jnp.where(enabled, jnp.maximum(jend - 1, 0), ptr[0])
        for j in range(n_prompt_tiles, n_tiles):
            segment(j, n_tiles, xss_hbm, enabled)
        return result

    @pl.when(b == 0)
    def _():
        xbuf[...] = jnp.zeros(xbuf.shape, U32)
        for ahead in range(LOOKAHEAD):
            issue_rows(ahead, ahead, ahead < nused)
        for c in weight_copies(be_ref[0], 0):
            c.start()
        wcur[0] = 1

    active = b < nused

    @pl.when(active)
    def _():
        slot = b % ROW_SLOTS
        e = be_ref[b]
        rows = jnp.minimum(tot_ref[e] - bq_ref[b] * ROW_BLOCK, ROW_BLOCK)
        pltpu.make_async_copy(xsp_hbm.at[pl.ds(0, rows)], xbuf.at[slot, pl.ds(0, rows)],
                              sems.at[slot]).wait()

        @pl.when(bq_ref[b] == 0)
        def _():
            ws = 1 - wcur[0]
            wcur[0] = ws
            for c in weight_copies(e, ws):
                c.wait()
            nxt = b + (tot_ref[e] + ROW_BLOCK - 1) // ROW_BLOCK

            @pl.when(nxt < nused)
            def _():
                for c in weight_copies(be_ref[jnp.minimum(nxt, pl.num_programs(0) - 1)], 1 - ws):
                    c.start()

        ws = wcur[0]

        def compute(nrows):
            x = issue_rows(b + LOOKAHEAD, (b + LOOKAHEAD) % ROW_SLOTS, b + LOOKAHEAD < nused,
                           between=lambda: _unpack_rows(xbuf.at[slot, pl.ds(0, nrows)]))
            gu = _dot(x, wgu_f[ws]) + bgu_ref[pl.ds(e, 1), :]
            g = jnp.minimum(gu[:, :D_FF], SWIGLU_LIMIT)
            u = jnp.clip(gu[:, D_FF:], -SWIGLU_LIMIT, SWIGLU_LIMIT)
            act = (u + 1.0) * (g * jax.nn.sigmoid(SWIGLU_ALPHA * g))
            y = (_dot(act.astype(BF16), wd_f[ws])
                 + (bdn_ref[pl.ds(e, 1), :] + _anchor_zero_row(xbuf)))
            _pack_rows(y.astype(BF16).astype(F32), ys_ref.at[pl.ds(0, nrows)])
            if nrows < ROW_BLOCK:
                ys_ref[nrows:] = jnp.zeros((ROW_BLOCK - nrows, PACK_S, LANES), U32)

        for piece in range(ROW_BLOCK // TAIL_ROWS):
            top = (piece + 1) * TAIL_ROWS
            pl.when((rows > top - TAIL_ROWS) & (rows <= top))(functools.partial(compute, top))

    @pl.when(jnp.logical_not(active))
    def _():
        ys_ref[...] = jnp.zeros(ys_ref.shape, U32)


def _experts(plan, xsp, xss, w_gu, b_gu, w_down, b_down, *, n_tiles, n_prompt_tiles, rows):
    nblk = rows // ROW_BLOCK

    grid_spec = pltpu.PrefetchScalarGridSpec(
        num_scalar_prefetch=7,
        grid=(nblk,),
        in_specs=[pl.BlockSpec(memory_space=pl.ANY),
                  pl.BlockSpec(memory_space=pl.ANY),
                  pl.BlockSpec(memory_space=pl.ANY),
                  pl.BlockSpec((N_EXPERTS, 2 * D_FF), lambda b, *_: (0, 0)),
                  pl.BlockSpec(memory_space=pl.ANY),
                  pl.BlockSpec((N_EXPERTS, D_MODEL), lambda b, *_: (0, 0))],
        out_specs=pl.BlockSpec((ROW_BLOCK, PACK_S, LANES), lambda b, *_: (b, 0, 0)),
        scratch_shapes=[pltpu.VMEM((ROW_SLOTS + 1, ROW_BLOCK, PACK_S, LANES), U32),
                        pltpu.SemaphoreType.DMA((ROW_SLOTS,)),
                        pltpu.VMEM((2, D_MODEL, 2 * D_FF), F32), pltpu.VMEM((2, D_FF, D_MODEL), F32),
                        pltpu.SemaphoreType.DMA((2, 2)),
                        pltpu.SMEM((1,), I32), pltpu.SMEM((1,), I32)],
    )
    return pl.pallas_call(
        functools.partial(_expert_kernel, n_tiles=n_tiles, n_prompt_tiles=n_prompt_tiles),
        grid_spec=grid_spec,
        out_shape=jax.ShapeDtypeStruct((rows, PACK_S, LANES), U32),
        compiler_params=pltpu.CompilerParams(
            dimension_semantics=("arbitrary",), vmem_limit_bytes=VMEM_LIMIT),
        name="experts",
    )(plan['block_e'], plan['block_q'], plan['n_used'], plan['src_t'], plan['off_t'], plan['n_t'],
      plan['tot'], xsp, xss, w_gu, b_gu, w_down, b_down)


def _combine_kernel(ysrc_ref, sloc_ref, n_ref, col_ref, h_ref, ys_hbm, o_ref, buf, sems,
                    *, tm, tile0):
    j = pl.program_id(0)
    nt = pl.num_programs(0)
    rt = TOP_K * tm

    def issue(t, slot, enabled):
        base = (tile0 + jnp.minimum(t, nt - 1)) * N_EXPERTS
        for e in range(N_EXPERTS):
            n = n_ref[base + e]

            @pl.when(enabled & (n > 0))
            def _():
                pltpu.make_async_copy(ys_hbm.at[pl.ds(ysrc_ref[base + e], n)],
                                      buf.at[slot, pl.ds(sloc_ref[base + e], n)],
                                      sems.at[slot]).start()

    @pl.when(j == 0)
    def _():
        buf[ROW_SLOTS, 0:8 // PACK_S] = jnp.zeros((8 // PACK_S, PACK_S, LANES), U32)
        for ahead in range(LOOKAHEAD):
            issue(ahead, ahead, ahead < nt)

    slot = j % ROW_SLOTS
    pltpu.make_async_copy(ys_hbm.at[pl.ds(0, rt)], buf.at[slot], sems.at[slot]).wait()
    y = _unpack_rows(buf.at[slot])
    issue(j + LOOKAHEAD, (j + LOOKAHEAD) % ROW_SLOTS, j + LOOKAHEAD < nt)
    col = col_ref[...]
    rio = lax.broadcasted_iota(I32, (tm, rt), 1)
    gmat = jnp.zeros((tm, rt), F32)
    for k in range(TOP_K):
        gmat = gmat + jnp.where(rio == col[:, k:k + 1].astype(I32),
                                col[:, TOP_K + k:TOP_K + k + 1], 0.0)
    o_ref[...] = (h_ref[...] + _anchor_zero_row(buf)) + _dot(gmat.astype(BF16), y)


def _combine(plan, col, h2d, ys, *, tm, tile0):
    t = h2d.shape[0]
    rt = TOP_K * tm
    grid_spec = pltpu.PrefetchScalarGridSpec(
        num_scalar_prefetch=3,
        grid=(t // tm,),
        in_specs=[pl.BlockSpec((tm, LANES), lambda i, *_: (i, 0)),
                  pl.BlockSpec((tm, D_MODEL), lambda i, *_: (i, 0)),
                  pl.BlockSpec(memory_space=pl.ANY)],
        out_specs=pl.BlockSpec((tm, D_MODEL), lambda i, *_: (i, 0)),
        scratch_shapes=[pltpu.VMEM((ROW_SLOTS + 1, rt, PACK_S, LANES), U32),
                        pltpu.SemaphoreType.DMA((ROW_SLOTS,))],
    )
    return pl.pallas_call(
        functools.partial(_combine_kernel, tm=tm, tile0=tile0),
        grid_spec=grid_spec,
        out_shape=jax.ShapeDtypeStruct((t, D_MODEL), F32),
        compiler_params=pltpu.CompilerParams(
            dimension_semantics=("arbitrary",), vmem_limit_bytes=VMEM_LIMIT),
        name="combine",
    )(plan['ysrc'], plan['sloc'], plan['n'], col, h2d, ys)


def kernel(x_prompt, x_sample, cache_k, cache_v, state_conv, norm1_g, w_in, q_norm_g, k_norm_g,
           sinks, conv_w, w_out, norm2_g, w_router, b_router, w_gu, b_gu, w_down, b_down):
    l = 0
    nbp, sp, _ = x_prompt.shape
    nbs, ss, _ = x_sample.shape
    tp, ts = nbp * sp, nbs * ss
    tmp, tms = TOKEN_TILE, ts
    ntp, nts = tp // tmp, ts // tms
    n_tiles = ntp + nts
    n_rows = (tp + ts) * TOP_K
    grp = jnp.arange(ATT_DIM) // HEAD_DIM
    w = {
        'g1': norm1_g[l][None, :],
        'w_in': w_in[l].astype(BF16),
        'qg': jnp.tile(q_norm_g[l], N_HEADS)[None, :],
        'kg': jnp.tile(k_norm_g[l], N_KV_HEADS)[None, :],
        'bd': (grp[:, None] == grp[None, :]).astype(BF16),
        'sinks': sinks[l][None, :],
        'conv_w': conv_w[l],
        'w_out': w_out[l].astype(BF16),
        'g2': norm2_g[l][None, :],
        'w_router_t': w_router[l].T,
        'b_router_t': b_router[l][:, None],
    }
    hp, xsp, colp, tcp, knp, vnp, unp = _mixer(x_prompt, None, w, tm=tmp, nsub=STREAMS_PER_STEP,
                                               st=tmp)
    cache = (cache_k[l].reshape(nbs, WINDOW, KV_DIM), cache_v[l].reshape(nbs, WINDOW, KV_DIM),
             state_conv[l])
    hs, xss, cols, tcs, kns, vns, uns = _mixer(x_sample, cache, w, tm=ss, nsub=nbs, st=tms)

    n = jnp.concatenate([tcp.reshape(ntp, N_EXPERTS), tcs.reshape(nts, N_EXPERTS)], axis=0)
    tile_base = jnp.concatenate([jnp.arange(ntp, dtype=I32) * (TOP_K * tmp),
                                 jnp.arange(nts, dtype=I32) * (TOP_K * tms)])
    sloc = jnp.cumsum(n, axis=1) - n
    off = jnp.cumsum(n, axis=0) - n
    tot = jnp.sum(n, axis=0)
    padded = (tot + ROW_BLOCK - 1) // ROW_BLOCK * ROW_BLOCK
    ends = jnp.cumsum(padded)
    base = ends - padded
    rows = (-(-n_rows // ROW_BLOCK) + N_EXPERTS) * ROW_BLOCK
    nblk = rows // ROW_BLOCK
    blk0 = jnp.arange(nblk, dtype=I32) * ROW_BLOCK
    block_e = jnp.minimum(jnp.sum((ends[None, :] <= blk0[:, None]).astype(I32), axis=1),
                          N_EXPERTS - 1)
    eq = (block_e[:, None] == jnp.arange(N_EXPERTS, dtype=I32)[None, :]).astype(I32)
    block_q = (blk0 - jnp.sum(eq * base[None, :], axis=1)) // ROW_BLOCK
    plan = {
        'block_e': block_e.astype(I32),
        'block_q': block_q.astype(I32),
        'n_used': (ends[-1:] // ROW_BLOCK).astype(I32),
        'src_t': (tile_base[:, None] + sloc).T.reshape(-1).astype(I32),
        'off_t': off.T.reshape(-1).astype(I32),
        'n_t': n.T.reshape(-1).astype(I32),
        'tot': tot.astype(I32),
        'ysrc': (base[None, :] + off).reshape(-1).astype(I32),
        'sloc': sloc.reshape(-1).astype(I32),
        'n': n.reshape(-1).astype(I32),
    }

    ys = _experts(plan, xsp.reshape(-1, PACK_S, LANES), xss.reshape(-1, PACK_S, LANES),
                  w_gu[l], b_gu[l], w_down[l], b_down[l],
                  n_tiles=n_tiles, n_prompt_tiles=ntp, rows=rows)
    yp = _combine(plan, colp.reshape(tp, LANES), hp.reshape(tp, D_MODEL), ys, tm=tmp, tile0=0)
    ysm = _combine(plan, cols.reshape(ts, LANES), hs.reshape(ts, D_MODEL), ys, tm=tms, tile0=ntp)

    kv5 = lambda a: a.reshape(1, a.shape[0], WINDOW, N_KV_HEADS, HEAD_DIM)
    return (yp.reshape(nbp, sp, D_MODEL), ysm.reshape(nbs, ss, D_MODEL),
            kv5(knp), kv5(vnp), unp[None], kv5(kns), kv5(vns), uns[None])
```

```python
import functools

import jax
import jax.numpy as jnp
from jax import lax
from jax.experimental import pallas as pl
from jax.experimental.pallas import tpu as pltpu

D_MODEL = 1024
CHUNK = 64
HEAD_DIM = 64
N_HEADS = 8
N_KV_HEADS = 2
Q_PER_KV = N_HEADS // N_KV_HEADS
ATT_DIM = N_HEADS * HEAD_DIM
KV_DIM = N_KV_HEADS * HEAD_DIM
GROUP_W = Q_PER_KV * HEAD_DIM
CONV_DIM = D_MODEL - ATT_DIM
CONV_W = 3
WINDOW = 128
IN_DIM = ATT_DIM + 2 * KV_DIM + 3 * CONV_DIM
N_EXPERTS = 32
TOP_K = 4
D_FF = D_MODEL
SWIGLU_LIMIT = 7.0
SWIGLU_ALPHA = 1.702
EPS = 1e-6
NEG = -1e30

LANES = 128
SUBLANES = 8
HALF = D_MODEL // 2
PACK_S = HALF // LANES
TOKEN_TILE = 256
STREAMS_PER_STEP = 4
ROW_BLOCK = 512
TAIL_ROWS = 128
INLINE_SEGMENTS = 24
LOOKAHEAD = 2
ROW_SLOTS = LOOKAHEAD + 1
VMEM_LIMIT = 56 * 1024 * 1024
MIXER_VMEM_LIMIT = 60 * 1024 * 1024

F32 = jnp.float32
BF16 = jnp.bfloat16
U32 = jnp.uint32
I32 = jnp.int32


def _rms_rows(x, g):
    return x * lax.rsqrt(jnp.mean(x * x, axis=-1, keepdims=True) + EPS) * g


def _split_bf16(x):
    hi = x.astype(BF16)
    lo = (x - hi.astype(F32)).astype(BF16)
    return hi, lo


def _dot(a, b):
    return jnp.dot(a, b, preferred_element_type=F32)


def _dot_nt(a, b):
    return lax.dot_general(a, b, (((1,), (1,)), ((), ())), preferred_element_type=F32)


def _head_rms(t, bd, g):
    ssq = _dot((t * t).astype(BF16), bd)
    return t * lax.rsqrt(ssq * (1.0 / HEAD_DIM) + EPS) * g


def _pack_rows(vals, out_ref):
    r = vals.shape[0]
    bits = pltpu.bitcast(vals, U32)
    word = (bits[:, :HALF] >> 16) | (bits[:, HALF:] & jnp.uint32(0xFFFF0000))
    flat = out_ref.reshape(r * PACK_S, LANES)
    for s in range(PACK_S):
        flat[pl.ds(s, r, stride=PACK_S), :] = word[:, s * LANES:(s + 1) * LANES]


def _unpack_rows(packed_ref):
    r = packed_ref.shape[0]
    flat = packed_ref.reshape(r * PACK_S, LANES)
    lo, hi = [], []
    for s in range(PACK_S):
        w = flat[pl.ds(s, r, stride=PACK_S), :]
        lo.append(pltpu.bitcast(w << 16, F32))
        hi.append(pltpu.bitcast(w & jnp.uint32(0xFFFF0000), F32))
    return jnp.concatenate(lo + hi, axis=1).astype(BF16)


def _anchor_zero_row(staging_ref):
    z = staging_ref.at[ROW_SLOTS, pl.ds(0, SUBLANES // PACK_S)].reshape(SUBLANES, LANES)[...]
    return jnp.concatenate([pltpu.bitcast(z, F32)[0:1, :]] * (D_MODEL // LANES), axis=1)


def _spread_kv(t):
    assert N_KV_HEADS == 2 and KV_DIM == LANES
    swapped = pltpu.roll(t, HEAD_DIM, axis=1)
    low = lax.broadcasted_iota(I32, t.shape, 1) < HEAD_DIM
    pair0 = jnp.where(low, t, swapped)
    pair1 = jnp.where(low, swapped, t)
    reps = GROUP_W // LANES
    return jnp.concatenate([pair0] * reps + [pair1] * reps, axis=1).astype(BF16)


def _attention(i, qn, kall, vall, sink_ref, *, mask_history):
    tm = qn.shape[0]
    k4 = _spread_kv(kall)
    v4 = _spread_kv(vall)
    lane_grp = lax.broadcasted_iota(I32, (CHUNK, GROUP_W), 1) // HEAD_DIM
    row_grp = lax.broadcasted_iota(I32, (Q_PER_KV * CHUNK, 1), 0) // CHUNK
    span = WINDOW + CHUNK
    att_rows = []
    for c in range(tm // CHUNK):
        per_kv = []
        for hk in range(N_KV_HEADS):
            qc = qn[c * CHUNK:(c + 1) * CHUNK, hk * GROUP_W:(hk + 1) * GROUP_W]
            qs = jnp.concatenate([jnp.where(lane_grp == g, qc, 0.0) for g in range(Q_PER_KV)],
                                 axis=0).astype(BF16)
            kw = k4[c * CHUNK:c * CHUNK + span, hk * GROUP_W:(hk + 1) * GROUP_W]
            vw = v4[c * CHUNK:c * CHUNK + span, hk * GROUP_W:(hk + 1) * GROUP_W]
            s = _dot_nt(qs, kw) * (HEAD_DIM ** -0.5)
            if mask_history and c * CHUNK < WINDOW:
                kcol = lax.broadcasted_iota(I32, (1, span), 1)
                s = jnp.where(kcol >= jnp.where(i == 0, WINDOW - c * CHUNK, 0), s, NEG)
            sink = jnp.zeros((Q_PER_KV * CHUNK, 1), F32)
            for g in range(Q_PER_KV):
                sink = jnp.where(row_grp == g, sink_ref[0, hk * Q_PER_KV + g], sink)
            m = jnp.maximum(jnp.max(s, axis=-1, keepdims=True), sink)
            p = jnp.exp(s - m)
            den = jnp.sum(p, axis=-1, keepdims=True) + jnp.exp(sink - m)
            o = _dot(p.astype(BF16), vw) / den
            oc = jnp.zeros((CHUNK, GROUP_W), F32)
            for g in range(Q_PER_KV):
                oc = oc + jnp.where(lane_grp == g, o[g * CHUNK:(g + 1) * CHUNK, :], 0.0)
            per_kv.append(oc)
        att_rows.append(jnp.concatenate(per_kv, axis=1))
    return jnp.concatenate(att_rows, axis=0)


def _mixer_kernel(*refs, tm, nsub, st, has_cache):
    refs = list(refs)
    x_ref = refs.pop(0)
    if has_cache:
        ck_ref, cv_ref, st_ref = refs.pop(0), refs.pop(0), refs.pop(0)
    (g1_ref, win_ref, qg_ref, kg_ref, bd_ref, sink_ref, cw_ref, wout_ref, g2_ref,
     wrt_ref, brt_ref, upper_ref, lower_ref,
     h_ref, xs_ref, col_ref, tcnt_ref, knew_ref, vnew_ref, unew_ref, kprev, vprev, ubuf) = refs
    i = pl.program_id(1)
    ntok = nsub * tm

    @pl.when(i == 0)
    def _():
        ubuf[:, 0:SUBLANES, :] = jnp.zeros((nsub, SUBLANES, CONV_DIM), F32)
        if has_cache:
            kprev[...] = ck_ref[...]
            vprev[...] = cv_ref[...]
            ubuf[:, SUBLANES - (CONV_W - 1):SUBLANES, :] = st_ref[...]
        else:
            kprev[...] = jnp.zeros(kprev.shape, F32)
            vprev[...] = jnp.zeros(vprev.shape, F32)

    x = jnp.concatenate([x_ref[sub] for sub in range(nsub)], axis=0) if nsub > 1 else x_ref[0]
    xn = _rms_rows(x, g1_ref[...])
    proj = _dot(xn.astype(BF16), win_ref[...])
    o0 = ATT_DIM
    o1 = o0 + KV_DIM
    o2 = o1 + KV_DIM
    o3 = o2 + CONV_DIM
    o4 = o3 + CONV_DIM
    bd = bd_ref[...]
    qn = _head_rms(proj[:, :o0], bd, qg_ref[...])
    kn = _head_rms(proj[:, o0:o1], bd[:KV_DIM, :KV_DIM], kg_ref[...])
    v = proj[:, o1:o2]
    bg = proj[:, o2:o3]
    u = proj[:, o3:o4] * proj[:, o4:]

    cw = cw_ref[...]
    att_parts, cv_parts = [], []
    for sub in range(nsub):
        r0, r1 = sub * tm, (sub + 1) * tm
        kall = jnp.concatenate([kprev[sub], kn[r0:r1]], axis=0)
        vall = jnp.concatenate([vprev[sub], v[r0:r1]], axis=0)
        att_parts.append(_attention(i, qn[r0:r1], kall, vall, sink_ref,
                                    mask_history=not has_cache))
        kprev[sub] = kall[tm:, :]
        vprev[sub] = vall[tm:, :]
        knew_ref[sub] = kall[tm:, :]
        vnew_ref[sub] = vall[tm:, :]

        us = u[r0:r1]
        ubuf[sub, SUBLANES:SUBLANES + tm, :] = us
        first = SUBLANES - (CONV_W - 1)
        cvs = ubuf[sub, first:first + tm, :] * cw[0:1, :]
        for tap in range(1, CONV_W - 1):
            cvs = cvs + ubuf[sub, first + tap:first + tap + tm, :] * cw[tap:tap + 1, :]
        cv_parts.append(cvs + us * cw[CONV_W - 1:CONV_W, :])
        ubuf[sub, 0:SUBLANES, :] = ubuf[sub, tm:tm + SUBLANES, :]
        unew_ref[sub] = us[tm - (CONV_W - 1):, :]
    att = jnp.concatenate(att_parts, axis=0) if nsub > 1 else att_parts[0]
    cv = jnp.concatenate(cv_parts, axis=0) if nsub > 1 else cv_parts[0]

    mix = jnp.concatenate([att, bg * cv], axis=1).astype(BF16)
    h = x + _dot(mix, wout_ref[...])
    for sub in range(nsub):
        h_ref[sub] = h[sub * tm:(sub + 1) * tm]
    hn = _rms_rows(h, g2_ref[...])

    hn_hi, hn_lo = _split_bf16(hn)
    wr_hi, wr_lo = _split_bf16(wrt_ref[...])
    logits = (_dot_nt(wr_hi, hn_hi) + _dot_nt(wr_hi, hn_lo) + _dot_nt(wr_lo, hn_hi)) + brt_ref[...]
    eio = lax.broadcasted_iota(I32, (N_EXPERTS, ntok), 0)
    work = logits
    tops, sels = [], []
    for _ in range(TOP_K):
        mk = jnp.max(work, axis=0, keepdims=True)
        ik = jnp.min(jnp.where(work == mk, eio, N_EXPERTS), axis=0, keepdims=True)
        sel = eio == ik
        work = jnp.where(sel, -jnp.inf, work)
        tops.append(mk)
        sels.append(sel)
    es = [jnp.exp(t - tops[0]) for t in tops]
    esum = es[0] + es[1] + es[2] + es[3]

    onehot = jnp.zeros((N_EXPERTS, ntok), F32)
    for sel in sels:
        onehot = onehot + jnp.where(sel, 1.0, 0.0)
    onehot_b = onehot.astype(BF16)
    smaller = _dot(lower_ref[...], onehot_b)
    nst = ntok // st
    where_parts = []
    for k in range(nst):
        c0, c1 = k * st, (k + 1) * st
        before = _dot(onehot_b[:, c0:c1], upper_ref[...])
        start = jnp.sum(smaller[:, c0:c1], axis=1, keepdims=True)
        where_parts.append(before + start)
    where_to = jnp.concatenate(where_parts, axis=1) if nst > 1 else where_parts[0]
    pos = [jnp.sum(jnp.where(sel, where_to, 0.0), axis=0, keepdims=True) for sel in sels]
    rio = lax.broadcasted_iota(I32, (TOP_K * st, st), 0)
    for k in range(nst):
        c0, c1 = k * st, (k + 1) * st
        hit = rio == pos[0][:, c0:c1].astype(I32)
        for slot in range(1, TOP_K):
            hit = hit | (rio == pos[slot][:, c0:c1].astype(I32))
        perm = jnp.where(hit, 1.0, 0.0).astype(BF16)
        _pack_rows(_dot(perm, hn_hi[c0:c1]), xs_ref.at[k, 0])
        tcnt_ref[k, 0] = _dot_nt(jnp.ones((1, st), BF16), onehot_b[:, c0:c1]).astype(I32)

    rows8 = jnp.concatenate(pos + [e / esum for e in es], axis=0)
    sq = jnp.concatenate([rows8, jnp.zeros((LANES - 2 * TOP_K, ntok), F32)], axis=0)
    cols = jnp.transpose(sq)
    for sub in range(nsub):
        col_ref[sub] = cols[sub * tm:(sub + 1) * tm]


def _mixer(x, cache, w, *, tm, nsub, st):
    nb, s, _ = x.shape
    nt = s // tm
    nst = nsub * tm // st
    has_cache = cache is not None
    full = lambda shape: pl.BlockSpec(shape, lambda b, i: (0,) * len(shape))
    per_b = lambda shape: pl.BlockSpec((nsub,) + shape, lambda b, i: (b, 0, 0))
    tile = lambda last: pl.BlockSpec((nsub, tm, last), lambda b, i: (b, i, 0))
    kv_shape = (WINDOW, KV_DIM)
    kv_spec = per_b(kv_shape)

    in_specs = [tile(D_MODEL)]
    args = [x]
    if has_cache:
        in_specs += [kv_spec, kv_spec, per_b((CONV_W - 1, CONV_DIM))]
        args += list(cache)
    tio = jnp.arange(st)
    eio = jnp.arange(N_EXPERTS)
    upper = (tio[:, None] < tio[None, :]).astype(BF16)
    lower = (eio[None, :] < eio[:, None]).astype(BF16)
    in_specs += [
        full((1, D_MODEL)), full((D_MODEL, IN_DIM)), full((1, ATT_DIM)),
        full((1, KV_DIM)), full((ATT_DIM, ATT_DIM)),
        pl.BlockSpec(memory_space=pltpu.SMEM),
        full((CONV_W, CONV_DIM)), full((D_MODEL, D_MODEL)), full((1, D_MODEL)),
        full((N_EXPERTS, D_MODEL)), full((N_EXPERTS, 1)), full((st, st)),
        full((N_EXPERTS, N_EXPERTS)),
    ]
    args += [w['g1'], w['w_in'], w['qg'], w['kg'], w['bd'], w['sinks'], w['conv_w'],
             w['w_out'], w['g2'], w['w_router_t'], w['b_router_t'], upper, lower]
    rt = TOP_K * st
    nsort = nb // nsub * nst
    out_shape = [
        jax.ShapeDtypeStruct((nb, s, D_MODEL), F32),
        jax.ShapeDtypeStruct((nsort, nt, rt, PACK_S, LANES), U32),
        jax.ShapeDtypeStruct((nb, s, LANES), F32),
        jax.ShapeDtypeStruct((nsort, nt, 1, N_EXPERTS), I32),
        jax.ShapeDtypeStruct((nb,) + kv_shape, F32),
        jax.ShapeDtypeStruct((nb,) + kv_shape, F32),
        jax.ShapeDtypeStruct((nb, CONV_W - 1, CONV_DIM), F32),
    ]
    out_specs = [
        tile(D_MODEL),
        pl.BlockSpec((nst, 1, rt, PACK_S, LANES), lambda b, i: (b, i, 0, 0, 0)),
        tile(LANES),
        pl.BlockSpec((nst, 1, 1, N_EXPERTS), lambda b, i: (b, i, 0, 0)),
        kv_spec, kv_spec, per_b((CONV_W - 1, CONV_DIM)),
    ]
    scratch = [pltpu.VMEM((nsub, WINDOW, KV_DIM), F32), pltpu.VMEM((nsub, WINDOW, KV_DIM), F32),
               pltpu.VMEM((nsub, tm + SUBLANES, CONV_DIM), F32)]
    return pl.pallas_call(
        functools.partial(_mixer_kernel, tm=tm, nsub=nsub, st=st, has_cache=has_cache),
        grid=(nb // nsub, nt),
        in_specs=in_specs,
        out_specs=out_specs,
        out_shape=out_shape,
        scratch_shapes=scratch,
        compiler_params=pltpu.CompilerParams(
            dimension_semantics=("arbitrary", "arbitrary"), vmem_limit_bytes=MIXER_VMEM_LIMIT),
        name="mixer_sample" if has_cache else "mixer_prompt",
    )(*args)


def _expert_kernel(be_ref, bq_ref, nused_ref, src_ref, off_ref, n_ref, tot_ref,
                   xsp_hbm, xss_hbm, wgu_hbm, bgu_ref, wd_hbm, bdn_ref, ys_ref,
                   xbuf, sems, wgu_f, wd_f, wsems, ptr, wcur, *, n_tiles, n_prompt_tiles):
    b = pl.program_id(0)
    nused = nused_ref[0]

    def weight_copies(e, ws):
        return (pltpu.make_async_copy(wgu_hbm.at[e], wgu_f.at[ws], wsems.at[0, ws]),
                pltpu.make_async_copy(wd_hbm.at[e], wd_f.at[ws], wsems.at[1, ws]))

    def issue_rows(blk, slot, enabled, between=lambda: None):
        blk = jnp.minimum(blk, pl.num_programs(0) - 1)
        e = be_ref[blk]
        q = bq_ref[blk]
        lo_row = q * ROW_BLOCK
        hi_row = lo_row + ROW_BLOCK
        base = e * n_tiles

        def segment(j, j_limit, src_hbm, ok):
            jc = jnp.minimum(j, j_limit - 1)
            o = off_ref[base + jc]
            inside = ok & (j < j_limit) & (o < hi_row)
            lo = jnp.maximum(o, lo_row)
            ln = jnp.minimum(o + n_ref[base + jc], hi_row) - lo

            @pl.when(inside & (ln > 0))
            def _():
                pltpu.make_async_copy(src_hbm.at[pl.ds(src_ref[base + jc] + (lo - o), ln)],
                                      xbuf.at[slot, pl.ds(lo - lo_row, ln)], sems.at[slot]).start()
            return inside

        j0 = jnp.where(q == 0, 0, ptr[0])

        def cond(j):
            return (enabled & (j < n_prompt_tiles)
                    & (off_ref[base + jnp.minimum(j, n_prompt_tiles - 1)] < hi_row))

        def body(j):
            segment(j, n_prompt_tiles, xsp_hbm, True)
            return j + 1

        jtail = lax.while_loop(cond, body, j0 + INLINE_SEGMENTS)
        result = between()
        count = jnp.int32(0)
        for k in range(INLINE_SEGMENTS):
            count = count + segment(j0 + k, n_prompt_tiles, xsp_hbm, enabled).astype(I32)
        jend = jnp.where(count == INLINE_SEGMENTS, jtail, j0 + count)
        ptr[0] = jnp.where(enabled, jnp.maximum(jend - 1, 0), ptr[0])
        for j in range(n_prompt_tiles, n_tiles):
            segment(j, n_tiles, xss_hbm, enabled)
        return result

    @pl.when(b == 0)
    def _():
        xbuf[...] = jnp.zeros(xbuf.shape, U32)
        for ahead in range(LOOKAHEAD):
            issue_rows(ahead, ahead, ahead < nused)
        for c in weight_copies(be_ref[0], 0):
            c.start()
        wcur[0] = 1

    active = b < nused

    @pl.when(active)
    def _():
        slot = b % ROW_SLOTS
        e = be_ref[b]
        rows = jnp.minimum(tot_ref[e] - bq_ref[b] * ROW_BLOCK, ROW_BLOCK)
        pltpu.make_async_copy(xsp_hbm.at[pl.ds(0, rows)], xbuf.at[slot, pl.ds(0, rows)],
                              sems.at[slot]).wait()

        @pl.when(bq_ref[b] == 0)
        def _():
            ws = 1 - wcur[0]
            wcur[0] = ws
            for c in weight_copies(e, ws):
                c.wait()
            nxt = b + (tot_ref[e] + ROW_BLOCK - 1) // ROW_BLOCK

            @pl.when(nxt < nused)
            def _():
                for c in weight_copies(be_ref[jnp.minimum(nxt, pl.num_programs(0) - 1)], 1 - ws):
                    c.start()

        ws = wcur[0]

        def compute(nrows):
            x = issue_rows(b + LOOKAHEAD, (b + LOOKAHEAD) % ROW_SLOTS, b + LOOKAHEAD < nused,
                           between=lambda: _unpack_rows(xbuf.at[slot, pl.ds(0, nrows)]))
            gu = _dot(x, wgu_f[ws]) + bgu_ref[pl.ds(e, 1), :]
            g = jnp.minimum(gu[:, :D_FF], SWIGLU_LIMIT)
            u = jnp.clip(gu[:, D_FF:], -SWIGLU_LIMIT, SWIGLU_LIMIT)
            act = (u + 1.0) * (g * jax.nn.sigmoid(SWIGLU_ALPHA * g))
            y = (_dot(act.astype(BF16), wd_f[ws])
                 + (bdn_ref[pl.ds(e, 1), :] + _anchor_zero_row(xbuf)))
            _pack_rows(y.astype(BF16).astype(F32), ys_ref.at[pl.ds(0, nrows)])
            if nrows < ROW_BLOCK:
                ys_ref[nrows:] = jnp.zeros((ROW_BLOCK - nrows, PACK_S, LANES), U32)

        for piece in range(ROW_BLOCK // TAIL_ROWS):
            top = (piece + 1) * TAIL_ROWS
            pl.when((rows > top - TAIL_ROWS) & (rows <= top))(functools.partial(compute, top))

    @pl.when(jnp.logical_not(active))
    def _():
        ys_ref[...] = jnp.zeros(ys_ref.shape, U32)


def _experts(plan, xsp, xss, w_gu, b_gu, w_down, b_down, *, n_tiles, n_prompt_tiles, rows):
    nblk = rows // ROW_BLOCK

    grid_spec = pltpu.PrefetchScalarGridSpec(
        num_scalar_prefetch=7,
        grid=(nblk,),
        in_specs=[pl.BlockSpec(memory_space=pl.ANY),
                  pl.BlockSpec(memory_space=pl.ANY),
                  pl.BlockSpec(memory_space=pl.ANY),
                  pl.BlockSpec((N_EXPERTS, 2 * D_FF), lambda b, *_: (0, 0)),
                  pl.BlockSpec(memory_space=pl.ANY),
                  pl.BlockSpec((N_EXPERTS, D_MODEL), lambda b, *_: (0, 0))],
        out_specs=pl.BlockSpec((ROW_BLOCK, PACK_S, LANES), lambda b, *_: (b, 0, 0)),
        scratch_shapes=[pltpu.VMEM((ROW_SLOTS + 1, ROW_BLOCK, PACK_S, LANES), U32),
                        pltpu.SemaphoreType.DMA((ROW_SLOTS,)),
                        pltpu.VMEM((2, D_MODEL, 2 * D_FF), F32), pltpu.VMEM((2, D_FF, D_MODEL), F32),
                        pltpu.SemaphoreType.DMA((2, 2)),
                        pltpu.SMEM((1,), I32), pltpu.SMEM((1,), I32)],
    )
    return pl.pallas_call(
        functools.partial(_expert_kernel, n_tiles=n_tiles, n_prompt_tiles=n_prompt_tiles),
        grid_spec=grid_spec,
        out_shape=jax.ShapeDtypeStruct((rows, PACK_S, LANES), U32),
        compiler_params=pltpu.CompilerParams(
            dimension_semantics=("arbitrary",), vmem_limit_bytes=VMEM_LIMIT),
        name="experts",
    )(plan['block_e'], plan['block_q'], plan['n_used'], plan['src_t'], plan['off_t'], plan['n_t'],
      plan['tot'], xsp, xss, w_gu, b_gu, w_down, b_down)


def _combine_kernel(ysrc_ref, sloc_ref, n_ref, col_ref, h_ref, ys_hbm, o_ref, buf, sems,
                    *, tm, tile0):
    j = pl.program_id(0)
    nt = pl.num_programs(0)
    rt = TOP_K * tm

    def issue(t, slot, enabled):
        base = (tile0 + jnp.minimum(t, nt - 1)) * N_EXPERTS
        for e in range(N_EXPERTS):
            n = n_ref[base + e]

            @pl.when(enabled & (n > 0))
            def _():
                pltpu.make_async_copy(ys_hbm.at[pl.ds(ysrc_ref[base + e], n)],
                                      buf.at[slot, pl.ds(sloc_ref[base + e], n)],
                                      sems.at[slot]).start()

    @pl.when(j == 0)
    def _():
        buf[ROW_SLOTS, 0:SUBLANES // PACK_S] = jnp.zeros((SUBLANES // PACK_S, PACK_S, LANES), U32)
        for ahead in range(LOOKAHEAD):
            issue(ahead, ahead, ahead < nt)

    slot = j % ROW_SLOTS
    pltpu.make_async_copy(ys_hbm.at[pl.ds(0, rt)], buf.at[slot], sems.at[slot]).wait()
    y = _unpack_rows(buf.at[slot])
    issue(j + LOOKAHEAD, (j + LOOKAHEAD) % ROW_SLOTS, j + LOOKAHEAD < nt)
    col = col_ref[...]
    rio = lax.broadcasted_iota(I32, (tm, rt), 1)
    gmat = jnp.zeros((tm, rt), F32)
    for k in range(TOP_K):
        gmat = gmat + jnp.where(rio == col[:, k:k + 1].astype(I32),
                                col[:, TOP_K + k:TOP_K + k + 1], 0.0)
    o_ref[...] = (h_ref[...] + _anchor_zero_row(buf)) + _dot(gmat.astype(BF16), y)


def _combine(plan, col, h2d, ys, *, tm, tile0):
    t = h2d.shape[0]
    rt = TOP_K * tm
    grid_spec = pltpu.PrefetchScalarGridSpec(
        num_scalar_prefetch=3,
        grid=(t // tm,),
        in_specs=[pl.BlockSpec((tm, LANES), lambda i, *_: (i, 0)),
                  pl.BlockSpec((tm, D_MODEL), lambda i, *_: (i, 0)),
                  pl.BlockSpec(memory_space=pl.ANY)],
        out_specs=pl.BlockSpec((tm, D_MODEL), lambda i, *_: (i, 0)),
        scratch_shapes=[pltpu.VMEM((ROW_SLOTS + 1, rt, PACK_S, LANES), U32),
                        pltpu.SemaphoreType.DMA((ROW_SLOTS,))],
    )
    return pl.pallas_call(
        functools.partial(_combine_kernel, tm=tm, tile0=tile0),
        grid_spec=grid_spec,
        out_shape=jax.ShapeDtypeStruct((t, D_MODEL), F32),
        compiler_params=pltpu.CompilerParams(
            dimension_semantics=("arbitrary",), vmem_limit_bytes=VMEM_LIMIT),
        name="combine",
    )(plan['ysrc'], plan['sloc'], plan['n'], col, h2d, ys)


def kernel(x_prompt, x_sample, cache_k, cache_v, state_conv, norm1_g, w_in, q_norm_g, k_norm_g,
           sinks, conv_w, w_out, norm2_g, w_router, b_router, w_gu, b_gu, w_down, b_down):
    l = 0
    nbp, sp, _ = x_prompt.shape
    nbs, ss, _ = x_sample.shape
    tp, ts = nbp * sp, nbs * ss
    tmp, tms = TOKEN_TILE, ts
    ntp, nts = tp // tmp, ts // tms
    n_tiles = ntp + nts
    n_rows = (tp + ts) * TOP_K
    grp = jnp.arange(ATT_DIM) // HEAD_DIM
    w = {
        'g1': norm1_g[l][None, :],
        'w_in': w_in[l].astype(BF16),
        'qg': jnp.tile(q_norm_g[l], N_HEADS)[None, :],
        'kg': jnp.tile(k_norm_g[l], N_KV_HEADS)[None, :],
        'bd': (grp[:, None] == grp[None, :]).astype(BF16),
        'sinks': sinks[l][None, :],
        'conv_w': conv_w[l],
        'w_out': w_out[l].astype(BF16),
        'g2': norm2_g[l][None, :],
        'w_router_t': w_router[l].T,
        'b_router_t': b_router[l][:, None],
    }
    hp, xsp, colp, tcp, knp, vnp, unp = _mixer(x_prompt, None, w, tm=tmp, nsub=STREAMS_PER_STEP,
                                               st=tmp)
    cache = (cache_k[l].reshape(nbs, WINDOW, KV_DIM), cache_v[l].reshape(nbs, WINDOW, KV_DIM),
             state_conv[l])
    hs, xss, cols, tcs, kns, vns, uns = _mixer(x_sample, cache, w, tm=ss, nsub=nbs, st=tms)

    n = jnp.concatenate([tcp.reshape(ntp, N_EXPERTS), tcs.reshape(nts, N_EXPERTS)], axis=0)
    tile_base = jnp.concatenate([jnp.arange(ntp, dtype=I32) * (TOP_K * tmp),
                                 jnp.arange(nts, dtype=I32) * (TOP_K * tms)])
    sloc = jnp.cumsum(n, axis=1) - n
    off = jnp.cumsum(n, axis=0) - n
    tot = jnp.sum(n, axis=0)
    padded = (tot + ROW_BLOCK - 1) // ROW_BLOCK * ROW_BLOCK
    ends = jnp.cumsum(padded)
    base = ends - padded
    rows = (-(-n_rows // ROW_BLOCK) + N_EXPERTS) * ROW_BLOCK
    nblk = rows // ROW_BLOCK
    blk0 = jnp.arange(nblk, dtype=I32) * ROW_BLOCK
    block_e = jnp.minimum(jnp.sum((ends[None, :] <= blk0[:, None]).astype(I32), axis=1),
                          N_EXPERTS - 1)
    eq = (block_e[:, None] == jnp.arange(N_EXPERTS, dtype=I32)[None, :]).astype(I32)
    block_q = (blk0 - jnp.sum(eq * base[None, :], axis=1)) // ROW_BLOCK
    plan = {
        'block_e': block_e.astype(I32),
        'block_q': block_q.astype(I32),
        'n_used': (ends[-1:] // ROW_BLOCK).astype(I32),
        'src_t': (tile_base[:, None] + sloc).T.reshape(-1).astype(I32),
        'off_t': off.T.reshape(-1).astype(I32),
        'n_t': n.T.reshape(-1).astype(I32),
        'tot': tot.astype(I32),
        'ysrc': (base[None, :] + off).reshape(-1).astype(I32),
        'sloc': sloc.reshape(-1).astype(I32),
        'n': n.reshape(-1).astype(I32),
    }

    ys = _experts(plan, xsp.reshape(-1, PACK_S, LANES), xss.reshape(-1, PACK_S, LANES),
                  w_gu[l], b_gu[l], w_down[l], b_down[l],
                  n_tiles=n_tiles, n_prompt_tiles=ntp, rows=rows)
    yp = _combine(plan, colp.reshape(tp, LANES), hp.reshape(tp, D_MODEL), ys, tm=tmp, tile0=0)
    ysm = _combine(plan, cols.reshape(ts, LANES), hs.reshape(ts, D_MODEL), ys, tm=tms, tile0=ntp)

    kv5 = lambda a: a.reshape(1, a.shape[0], WINDOW, N_KV_HEADS, HEAD_DIM)
    return (yp.reshape(nbp, sp, D_MODEL), ysm.reshape(nbs, ss, D_MODEL),
            kv5(knp), kv5(vnp), unp[None], kv5(kns), kv5(vns), uns[None])
```

```python
import functools

import jax
import jax.numpy as jnp
from jax import lax
from jax.experimental import pallas as pl
from jax.experimental.pallas import tpu as pltpu

D_MODEL = 1024
CHUNK = 64
HEAD_DIM = 64
N_HEADS = 8
N_KV_HEADS = 2
Q_PER_KV = N_HEADS // N_KV_HEADS
ATT_DIM = N_HEADS * HEAD_DIM
KV_DIM = N_KV_HEADS * HEAD_DIM
GROUP_W = Q_PER_KV * HEAD_DIM
CONV_DIM = D_MODEL - ATT_DIM
CONV_W = 3
WINDOW = 128
IN_DIM = ATT_DIM + 2 * KV_DIM + 3 * CONV_DIM
N_EXPERTS = 32
TOP_K = 4
D_FF = D_MODEL
SWIGLU_LIMIT = 7.0
SWIGLU_ALPHA = 1.702
EPS = 1e-6
NEG = -1e30

LANES = 128
SUBLANES = 8
HALF = D_MODEL // 2
PACK_S = HALF // LANES
TOKEN_TILE = 256
STREAMS_PER_STEP = 4
ROW_BLOCK = 512
TAIL_ROWS = 128
INLINE_SEGMENTS = 20
LOOKAHEAD = 2
ROW_SLOTS = LOOKAHEAD + 1
VMEM_LIMIT = 56 * 1024 * 1024
MIXER_VMEM_LIMIT = 60 * 1024 * 1024

F32 = jnp.float32
BF16 = jnp.bfloat16
U32 = jnp.uint32
I32 = jnp.int32


def _rms_rows(x, g):
    return x * lax.rsqrt(jnp.mean(x * x, axis=-1, keepdims=True) + EPS) * g


def _split_bf16(x):
    hi = x.astype(BF16)
    lo = (x - hi.astype(F32)).astype(BF16)
    return hi, lo


def _dot(a, b):
    return jnp.dot(a, b, preferred_element_type=F32)


def _dot_nt(a, b):
    return lax.dot_general(a, b, (((1,), (1,)), ((), ())), preferred_element_type=F32)


def _head_rms(t, bd, g):
    ssq = _dot((t * t).astype(BF16), bd)
    return t * lax.rsqrt(ssq * (1.0 / HEAD_DIM) + EPS) * g


def _pack_rows(vals, out_ref):
    r = vals.shape[0]
    bits = pltpu.bitcast(vals, U32)
    word = (bits[:, :HALF] >> 16) | (bits[:, HALF:] & jnp.uint32(0xFFFF0000))
    flat = out_ref.reshape(r * PACK_S, LANES)
    for s in range(PACK_S):
        flat[pl.ds(s, r, stride=PACK_S), :] = word[:, s * LANES:(s + 1) * LANES]


def _unpack_rows(packed_ref):
    r = packed_ref.shape[0]
    flat = packed_ref.reshape(r * PACK_S, LANES)
    lo, hi = [], []
    for s in range(PACK_S):
        w = flat[pl.ds(s, r, stride=PACK_S), :]
        lo.append(pltpu.bitcast(w << 16, F32))
        hi.append(pltpu.bitcast(w & jnp.uint32(0xFFFF0000), F32))
    return jnp.concatenate(lo + hi, axis=1).astype(BF16)


def _anchor_zero_row(staging_ref):
    z = staging_ref.at[ROW_SLOTS, pl.ds(0, SUBLANES // PACK_S)].reshape(SUBLANES, LANES)[...]
    return jnp.concatenate([pltpu.bitcast(z, F32)[0:1, :]] * (D_MODEL // LANES), axis=1)


def _spread_kv(t):
    assert N_KV_HEADS == 2 and KV_DIM == LANES
    swapped = pltpu.roll(t, HEAD_DIM, axis=1)
    low = lax.broadcasted_iota(I32, t.shape, 1) < HEAD_DIM
    pair0 = jnp.where(low, t, swapped)
    pair1 = jnp.where(low, swapped, t)
    reps = GROUP_W // LANES
    return jnp.concatenate([pair0] * reps + [pair1] * reps, axis=1).astype(BF16)


def _attention(i, qn, kall, vall, sink_ref, *, mask_history):
    tm = qn.shape[0]
    k4 = _spread_kv(kall)
    v4 = _spread_kv(vall)
    lane_grp = lax.broadcasted_iota(I32, (CHUNK, GROUP_W), 1) // HEAD_DIM
    row_grp = lax.broadcasted_iota(I32, (Q_PER_KV * CHUNK, 1), 0) // CHUNK
    span = WINDOW + CHUNK
    att_rows = []
    for c in range(tm // CHUNK):
        per_kv = []
        for hk in range(N_KV_HEADS):
            qc = qn[c * CHUNK:(c + 1) * CHUNK, hk * GROUP_W:(hk + 1) * GROUP_W]
            qs = jnp.concatenate([jnp.where(lane_grp == g, qc, 0.0) for g in range(Q_PER_KV)],
                                 axis=0).astype(BF16)
            kw = k4[c * CHUNK:c * CHUNK + span, hk * GROUP_W:(hk + 1) * GROUP_W]
            vw = v4[c * CHUNK:c * CHUNK + span, hk * GROUP_W:(hk + 1) * GROUP_W]
            s = _dot_nt(qs, kw) * (HEAD_DIM ** -0.5)
            if mask_history and c * CHUNK < WINDOW:
                kcol = lax.broadcasted_iota(I32, (1, span), 1)
                s = jnp.where(kcol >= jnp.where(i == 0, WINDOW - c * CHUNK, 0), s, NEG)
            sink = jnp.zeros((Q_PER_KV * CHUNK, 1), F32)
            for g in range(Q_PER_KV):
                sink = jnp.where(row_grp == g, sink_ref[0, hk * Q_PER_KV + g], sink)
            m = jnp.maximum(jnp.max(s, axis=-1, keepdims=True), sink)
            p = jnp.exp(s - m)
            den = jnp.sum(p, axis=-1, keepdims=True) + jnp.exp(sink - m)
            o = _dot(p.astype(BF16), vw) / den
            oc = jnp.zeros((CHUNK, GROUP_W), F32)
            for g in range(Q_PER_KV):
                oc = oc + jnp.where(lane_grp == g, o[g * CHUNK:(g + 1) * CHUNK, :], 0.0)
            per_kv.append(oc)
        att_rows.append(jnp.concatenate(per_kv, axis=1))
    return jnp.concatenate(att_rows, axis=0)


def _mixer_kernel(*refs, tm, nsub, st, has_cache):
    refs = list(refs)
    x_ref = refs.pop(0)
    if has_cache:
        ck_ref, cv_ref, st_ref = refs.pop(0), refs.pop(0), refs.pop(0)
    (g1_ref, win_ref, qg_ref, kg_ref, bd_ref, sink_ref, cw_ref, wout_ref, g2_ref,
     wrt_ref, brt_ref, upper_ref, lower_ref,
     h_ref, xs_ref, col_ref, tcnt_ref, knew_ref, vnew_ref, unew_ref, kprev, vprev, ubuf) = refs
    i = pl.program_id(1)
    ntok = nsub * tm

    @pl.when(i == 0)
    def _():
        ubuf[:, 0:SUBLANES, :] = jnp.zeros((nsub, SUBLANES, CONV_DIM), F32)
        if has_cache:
            kprev[...] = ck_ref[...]
            vprev[...] = cv_ref[...]
            ubuf[:, SUBLANES - (CONV_W - 1):SUBLANES, :] = st_ref[...]
        else:
            kprev[...] = jnp.zeros(kprev.shape, F32)
            vprev[...] = jnp.zeros(vprev.shape, F32)

    x = jnp.concatenate([x_ref[sub] for sub in range(nsub)], axis=0) if nsub > 1 else x_ref[0]
    xn = _rms_rows(x, g1_ref[...])
    proj = _dot(xn.astype(BF16), win_ref[...])
    o0 = ATT_DIM
    o1 = o0 + KV_DIM
    o2 = o1 + KV_DIM
    o3 = o2 + CONV_DIM
    o4 = o3 + CONV_DIM
    bd = bd_ref[...]
    qn = _head_rms(proj[:, :o0], bd, qg_ref[...])
    kn = _head_rms(proj[:, o0:o1], bd[:KV_DIM, :KV_DIM], kg_ref[...])
    v = proj[:, o1:o2]
    bg = proj[:, o2:o3]
    u = proj[:, o3:o4] * proj[:, o4:]

    cw = cw_ref[...]
    att_parts, cv_parts = [], []
    for sub in range(nsub):
        r0, r1 = sub * tm, (sub + 1) * tm
        kall = jnp.concatenate([kprev[sub], kn[r0:r1]], axis=0)
        vall = jnp.concatenate([vprev[sub], v[r0:r1]], axis=0)
        att_parts.append(_attention(i, qn[r0:r1], kall, vall, sink_ref,
                                    mask_history=not has_cache))
        kprev[sub] = kall[tm:, :]
        vprev[sub] = vall[tm:, :]
        knew_ref[sub] = kall[tm:, :]
        vnew_ref[sub] = vall[tm:, :]

        us = u[r0:r1]
        ubuf[sub, SUBLANES:SUBLANES + tm, :] = us
        first = SUBLANES - (CONV_W - 1)
        cvs = ubuf[sub, first:first + tm, :] * cw[0:1, :]
        for tap in range(1, CONV_W - 1):
            cvs = cvs + ubuf[sub, first + tap:first + tap + tm, :] * cw[tap:tap + 1, :]
        cv_parts.append(cvs + us * cw[CONV_W - 1:CONV_W, :])
        ubuf[sub, 0:SUBLANES, :] = ubuf[sub, tm:tm + SUBLANES, :]
        unew_ref[sub] = us[tm - (CONV_W - 1):, :]
    att = jnp.concatenate(att_parts, axis=0) if nsub > 1 else att_parts[0]
    cv = jnp.concatenate(cv_parts, axis=0) if nsub > 1 else cv_parts[0]

    mix = jnp.concatenate([att, bg * cv], axis=1).astype(BF16)
    h = x + _dot(mix, wout_ref[...])
    for sub in range(nsub):
        h_ref[sub] = h[sub * tm:(sub + 1) * tm]
    hn = _rms_rows(h, g2_ref[...])

    hn_hi, hn_lo = _split_bf16(hn)
    wr_hi, wr_lo = _split_bf16(wrt_ref[...])
    logits = (_dot_nt(wr_hi, hn_hi) + _dot_nt(wr_hi, hn_lo) + _dot_nt(wr_lo, hn_hi)) + brt_ref[...]
    eio = lax.broadcasted_iota(I32, (N_EXPERTS, ntok), 0)
    work = logits
    tops, sels = [], []
    for _ in range(TOP_K):
        mk = jnp.max(work, axis=0, keepdims=True)
        ik = jnp.min(jnp.where(work == mk, eio, N_EXPERTS), axis=0, keepdims=True)
        sel = eio == ik
        work = jnp.where(sel, -jnp.inf, work)
        tops.append(mk)
        sels.append(sel)
    es = [jnp.exp(t - tops[0]) for t in tops]
    esum = es[0] + es[1] + es[2] + es[3]

    onehot = jnp.zeros((N_EXPERTS, ntok), F32)
    for sel in sels:
        onehot = onehot + jnp.where(sel, 1.0, 0.0)
    onehot_b = onehot.astype(BF16)
    smaller = _dot(lower_ref[...], onehot_b)
    nst = ntok // st
    where_parts = []
    for k in range(nst):
        c0, c1 = k * st, (k + 1) * st
        before = _dot(onehot_b[:, c0:c1], upper_ref[...])
        start = jnp.sum(smaller[:, c0:c1], axis=1, keepdims=True)
        where_parts.append(before + start)
    where_to = jnp.concatenate(where_parts, axis=1) if nst > 1 else where_parts[0]
    pos = [jnp.sum(jnp.where(sel, where_to, 0.0), axis=0, keepdims=True) for sel in sels]
    rio = lax.broadcasted_iota(I32, (TOP_K * st, st), 0)
    for k in range(nst):
        c0, c1 = k * st, (k + 1) * st
        hit = rio == pos[0][:, c0:c1].astype(I32)
        for slot in range(1, TOP_K):
            hit = hit | (rio == pos[slot][:, c0:c1].astype(I32))
        perm = jnp.where(hit, 1.0, 0.0).astype(BF16)
        _pack_rows(_dot(perm, hn_hi[c0:c1]), xs_ref.at[k, 0])
        tcnt_ref[k, 0] = _dot_nt(jnp.ones((1, st), BF16), onehot_b[:, c0:c1]).astype(I32)

    rows8 = jnp.concatenate(pos + [e / esum for e in es], axis=0)
    sq = jnp.concatenate([rows8, jnp.zeros((LANES - 2 * TOP_K, ntok), F32)], axis=0)
    cols = jnp.transpose(sq)
    for sub in range(nsub):
        col_ref[sub] = cols[sub * tm:(sub + 1) * tm]


def _mixer(x, cache, w, *, tm, nsub, st):
    nb, s, _ = x.shape
    nt = s // tm
    nst = nsub * tm // st
    has_cache = cache is not None
    full = lambda shape: pl.BlockSpec(shape, lambda b, i: (0,) * len(shape))
    per_b = lambda shape: pl.BlockSpec((nsub,) + shape, lambda b, i: (b, 0, 0))
    tile = lambda last: pl.BlockSpec((nsub, tm, last), lambda b, i: (b, i, 0))
    kv_shape = (WINDOW, KV_DIM)
    kv_spec = per_b(kv_shape)

    in_specs = [tile(D_MODEL)]
    args = [x]
    if has_cache:
        in_specs += [kv_spec, kv_spec, per_b((CONV_W - 1, CONV_DIM))]
        args += list(cache)
    tio = jnp.arange(st)
    eio = jnp.arange(N_EXPERTS)
    upper = (tio[:, None] < tio[None, :]).astype(BF16)
    lower = (eio[None, :] < eio[:, None]).astype(BF16)
    in_specs += [
        full((1, D_MODEL)), full((D_MODEL, IN_DIM)), full((1, ATT_DIM)),
        full((1, KV_DIM)), full((ATT_DIM, ATT_DIM)),
        pl.BlockSpec(memory_space=pltpu.SMEM),
        full((CONV_W, CONV_DIM)), full((D_MODEL, D_MODEL)), full((1, D_MODEL)),
        full((N_EXPERTS, D_MODEL)), full((N_EXPERTS, 1)), full((st, st)),
        full((N_EXPERTS, N_EXPERTS)),
    ]
    args += [w['g1'], w['w_in'], w['qg'], w['kg'], w['bd'], w['sinks'], w['conv_w'],
             w['w_out'], w['g2'], w['w_router_t'], w['b_router_t'], upper, lower]
    rt = TOP_K * st
    nsort = nb // nsub * nst
    out_shape = [
        jax.ShapeDtypeStruct((nb, s, D_MODEL), F32),
        jax.ShapeDtypeStruct((nsort, nt, rt, PACK_S, LANES), U32),
        jax.ShapeDtypeStruct((nb, s, LANES), F32),
        jax.ShapeDtypeStruct((nsort, nt, 1, N_EXPERTS), I32),
        jax.ShapeDtypeStruct((nb,) + kv_shape, F32),
        jax.ShapeDtypeStruct((nb,) + kv_shape, F32),
        jax.ShapeDtypeStruct((nb, CONV_W - 1, CONV_DIM), F32),
    ]
    out_specs = [
        tile(D_MODEL),
        pl.BlockSpec((nst, 1, rt, PACK_S, LANES), lambda b, i: (b, i, 0, 0, 0)),
        tile(LANES),
        pl.BlockSpec((nst, 1, 1, N_EXPERTS), lambda b, i: (b, i, 0, 0)),
        kv_spec, kv_spec, per_b((CONV_W - 1, CONV_DIM)),
    ]
    scratch = [pltpu.VMEM((nsub, WINDOW, KV_DIM), F32), pltpu.VMEM((nsub, WINDOW, KV_DIM), F32),
               pltpu.VMEM((nsub, tm + SUBLANES, CONV_DIM), F32)]
    return pl.pallas_call(
        functools.partial(_mixer_kernel, tm=tm, nsub=nsub, st=st, has_cache=has_cache),
        grid=(nb // nsub, nt),
        in_specs=in_specs,
        out_specs=out_specs,
        out_shape=out_shape,
        scratch_shapes=scratch,
        compiler_params=pltpu.CompilerParams(
            dimension_semantics=("arbitrary", "arbitrary"), vmem_limit_bytes=MIXER_VMEM_LIMIT),
        name="mixer_sample" if has_cache else "mixer_prompt",
    )(*args)


def _expert_kernel(be_ref, bq_ref, nused_ref, src_ref, off_ref, n_ref, tot_ref,
                   xsp_hbm, xss_hbm, wgu_hbm, bgu_ref, wd_hbm, bdn_ref, ys_ref,
                   xbuf, sems, wgu_f, wd_f, wsems, ptr, wcur, *, n_tiles, n_prompt_tiles):
    b = pl.program_id(0)
    nused = nused_ref[0]

    def weight_copies(e, ws):
        return (pltpu.make_async_copy(wgu_hbm.at[e], wgu_f.at[ws], wsems.at[0, ws]),
                pltpu.make_async_copy(wd_hbm.at[e], wd_f.at[ws], wsems.at[1, ws]))

    def issue_rows(blk, slot, enabled, between=lambda: None):
        blk = jnp.minimum(blk, pl.num_programs(0) - 1)
        e = be_ref[blk]
        q = bq_ref[blk]
        lo_row = q * ROW_BLOCK
        hi_row = lo_row + ROW_BLOCK
        base = e * n_tiles

        def segment(j, j_limit, src_hbm, ok):
            jc = jnp.minimum(j, j_limit - 1)
            o = off_ref[base + jc]
            inside = ok & (j < j_limit) & (o < hi_row)
            lo = jnp.maximum(o, lo_row)
            ln = jnp.minimum(o + n_ref[base + jc], hi_row) - lo

            @pl.when(inside & (ln > 0))
            def _():
                pltpu.make_async_copy(src_hbm.at[pl.ds(src_ref[base + jc] + (lo - o), ln)],
                                      xbuf.at[slot, pl.ds(lo - lo_row, ln)], sems.at[slot]).start()
            return inside

        j0 = jnp.where(q == 0, 0, ptr[0])

        def cond(j):
            return (enabled & (j < n_prompt_tiles)
                    & (off_ref[base + jnp.minimum(j, n_prompt_tiles - 1)] < hi_row))

        def body(j):
            segment(j, n_prompt_tiles, xsp_hbm, True)
            return j + 1

        jtail = lax.while_loop(cond, body, j0 + INLINE_SEGMENTS)
        result = between()
        count = jnp.int32(0)
        for k in range(INLINE_SEGMENTS):
            count = count + segment(j0 + k, n_prompt_tiles, xsp_hbm, enabled).astype(I32)
        jend = jnp.where(count == INLINE_SEGMENTS, jtail, j0 + count)
        ptr[0] = jnp.where(enabled, jnp.maximum(jend - 1, 0), ptr[0])
        for j in range(n_prompt_tiles, n_tiles):
            segment(j, n_tiles, xss_hbm, enabled)
        return result

    @pl.when(b == 0)
    def _():
        xbuf[...] = jnp.zeros(xbuf.shape, U32)
        for ahead in range(LOOKAHEAD):
            issue_rows(ahead, ahead, ahead < nused)
        for c in weight_copies(be_ref[0], 0):
            c.start()
        wcur[0] = 1

    active = b < nused

    @pl.when(active)
    def _():
        slot = b % ROW_SLOTS
        e = be_ref[b]
        rows = jnp.minimum(tot_ref[e] - bq_ref[b] * ROW_BLOCK, ROW_BLOCK)
        pltpu.make_async_copy(xsp_hbm.at[pl.ds(0, rows)], xbuf.at[slot, pl.ds(0, rows)],
                              sems.at[slot]).wait()

        @pl.when(bq_ref[b] == 0)
        def _():
            ws = 1 - wcur[0]
            wcur[0] = ws
            for c in weight_copies(e, ws):
                c.wait()
            nxt = b + (tot_ref[e] + ROW_BLOCK - 1) // ROW_BLOCK

            @pl.when(nxt < nused)
            def _():
                for c in weight_copies(be_ref[jnp.minimum(nxt, pl.num_programs(0) - 1)], 1 - ws):
                    c.start()

        ws = wcur[0]

        def compute(nrows):
            x = issue_rows(b + LOOKAHEAD, (b + LOOKAHEAD) % ROW_SLOTS, b + LOOKAHEAD < nused,
                           between=lambda: _unpack_rows(xbuf.at[slot, pl.ds(0, nrows)]))
            gu = _dot(x, wgu_f[ws]) + bgu_ref[pl.ds(e, 1), :]
            g = jnp.minimum(gu[:, :D_FF], SWIGLU_LIMIT)
            u = jnp.clip(gu[:, D_FF:], -SWIGLU_LIMIT, SWIGLU_LIMIT)
            act = (u + 1.0) * (g * jax.nn.sigmoid(SWIGLU_ALPHA * g))
            y = (_dot(act.astype(BF16), wd_f[ws])
                 + (bdn_ref[pl.ds(e, 1), :] + _anchor_zero_row(xbuf)))
            _pack_rows(y.astype(BF16).astype(F32), ys_ref.at[pl.ds(0, nrows)])
            if nrows < ROW_BLOCK:
                ys_ref[nrows:] = jnp.zeros((ROW_BLOCK - nrows, PACK_S, LANES), U32)

        for piece in range(ROW_BLOCK // TAIL_ROWS):
            top = (piece + 1) * TAIL_ROWS
            pl.when((rows > top - TAIL_ROWS) & (rows <= top))(functools.partial(compute, top))

    @pl.when(jnp.logical_not(active))
    def _():
        ys_ref[...] = jnp.zeros(ys_ref.shape, U32)


def _experts(plan, xsp, xss, w_gu, b_gu, w_down, b_down, *, n_tiles, n_prompt_tiles, rows):
    nblk = rows // ROW_BLOCK

    grid_spec = pltpu.PrefetchScalarGridSpec(
        num_scalar_prefetch=7,
        grid=(nblk,),
        in_specs=[pl.BlockSpec(memory_space=pl.ANY),
                  pl.BlockSpec(memory_space=pl.ANY),
                  pl.BlockSpec(memory_space=pl.ANY),
                  pl.BlockSpec((N_EXPERTS, 2 * D_FF), lambda b, *_: (0, 0)),
                  pl.BlockSpec(memory_space=pl.ANY),
                  pl.BlockSpec((N_EXPERTS, D_MODEL), lambda b, *_: (0, 0))],
        out_specs=pl.BlockSpec((ROW_BLOCK, PACK_S, LANES), lambda b, *_: (b, 0, 0)),
        scratch_shapes=[pltpu.VMEM((ROW_SLOTS + 1, ROW_BLOCK, PACK_S, LANES), U32),
                        pltpu.SemaphoreType.DMA((ROW_SLOTS,)),
                        pltpu.VMEM((2, D_MODEL, 2 * D_FF), F32), pltpu.VMEM((2, D_FF, D_MODEL), F32),
                        pltpu.SemaphoreType.DMA((2, 2)),
                        pltpu.SMEM((1,), I32), pltpu.SMEM((1,), I32)],
    )
    return pl.pallas_call(
        functools.partial(_expert_kernel, n_tiles=n_tiles, n_prompt_tiles=n_prompt_tiles),
        grid_spec=grid_spec,
        out_shape=jax.ShapeDtypeStruct((rows, PACK_S, LANES), U32),
        compiler_params=pltpu.CompilerParams(
            dimension_semantics=("arbitrary",), vmem_limit_bytes=VMEM_LIMIT),
        name="experts",
    )(plan['block_e'], plan['block_q'], plan['n_used'], plan['src_t'], plan['off_t'], plan['n_t'],
      plan['tot'], xsp, xss, w_gu, b_gu, w_down, b_down)


def _combine_kernel(ysrc_ref, sloc_ref, n_ref, col_ref, h_ref, ys_hbm, o_ref, buf, sems,
                    *, tm, tile0):
    j = pl.program_id(0)
    nt = pl.num_programs(0)
    rt = TOP_K * tm

    def issue(t, slot, enabled):
        base = (tile0 + jnp.minimum(t, nt - 1)) * N_EXPERTS
        for e in range(N_EXPERTS):
            n = n_ref[base + e]

            @pl.when(enabled & (n > 0))
            def _():
                pltpu.make_async_copy(ys_hbm.at[pl.ds(ysrc_ref[base + e], n)],
                                      buf.at[slot, pl.ds(sloc_ref[base + e], n)],
                                      sems.at[slot]).start()

    @pl.when(j == 0)
    def _():
        buf[ROW_SLOTS, 0:SUBLANES // PACK_S] = jnp.zeros((SUBLANES // PACK_S, PACK_S, LANES), U32)
        for ahead in range(LOOKAHEAD):
            issue(ahead, ahead, ahead < nt)

    slot = j % ROW_SLOTS
    pltpu.make_async_copy(ys_hbm.at[pl.ds(0, rt)], buf.at[slot], sems.at[slot]).wait()
    y = _unpack_rows(buf.at[slot])
    issue(j + LOOKAHEAD, (j + LOOKAHEAD) % ROW_SLOTS, j + LOOKAHEAD < nt)
    col = col_ref[...]
    rio = lax.broadcasted_iota(I32, (tm, rt), 1)
    gmat = jnp.zeros((tm, rt), F32)
    for k in range(TOP_K):
        gmat = gmat + jnp.where(rio == col[:, k:k + 1].astype(I32),
                                col[:, TOP_K + k:TOP_K + k + 1], 0.0)
    o_ref[...] = (h_ref[...] + _anchor_zero_row(buf)) + _dot(gmat.astype(BF16), y)


def _combine(plan, col, h2d, ys, *, tm, tile0):
    t = h2d.shape[0]
    rt = TOP_K * tm
    grid_spec = pltpu.PrefetchScalarGridSpec(
        num_scalar_prefetch=3,
        grid=(t // tm,),
        in_specs=[pl.BlockSpec((tm, LANES), lambda i, *_: (i, 0)),
                  pl.BlockSpec((tm, D_MODEL), lambda i, *_: (i, 0)),
                  pl.BlockSpec(memory_space=pl.ANY)],
        out_specs=pl.BlockSpec((tm, D_MODEL), lambda i, *_: (i, 0)),
        scratch_shapes=[pltpu.VMEM((ROW_SLOTS + 1, rt, PACK_S, LANES), U32),
                        pltpu.SemaphoreType.DMA((ROW_SLOTS,))],
    )
    return pl.pallas_call(
        functools.partial(_combine_kernel, tm=tm, tile0=tile0),
        grid_spec=grid_spec,
        out_shape=jax.ShapeDtypeStruct((t, D_MODEL), F32),
        compiler_params=pltpu.CompilerParams(
            dimension_semantics=("arbitrary",), vmem_limit_bytes=VMEM_LIMIT),
        name="combine",
    )(plan['ysrc'], plan['sloc'], plan['n'], col, h2d, ys)


def kernel(x_prompt, x_sample, cache_k, cache_v, state_conv, norm1_g, w_in, q_norm_g, k_norm_g,
           sinks, conv_w, w_out, norm2_g, w_router, b_router, w_gu, b_gu, w_down, b_down):
    l = 0
    nbp, sp, _ = x_prompt.shape
    nbs, ss, _ = x_sample.shape
    tp, ts = nbp * sp, nbs * ss
    tmp, tms = TOKEN_TILE, ts
    ntp, nts = tp // tmp, ts // tms
    n_tiles = ntp + nts
    n_rows = (tp + ts) * TOP_K
    grp = jnp.arange(ATT_DIM) // HEAD_DIM
    w = {
        'g1': norm1_g[l][None, :],
        'w_in': w_in[l].astype(BF16),
        'qg': jnp.tile(q_norm_g[l], N_HEADS)[None, :],
        'kg': jnp.tile(k_norm_g[l], N_KV_HEADS)[None, :],
        'bd': (grp[:, None] == grp[None, :]).astype(BF16),
        'sinks': sinks[l][None, :],
        'conv_w': conv_w[l],
        'w_out': w_out[l].astype(BF16),
        'g2': norm2_g[l][None, :],
        'w_router_t': w_router[l].T,
        'b_router_t': b_router[l][:, None],
    }
    hp, xsp, colp, tcp, knp, vnp, unp = _mixer(x_prompt, None, w, tm=tmp, nsub=STREAMS_PER_STEP,
                                               st=tmp)
    cache = (cache_k[l].reshape(nbs, WINDOW, KV_DIM), cache_v[l].reshape(nbs, WINDOW, KV_DIM),
             state_conv[l])
    hs, xss, cols, tcs, kns, vns, uns = _mixer(x_sample, cache, w, tm=ss, nsub=nbs, st=tms)

    n = jnp.concatenate([tcp.reshape(ntp, N_EXPERTS), tcs.reshape(nts, N_EXPERTS)], axis=0)
    tile_base = jnp.concatenate([jnp.arange(ntp, dtype=I32) * (TOP_K * tmp),
                                 jnp.arange(nts, dtype=I32) * (TOP_K * tms)])
    sloc = jnp.cumsum(n, axis=1) - n
    off = jnp.cumsum(n, axis=0) - n
    tot = jnp.sum(n, axis=0)
    padded = (tot + ROW_BLOCK - 1) // ROW_BLOCK * ROW_BLOCK
    ends = jnp.cumsum(padded)
    base = ends - padded
    rows = (-(-n_rows // ROW_BLOCK) + N_EXPERTS) * ROW_BLOCK
    nblk = rows // ROW_BLOCK
    blk0 = jnp.arange(nblk, dtype=I32) * ROW_BLOCK
    block_e = jnp.minimum(jnp.sum((ends[None, :] <= blk0[:, None]).astype(I32), axis=1),
                          N_EXPERTS - 1)
    eq = (block_e[:, None] == jnp.arange(N_EXPERTS, dtype=I32)[None, :]).astype(I32)
    block_q = (blk0 - jnp.sum(eq * base[None, :], axis=1)) // ROW_BLOCK
    plan = {
        'block_e': block_e.astype(I32),
        'block_q': block_q.astype(I32),
        'n_used': (ends[-1:] // ROW_BLOCK).astype(I32),
        'src_t': (tile_base[:, None] + sloc).T.reshape(-1).astype(I32),
        'off_t': off.T.reshape(-1).astype(I32),
        'n_t': n.T.reshape(-1).astype(I32),
        'tot': tot.astype(I32),
        'ysrc': (base[None, :] + off).reshape(-1).astype(I32),
        'sloc': sloc.reshape(-1).astype(I32),
        'n': n.reshape(-1).astype(I32),
    }

    ys = _experts(plan, xsp.reshape(-1, PACK_S, LANES), xss.reshape(-1, PACK_S, LANES),
                  w_gu[l], b_gu[l], w_down[l], b_down[l],
                  n_tiles=n_tiles, n_prompt_tiles=ntp, rows=rows)
    yp = _combine(plan, colp.reshape(tp, LANES), hp.reshape(tp, D_MODEL), ys, tm=tmp, tile0=0)
    ysm = _combine(plan, cols.reshape(ts, LANES), hs.reshape(ts, D_MODEL), ys, tm=tms, tile0=ntp)

    kv5 = lambda a: a.reshape(1, a.shape[0], WINDOW, N_KV_HEADS, HEAD_DIM)
    return (yp.reshape(nbp, sp, D_MODEL), ysm.reshape(nbs, ss, D_MODEL),
            kv5(knp), kv5(vnp), unp[None], kv5(kns), kv5(vns), uns[None])
```

```python
import functools

import jax
import jax.numpy as jnp
from jax import lax
from jax.experimental import pallas as pl
from jax.experimental.pallas import tpu as pltpu

D_MODEL = 1024
CHUNK = 64
HEAD_DIM = 64
N_HEADS = 8
N_KV_HEADS = 2
Q_PER_KV = N_HEADS // N_KV_HEADS
ATT_DIM = N_HEADS * HEAD_DIM
KV_DIM = N_KV_HEADS * HEAD_DIM
GROUP_W = Q_PER_KV * HEAD_DIM
CONV_DIM = D_MODEL - ATT_DIM
CONV_W = 3
WINDOW = 128
IN_DIM = ATT_DIM + 2 * KV_DIM + 3 * CONV_DIM
N_EXPERTS = 32
TOP_K = 4
D_FF = D_MODEL
SWIGLU_LIMIT = 7.0
SWIGLU_ALPHA = 1.702
EPS = 1e-6
NEG = -1e30

LANES = 128
SUBLANES = 8
HALF = D_MODEL // 2
PACK_S = HALF // LANES
TOKEN_TILE = 256
STREAMS_PER_STEP = 4
ROW_BLOCK = 512
TAIL_ROWS = 128
INLINE_SEGMENTS = 20
LOOKAHEAD = 2
ROW_SLOTS = LOOKAHEAD + 1
VMEM_LIMIT = 56 * 1024 * 1024
MIXER_VMEM_LIMIT = 60 * 1024 * 1024

F32 = jnp.float32
BF16 = jnp.bfloat16
U32 = jnp.uint32
I32 = jnp.int32


def _rms_rows(x, g):
    return x * lax.rsqrt(jnp.mean(x * x, axis=-1, keepdims=True) + EPS) * g


def _split_bf16(x):
    hi = x.astype(BF16)
    lo = (x - hi.astype(F32)).astype(BF16)
    return hi, lo


def _dot(a, b):
    return jnp.dot(a, b, preferred_element_type=F32)


def _dot_nt(a, b):
    return lax.dot_general(a, b, (((1,), (1,)), ((), ())), preferred_element_type=F32)


def _head_rms(t, bd, g):
    ssq = _dot((t * t).astype(BF16), bd)
    return t * lax.rsqrt(ssq * (1.0 / HEAD_DIM) + EPS) * g


def _pack_rows(vals, out_ref):
    r = vals.shape[0]
    bits = pltpu.bitcast(vals, U32)
    word = (bits[:, :HALF] >> 16) | (bits[:, HALF:] & jnp.uint32(0xFFFF0000))
    flat = out_ref.reshape(r * PACK_S, LANES)
    for s in range(PACK_S):
        flat[pl.ds(s, r, stride=PACK_S), :] = word[:, s * LANES:(s + 1) * LANES]


def _unpack_rows(packed_ref):
    r = packed_ref.shape[0]
    flat = packed_ref.reshape(r * PACK_S, LANES)
    lo, hi = [], []
    for s in range(PACK_S):
        w = flat[pl.ds(s, r, stride=PACK_S), :]
        lo.append(pltpu.bitcast(w << 16, F32))
        hi.append(pltpu.bitcast(w & jnp.uint32(0xFFFF0000), F32))
    return jnp.concatenate(lo + hi, axis=1).astype(BF16)


def _anchor_zero_row(staging_ref):
    z = staging_ref.at[ROW_SLOTS, pl.ds(0, SUBLANES // PACK_S)].reshape(SUBLANES, LANES)[...]
    return jnp.concatenate([pltpu.bitcast(z, F32)[0:1, :]] * (D_MODEL // LANES), axis=1)


def _spread_kv(t):
    assert N_KV_HEADS == 2 and KV_DIM == LANES
    swapped = pltpu.roll(t, HEAD_DIM, axis=1)
    low = lax.broadcasted_iota(I32, t.shape, 1) < HEAD_DIM
    pair0 = jnp.where(low, t, swapped)
    pair1 = jnp.where(low, swapped, t)
    reps = GROUP_W // LANES
    return jnp.concatenate([pair0] * reps + [pair1] * reps, axis=1).astype(BF16)


def _attention(i, qn, kall, vall, sink_ref, *, mask_history):
    tm = qn.shape[0]
    k4 = _spread_kv(kall)
    v4 = _spread_kv(vall)
    lane_grp = lax.broadcasted_iota(I32, (CHUNK, GROUP_W), 1) // HEAD_DIM
    row_grp = lax.broadcasted_iota(I32, (Q_PER_KV * CHUNK, 1), 0) // CHUNK
    span = WINDOW + CHUNK
    att_rows = []
    for c in range(tm // CHUNK):
        per_kv = []
        for hk in range(N_KV_HEADS):
            qc = qn[c * CHUNK:(c + 1) * CHUNK, hk * GROUP_W:(hk + 1) * GROUP_W]
            qs = jnp.concatenate([jnp.where(lane_grp == g, qc, 0.0) for g in range(Q_PER_KV)],
                                 axis=0).astype(BF16)
            kw = k4[c * CHUNK:c * CHUNK + span, hk * GROUP_W:(hk + 1) * GROUP_W]
            vw = v4[c * CHUNK:c * CHUNK + span, hk * GROUP_W:(hk + 1) * GROUP_W]
            s = _dot_nt(qs, kw) * (HEAD_DIM ** -0.5)
            if mask_history and c * CHUNK < WINDOW:
                kcol = lax.broadcasted_iota(I32, (1, span), 1)
                s = jnp.where(kcol >= jnp.where(i == 0, WINDOW - c * CHUNK, 0), s, NEG)
            sink = jnp.zeros((Q_PER_KV * CHUNK, 1), F32)
            for g in range(Q_PER_KV):
                sink = jnp.where(row_grp == g, sink_ref[0, hk * Q_PER_KV + g], sink)
            m = jnp.maximum(jnp.max(s, axis=-1, keepdims=True), sink)
            p = jnp.exp(s - m)
            den = jnp.sum(p, axis=-1, keepdims=True) + jnp.exp(sink - m)
            o = _dot(p.astype(BF16), vw) / den
            oc = jnp.zeros((CHUNK, GROUP_W), F32)
            for g in range(Q_PER_KV):
                oc = oc + jnp.where(lane_grp == g, o[g * CHUNK:(g + 1) * CHUNK, :], 0.0)
            per_kv.append(oc)
        att_rows.append(jnp.concatenate(per_kv, axis=1))
    return jnp.concatenate(att_rows, axis=0)


def _mixer_kernel(*refs, tm, nsub, st, has_cache):
    refs = list(refs)
    x_ref = refs.pop(0)
    if has_cache:
        ck_ref, cv_ref, st_ref = refs.pop(0), refs.pop(0), refs.pop(0)
    (g1_ref, win_ref, qg_ref, kg_ref, bd_ref, sink_ref, cw_ref, wout_ref, g2_ref,
     wrt_ref, brt_ref, upper_ref, lower_ref,
     h_ref, xs_ref, col_ref, tcnt_ref, knew_ref, vnew_ref, unew_ref, kprev, vprev, ubuf) = refs
    i = pl.program_id(1)
    ntok = nsub * tm

    @pl.when(i == 0)
    def _():
        ubuf[:, 0:SUBLANES, :] = jnp.zeros((nsub, SUBLANES, CONV_DIM), F32)
        if has_cache:
            kprev[...] = ck_ref[...]
            vprev[...] = cv_ref[...]
            ubuf[:, SUBLANES - (CONV_W - 1):SUBLANES, :] = st_ref[...]
        else:
            kprev[...] = jnp.zeros(kprev.shape, F32)
            vprev[...] = jnp.zeros(vprev.shape, F32)

    x = jnp.concatenate([x_ref[sub] for sub in range(nsub)], axis=0) if nsub > 1 else x_ref[0]
    xn = _rms_rows(x, g1_ref[...])
    proj = _dot(xn.astype(BF16), win_ref[...])
    o0 = ATT_DIM
    o1 = o0 + KV_DIM
    o2 = o1 + KV_DIM
    o3 = o2 + CONV_DIM
    o4 = o3 + CONV_DIM
    bd = bd_ref[...]
    qn = _head_rms(proj[:, :o0], bd, qg_ref[...])
    kn = _head_rms(proj[:, o0:o1], bd[:KV_DIM, :KV_DIM], kg_ref[...])
    v = proj[:, o1:o2]
    bg = proj[:, o2:o3]
    u = proj[:, o3:o4] * proj[:, o4:]

    cw = cw_ref[...]
    att_parts, cv_parts = [], []
    for sub in range(nsub):
        r0, r1 = sub * tm, (sub + 1) * tm
        kall = jnp.concatenate([kprev[sub], kn[r0:r1]], axis=0)
        vall = jnp.concatenate([vprev[sub], v[r0:r1]], axis=0)
        att_parts.append(_attention(i, qn[r0:r1], kall, vall, sink_ref,
                                    mask_history=not has_cache))
        kprev[sub] = kall[tm:, :]
        vprev[sub] = vall[tm:, :]
        knew_ref[sub] = kall[tm:, :]
        vnew_ref[sub] = vall[tm:, :]

        us = u[r0:r1]
        ubuf[sub, SUBLANES:SUBLANES + tm, :] = us
        first = SUBLANES - (CONV_W - 1)
        cvs = ubuf[sub, first:first + tm, :] * cw[0:1, :]
        for tap in range(1, CONV_W - 1):
            cvs = cvs + ubuf[sub, first + tap:first + tap + tm, :] * cw[tap:tap + 1, :]
        cv_parts.append(cvs + us * cw[CONV_W - 1:CONV_W, :])
        ubuf[sub, 0:SUBLANES, :] = ubuf[sub, tm:tm + SUBLANES, :]
        unew_ref[sub] = us[tm - (CONV_W - 1):, :]
    att = jnp.concatenate(att_parts, axis=0) if nsub > 1 else att_parts[0]
    cv = jnp.concatenate(cv_parts, axis=0) if nsub > 1 else cv_parts[0]

    mix = jnp.concatenate([att, bg * cv], axis=1).astype(BF16)
    h = x + _dot(mix, wout_ref[...])
    for sub in range(nsub):
        h_ref[sub] = h[sub * tm:(sub + 1) * tm]
    hn = _rms_rows(h, g2_ref[...])

    hn_hi, hn_lo = _split_bf16(hn)
    wr_hi, wr_lo = _split_bf16(wrt_ref[...])
    logits = (_dot_nt(wr_hi, hn_hi) + _dot_nt(wr_hi, hn_lo) + _dot_nt(wr_lo, hn_hi)) + brt_ref[...]
    eio = lax.broadcasted_iota(I32, (N_EXPERTS, ntok), 0)
    work = logits
    tops, sels = [], []
    for _ in range(TOP_K):
        mk = jnp.max(work, axis=0, keepdims=True)
        ik = jnp.min(jnp.where(work == mk, eio, N_EXPERTS), axis=0, keepdims=True)
        sel = eio == ik
        work = jnp.where(sel, -jnp.inf, work)
        tops.append(mk)
        sels.append(sel)
    es = [jnp.exp(t - tops[0]) for t in tops]
    esum = es[0] + es[1] + es[2] + es[3]

    onehot = jnp.zeros((N_EXPERTS, ntok), F32)
    for sel in sels:
        onehot = onehot + jnp.where(sel, 1.0, 0.0)
    onehot_b = onehot.astype(BF16)
    smaller = _dot(lower_ref[...], onehot_b)
    nst = ntok // st
    where_parts = []
    for k in range(nst):
        c0, c1 = k * st, (k + 1) * st
        before = _dot(onehot_b[:, c0:c1], upper_ref[...])
        start = jnp.sum(smaller[:, c0:c1], axis=1, keepdims=True)
        where_parts.append(before + start)
    where_to = jnp.concatenate(where_parts, axis=1) if nst > 1 else where_parts[0]
    pos = [jnp.sum(jnp.where(sel, where_to, 0.0), axis=0, keepdims=True) for sel in sels]
    rio = lax.broadcasted_iota(I32, (TOP_K * st, st), 0)
    for k in range(nst):
        c0, c1 = k * st, (k + 1) * st
        hit = rio == pos[0][:, c0:c1].astype(I32)
        for slot in range(1, TOP_K):
            hit = hit | (rio == pos[slot][:, c0:c1].astype(I32))
        perm = jnp.where(hit, 1.0, 0.0).astype(BF16)
        _pack_rows(_dot(perm, hn_hi[c0:c1]), xs_ref.at[k, 0])
        tcnt_ref[k, 0] = _dot_nt(jnp.ones((1, st), BF16), onehot_b[:, c0:c1]).astype(I32)

    rows8 = jnp.concatenate(pos + [e / esum for e in es], axis=0)
    sq = jnp.concatenate([rows8, jnp.zeros((LANES - 2 * TOP_K, ntok), F32)], axis=0)
    cols = jnp.transpose(sq)
    for sub in range(nsub):
        col_ref[sub] = cols[sub * tm:(sub + 1) * tm]


def _mixer(x, cache, w, *, tm, nsub, st):
    nb, s, _ = x.shape
    nt = s // tm
    nst = nsub * tm // st
    has_cache = cache is not None
    full = lambda shape: pl.BlockSpec(shape, lambda b, i: (0,) * len(shape))
    per_b = lambda shape: pl.BlockSpec((nsub,) + shape, lambda b, i: (b, 0, 0))
    tile = lambda last: pl.BlockSpec((nsub, tm, last), lambda b, i: (b, i, 0))
    kv_shape = (WINDOW, KV_DIM)
    kv_spec = per_b(kv_shape)

    in_specs = [tile(D_MODEL)]
    args = [x]
    if has_cache:
        in_specs += [kv_spec, kv_spec, per_b((CONV_W - 1, CONV_DIM))]
        args += list(cache)
    tio = jnp.arange(st)
    eio = jnp.arange(N_EXPERTS)
    upper = (tio[:, None] < tio[None, :]).astype(BF16)
    lower = (eio[None, :] < eio[:, None]).astype(BF16)
    in_specs += [
        full((1, D_MODEL)), full((D_MODEL, IN_DIM)), full((1, ATT_DIM)),
        full((1, KV_DIM)), full((ATT_DIM, ATT_DIM)),
        pl.BlockSpec(memory_space=pltpu.SMEM),
        full((CONV_W, CONV_DIM)), full((D_MODEL, D_MODEL)), full((1, D_MODEL)),
        full((N_EXPERTS, D_MODEL)), full((N_EXPERTS, 1)), full((st, st)),
        full((N_EXPERTS, N_EXPERTS)),
    ]
    args += [w['g1'], w['w_in'], w['qg'], w['kg'], w['bd'], w['sinks'], w['conv_w'],
             w['w_out'], w['g2'], w['w_router_t'], w['b_router_t'], upper, lower]
    rt = TOP_K * st
    nsort = nb // nsub * nst
    out_shape = [
        jax.ShapeDtypeStruct((nb, s, D_MODEL), F32),
        jax.ShapeDtypeStruct((nsort, nt, rt, PACK_S, LANES), U32),
        jax.ShapeDtypeStruct((nb, s, LANES), F32),
        jax.ShapeDtypeStruct((nsort, nt, 1, N_EXPERTS), I32),
        jax.ShapeDtypeStruct((nb,) + kv_shape, F32),
        jax.ShapeDtypeStruct((nb,) + kv_shape, F32),
        jax.ShapeDtypeStruct((nb, CONV_W - 1, CONV_DIM), F32),
    ]
    out_specs = [
        tile(D_MODEL),
        pl.BlockSpec((nst, 1, rt, PACK_S, LANES), lambda b, i: (b, i, 0, 0, 0)),
        tile(LANES),
        pl.BlockSpec((nst, 1, 1, N_EXPERTS), lambda b, i: (b, i, 0, 0)),
        kv_spec, kv_spec, per_b((CONV_W - 1, CONV_DIM)),
    ]
    scratch = [pltpu.VMEM((nsub, WINDOW, KV_DIM), F32), pltpu.VMEM((nsub, WINDOW, KV_DIM), F32),
               pltpu.VMEM((nsub, tm + SUBLANES, CONV_DIM), F32)]
    return pl.pallas_call(
        functools.partial(_mixer_kernel, tm=tm, nsub=nsub, st=st, has_cache=has_cache),
        grid=(nb // nsub, nt),
        in_specs=in_specs,
        out_specs=out_specs,
        out_shape=out_shape,
        scratch_shapes=scratch,
        compiler_params=pltpu.CompilerParams(
            dimension_semantics=("arbitrary", "arbitrary"), vmem_limit_bytes=MIXER_VMEM_LIMIT),
        name="mixer_sample" if has_cache else "mixer_prompt",
    )(*args)


def _expert_kernel(be_ref, bq_ref, nused_ref, src_ref, off_ref, n_ref, tot_ref,
                   xsp_hbm, xss_hbm, wgu_hbm, bgu_ref, wd_hbm, bdn_ref, ys_ref,
                   xbuf, sems, wgu_f, wd_f, wsems, ptr, wcur, *, n_tiles, n_prompt_tiles):
    b = pl.program_id(0)
    nused = nused_ref[0]

    def weight_copies(e, ws):
        return (pltpu.make_async_copy(wgu_hbm.at[e], wgu_f.at[ws], wsems.at[0, ws]),
                pltpu.make_async_copy(wd_hbm.at[e], wd_f.at[ws], wsems.at[1, ws]))

    def issue_rows(blk, slot, enabled, between=lambda: None):
        blk = jnp.minimum(blk, pl.num_programs(0) - 1)
        e = be_ref[blk]
        q = bq_ref[blk]
        lo_row = q * ROW_BLOCK
        hi_row = lo_row + ROW_BLOCK
        base = e * n_tiles

        def segment(j, j_limit, src_hbm, ok):
            jc = jnp.minimum(j, j_limit - 1)
            o = off_ref[base + jc]
            inside = ok & (j < j_limit) & (o < hi_row)
            lo = jnp.maximum(o, lo_row)
            ln = jnp.minimum(o + n_ref[base + jc], hi_row) - lo

            @pl.when(inside & (ln > 0))
            def _():
                pltpu.make_async_copy(src_hbm.at[pl.ds(src_ref[base + jc] + (lo - o), ln)],
                                      xbuf.at[slot, pl.ds(lo - lo_row, ln)], sems.at[slot]).start()
            return inside

        j0 = jnp.where(q == 0, 0, ptr[0])

        def cond(j):
            return (enabled & (j < n_prompt_tiles)
                    & (off_ref[base + jnp.minimum(j, n_prompt_tiles - 1)] < hi_row))

        def body(j):
            segment(j, n_prompt_tiles, xsp_hbm, True)
            return j + 1

        jtail = lax.while_loop(cond, body, j0 + INLINE_SEGMENTS)
        result = between()
        count = jnp.int32(0)
        for k in range(INLINE_SEGMENTS):
            count = count + segment(j0 + k, n_prompt_tiles, xsp_hbm, enabled).astype(I32)
        jend = jnp.where(count == INLINE_SEGMENTS, jtail, j0 + count)
        ptr[0] = jnp.where(enabled, jnp.maximum(jend - 1, 0), ptr[0])
        for j in range(n_prompt_tiles, n_tiles):
            segment(j, n_tiles, xss_hbm, enabled)
        return result

    @pl.when(b == 0)
    def _():
        xbuf[...] = jnp.zeros(xbuf.shape, U32)
        for ahead in range(LOOKAHEAD):
            issue_rows(ahead, ahead, ahead < nused)
        for c in weight_copies(be_ref[0], 0):
            c.start()
        wcur[0] = 1

    active = b < nused

    @pl.when(active)
    def _():
        slot = b % ROW_SLOTS
        e = be_ref[b]
        rows = jnp.minimum(tot_ref[e] - bq_ref[b] * ROW_BLOCK, ROW_BLOCK)
        pltpu.make_async_copy(xsp_hbm.at[pl.ds(0, rows)], xbuf.at[slot, pl.ds(0, rows)],
                              sems.at[slot]).wait()

        @pl.when(bq_ref[b] == 0)
        def _():
            ws = 1 - wcur[0]
            wcur[0] = ws
            for c in weight_copies(e, ws):
                c.wait()
            nxt = b + (tot_ref[e] + ROW_BLOCK - 1) // ROW_BLOCK

            @pl.when(nxt < nused)
            def _():
                for c in weight_copies(be_ref[jnp.minimum(nxt, pl.num_programs(0) - 1)], 1 - ws):
                    c.start()

        ws = wcur[0]

        def compute(nrows):
            x = issue_rows(b + LOOKAHEAD, (b + LOOKAHEAD) % ROW_SLOTS, b + LOOKAHEAD < nused,
                           between=lambda: _unpack_rows(xbuf.at[slot, pl.ds(0, nrows)]))
            gu = _dot(x, wgu_f[ws]) + bgu_ref[pl.ds(e, 1), :]
            g = jnp.minimum(gu[:, :D_FF], SWIGLU_LIMIT)
            u = jnp.clip(gu[:, D_FF:], -SWIGLU_LIMIT, SWIGLU_LIMIT)
            act = (u + 1.0) * (g * jax.nn.sigmoid(SWIGLU_ALPHA * g))
            y = (_dot(act.astype(BF16), wd_f[ws])
                 + (bdn_ref[pl.ds(e, 1), :] + _anchor_zero_row(xbuf)))
            _pack_rows(y.astype(BF16).astype(F32), ys_ref.at[pl.ds(0, nrows)])
            if nrows < ROW_BLOCK:
                ys_ref[nrows:] = jnp.zeros((ROW_BLOCK - nrows, PACK_S, LANES), U32)

        for piece in range(ROW_BLOCK // TAIL_ROWS):
            top = (piece + 1) * TAIL_ROWS
            pl.when((rows > top - TAIL_ROWS) & (rows <= top))(functools.partial(compute, top))

    @pl.when(jnp.logical_not(active))
    def _():
        ys_ref[...] = jnp.zeros(ys_ref.shape, U32)


def _experts(plan, xsp, xss, w_gu, b_gu, w_down, b_down, *, n_tiles, n_prompt_tiles, rows):
    nblk = rows // ROW_BLOCK

    grid_spec = pltpu.PrefetchScalarGridSpec(
        num_scalar_prefetch=7,
        grid=(nblk,),
        in_specs=[pl.BlockSpec(memory_space=pl.ANY),
                  pl.BlockSpec(memory_space=pl.ANY),
                  pl.BlockSpec(memory_space=pl.ANY),
                  pl.BlockSpec((N_EXPERTS, 2 * D_FF), lambda b, *_: (0, 0)),
                  pl.BlockSpec(memory_space=pl.ANY),
                  pl.BlockSpec((N_EXPERTS, D_MODEL), lambda b, *_: (0, 0))],
        out_specs=pl.BlockSpec((ROW_BLOCK, PACK_S, LANES), lambda b, *_: (b, 0, 0)),
        scratch_shapes=[pltpu.VMEM((ROW_SLOTS + 1, ROW_BLOCK, PACK_S, LANES), U32),
                        pltpu.SemaphoreType.DMA((ROW_SLOTS,)),
                        pltpu.VMEM((2, D_MODEL, 2 * D_FF), F32), pltpu.VMEM((2, D_FF, D_MODEL), F32),
                        pltpu.SemaphoreType.DMA((2, 2)),
                        pltpu.SMEM((1,), I32), pltpu.SMEM((1,), I32)],
    )
    return pl.pallas_call(
        functools.partial(_expert_kernel, n_tiles=n_tiles, n_prompt_tiles=n_prompt_tiles),
        grid_spec=grid_spec,
        out_shape=jax.ShapeDtypeStruct((rows, PACK_S, LANES), U32),
        compiler_params=pltpu.CompilerParams(
            dimension_semantics=("arbitrary",), vmem_limit_bytes=VMEM_LIMIT),
        name="experts",
    )(plan['block_e'], plan['block_q'], plan['n_used'], plan['src_t'], plan['off_t'], plan['n_t'],
      plan['tot'], xsp, xss, w_gu, b_gu, w_down, b_down)


def _combine_kernel(ysrc_ref, sloc_ref, n_ref, col_ref, h_ref, ys_hbm, o_ref, buf, sems,
                    *, tm, tile0):
    j = pl.program_id(0)
    nt = pl.num_programs(0)
    rt = TOP_K * tm

    def issue(t, slot, enabled):
        base = (tile0 + jnp.minimum(t, nt - 1)) * N_EXPERTS
        for e in range(N_EXPERTS):
            n = n_ref[base + e]

            @pl.when(enabled & (n > 0))
            def _():
                pltpu.make_async_copy(ys_hbm.at[pl.ds(ysrc_ref[base + e], n)],
                                      buf.at[slot, pl.ds(sloc_ref[base + e], n)],
                                      sems.at[slot]).start(priority=e % 2)

    @pl.when(j == 0)
    def _():
        buf[ROW_SLOTS, 0:SUBLANES // PACK_S] = jnp.zeros((SUBLANES // PACK_S, PACK_S, LANES), U32)
        for ahead in range(LOOKAHEAD):
            issue(ahead, ahead, ahead < nt)

    slot = j % ROW_SLOTS
    pltpu.make_async_copy(ys_hbm.at[pl.ds(0, rt)], buf.at[slot], sems.at[slot]).wait()
    y = _unpack_rows(buf.at[slot])
    issue(j + LOOKAHEAD, (j + LOOKAHEAD) % ROW_SLOTS, j + LOOKAHEAD < nt)
    col = col_ref[...]
    rio = lax.broadcasted_iota(I32, (tm, rt), 1)
    gmat = jnp.zeros((tm, rt), F32)
    for k in range(TOP_K):
        gmat = gmat + jnp.where(rio == col[:, k:k + 1].astype(I32),
                                col[:, TOP_K + k:TOP_K + k + 1], 0.0)
    o_ref[...] = (h_ref[...] + _anchor_zero_row(buf)) + _dot(gmat.astype(BF16), y)


def _combine(plan, col, h2d, ys, *, tm, tile0):
    t = h2d.shape[0]
    rt = TOP_K * tm
    grid_spec = pltpu.PrefetchScalarGridSpec(
        num_scalar_prefetch=3,
        grid=(t // tm,),
        in_specs=[pl.BlockSpec((tm, LANES), lambda i, *_: (i, 0)),
                  pl.BlockSpec((tm, D_MODEL), lambda i, *_: (i, 0)),
                  pl.BlockSpec(memory_space=pl.ANY)],
        out_specs=pl.BlockSpec((tm, D_MODEL), lambda i, *_: (i, 0)),
        scratch_shapes=[pltpu.VMEM((ROW_SLOTS + 1, rt, PACK_S, LANES), U32),
                        pltpu.SemaphoreType.DMA((ROW_SLOTS,))],
    )
    return pl.pallas_call(
        functools.partial(_combine_kernel, tm=tm, tile0=tile0),
        grid_spec=grid_spec,
        out_shape=jax.ShapeDtypeStruct((t, D_MODEL), F32),
        compiler_params=pltpu.CompilerParams(
            dimension_semantics=("arbitrary",), vmem_limit_bytes=VMEM_LIMIT),
        name="combine",
    )(plan['ysrc'], plan['sloc'], plan['n'], col, h2d, ys)


def kernel(x_prompt, x_sample, cache_k, cache_v, state_conv, norm1_g, w_in, q_norm_g, k_norm_g,
           sinks, conv_w, w_out, norm2_g, w_router, b_router, w_gu, b_gu, w_down, b_down):
    l = 0
    nbp, sp, _ = x_prompt.shape
    nbs, ss, _ = x_sample.shape
    tp, ts = nbp * sp, nbs * ss
    tmp, tms = TOKEN_TILE, ts
    ntp, nts = tp // tmp, ts // tms
    n_tiles = ntp + nts
    n_rows = (tp + ts) * TOP_K
    grp = jnp.arange(ATT_DIM) // HEAD_DIM
    w = {
        'g1': norm1_g[l][None, :],
        'w_in': w_in[l].astype(BF16),
        'qg': jnp.tile(q_norm_g[l], N_HEADS)[None, :],
        'kg': jnp.tile(k_norm_g[l], N_KV_HEADS)[None, :],
        'bd': (grp[:, None] == grp[None, :]).astype(BF16),
        'sinks': sinks[l][None, :],
        'conv_w': conv_w[l],
        'w_out': w_out[l].astype(BF16),
        'g2': norm2_g[l][None, :],
        'w_router_t': w_router[l].T,
        'b_router_t': b_router[l][:, None],
    }
    hp, xsp, colp, tcp, knp, vnp, unp = _mixer(x_prompt, None, w, tm=tmp, nsub=STREAMS_PER_STEP,
                                               st=tmp)
    cache = (cache_k[l].reshape(nbs, WINDOW, KV_DIM), cache_v[l].reshape(nbs, WINDOW, KV_DIM),
             state_conv[l])
    hs, xss, cols, tcs, kns, vns, uns = _mixer(x_sample, cache, w, tm=ss, nsub=nbs, st=tms)

    n = jnp.concatenate([tcp.reshape(ntp, N_EXPERTS), tcs.reshape(nts, N_EXPERTS)], axis=0)
    tile_base = jnp.concatenate([jnp.arange(ntp, dtype=I32) * (TOP_K * tmp),
                                 jnp.arange(nts, dtype=I32) * (TOP_K * tms)])
    sloc = jnp.cumsum(n, axis=1) - n
    off = jnp.cumsum(n, axis=0) - n
    tot = jnp.sum(n, axis=0)
    padded = (tot + ROW_BLOCK - 1) // ROW_BLOCK * ROW_BLOCK
    ends = jnp.cumsum(padded)
    base = ends - padded
    rows = (-(-n_rows // ROW_BLOCK) + N_EXPERTS) * ROW_BLOCK
    nblk = rows // ROW_BLOCK
    blk0 = jnp.arange(nblk, dtype=I32) * ROW_BLOCK
    block_e = jnp.minimum(jnp.sum((ends[None, :] <= blk0[:, None]).astype(I32), axis=1),
                          N_EXPERTS - 1)
    eq = (block_e[:, None] == jnp.arange(N_EXPERTS, dtype=I32)[None, :]).astype(I32)
    block_q = (blk0 - jnp.sum(eq * base[None, :], axis=1)) // ROW_BLOCK
    plan = {
        'block_e': block_e.astype(I32),
        'block_q': block_q.astype(I32),
        'n_used': (ends[-1:] // ROW_BLOCK).astype(I32),
        'src_t': (tile_base[:, None] + sloc).T.reshape(-1).astype(I32),
        'off_t': off.T.reshape(-1).astype(I32),
        'n_t': n.T.reshape(-1).astype(I32),
        'tot': tot.astype(I32),
        'ysrc': (base[None, :] + off).reshape(-1).astype(I32),
        'sloc': sloc.reshape(-1).astype(I32),
        'n': n.reshape(-1).astype(I32),
    }

    ys = _experts(plan, xsp.reshape(-1, PACK_S, LANES), xss.reshape(-1, PACK_S, LANES),
                  w_gu[l], b_gu[l], w_down[l], b_down[l],
                  n_tiles=n_tiles, n_prompt_tiles=ntp, rows=rows)
    yp = _combine(plan, colp.reshape(tp, LANES), hp.reshape(tp, D_MODEL), ys, tm=tmp, tile0=0)
    ysm = _combine(plan, cols.reshape(ts, LANES), hs.reshape(ts, D_MODEL), ys, tm=tms, tile0=ntp)

    kv5 = lambda a: a.reshape(1, a.shape[0], WINDOW, N_KV_HEADS, HEAD_DIM)
    return (yp.reshape(nbp, sp, D_MODEL), ysm.reshape(nbs, ss, D_MODEL),
            kv5(knp), kv5(vnp), unp[None], kv5(kns), kv5(vns), uns[None])
```

```python
import functools

import jax
import jax.numpy as jnp
from jax import lax
from jax.experimental import pallas as pl
from jax.experimental.pallas import tpu as pltpu

D_MODEL = 1024
CHUNK = 64
HEAD_DIM = 64
N_HEADS = 8
N_KV_HEADS = 2
Q_PER_KV = N_HEADS // N_KV_HEADS
ATT_DIM = N_HEADS * HEAD_DIM
KV_DIM = N_KV_HEADS * HEAD_DIM
GROUP_W = Q_PER_KV * HEAD_DIM
CONV_DIM = D_MODEL - ATT_DIM
CONV_W = 3
WINDOW = 128
IN_DIM = ATT_DIM + 2 * KV_DIM + 3 * CONV_DIM
N_EXPERTS = 32
TOP_K = 4
D_FF = D_MODEL
SWIGLU_LIMIT = 7.0
SWIGLU_ALPHA = 1.702
EPS = 1e-6
NEG = -1e30

LANES = 128
SUBLANES = 8
HALF = D_MODEL // 2
PACK_S = HALF // LANES
TOKEN_TILE = 256
STREAMS_PER_STEP = 4
COMBINE_TILES = 2
ROW_BLOCK = 512
TAIL_ROWS = 128
INLINE_SEGMENTS = 20
LOOKAHEAD = 2
ROW_SLOTS = LOOKAHEAD + 1
VMEM_LIMIT = 56 * 1024 * 1024
MIXER_VMEM_LIMIT = 60 * 1024 * 1024

F32 = jnp.float32
BF16 = jnp.bfloat16
U32 = jnp.uint32
I32 = jnp.int32


def _rms_rows(x, g):
    return x * lax.rsqrt(jnp.mean(x * x, axis=-1, keepdims=True) + EPS) * g


def _split_bf16(x):
    hi = x.astype(BF16)
    lo = (x - hi.astype(F32)).astype(BF16)
    return hi, lo


def _dot(a, b):
    return jnp.dot(a, b, preferred_element_type=F32)


def _dot_nt(a, b):
    return lax.dot_general(a, b, (((1,), (1,)), ((), ())), preferred_element_type=F32)


def _head_rms(t, bd, g):
    ssq = _dot((t * t).astype(BF16), bd)
    return t * lax.rsqrt(ssq * (1.0 / HEAD_DIM) + EPS) * g


def _pack_rows(vals, out_ref):
    r = vals.shape[0]
    bits = pltpu.bitcast(vals, U32)
    word = (bits[:, :HALF] >> 16) | (bits[:, HALF:] & jnp.uint32(0xFFFF0000))
    flat = out_ref.reshape(r * PACK_S, LANES)
    for s in range(PACK_S):
        flat[pl.ds(s, r, stride=PACK_S), :] = word[:, s * LANES:(s + 1) * LANES]


def _unpack_rows(packed_ref):
    r = packed_ref.shape[0]
    flat = packed_ref.reshape(r * PACK_S, LANES)
    lo, hi = [], []
    for s in range(PACK_S):
        w = flat[pl.ds(s, r, stride=PACK_S), :]
        lo.append(pltpu.bitcast(w << 16, F32))
        hi.append(pltpu.bitcast(w & jnp.uint32(0xFFFF0000), F32))
    return jnp.concatenate(lo + hi, axis=1).astype(BF16)


def _anchor_zero_row(staging_ref):
    z = staging_ref.at[ROW_SLOTS, pl.ds(0, SUBLANES // PACK_S)].reshape(SUBLANES, LANES)[...]
    return jnp.concatenate([pltpu.bitcast(z, F32)[0:1, :]] * (D_MODEL // LANES), axis=1)


def _spread_kv(t):
    assert N_KV_HEADS == 2 and KV_DIM == LANES
    swapped = pltpu.roll(t, HEAD_DIM, axis=1)
    low = lax.broadcasted_iota(I32, t.shape, 1) < HEAD_DIM
    pair0 = jnp.where(low, t, swapped)
    pair1 = jnp.where(low, swapped, t)
    reps = GROUP_W // LANES
    return jnp.concatenate([pair0] * reps + [pair1] * reps, axis=1).astype(BF16)


def _attention(i, qn, kall, vall, sink_ref, *, mask_history):
    tm = qn.shape[0]
    k4 = _spread_kv(kall)
    v4 = _spread_kv(vall)
    lane_grp = lax.broadcasted_iota(I32, (CHUNK, GROUP_W), 1) // HEAD_DIM
    row_grp = lax.broadcasted_iota(I32, (Q_PER_KV * CHUNK, 1), 0) // CHUNK
    span = WINDOW + CHUNK
    att_rows = []
    for c in range(tm // CHUNK):
        per_kv = []
        for hk in range(N_KV_HEADS):
            qc = qn[c * CHUNK:(c + 1) * CHUNK, hk * GROUP_W:(hk + 1) * GROUP_W]
            qs = jnp.concatenate([jnp.where(lane_grp == g, qc, 0.0) for g in range(Q_PER_KV)],
                                 axis=0).astype(BF16)
            kw = k4[c * CHUNK:c * CHUNK + span, hk * GROUP_W:(hk + 1) * GROUP_W]
            vw = v4[c * CHUNK:c * CHUNK + span, hk * GROUP_W:(hk + 1) * GROUP_W]
            s = _dot_nt(qs, kw) * (HEAD_DIM ** -0.5)
            if mask_history and c * CHUNK < WINDOW:
                kcol = lax.broadcasted_iota(I32, (1, span), 1)
                s = jnp.where(kcol >= jnp.where(i == 0, WINDOW - c * CHUNK, 0), s, NEG)
            sink = jnp.zeros((Q_PER_KV * CHUNK, 1), F32)
            for g in range(Q_PER_KV):
                sink = jnp.where(row_grp == g, sink_ref[0, hk * Q_PER_KV + g], sink)
            m = jnp.maximum(jnp.max(s, axis=-1, keepdims=True), sink)
            p = jnp.exp(s - m)
            den = jnp.sum(p, axis=-1, keepdims=True) + jnp.exp(sink - m)
            o = _dot(p.astype(BF16), vw) / den
            oc = jnp.zeros((CHUNK, GROUP_W), F32)
            for g in range(Q_PER_KV):
                oc = oc + jnp.where(lane_grp == g, o[g * CHUNK:(g + 1) * CHUNK, :], 0.0)
            per_kv.append(oc)
        att_rows.append(jnp.concatenate(per_kv, axis=1))
    return jnp.concatenate(att_rows, axis=0)


def _mixer_kernel(*refs, tm, nsub, st, has_cache):
    refs = list(refs)
    x_ref = refs.pop(0)
    if has_cache:
        ck_ref, cv_ref, st_ref = refs.pop(0), refs.pop(0), refs.pop(0)
    (g1_ref, win_ref, qg_ref, kg_ref, bd_ref, sink_ref, cw_ref, wout_ref, g2_ref,
     wrt_ref, brt_ref, upper_ref, lower_ref,
     h_ref, xs_ref, col_ref, tcnt_ref, knew_ref, vnew_ref, unew_ref, kprev, vprev, ubuf) = refs
    i = pl.program_id(1)
    ntok = nsub * tm

    @pl.when(i == 0)
    def _():
        ubuf[:, 0:SUBLANES, :] = jnp.zeros((nsub, SUBLANES, CONV_DIM), F32)
        if has_cache:
            kprev[...] = ck_ref[...]
            vprev[...] = cv_ref[...]
            ubuf[:, SUBLANES - (CONV_W - 1):SUBLANES, :] = st_ref[...]
        else:
            kprev[...] = jnp.zeros(kprev.shape, F32)
            vprev[...] = jnp.zeros(vprev.shape, F32)

    x = jnp.concatenate([x_ref[sub] for sub in range(nsub)], axis=0) if nsub > 1 else x_ref[0]
    xn = _rms_rows(x, g1_ref[...])
    proj = _dot(xn.astype(BF16), win_ref[...])
    o0 = ATT_DIM
    o1 = o0 + KV_DIM
    o2 = o1 + KV_DIM
    o3 = o2 + CONV_DIM
    o4 = o3 + CONV_DIM
    bd = bd_ref[...]
    qn = _head_rms(proj[:, :o0], bd, qg_ref[...])
    kn = _head_rms(proj[:, o0:o1], bd[:KV_DIM, :KV_DIM], kg_ref[...])
    v = proj[:, o1:o2]
    bg = proj[:, o2:o3]
    u = proj[:, o3:o4] * proj[:, o4:]

    cw = cw_ref[...]
    att_parts, cv_parts = [], []
    for sub in range(nsub):
        r0, r1 = sub * tm, (sub + 1) * tm
        kall = jnp.concatenate([kprev[sub], kn[r0:r1]], axis=0)
        vall = jnp.concatenate([vprev[sub], v[r0:r1]], axis=0)
        att_parts.append(_attention(i, qn[r0:r1], kall, vall, sink_ref,
                                    mask_history=not has_cache))
        kprev[sub] = kall[tm:, :]
        vprev[sub] = vall[tm:, :]
        knew_ref[sub] = kall[tm:, :]
        vnew_ref[sub] = vall[tm:, :]

        us = u[r0:r1]
        ubuf[sub, SUBLANES:SUBLANES + tm, :] = us
        first = SUBLANES - (CONV_W - 1)
        cvs = ubuf[sub, first:first + tm, :] * cw[0:1, :]
        for tap in range(1, CONV_W - 1):
            cvs = cvs + ubuf[sub, first + tap:first + tap + tm, :] * cw[tap:tap + 1, :]
        cv_parts.append(cvs + us * cw[CONV_W - 1:CONV_W, :])
        ubuf[sub, 0:SUBLANES, :] = ubuf[sub, tm:tm + SUBLANES, :]
        unew_ref[sub] = us[tm - (CONV_W - 1):, :]
    att = jnp.concatenate(att_parts, axis=0) if nsub > 1 else att_parts[0]
    cv = jnp.concatenate(cv_parts, axis=0) if nsub > 1 else cv_parts[0]

    mix = jnp.concatenate([att, bg * cv], axis=1).astype(BF16)
    h = x + _dot(mix, wout_ref[...])
    for sub in range(nsub):
        h_ref[sub] = h[sub * tm:(sub + 1) * tm]
    hn = _rms_rows(h, g2_ref[...])

    hn_hi, hn_lo = _split_bf16(hn)
    wr_hi, wr_lo = _split_bf16(wrt_ref[...])
    logits = (_dot_nt(wr_hi, hn_hi) + _dot_nt(wr_hi, hn_lo) + _dot_nt(wr_lo, hn_hi)) + brt_ref[...]
    eio = lax.broadcasted_iota(I32, (N_EXPERTS, ntok), 0)
    work = logits
    tops, sels = [], []
    for _ in range(TOP_K):
        mk = jnp.max(work, axis=0, keepdims=True)
        ik = jnp.min(jnp.where(work == mk, eio, N_EXPERTS), axis=0, keepdims=True)
        sel = eio == ik
        work = jnp.where(sel, -jnp.inf, work)
        tops.append(mk)
        sels.append(sel)
    es = [jnp.exp(t - tops[0]) for t in tops]
    esum = es[0] + es[1] + es[2] + es[3]

    onehot = jnp.zeros((N_EXPERTS, ntok), F32)
    for sel in sels:
        onehot = onehot + jnp.where(sel, 1.0, 0.0)
    onehot_b = onehot.astype(BF16)
    smaller = _dot(lower_ref[...], onehot_b)
    nst = ntok // st
    where_parts = []
    for k in range(nst):
        c0, c1 = k * st, (k + 1) * st
        before = _dot(onehot_b[:, c0:c1], upper_ref[...])
        start = jnp.sum(smaller[:, c0:c1], axis=1, keepdims=True)
        where_parts.append(before + start)
    where_to = jnp.concatenate(where_parts, axis=1) if nst > 1 else where_parts[0]
    pos = [jnp.sum(jnp.where(sel, where_to, 0.0), axis=0, keepdims=True) for sel in sels]
    rio = lax.broadcasted_iota(I32, (TOP_K * st, st), 0)
    for k in range(nst):
        c0, c1 = k * st, (k + 1) * st
        hit = rio == pos[0][:, c0:c1].astype(I32)
        for slot in range(1, TOP_K):
            hit = hit | (rio == pos[slot][:, c0:c1].astype(I32))
        perm = jnp.where(hit, 1.0, 0.0).astype(BF16)
        _pack_rows(_dot(perm, hn_hi[c0:c1]), xs_ref.at[k, 0])
        tcnt_ref[k, 0] = _dot_nt(jnp.ones((1, st), BF16), onehot_b[:, c0:c1]).astype(I32)

    rows8 = jnp.concatenate(pos + [e / esum for e in es], axis=0)
    sq = jnp.concatenate([rows8, jnp.zeros((LANES - 2 * TOP_K, ntok), F32)], axis=0)
    cols = jnp.transpose(sq)
    for sub in range(nsub):
        col_ref[sub] = cols[sub * tm:(sub + 1) * tm]


def _mixer(x, cache, w, *, tm, nsub, st):
    nb, s, _ = x.shape
    nt = s // tm
    nst = nsub * tm // st
    has_cache = cache is not None
    full = lambda shape: pl.BlockSpec(shape, lambda b, i: (0,) * len(shape))
    per_b = lambda shape: pl.BlockSpec((nsub,) + shape, lambda b, i: (b, 0, 0))
    tile = lambda last: pl.BlockSpec((nsub, tm, last), lambda b, i: (b, i, 0))
    kv_shape = (WINDOW, KV_DIM)
    kv_spec = per_b(kv_shape)

    in_specs = [tile(D_MODEL)]
    args = [x]
    if has_cache:
        in_specs += [kv_spec, kv_spec, per_b((CONV_W - 1, CONV_DIM))]
        args += list(cache)
    tio = jnp.arange(st)
    eio = jnp.arange(N_EXPERTS)
    upper = (tio[:, None] < tio[None, :]).astype(BF16)
    lower = (eio[None, :] < eio[:, None]).astype(BF16)
    in_specs += [
        full((1, D_MODEL)), full((D_MODEL, IN_DIM)), full((1, ATT_DIM)),
        full((1, KV_DIM)), full((ATT_DIM, ATT_DIM)),
        pl.BlockSpec(memory_space=pltpu.SMEM),
        full((CONV_W, CONV_DIM)), full((D_MODEL, D_MODEL)), full((1, D_MODEL)),
        full((N_EXPERTS, D_MODEL)), full((N_EXPERTS, 1)), full((st, st)),
        full((N_EXPERTS, N_EXPERTS)),
    ]
    args += [w['g1'], w['w_in'], w['qg'], w['kg'], w['bd'], w['sinks'], w['conv_w'],
             w['w_out'], w['g2'], w['w_router_t'], w['b_router_t'], upper, lower]
    rt = TOP_K * st
    nsort = nb // nsub * nst
    out_shape = [
        jax.ShapeDtypeStruct((nb, s, D_MODEL), F32),
        jax.ShapeDtypeStruct((nsort, nt, rt, PACK_S, LANES), U32),
        jax.ShapeDtypeStruct((nb, s, LANES), F32),
        jax.ShapeDtypeStruct((nsort, nt, 1, N_EXPERTS), I32),
        jax.ShapeDtypeStruct((nb,) + kv_shape, F32),
        jax.ShapeDtypeStruct((nb,) + kv_shape, F32),
        jax.ShapeDtypeStruct((nb, CONV_W - 1, CONV_DIM), F32),
    ]
    out_specs = [
        tile(D_MODEL),
        pl.BlockSpec((nst, 1, rt, PACK_S, LANES), lambda b, i: (b, i, 0, 0, 0)),
        tile(LANES),
        pl.BlockSpec((nst, 1, 1, N_EXPERTS), lambda b, i: (b, i, 0, 0)),
        kv_spec, kv_spec, per_b((CONV_W - 1, CONV_DIM)),
    ]
    scratch = [pltpu.VMEM((nsub, WINDOW, KV_DIM), F32), pltpu.VMEM((nsub, WINDOW, KV_DIM), F32),
               pltpu.VMEM((nsub, tm + SUBLANES, CONV_DIM), F32)]
    return pl.pallas_call(
        functools.partial(_mixer_kernel, tm=tm, nsub=nsub, st=st, has_cache=has_cache),
        grid=(nb // nsub, nt),
        in_specs=in_specs,
        out_specs=out_specs,
        out_shape=out_shape,
        scratch_shapes=scratch,
        compiler_params=pltpu.CompilerParams(
            dimension_semantics=("arbitrary", "arbitrary"), vmem_limit_bytes=MIXER_VMEM_LIMIT),
        name="mixer_sample" if has_cache else "mixer_prompt",
    )(*args)


def _expert_kernel(be_ref, bq_ref, nused_ref, src_ref, off_ref, n_ref, tot_ref,
                   xsp_hbm, xss_hbm, wgu_hbm, bgu_ref, wd_hbm, bdn_ref, ys_ref,
                   xbuf, sems, wgu_f, wd_f, wsems, ptr, wcur, *, n_tiles, n_prompt_tiles):
    b = pl.program_id(0)
    nused = nused_ref[0]

    def weight_copies(e, ws):
        return (pltpu.make_async_copy(wgu_hbm.at[e], wgu_f.at[ws], wsems.at[0, ws]),
                pltpu.make_async_copy(wd_hbm.at[e], wd_f.at[ws], wsems.at[1, ws]))

    def issue_rows(blk, slot, enabled, between=lambda: None):
        blk = jnp.minimum(blk, pl.num_programs(0) - 1)
        e = be_ref[blk]
        q = bq_ref[blk]
        lo_row = q * ROW_BLOCK
        hi_row = lo_row + ROW_BLOCK
        base = e * n_tiles

        def segment(j, j_limit, src_hbm, ok):
            jc = jnp.minimum(j, j_limit - 1)
            o = off_ref[base + jc]
            inside = ok & (j < j_limit) & (o < hi_row)
            lo = jnp.maximum(o, lo_row)
            ln = jnp.minimum(o + n_ref[base + jc], hi_row) - lo

            @pl.when(inside & (ln > 0))
            def _():
                pltpu.make_async_copy(src_hbm.at[pl.ds(src_ref[base + jc] + (lo - o), ln)],
                                      xbuf.at[slot, pl.ds(lo - lo_row, ln)], sems.at[slot]).start()
            return inside

        j0 = jnp.where(q == 0, 0, ptr[0])

        def cond(j):
            return (enabled & (j < n_prompt_tiles)
                    & (off_ref[base + jnp.minimum(j, n_prompt_tiles - 1)] < hi_row))

        def body(j):
            segment(j, n_prompt_tiles, xsp_hbm, True)
            return j + 1

        jtail = lax.while_loop(cond, body, j0 + INLINE_SEGMENTS)
        result = between()
        count = jnp.int32(0)
        for k in range(INLINE_SEGMENTS):
            count = count + segment(j0 + k, n_prompt_tiles, xsp_hbm, enabled).astype(I32)
        jend = jnp.where(count == INLINE_SEGMENTS, jtail, j0 + count)
        ptr[0] = jnp.where(enabled, jnp.maximum(jend - 1, 0), ptr[0])
        for j in range(n_prompt_tiles, n_tiles):
            segment(j, n_tiles, xss_hbm, enabled)
        return result

    @pl.when(b == 0)
    def _():
        xbuf[...] = jnp.zeros(xbuf.shape, U32)
        for ahead in range(LOOKAHEAD):
            issue_rows(ahead, ahead, ahead < nused)
        for c in weight_copies(be_ref[0], 0):
            c.start()
        wcur[0] = 1

    active = b < nused

    @pl.when(active)
    def _():
        slot = b % ROW_SLOTS
        e = be_ref[b]
        rows = jnp.minimum(tot_ref[e] - bq_ref[b] * ROW_BLOCK, ROW_BLOCK)
        pltpu.make_async_copy(xsp_hbm.at[pl.ds(0, rows)], xbuf.at[slot, pl.ds(0, rows)],
                              sems.at[slot]).wait()

        @pl.when(bq_ref[b] == 0)
        def _():
            ws = 1 - wcur[0]
            wcur[0] = ws
            for c in weight_copies(e, ws):
                c.wait()
            nxt = b + (tot_ref[e] + ROW_BLOCK - 1) // ROW_BLOCK

            @pl.when(nxt < nused)
            def _():
                for c in weight_copies(be_ref[jnp.minimum(nxt, pl.num_programs(0) - 1)], 1 - ws):
                    c.start()

        ws = wcur[0]

        def compute(nrows):
            x = issue_rows(b + LOOKAHEAD, (b + LOOKAHEAD) % ROW_SLOTS, b + LOOKAHEAD < nused,
                           between=lambda: _unpack_rows(xbuf.at[slot, pl.ds(0, nrows)]))
            gu = _dot(x, wgu_f[ws]) + bgu_ref[pl.ds(e, 1), :]
            g = jnp.minimum(gu[:, :D_FF], SWIGLU_LIMIT)
            u = jnp.clip(gu[:, D_FF:], -SWIGLU_LIMIT, SWIGLU_LIMIT)
            act = (u + 1.0) * (g * jax.nn.sigmoid(SWIGLU_ALPHA * g))
            y = (_dot(act.astype(BF16), wd_f[ws])
                 + (bdn_ref[pl.ds(e, 1), :] + _anchor_zero_row(xbuf)))
            _pack_rows(y.astype(BF16).astype(F32), ys_ref.at[pl.ds(0, nrows)])
            if nrows < ROW_BLOCK:
                ys_ref[nrows:] = jnp.zeros((ROW_BLOCK - nrows, PACK_S, LANES), U32)

        for piece in range(ROW_BLOCK // TAIL_ROWS):
            top = (piece + 1) * TAIL_ROWS
            pl.when((rows > top - TAIL_ROWS) & (rows <= top))(functools.partial(compute, top))

    @pl.when(jnp.logical_not(active))
    def _():
        ys_ref[...] = jnp.zeros(ys_ref.shape, U32)


def _experts(plan, xsp, xss, w_gu, b_gu, w_down, b_down, *, n_tiles, n_prompt_tiles, rows):
    nblk = rows // ROW_BLOCK

    grid_spec = pltpu.PrefetchScalarGridSpec(
        num_scalar_prefetch=7,
        grid=(nblk,),
        in_specs=[pl.BlockSpec(memory_space=pl.ANY),
                  pl.BlockSpec(memory_space=pl.ANY),
                  pl.BlockSpec(memory_space=pl.ANY),
                  pl.BlockSpec((N_EXPERTS, 2 * D_FF), lambda b, *_: (0, 0)),
                  pl.BlockSpec(memory_space=pl.ANY),
                  pl.BlockSpec((N_EXPERTS, D_MODEL), lambda b, *_: (0, 0))],
        out_specs=pl.BlockSpec((ROW_BLOCK, PACK_S, LANES), lambda b, *_: (b, 0, 0)),
        scratch_shapes=[pltpu.VMEM((ROW_SLOTS + 1, ROW_BLOCK, PACK_S, LANES), U32),
                        pltpu.SemaphoreType.DMA((ROW_SLOTS,)),
                        pltpu.VMEM((2, D_MODEL, 2 * D_FF), F32), pltpu.VMEM((2, D_FF, D_MODEL), F32),
                        pltpu.SemaphoreType.DMA((2, 2)),
                        pltpu.SMEM((1,), I32), pltpu.SMEM((1,), I32)],
    )
    return pl.pallas_call(
        functools.partial(_expert_kernel, n_tiles=n_tiles, n_prompt_tiles=n_prompt_tiles),
        grid_spec=grid_spec,
        out_shape=jax.ShapeDtypeStruct((rows, PACK_S, LANES), U32),
        compiler_params=pltpu.CompilerParams(
            dimension_semantics=("arbitrary",), vmem_limit_bytes=VMEM_LIMIT),
        name="experts",
    )(plan['block_e'], plan['block_q'], plan['n_used'], plan['src_t'], plan['off_t'], plan['n_t'],
      plan['tot'], xsp, xss, w_gu, b_gu, w_down, b_down)


def _combine_kernel(ysrc_ref, sloc_ref, n_ref, col_ref, h_ref, ys_hbm, o_ref, buf, sems,
                    *, tm, tile0, tps):
    j = pl.program_id(0)
    nt = pl.num_programs(0)
    rt = TOP_K * tm

    def issue(step, slot, enabled):
        for u in range(tps):
            base = (tile0 + jnp.minimum(step, nt - 1) * tps + u) * N_EXPERTS
            for e in range(N_EXPERTS):
                n = n_ref[base + e]

                @pl.when(enabled & (n > 0))
                def _():
                    pltpu.make_async_copy(ys_hbm.at[pl.ds(ysrc_ref[base + e], n)],
                                          buf.at[slot, pl.ds(u * rt + sloc_ref[base + e], n)],
                                          sems.at[slot]).start()

    @pl.when(j == 0)
    def _():
        buf[ROW_SLOTS, 0:SUBLANES // PACK_S] = jnp.zeros((SUBLANES // PACK_S, PACK_S, LANES), U32)
        for ahead in range(LOOKAHEAD):
            issue(ahead, ahead, ahead < nt)

    slot = j % ROW_SLOTS
    pltpu.make_async_copy(ys_hbm.at[pl.ds(0, tps * rt)], buf.at[slot], sems.at[slot]).wait()
    ys = [_unpack_rows(buf.at[slot, pl.ds(u * rt, rt)]) for u in range(tps)]
    issue(j + LOOKAHEAD, (j + LOOKAHEAD) % ROW_SLOTS, j + LOOKAHEAD < nt)
    rio = lax.broadcasted_iota(I32, (tm, rt), 1)
    zero = _anchor_zero_row(buf)
    for u in range(tps):
        tok = slice(u * tm, (u + 1) * tm)
        col = col_ref[tok, :]
        gmat = jnp.zeros((tm, rt), F32)
        for k in range(TOP_K):
            gmat = gmat + jnp.where(rio == col[:, k:k + 1].astype(I32),
                                    col[:, TOP_K + k:TOP_K + k + 1], 0.0)
        o_ref[tok, :] = (h_ref[tok, :] + zero) + _dot(gmat.astype(BF16), ys[u])


def _combine(plan, col, h2d, ys, *, tm, tile0, tps):
    t = h2d.shape[0]
    rt = TOP_K * tm
    tok = tm * tps
    grid_spec = pltpu.PrefetchScalarGridSpec(
        num_scalar_prefetch=3,
        grid=(t // tok,),
        in_specs=[pl.BlockSpec((tok, LANES), lambda i, *_: (i, 0)),
                  pl.BlockSpec((tok, D_MODEL), lambda i, *_: (i, 0)),
                  pl.BlockSpec(memory_space=pl.ANY)],
        out_specs=pl.BlockSpec((tok, D_MODEL), lambda i, *_: (i, 0)),
        scratch_shapes=[pltpu.VMEM((ROW_SLOTS + 1, tps * rt, PACK_S, LANES), U32),
                        pltpu.SemaphoreType.DMA((ROW_SLOTS,))],
    )
    return pl.pallas_call(
        functools.partial(_combine_kernel, tm=tm, tile0=tile0, tps=tps),
        grid_spec=grid_spec,
        out_shape=jax.ShapeDtypeStruct((t, D_MODEL), F32),
        compiler_params=pltpu.CompilerParams(
            dimension_semantics=("arbitrary",), vmem_limit_bytes=VMEM_LIMIT),
        name="combine",
    )(plan['ysrc'], plan['sloc'], plan['n'], col, h2d, ys)


def kernel(x_prompt, x_sample, cache_k, cache_v, state_conv, norm1_g, w_in, q_norm_g, k_norm_g,
           sinks, conv_w, w_out, norm2_g, w_router, b_router, w_gu, b_gu, w_down, b_down):
    l = 0
    nbp, sp, _ = x_prompt.shape
    nbs, ss, _ = x_sample.shape
    tp, ts = nbp * sp, nbs * ss
    tmp, tms = TOKEN_TILE, ts
    ntp, nts = tp // tmp, ts // tms
    n_tiles = ntp + nts
    n_rows = (tp + ts) * TOP_K
    grp = jnp.arange(ATT_DIM) // HEAD_DIM
    w = {
        'g1': norm1_g[l][None, :],
        'w_in': w_in[l].astype(BF16),
        'qg': jnp.tile(q_norm_g[l], N_HEADS)[None, :],
        'kg': jnp.tile(k_norm_g[l], N_KV_HEADS)[None, :],
        'bd': (grp[:, None] == grp[None, :]).astype(BF16),
        'sinks': sinks[l][None, :],
        'conv_w': conv_w[l],
        'w_out': w_out[l].astype(BF16),
        'g2': norm2_g[l][None, :],
        'w_router_t': w_router[l].T,
        'b_router_t': b_router[l][:, None],
    }
    hp, xsp, colp, tcp, knp, vnp, unp = _mixer(x_prompt, None, w, tm=tmp, nsub=STREAMS_PER_STEP,
                                               st=tmp)
    cache = (cache_k[l].reshape(nbs, WINDOW, KV_DIM), cache_v[l].reshape(nbs, WINDOW, KV_DIM),
             state_conv[l])
    hs, xss, cols, tcs, kns, vns, uns = _mixer(x_sample, cache, w, tm=ss, nsub=nbs, st=tms)

    n = jnp.concatenate([tcp.reshape(ntp, N_EXPERTS), tcs.reshape(nts, N_EXPERTS)], axis=0)
    tile_base = jnp.concatenate([jnp.arange(ntp, dtype=I32) * (TOP_K * tmp),
                                 jnp.arange(nts, dtype=I32) * (TOP_K * tms)])
    sloc = jnp.cumsum(n, axis=1) - n
    off = jnp.cumsum(n, axis=0) - n
    tot = jnp.sum(n, axis=0)
    padded = (tot + ROW_BLOCK - 1) // ROW_BLOCK * ROW_BLOCK
    ends = jnp.cumsum(padded)
    base = ends - padded
    rows = (-(-n_rows // ROW_BLOCK) + N_EXPERTS) * ROW_BLOCK
    nblk = rows // ROW_BLOCK
    blk0 = jnp.arange(nblk, dtype=I32) * ROW_BLOCK
    block_e = jnp.minimum(jnp.sum((ends[None, :] <= blk0[:, None]).astype(I32), axis=1),
                          N_EXPERTS - 1)
    eq = (block_e[:, None] == jnp.arange(N_EXPERTS, dtype=I32)[None, :]).astype(I32)
    block_q = (blk0 - jnp.sum(eq * base[None, :], axis=1)) // ROW_BLOCK
    plan = {
        'block_e': block_e.astype(I32),
        'block_q': block_q.astype(I32),
        'n_used': (ends[-1:] // ROW_BLOCK).astype(I32),
        'src_t': (tile_base[:, None] + sloc).T.reshape(-1).astype(I32),
        'off_t': off.T.reshape(-1).astype(I32),
        'n_t': n.T.reshape(-1).astype(I32),
        'tot': tot.astype(I32),
        'ysrc': (base[None, :] + off).reshape(-1).astype(I32),
        'sloc': sloc.reshape(-1).astype(I32),
        'n': n.reshape(-1).astype(I32),
    }

    ys = _experts(plan, xsp.reshape(-1, PACK_S, LANES), xss.reshape(-1, PACK_S, LANES),
                  w_gu[l], b_gu[l], w_down[l], b_down[l],
                  n_tiles=n_tiles, n_prompt_tiles=ntp, rows=rows)
    yp = _combine(plan, colp.reshape(tp, LANES), hp.reshape(tp, D_MODEL), ys, tm=tmp, tile0=0,
                  tps=COMBINE_TILES)
    ysm = _combine(plan, cols.reshape(ts, LANES), hs.reshape(ts, D_MODEL), ys, tm=tms, tile0=ntp,
                   tps=1)

    kv5 = lambda a: a.reshape(1, a.shape[0], WINDOW, N_KV_HEADS, HEAD_DIM)
    return (yp.reshape(nbp, sp, D_MODEL), ysm.reshape(nbs, ss, D_MODEL),
            kv5(knp), kv5(vnp), unp[None], kv5(kns), kv5(vns), uns[None])
```

```python
import functools

import jax
import jax.numpy as jnp
from jax import lax
from jax.experimental import pallas as pl
from jax.experimental.pallas import tpu as pltpu

D_MODEL = 1024
CHUNK = 64
HEAD_DIM = 64
N_HEADS = 8
N_KV_HEADS = 2
Q_PER_KV = N_HEADS // N_KV_HEADS
ATT_DIM = N_HEADS * HEAD_DIM
KV_DIM = N_KV_HEADS * HEAD_DIM
GROUP_W = Q_PER_KV * HEAD_DIM
CONV_DIM = D_MODEL - ATT_DIM
CONV_W = 3
WINDOW = 128
IN_DIM = ATT_DIM + 2 * KV_DIM + 3 * CONV_DIM
N_EXPERTS = 32
TOP_K = 4
D_FF = D_MODEL
SWIGLU_LIMIT = 7.0
SWIGLU_ALPHA = 1.702
EPS = 1e-6
NEG = -1e30

LANES = 128
SUBLANES = 8
HALF = D_MODEL // 2
PACK_S = HALF // LANES
TOKEN_TILE = 256
STREAMS_PER_STEP = 4
COMBINE_TILES = 2
ROW_BLOCK = 1024
TAIL_ROWS = 256
INLINE_SEGMENTS = 40
LOOKAHEAD = 2
ROW_SLOTS = LOOKAHEAD + 1
VMEM_LIMIT = 56 * 1024 * 1024
MIXER_VMEM_LIMIT = 60 * 1024 * 1024

F32 = jnp.float32
BF16 = jnp.bfloat16
U32 = jnp.uint32
I32 = jnp.int32


def _rms_rows(x, g):
    return x * lax.rsqrt(jnp.mean(x * x, axis=-1, keepdims=True) + EPS) * g


def _split_bf16(x):
    hi = x.astype(BF16)
    lo = (x - hi.astype(F32)).astype(BF16)
    return hi, lo


def _dot(a, b):
    return jnp.dot(a, b, preferred_element_type=F32)


def _dot_nt(a, b):
    return lax.dot_general(a, b, (((1,), (1,)), ((), ())), preferred_element_type=F32)


def _head_rms(t, bd, g):
    ssq = _dot((t * t).astype(BF16), bd)
    return t * lax.rsqrt(ssq * (1.0 / HEAD_DIM) + EPS) * g


def _pack_rows(vals, out_ref):
    r = vals.shape[0]
    bits = pltpu.bitcast(vals, U32)
    word = (bits[:, :HALF] >> 16) | (bits[:, HALF:] & jnp.uint32(0xFFFF0000))
    flat = out_ref.reshape(r * PACK_S, LANES)
    for s in range(PACK_S):
        flat[pl.ds(s, r, stride=PACK_S), :] = word[:, s * LANES:(s + 1) * LANES]


def _unpack_rows(packed_ref):
    r = packed_ref.shape[0]
    flat = packed_ref.reshape(r * PACK_S, LANES)
    lo, hi = [], []
    for s in range(PACK_S):
        w = flat[pl.ds(s, r, stride=PACK_S), :]
        lo.append(pltpu.bitcast(w << 16, F32))
        hi.append(pltpu.bitcast(w & jnp.uint32(0xFFFF0000), F32))
    return jnp.concatenate(lo + hi, axis=1).astype(BF16)


def _anchor_zero_row(staging_ref):
    z = staging_ref.at[ROW_SLOTS, pl.ds(0, SUBLANES // PACK_S)].reshape(SUBLANES, LANES)[...]
    return jnp.concatenate([pltpu.bitcast(z, F32)[0:1, :]] * (D_MODEL // LANES), axis=1)


def _spread_kv(t):
    assert N_KV_HEADS == 2 and KV_DIM == LANES
    swapped = pltpu.roll(t, HEAD_DIM, axis=1)
    low = lax.broadcasted_iota(I32, t.shape, 1) < HEAD_DIM
    pair0 = jnp.where(low, t, swapped)
    pair1 = jnp.where(low, swapped, t)
    reps = GROUP_W // LANES
    return jnp.concatenate([pair0] * reps + [pair1] * reps, axis=1).astype(BF16)


def _attention(i, qn, kall, vall, sink_ref, *, mask_history):
    tm = qn.shape[0]
    k4 = _spread_kv(kall)
    v4 = _spread_kv(vall)
    lane_grp = lax.broadcasted_iota(I32, (CHUNK, GROUP_W), 1) // HEAD_DIM
    row_grp = lax.broadcasted_iota(I32, (Q_PER_KV * CHUNK, 1), 0) // CHUNK
    span = WINDOW + CHUNK
    att_rows = []
    for c in range(tm // CHUNK):
        per_kv = []
        for hk in range(N_KV_HEADS):
            qc = qn[c * CHUNK:(c + 1) * CHUNK, hk * GROUP_W:(hk + 1) * GROUP_W]
            qs = jnp.concatenate([jnp.where(lane_grp == g, qc, 0.0) for g in range(Q_PER_KV)],
                                 axis=0).astype(BF16)
            kw = k4[c * CHUNK:c * CHUNK + span, hk * GROUP_W:(hk + 1) * GROUP_W]
            vw = v4[c * CHUNK:c * CHUNK + span, hk * GROUP_W:(hk + 1) * GROUP_W]
            s = _dot_nt(qs, kw) * (HEAD_DIM ** -0.5)
            if mask_history and c * CHUNK < WINDOW:
                kcol = lax.broadcasted_iota(I32, (1, span), 1)
                s = jnp.where(kcol >= jnp.where(i == 0, WINDOW - c * CHUNK, 0), s, NEG)
            sink = jnp.zeros((Q_PER_KV * CHUNK, 1), F32)
            for g in range(Q_PER_KV):
                sink = jnp.where(row_grp == g, sink_ref[0, hk * Q_PER_KV + g], sink)
            m = jnp.maximum(jnp.max(s, axis=-1, keepdims=True), sink)
            p = jnp.exp(s - m)
            den = jnp.sum(p, axis=-1, keepdims=True) + jnp.exp(sink - m)
            o = _dot(p.astype(BF16), vw) / den
            oc = jnp.zeros((CHUNK, GROUP_W), F32)
            for g in range(Q_PER_KV):
                oc = oc + jnp.where(lane_grp == g, o[g * CHUNK:(g + 1) * CHUNK, :], 0.0)
            per_kv.append(oc)
        att_rows.append(jnp.concatenate(per_kv, axis=1))
    return jnp.concatenate(att_rows, axis=0)


def _mixer_kernel(*refs, tm, nsub, st, has_cache):
    refs = list(refs)
    x_ref = refs.pop(0)
    if has_cache:
        ck_ref, cv_ref, st_ref = refs.pop(0), refs.pop(0), refs.pop(0)
    (g1_ref, win_ref, qg_ref, kg_ref, bd_ref, sink_ref, cw_ref, wout_ref, g2_ref,
     wrt_ref, brt_ref, upper_ref, lower_ref,
     h_ref, xs_ref, col_ref, tcnt_ref, knew_ref, vnew_ref, unew_ref, kprev, vprev, ubuf) = refs
    i = pl.program_id(1)
    ntok = nsub * tm

    @pl.when(i == 0)
    def _():
        ubuf[:, 0:SUBLANES, :] = jnp.zeros((nsub, SUBLANES, CONV_DIM), F32)
        if has_cache:
            kprev[...] = ck_ref[...]
            vprev[...] = cv_ref[...]
            ubuf[:, SUBLANES - (CONV_W - 1):SUBLANES, :] = st_ref[...]
        else:
            kprev[...] = jnp.zeros(kprev.shape, F32)
            vprev[...] = jnp.zeros(vprev.shape, F32)

    x = jnp.concatenate([x_ref[sub] for sub in range(nsub)], axis=0) if nsub > 1 else x_ref[0]
    xn = _rms_rows(x, g1_ref[...])
    proj = _dot(xn.astype(BF16), win_ref[...])
    o0 = ATT_DIM
    o1 = o0 + KV_DIM
    o2 = o1 + KV_DIM
    o3 = o2 + CONV_DIM
    o4 = o3 + CONV_DIM
    bd = bd_ref[...]
    qn = _head_rms(proj[:, :o0], bd, qg_ref[...])
    kn = _head_rms(proj[:, o0:o1], bd[:KV_DIM, :KV_DIM], kg_ref[...])
    v = proj[:, o1:o2]
    bg = proj[:, o2:o3]
    u = proj[:, o3:o4] * proj[:, o4:]

    cw = cw_ref[...]
    att_parts, cv_parts = [], []
    for sub in range(nsub):
        r0, r1 = sub * tm, (sub + 1) * tm
        kall = jnp.concatenate([kprev[sub], kn[r0:r1]], axis=0)
        vall = jnp.concatenate([vprev[sub], v[r0:r1]], axis=0)
        att_parts.append(_attention(i, qn[r0:r1], kall, vall, sink_ref,
                                    mask_history=not has_cache))
        kprev[sub] = kall[tm:, :]
        vprev[sub] = vall[tm:, :]
        knew_ref[sub] = kall[tm:, :]
        vnew_ref[sub] = vall[tm:, :]

        us = u[r0:r1]
        ubuf[sub, SUBLANES:SUBLANES + tm, :] = us
        first = SUBLANES - (CONV_W - 1)
        cvs = ubuf[sub, first:first + tm, :] * cw[0:1, :]
        for tap in range(1, CONV_W - 1):
            cvs = cvs + ubuf[sub, first + tap:first + tap + tm, :] * cw[tap:tap + 1, :]
        cv_parts.append(cvs + us * cw[CONV_W - 1:CONV_W, :])
        ubuf[sub, 0:SUBLANES, :] = ubuf[sub, tm:tm + SUBLANES, :]
        unew_ref[sub] = us[tm - (CONV_W - 1):, :]
    att = jnp.concatenate(att_parts, axis=0) if nsub > 1 else att_parts[0]
    cv = jnp.concatenate(cv_parts, axis=0) if nsub > 1 else cv_parts[0]

    mix = jnp.concatenate([att, bg * cv], axis=1).astype(BF16)
    h = x + _dot(mix, wout_ref[...])
    for sub in range(nsub):
        h_ref[sub] = h[sub * tm:(sub + 1) * tm]
    hn = _rms_rows(h, g2_ref[...])

    hn_hi, hn_lo = _split_bf16(hn)
    wr_hi, wr_lo = _split_bf16(wrt_ref[...])
    logits = (_dot_nt(wr_hi, hn_hi) + _dot_nt(wr_hi, hn_lo) + _dot_nt(wr_lo, hn_hi)) + brt_ref[...]
    eio = lax.broadcasted_iota(I32, (N_EXPERTS, ntok), 0)
    work = logits
    tops, sels = [], []
    for _ in range(TOP_K):
        mk = jnp.max(work, axis=0, keepdims=True)
        ik = jnp.min(jnp.where(work == mk, eio, N_EXPERTS), axis=0, keepdims=True)
        sel = eio == ik
        work = jnp.where(sel, -jnp.inf, work)
        tops.append(mk)
        sels.append(sel)
    es = [jnp.exp(t - tops[0]) for t in tops]
    esum = es[0] + es[1] + es[2] + es[3]

    onehot = jnp.zeros((N_EXPERTS, ntok), F32)
    for sel in sels:
        onehot = onehot + jnp.where(sel, 1.0, 0.0)
    onehot_b = onehot.astype(BF16)
    smaller = _dot(lower_ref[...], onehot_b)
    nst = ntok // st
    where_parts = []
    for k in range(nst):
        c0, c1 = k * st, (k + 1) * st
        before = _dot(onehot_b[:, c0:c1], upper_ref[...])
        start = jnp.sum(smaller[:, c0:c1], axis=1, keepdims=True)
        where_parts.append(before + start)
    where_to = jnp.concatenate(where_parts, axis=1) if nst > 1 else where_parts[0]
    pos = [jnp.sum(jnp.where(sel, where_to, 0.0), axis=0, keepdims=True) for sel in sels]
    rio = lax.broadcasted_iota(I32, (TOP_K * st, st), 0)
    for k in range(nst):
        c0, c1 = k * st, (k + 1) * st
        hit = rio == pos[0][:, c0:c1].astype(I32)
        for slot in range(1, TOP_K):
            hit = hit | (rio == pos[slot][:, c0:c1].astype(I32))
        perm = jnp.where(hit, 1.0, 0.0).astype(BF16)
        _pack_rows(_dot(perm, hn_hi[c0:c1]), xs_ref.at[k, 0])
        tcnt_ref[k, 0] = _dot_nt(jnp.ones((1, st), BF16), onehot_b[:, c0:c1]).astype(I32)

    rows8 = jnp.concatenate(pos + [e / esum for e in es], axis=0)
    sq = jnp.concatenate([rows8, jnp.zeros((LANES - 2 * TOP_K, ntok), F32)], axis=0)
    cols = jnp.transpose(sq)
    for sub in range(nsub):
        col_ref[sub] = cols[sub * tm:(sub + 1) * tm]


def _mixer(x, cache, w, *, tm, nsub, st):
    nb, s, _ = x.shape
    nt = s // tm
    nst = nsub * tm // st
    has_cache = cache is not None
    full = lambda shape: pl.BlockSpec(shape, lambda b, i: (0,) * len(shape))
    per_b = lambda shape: pl.BlockSpec((nsub,) + shape, lambda b, i: (b, 0, 0))
    tile = lambda last: pl.BlockSpec((nsub, tm, last), lambda b, i: (b, i, 0))
    kv_shape = (WINDOW, KV_DIM)
    kv_spec = per_b(kv_shape)

    in_specs = [tile(D_MODEL)]
    args = [x]
    if has_cache:
        in_specs += [kv_spec, kv_spec, per_b((CONV_W - 1, CONV_DIM))]
        args += list(cache)
    tio = jnp.arange(st)
    eio = jnp.arange(N_EXPERTS)
    upper = (tio[:, None] < tio[None, :]).astype(BF16)
    lower = (eio[None, :] < eio[:, None]).astype(BF16)
    in_specs += [
        full((1, D_MODEL)), full((D_MODEL, IN_DIM)), full((1, ATT_DIM)),
        full((1, KV_DIM)), full((ATT_DIM, ATT_DIM)),
        pl.BlockSpec(memory_space=pltpu.SMEM),
        full((CONV_W, CONV_DIM)), full((D_MODEL, D_MODEL)), full((1, D_MODEL)),
        full((N_EXPERTS, D_MODEL)), full((N_EXPERTS, 1)), full((st, st)),
        full((N_EXPERTS, N_EXPERTS)),
    ]
    args += [w['g1'], w['w_in'], w['qg'], w['kg'], w['bd'], w['sinks'], w['conv_w'],
             w['w_out'], w['g2'], w['w_router_t'], w['b_router_t'], upper, lower]
    rt = TOP_K * st
    nsort = nb // nsub * nst
    out_shape = [
        jax.ShapeDtypeStruct((nb, s, D_MODEL), F32),
        jax.ShapeDtypeStruct((nsort, nt, rt, PACK_S, LANES), U32),
        jax.ShapeDtypeStruct((nb, s, LANES), F32),
        jax.ShapeDtypeStruct((nsort, nt, 1, N_EXPERTS), I32),
        jax.ShapeDtypeStruct((nb,) + kv_shape, F32),
        jax.ShapeDtypeStruct((nb,) + kv_shape, F32),
        jax.ShapeDtypeStruct((nb, CONV_W - 1, CONV_DIM), F32),
    ]
    out_specs = [
        tile(D_MODEL),
        pl.BlockSpec((nst, 1, rt, PACK_S, LANES), lambda b, i: (b, i, 0, 0, 0)),
        tile(LANES),
        pl.BlockSpec((nst, 1, 1, N_EXPERTS), lambda b, i: (b, i, 0, 0)),
        kv_spec, kv_spec, per_b((CONV_W - 1, CONV_DIM)),
    ]
    scratch = [pltpu.VMEM((nsub, WINDOW, KV_DIM), F32), pltpu.VMEM((nsub, WINDOW, KV_DIM), F32),
               pltpu.VMEM((nsub, tm + SUBLANES, CONV_DIM), F32)]
    return pl.pallas_call(
        functools.partial(_mixer_kernel, tm=tm, nsub=nsub, st=st, has_cache=has_cache),
        grid=(nb // nsub, nt),
        in_specs=in_specs,
        out_specs=out_specs,
        out_shape=out_shape,
        scratch_shapes=scratch,
        compiler_params=pltpu.CompilerParams(
            dimension_semantics=("arbitrary", "arbitrary"), vmem_limit_bytes=MIXER_VMEM_LIMIT),
        name="mixer_sample" if has_cache else "mixer_prompt",
    )(*args)


def _expert_kernel(be_ref, bq_ref, nused_ref, src_ref, off_ref, n_ref, tot_ref,
                   xsp_hbm, xss_hbm, wgu_hbm, bgu_ref, wd_hbm, bdn_ref, ys_ref,
                   xbuf, sems, wgu_f, wd_f, wsems, ptr, wcur, *, n_tiles, n_prompt_tiles):
    b = pl.program_id(0)
    nused = nused_ref[0]

    def weight_copies(e, ws):
        return (pltpu.make_async_copy(wgu_hbm.at[e], wgu_f.at[ws], wsems.at[0, ws]),
                pltpu.make_async_copy(wd_hbm.at[e], wd_f.at[ws], wsems.at[1, ws]))

    def issue_rows(blk, slot, enabled, between=lambda: None):
        blk = jnp.minimum(blk, pl.num_programs(0) - 1)
        e = be_ref[blk]
        q = bq_ref[blk]
        lo_row = q * ROW_BLOCK
        hi_row = lo_row + ROW_BLOCK
        base = e * n_tiles

        def segment(j, j_limit, src_hbm, ok):
            jc = jnp.minimum(j, j_limit - 1)
            o = off_ref[base + jc]
            inside = ok & (j < j_limit) & (o < hi_row)
            lo = jnp.maximum(o, lo_row)
            ln = jnp.minimum(o + n_ref[base + jc], hi_row) - lo

            @pl.when(inside & (ln > 0))
            def _():
                pltpu.make_async_copy(src_hbm.at[pl.ds(src_ref[base + jc] + (lo - o), ln)],
                                      xbuf.at[slot, pl.ds(lo - lo_row, ln)], sems.at[slot]).start()
            return inside

        j0 = jnp.where(q == 0, 0, ptr[0])

        def cond(j):
            return (enabled & (j < n_prompt_tiles)
                    & (off_ref[base + jnp.minimum(j, n_prompt_tiles - 1)] < hi_row))

        def body(j):
            segment(j, n_prompt_tiles, xsp_hbm, True)
            return j + 1

        jtail = lax.while_loop(cond, body, j0 + INLINE_SEGMENTS)
        result = between()
        count = jnp.int32(0)
        for k in range(INLINE_SEGMENTS):
            count = count + segment(j0 + k, n_prompt_tiles, xsp_hbm, enabled).astype(I32)
        jend = jnp.where(count == INLINE_SEGMENTS, jtail, j0 + count)
        ptr[0] = jnp.where(enabled, jnp.maximum(jend - 1, 0), ptr[0])
        for j in range(n_prompt_tiles, n_tiles):
            segment(j, n_tiles, xss_hbm, enabled)
        return result

    @pl.when(b == 0)
    def _():
        xbuf[...] = jnp.zeros(xbuf.shape, U32)
        for ahead in range(LOOKAHEAD):
            issue_rows(ahead, ahead, ahead < nused)
        for c in weight_copies(be_ref[0], 0):
            c.start()
        wcur[0] = 1

    active = b < nused

    @pl.when(active)
    def _():
        slot = b % ROW_SLOTS
        e = be_ref[b]
        rows = jnp.minimum(tot_ref[e] - bq_ref[b] * ROW_BLOCK, ROW_BLOCK)
        pltpu.make_async_copy(xsp_hbm.at[pl.ds(0, rows)], xbuf.at[slot, pl.ds(0, rows)],
                              sems.at[slot]).wait()

        @pl.when(bq_ref[b] == 0)
        def _():
            ws = 1 - wcur[0]
            wcur[0] = ws
            for c in weight_copies(e, ws):
                c.wait()
            nxt = b + (tot_ref[e] + ROW_BLOCK - 1) // ROW_BLOCK

            @pl.when(nxt < nused)
            def _():
                for c in weight_copies(be_ref[jnp.minimum(nxt, pl.num_programs(0) - 1)], 1 - ws):
                    c.start()

        ws = wcur[0]

        def compute(nrows):
            x = issue_rows(b + LOOKAHEAD, (b + LOOKAHEAD) % ROW_SLOTS, b + LOOKAHEAD < nused,
                           between=lambda: _unpack_rows(xbuf.at[slot, pl.ds(0, nrows)]))
            gu = _dot(x, wgu_f[ws]) + bgu_ref[pl.ds(e, 1), :]
            g = jnp.minimum(gu[:, :D_FF], SWIGLU_LIMIT)
            u = jnp.clip(gu[:, D_FF:], -SWIGLU_LIMIT, SWIGLU_LIMIT)
            act = (u + 1.0) * (g * jax.nn.sigmoid(SWIGLU_ALPHA * g))
            y = (_dot(act.astype(BF16), wd_f[ws])
                 + (bdn_ref[pl.ds(e, 1), :] + _anchor_zero_row(xbuf)))
            _pack_rows(y.astype(BF16).astype(F32), ys_ref.at[pl.ds(0, nrows)])
            if nrows < ROW_BLOCK:
                ys_ref[nrows:] = jnp.zeros((ROW_BLOCK - nrows, PACK_S, LANES), U32)

        for piece in range(ROW_BLOCK // TAIL_ROWS):
            top = (piece + 1) * TAIL_ROWS
            pl.when((rows > top - TAIL_ROWS) & (rows <= top))(functools.partial(compute, top))

    @pl.when(jnp.logical_not(active))
    def _():
        ys_ref[...] = jnp.zeros(ys_ref.shape, U32)


def _experts(plan, xsp, xss, w_gu, b_gu, w_down, b_down, *, n_tiles, n_prompt_tiles, rows):
    nblk = rows // ROW_BLOCK

    grid_spec = pltpu.PrefetchScalarGridSpec(
        num_scalar_prefetch=7,
        grid=(nblk,),
        in_specs=[pl.BlockSpec(memory_space=pl.ANY),
                  pl.BlockSpec(memory_space=pl.ANY),
                  pl.BlockSpec(memory_space=pl.ANY),
                  pl.BlockSpec((N_EXPERTS, 2 * D_FF), lambda b, *_: (0, 0)),
                  pl.BlockSpec(memory_space=pl.ANY),
                  pl.BlockSpec((N_EXPERTS, D_MODEL), lambda b, *_: (0, 0))],
        out_specs=pl.BlockSpec((ROW_BLOCK, PACK_S, LANES), lambda b, *_: (b, 0, 0)),
        scratch_shapes=[pltpu.VMEM((ROW_SLOTS + 1, ROW_BLOCK, PACK_S, LANES), U32),
                        pltpu.SemaphoreType.DMA((ROW_SLOTS,)),
                        pltpu.VMEM((2, D_MODEL, 2 * D_FF), F32), pltpu.VMEM((2, D_FF, D_MODEL), F32),
                        pltpu.SemaphoreType.DMA((2, 2)),
                        pltpu.SMEM((1,), I32), pltpu.SMEM((1,), I32)],
    )
    return pl.pallas_call(
        functools.partial(_expert_kernel, n_tiles=n_tiles, n_prompt_tiles=n_prompt_tiles),
        grid_spec=grid_spec,
        out_shape=jax.ShapeDtypeStruct((rows, PACK_S, LANES), U32),
        compiler_params=pltpu.CompilerParams(
            dimension_semantics=("arbitrary",), vmem_limit_bytes=VMEM_LIMIT),
        name="experts",
    )(plan['block_e'], plan['block_q'], plan['n_used'], plan['src_t'], plan['off_t'], plan['n_t'],
      plan['tot'], xsp, xss, w_gu, b_gu, w_down, b_down)


def _combine_kernel(ysrc_ref, sloc_ref, n_ref, col_ref, h_ref, ys_hbm, o_ref, buf, sems,
                    *, tm, tile0, tps):
    j = pl.program_id(0)
    nt = pl.num_programs(0)
    rt = TOP_K * tm

    def issue(step, slot, enabled):
        for u in range(tps):
            base = (tile0 + jnp.minimum(step, nt - 1) * tps + u) * N_EXPERTS
            for e in range(N_EXPERTS):
                n = n_ref[base + e]

                @pl.when(enabled & (n > 0))
                def _():
                    pltpu.make_async_copy(ys_hbm.at[pl.ds(ysrc_ref[base + e], n)],
                                          buf.at[slot, pl.ds(u * rt + sloc_ref[base + e], n)],
                                          sems.at[slot]).start()

    @pl.when(j == 0)
    def _():
        buf[ROW_SLOTS, 0:SUBLANES // PACK_S] = jnp.zeros((SUBLANES // PACK_S, PACK_S, LANES), U32)
        for ahead in range(LOOKAHEAD):
            issue(ahead, ahead, ahead < nt)

    slot = j % ROW_SLOTS
    pltpu.make_async_copy(ys_hbm.at[pl.ds(0, tps * rt)], buf.at[slot], sems.at[slot]).wait()
    ys = [_unpack_rows(buf.at[slot, pl.ds(u * rt, rt)]) for u in range(tps)]
    issue(j + LOOKAHEAD, (j + LOOKAHEAD) % ROW_SLOTS, j + LOOKAHEAD < nt)
    rio = lax.broadcasted_iota(I32, (tm, rt), 1)
    zero = _anchor_zero_row(buf)
    for u in range(tps):
        tok = slice(u * tm, (u + 1) * tm)
        col = col_ref[tok, :]
        gmat = jnp.zeros((tm, rt), F32)
        for k in range(TOP_K):
            gmat = gmat + jnp.where(rio == col[:, k:k + 1].astype(I32),
                                    col[:, TOP_K + k:TOP_K + k + 1], 0.0)
        o_ref[tok, :] = (h_ref[tok, :] + zero) + _dot(gmat.astype(BF16), ys[u])


def _combine(plan, col, h2d, ys, *, tm, tile0, tps):
    t = h2d.shape[0]
    rt = TOP_K * tm
    tok = tm * tps
    grid_spec = pltpu.PrefetchScalarGridSpec(
        num_scalar_prefetch=3,
        grid=(t // tok,),
        in_specs=[pl.BlockSpec((tok, LANES), lambda i, *_: (i, 0)),
                  pl.BlockSpec((tok, D_MODEL), lambda i, *_: (i, 0)),
                  pl.BlockSpec(memory_space=pl.ANY)],
        out_specs=pl.BlockSpec((tok, D_MODEL), lambda i, *_: (i, 0)),
        scratch_shapes=[pltpu.VMEM((ROW_SLOTS + 1, tps * rt, PACK_S, LANES), U32),
                        pltpu.SemaphoreType.DMA((ROW_SLOTS,))],
    )
    return pl.pallas_call(
        functools.partial(_combine_kernel, tm=tm, tile0=tile0, tps=tps),
        grid_spec=grid_spec,
        out_shape=jax.ShapeDtypeStruct((t, D_MODEL), F32),
        compiler_params=pltpu.CompilerParams(
            dimension_semantics=("arbitrary",), vmem_limit_bytes=VMEM_LIMIT),
        name="combine",
    )(plan['ysrc'], plan['sloc'], plan['n'], col, h2d, ys)


def kernel(x_prompt, x_sample, cache_k, cache_v, state_conv, norm1_g, w_in, q_norm_g, k_norm_g,
           sinks, conv_w, w_out, norm2_g, w_router, b_router, w_gu, b_gu, w_down, b_down):
    l = 0
    nbp, sp, _ = x_prompt.shape
    nbs, ss, _ = x_sample.shape
    tp, ts = nbp * sp, nbs * ss
    tmp, tms = TOKEN_TILE, ts
    ntp, nts = tp // tmp, ts // tms
    n_tiles = ntp + nts
    n_rows = (tp + ts) * TOP_K
    grp = jnp.arange(ATT_DIM) // HEAD_DIM
    w = {
        'g1': norm1_g[l][None, :],
        'w_in': w_in[l].astype(BF16),
        'qg': jnp.tile(q_norm_g[l], N_HEADS)[None, :],
        'kg': jnp.tile(k_norm_g[l], N_KV_HEADS)[None, :],
        'bd': (grp[:, None] == grp[None, :]).astype(BF16),
        'sinks': sinks[l][None, :],
        'conv_w': conv_w[l],
        'w_out': w_out[l].astype(BF16),
        'g2': norm2_g[l][None, :],
        'w_router_t': w_router[l].T,
        'b_router_t': b_router[l][:, None],
    }
    hp, xsp, colp, tcp, knp, vnp, unp = _mixer(x_prompt, None, w, tm=tmp, nsub=STREAMS_PER_STEP,
                                               st=tmp)
    cache = (cache_k[l].reshape(nbs, WINDOW, KV_DIM), cache_v[l].reshape(nbs, WINDOW, KV_DIM),
             state_conv[l])
    hs, xss, cols, tcs, kns, vns, uns = _mixer(x_sample, cache, w, tm=ss, nsub=nbs, st=tms)

    n = jnp.concatenate([tcp.reshape(ntp, N_EXPERTS), tcs.reshape(nts, N_EXPERTS)], axis=0)
    tile_base = jnp.concatenate([jnp.arange(ntp, dtype=I32) * (TOP_K * tmp),
                                 jnp.arange(nts, dtype=I32) * (TOP_K * tms)])
    sloc = jnp.cumsum(n, axis=1) - n
    off = jnp.cumsum(n, axis=0) - n
    tot = jnp.sum(n, axis=0)
    padded = (tot + ROW_BLOCK - 1) // ROW_BLOCK * ROW_BLOCK
    ends = jnp.cumsum(padded)
    base = ends - padded
    rows = (-(-n_rows // ROW_BLOCK) + N_EXPERTS) * ROW_BLOCK
    nblk = rows // ROW_BLOCK
    blk0 = jnp.arange(nblk, dtype=I32) * ROW_BLOCK
    block_e = jnp.minimum(jnp.sum((ends[None, :] <= blk0[:, None]).astype(I32), axis=1),
                          N_EXPERTS - 1)
    eq = (block_e[:, None] == jnp.arange(N_EXPERTS, dtype=I32)[None, :]).astype(I32)
    block_q = (blk0 - jnp.sum(eq * base[None, :], axis=1)) // ROW_BLOCK
    plan = {
        'block_e': block_e.astype(I32),
        'block_q': block_q.astype(I32),
        'n_used': (ends[-1:] // ROW_BLOCK).astype(I32),
        'src_t': (tile_base[:, None] + sloc).T.reshape(-1).astype(I32),
        'off_t': off.T.reshape(-1).astype(I32),
        'n_t': n.T.reshape(-1).astype(I32),
        'tot': tot.astype(I32),
        'ysrc': (base[None, :] + off).reshape(-1).astype(I32),
        'sloc': sloc.reshape(-1).astype(I32),
        'n': n.reshape(-1).astype(I32),
    }

    ys = _experts(plan, xsp.reshape(-1, PACK_S, LANES), xss.reshape(-1, PACK_S, LANES),
                  w_gu[l], b_gu[l], w_down[l], b_down[l],
                  n_tiles=n_tiles, n_prompt_tiles=ntp, rows=rows)
    yp = _combine(plan, colp.reshape(tp, LANES), hp.reshape(tp, D_MODEL), ys, tm=tmp, tile0=0,
                  tps=COMBINE_TILES)
    ysm = _combine(plan, cols.reshape(ts, LANES), hs.reshape(ts, D_MODEL), ys, tm=tms, tile0=ntp,
                   tps=1)

    kv5 = lambda a: a.reshape(1, a.shape[0], WINDOW, N_KV_HEADS, HEAD_DIM)
    return (yp.reshape(nbp, sp, D_MODEL), ysm.reshape(nbs, ss, D_MODEL),
            kv5(knp), kv5(vnp), unp[None], kv5(kns), kv5(vns), uns[None])
```

```python
import functools

import jax
import jax.numpy as jnp
from jax import lax
from jax.experimental import pallas as pl
from jax.experimental.pallas import tpu as pltpu

D_MODEL = 1024
CHUNK = 64
HEAD_DIM = 64
N_HEADS = 8
N_KV_HEADS = 2
Q_PER_KV = N_HEADS // N_KV_HEADS
ATT_DIM = N_HEADS * HEAD_DIM
KV_DIM = N_KV_HEADS * HEAD_DIM
GROUP_W = Q_PER_KV * HEAD_DIM
CONV_DIM = D_MODEL - ATT_DIM
CONV_W = 3
WINDOW = 128
IN_DIM = ATT_DIM + 2 * KV_DIM + 3 * CONV_DIM
N_EXPERTS = 32
TOP_K = 4
D_FF = D_MODEL
SWIGLU_LIMIT = 7.0
SWIGLU_ALPHA = 1.702
EPS = 1e-6
NEG = -1e30

LANES = 128
SUBLANES = 8
HALF = D_MODEL // 2
PACK_S = HALF // LANES
TOKEN_TILE = 256
STREAMS_PER_STEP = 4
COMBINE_TILES = 2
ROW_BLOCK = 1024
TAIL_ROWS = 128
INLINE_SEGMENTS = 40
LOOKAHEAD = 2
ROW_SLOTS = LOOKAHEAD + 1
VMEM_LIMIT = 56 * 1024 * 1024
MIXER_VMEM_LIMIT = 60 * 1024 * 1024

F32 = jnp.float32
BF16 = jnp.bfloat16
U32 = jnp.uint32
I32 = jnp.int32


def _rms_rows(x, g):
    return x * lax.rsqrt(jnp.mean(x * x, axis=-1, keepdims=True) + EPS) * g


def _split_bf16(x):
    hi = x.astype(BF16)
    lo = (x - hi.astype(F32)).astype(BF16)
    return hi, lo


def _dot(a, b):
    return jnp.dot(a, b, preferred_element_type=F32)


def _dot_nt(a, b):
    return lax.dot_general(a, b, (((1,), (1,)), ((), ())), preferred_element_type=F32)


def _head_rms(t, bd, g):
    ssq = _dot((t * t).astype(BF16), bd)
    return t * lax.rsqrt(ssq * (1.0 / HEAD_DIM) + EPS) * g


def _pack_rows(vals, out_ref):
    r = vals.shape[0]
    bits = pltpu.bitcast(vals, U32)
    word = (bits[:, :HALF] >> 16) | (bits[:, HALF:] & jnp.uint32(0xFFFF0000))
    flat = out_ref.reshape(r * PACK_S, LANES)
    for s in range(PACK_S):
        flat[pl.ds(s, r, stride=PACK_S), :] = word[:, s * LANES:(s + 1) * LANES]


def _unpack_rows(packed_ref):
    r = packed_ref.shape[0]
    flat = packed_ref.reshape(r * PACK_S, LANES)
    lo, hi = [], []
    for s in range(PACK_S):
        w = flat[pl.ds(s, r, stride=PACK_S), :]
        lo.append(pltpu.bitcast(w << 16, F32))
        hi.append(pltpu.bitcast(w & jnp.uint32(0xFFFF0000), F32))
    return jnp.concatenate(lo + hi, axis=1).astype(BF16)


def _anchor_zero_row(staging_ref):
    z = staging_ref.at[ROW_SLOTS, pl.ds(0, SUBLANES // PACK_S)].reshape(SUBLANES, LANES)[...]
    return jnp.concatenate([pltpu.bitcast(z, F32)[0:1, :]] * (D_MODEL // LANES), axis=1)


def _spread_kv(t):
    assert N_KV_HEADS == 2 and KV_DIM == LANES
    swapped = pltpu.roll(t, HEAD_DIM, axis=1)
    low = lax.broadcasted_iota(I32, t.shape, 1) < HEAD_DIM
    pair0 = jnp.where(low, t, swapped)
    pair1 = jnp.where(low, swapped, t)
    reps = GROUP_W // LANES
    return jnp.concatenate([pair0] * reps + [pair1] * reps, axis=1).astype(BF16)


def _attention(i, qn, kall, vall, sink_ref, *, mask_history):
    tm = qn.shape[0]
    k4 = _spread_kv(kall)
    v4 = _spread_kv(vall)
    lane_grp = lax.broadcasted_iota(I32, (CHUNK, GROUP_W), 1) // HEAD_DIM
    row_grp = lax.broadcasted_iota(I32, (Q_PER_KV * CHUNK, 1), 0) // CHUNK
    span = WINDOW + CHUNK
    att_rows = []
    for c in range(tm // CHUNK):
        per_kv = []
        for hk in range(N_KV_HEADS):
            qc = qn[c * CHUNK:(c + 1) * CHUNK, hk * GROUP_W:(hk + 1) * GROUP_W]
            qs = jnp.concatenate([jnp.where(lane_grp == g, qc, 0.0) for g in range(Q_PER_KV)],
                                 axis=0).astype(BF16)
            kw = k4[c * CHUNK:c * CHUNK + span, hk * GROUP_W:(hk + 1) * GROUP_W]
            vw = v4[c * CHUNK:c * CHUNK + span, hk * GROUP_W:(hk + 1) * GROUP_W]
            s = _dot_nt(qs, kw) * (HEAD_DIM ** -0.5)
            if mask_history and c * CHUNK < WINDOW:
                kcol = lax.broadcasted_iota(I32, (1, span), 1)
                s = jnp.where(kcol >= jnp.where(i == 0, WINDOW - c * CHUNK, 0), s, NEG)
            sink = jnp.zeros((Q_PER_KV * CHUNK, 1), F32)
            for g in range(Q_PER_KV):
                sink = jnp.where(row_grp == g, sink_ref[0, hk * Q_PER_KV + g], sink)
            m = jnp.maximum(jnp.max(s, axis=-1, keepdims=True), sink)
            p = jnp.exp(s - m)
            den = jnp.sum(p, axis=-1, keepdims=True) + jnp.exp(sink - m)
            o = _dot(p.astype(BF16), vw) / den
            oc = jnp.zeros((CHUNK, GROUP_W), F32)
            for g in range(Q_PER_KV):
                oc = oc + jnp.where(lane_grp == g, o[g * CHUNK:(g + 1) * CHUNK, :], 0.0)
            per_kv.append(oc)
        att_rows.append(jnp.concatenate(per_kv, axis=1))
    return jnp.concatenate(att_rows, axis=0)


def _mixer_kernel(*refs, tm, nsub, st, has_cache):
    refs = list(refs)
    x_ref = refs.pop(0)
    if has_cache:
        ck_ref, cv_ref, st_ref = refs.pop(0), refs.pop(0), refs.pop(0)
    (g1_ref, win_ref, qg_ref, kg_ref, bd_ref, sink_ref, cw_ref, wout_ref, g2_ref,
     wrt_ref, brt_ref, upper_ref, lower_ref,
     h_ref, xs_ref, col_ref, tcnt_ref, knew_ref, vnew_ref, unew_ref, kprev, vprev, ubuf) = refs
    i = pl.program_id(1)
    ntok = nsub * tm

    @pl.when(i == 0)
    def _():
        ubuf[:, 0:SUBLANES, :] = jnp.zeros((nsub, SUBLANES, CONV_DIM), F32)
        if has_cache:
            kprev[...] = ck_ref[...]
            vprev[...] = cv_ref[...]
            ubuf[:, SUBLANES - (CONV_W - 1):SUBLANES, :] = st_ref[...]
        else:
            kprev[...] = jnp.zeros(kprev.shape, F32)
            vprev[...] = jnp.zeros(vprev.shape, F32)

    x = jnp.concatenate([x_ref[sub] for sub in range(nsub)], axis=0) if nsub > 1 else x_ref[0]
    xn = _rms_rows(x, g1_ref[...])
    proj = _dot(xn.astype(BF16), win_ref[...])
    o0 = ATT_DIM
    o1 = o0 + KV_DIM
    o2 = o1 + KV_DIM
    o3 = o2 + CONV_DIM
    o4 = o3 + CONV_DIM
    bd = bd_ref[...]
    qn = _head_rms(proj[:, :o0], bd, qg_ref[...])
    kn = _head_rms(proj[:, o0:o1], bd[:KV_DIM, :KV_DIM], kg_ref[...])
    v = proj[:, o1:o2]
    bg = proj[:, o2:o3]
    u = proj[:, o3:o4] * proj[:, o4:]

    cw = cw_ref[...]
    att_parts, cv_parts = [], []
    for sub in range(nsub):
        r0, r1 = sub * tm, (sub + 1) * tm
        kall = jnp.concatenate([kprev[sub], kn[r0:r1]], axis=0)
        vall = jnp.concatenate([vprev[sub], v[r0:r1]], axis=0)
        att_parts.append(_attention(i, qn[r0:r1], kall, vall, sink_ref,
                                    mask_history=not has_cache))
        kprev[sub] = kall[tm:, :]
        vprev[sub] = vall[tm:, :]
        knew_ref[sub] = kall[tm:, :]
        vnew_ref[sub] = vall[tm:, :]

        us = u[r0:r1]
        ubuf[sub, SUBLANES:SUBLANES + tm, :] = us
        first = SUBLANES - (CONV_W - 1)
        cvs = ubuf[sub, first:first + tm, :] * cw[0:1, :]
        for tap in range(1, CONV_W - 1):
            cvs = cvs + ubuf[sub, first + tap:first + tap + tm, :] * cw[tap:tap + 1, :]
        cv_parts.append(cvs + us * cw[CONV_W - 1:CONV_W, :])
        ubuf[sub, 0:SUBLANES, :] = ubuf[sub, tm:tm + SUBLANES, :]
        unew_ref[sub] = us[tm - (CONV_W - 1):, :]
    att = jnp.concatenate(att_parts, axis=0) if nsub > 1 else att_parts[0]
    cv = jnp.concatenate(cv_parts, axis=0) if nsub > 1 else cv_parts[0]

    mix = jnp.concatenate([att, bg * cv], axis=1).astype(BF16)
    h = x + _dot(mix, wout_ref[...])
    for sub in range(nsub):
        h_ref[sub] = h[sub * tm:(sub + 1) * tm]
    hn = _rms_rows(h, g2_ref[...])

    hn_hi, hn_lo = _split_bf16(hn)
    wr_hi, wr_lo = _split_bf16(wrt_ref[...])
    logits = (_dot_nt(wr_hi, hn_hi) + _dot_nt(wr_hi, hn_lo) + _dot_nt(wr_lo, hn_hi)) + brt_ref[...]
    eio = lax.broadcasted_iota(I32, (N_EXPERTS, ntok), 0)
    work = logits
    tops, sels = [], []
    for _ in range(TOP_K):
        mk = jnp.max(work, axis=0, keepdims=True)
        ik = jnp.min(jnp.where(work == mk, eio, N_EXPERTS), axis=0, keepdims=True)
        sel = eio == ik
        work = jnp.where(sel, -jnp.inf, work)
        tops.append(mk)
        sels.append(sel)
    es = [jnp.exp(t - tops[0]) for t in tops]
    esum = es[0] + es[1] + es[2] + es[3]

    onehot = jnp.zeros((N_EXPERTS, ntok), F32)
    for sel in sels:
        onehot = onehot + jnp.where(sel, 1.0, 0.0)
    onehot_b = onehot.astype(BF16)
    smaller = _dot(lower_ref[...], onehot_b)
    nst = ntok // st
    where_parts = []
    for k in range(nst):
        c0, c1 = k * st, (k + 1) * st
        before = _dot(onehot_b[:, c0:c1], upper_ref[...])
        start = jnp.sum(smaller[:, c0:c1], axis=1, keepdims=True)
        where_parts.append(before + start)
    where_to = jnp.concatenate(where_parts, axis=1) if nst > 1 else where_parts[0]
    pos = [jnp.sum(jnp.where(sel, where_to, 0.0), axis=0, keepdims=True) for sel in sels]
    rio = lax.broadcasted_iota(I32, (TOP_K * st, st), 0)
    for k in range(nst):
        c0, c1 = k * st, (k + 1) * st
        hit = rio == pos[0][:, c0:c1].astype(I32)
        for slot in range(1, TOP_K):
            hit = hit | (rio == pos[slot][:, c0:c1].astype(I32))
        perm = jnp.where(hit, 1.0, 0.0).astype(BF16)
        _pack_rows(_dot(perm, hn_hi[c0:c1]), xs_ref.at[k, 0])
        tcnt_ref[k, 0] = _dot_nt(jnp.ones((1, st), BF16), onehot_b[:, c0:c1]).astype(I32)

    rows8 = jnp.concatenate(pos + [e / esum for e in es], axis=0)
    sq = jnp.concatenate([rows8, jnp.zeros((LANES - 2 * TOP_K, ntok), F32)], axis=0)
    cols = jnp.transpose(sq)
    for sub in range(nsub):
        col_ref[sub] = cols[sub * tm:(sub + 1) * tm]


def _mixer(x, cache, w, *, tm, nsub, st):
    nb, s, _ = x.shape
    nt = s // tm
    nst = nsub * tm // st
    has_cache = cache is not None
    full = lambda shape: pl.BlockSpec(shape, lambda b, i: (0,) * len(shape))
    per_b = lambda shape: pl.BlockSpec((nsub,) + shape, lambda b, i: (b, 0, 0))
    tile = lambda last: pl.BlockSpec((nsub, tm, last), lambda b, i: (b, i, 0))
    kv_shape = (WINDOW, KV_DIM)
    kv_spec = per_b(kv_shape)

    in_specs = [tile(D_MODEL)]
    args = [x]
    if has_cache:
        in_specs += [kv_spec, kv_spec, per_b((CONV_W - 1, CONV_DIM))]
        args += list(cache)
    tio = jnp.arange(st)
    eio = jnp.arange(N_EXPERTS)
    upper = (tio[:, None] < tio[None, :]).astype(BF16)
    lower = (eio[None, :] < eio[:, None]).astype(BF16)
    in_specs += [
        full((1, D_MODEL)), full((D_MODEL, IN_DIM)), full((1, ATT_DIM)),
        full((1, KV_DIM)), full((ATT_DIM, ATT_DIM)),
        pl.BlockSpec(memory_space=pltpu.SMEM),
        full((CONV_W, CONV_DIM)), full((D_MODEL, D_MODEL)), full((1, D_MODEL)),
        full((N_EXPERTS, D_MODEL)), full((N_EXPERTS, 1)), full((st, st)),
        full((N_EXPERTS, N_EXPERTS)),
    ]
    args += [w['g1'], w['w_in'], w['qg'], w['kg'], w['bd'], w['sinks'], w['conv_w'],
             w['w_out'], w['g2'], w['w_router_t'], w['b_router_t'], upper, lower]
    rt = TOP_K * st
    nsort = nb // nsub * nst
    out_shape = [
        jax.ShapeDtypeStruct((nb, s, D_MODEL), F32),
        jax.ShapeDtypeStruct((nsort, nt, rt, PACK_S, LANES), U32),
        jax.ShapeDtypeStruct((nb, s, LANES), F32),
        jax.ShapeDtypeStruct((nsort, nt, 1, N_EXPERTS), I32),
        jax.ShapeDtypeStruct((nb,) + kv_shape, F32),
        jax.ShapeDtypeStruct((nb,) + kv_shape, F32),
        jax.ShapeDtypeStruct((nb, CONV_W - 1, CONV_DIM), F32),
    ]
    out_specs = [
        tile(D_MODEL),
        pl.BlockSpec((nst, 1, rt, PACK_S, LANES), lambda b, i: (b, i, 0, 0, 0)),
        tile(LANES),
        pl.BlockSpec((nst, 1, 1, N_EXPERTS), lambda b, i: (b, i, 0, 0)),
        kv_spec, kv_spec, per_b((CONV_W - 1, CONV_DIM)),
    ]
    scratch = [pltpu.VMEM((nsub, WINDOW, KV_DIM), F32), pltpu.VMEM((nsub, WINDOW, KV_DIM), F32),
               pltpu.VMEM((nsub, tm + SUBLANES, CONV_DIM), F32)]
    return pl.pallas_call(
        functools.partial(_mixer_kernel, tm=tm, nsub=nsub, st=st, has_cache=has_cache),
        grid=(nb // nsub, nt),
        in_specs=in_specs,
        out_specs=out_specs,
        out_shape=out_shape,
        scratch_shapes=scratch,
        compiler_params=pltpu.CompilerParams(
            dimension_semantics=("arbitrary", "arbitrary"), vmem_limit_bytes=MIXER_VMEM_LIMIT),
        name="mixer_sample" if has_cache else "mixer_prompt",
    )(*args)


def _expert_kernel(be_ref, bq_ref, nused_ref, src_ref, off_ref, n_ref, tot_ref,
                   xsp_hbm, xss_hbm, wgu_hbm, bgu_ref, wd_hbm, bdn_ref, ys_ref,
                   xbuf, sems, wgu_f, wd_f, wsems, ptr, wcur, *, n_tiles, n_prompt_tiles):
    b = pl.program_id(0)
    nused = nused_ref[0]

    def weight_copies(e, ws):
        return (pltpu.make_async_copy(wgu_hbm.at[e], wgu_f.at[ws], wsems.at[0, ws]),
                pltpu.make_async_copy(wd_hbm.at[e], wd_f.at[ws], wsems.at[1, ws]))

    def issue_rows(blk, slot, enabled, between=lambda: None):
        blk = jnp.minimum(blk, pl.num_programs(0) - 1)
        e = be_ref[blk]
        q = bq_ref[blk]
        lo_row = q * ROW_BLOCK
        hi_row = lo_row + ROW_BLOCK
        base = e * n_tiles

        def segment(j, j_limit, src_hbm, ok):
            jc = jnp.minimum(j, j_limit - 1)
            o = off_ref[base + jc]
            inside = ok & (j < j_limit) & (o < hi_row)
            lo = jnp.maximum(o, lo_row)
            ln = jnp.minimum(o + n_ref[base + jc], hi_row) - lo

            @pl.when(inside & (ln > 0))
            def _():
                pltpu.make_async_copy(src_hbm.at[pl.ds(src_ref[base + jc] + (lo - o), ln)],
                                      xbuf.at[slot, pl.ds(lo - lo_row, ln)], sems.at[slot]).start()
            return inside

        j0 = jnp.where(q == 0, 0, ptr[0])

        def cond(j):
            return (enabled & (j < n_prompt_tiles)
                    & (off_ref[base + jnp.minimum(j, n_prompt_tiles - 1)] < hi_row))

        def body(j):
            segment(j, n_prompt_tiles, xsp_hbm, True)
            return j + 1

        jtail = lax.while_loop(cond, body, j0 + INLINE_SEGMENTS)
        result = between()
        count = jnp.int32(0)
        for k in range(INLINE_SEGMENTS):
            count = count + segment(j0 + k, n_prompt_tiles, xsp_hbm, enabled).astype(I32)
        jend = jnp.where(count == INLINE_SEGMENTS, jtail, j0 + count)
        ptr[0] = jnp.where(enabled, jnp.maximum(jend - 1, 0), ptr[0])
        for j in range(n_prompt_tiles, n_tiles):
            segment(j, n_tiles, xss_hbm, enabled)
        return result

    @pl.when(b == 0)
    def _():
        xbuf[...] = jnp.zeros(xbuf.shape, U32)
        for ahead in range(LOOKAHEAD):
            issue_rows(ahead, ahead, ahead < nused)
        for c in weight_copies(be_ref[0], 0):
            c.start()
        wcur[0] = 1

    active = b < nused

    @pl.when(active)
    def _():
        slot = b % ROW_SLOTS
        e = be_ref[b]
        rows = jnp.minimum(tot_ref[e] - bq_ref[b] * ROW_BLOCK, ROW_BLOCK)
        pltpu.make_async_copy(xsp_hbm.at[pl.ds(0, rows)], xbuf.at[slot, pl.ds(0, rows)],
                              sems.at[slot]).wait()

        @pl.when(bq_ref[b] == 0)
        def _():
            ws = 1 - wcur[0]
            wcur[0] = ws
            for c in weight_copies(e, ws):
                c.wait()
            nxt = b + (tot_ref[e] + ROW_BLOCK - 1) // ROW_BLOCK

            @pl.when(nxt < nused)
            def _():
                for c in weight_copies(be_ref[jnp.minimum(nxt, pl.num_programs(0) - 1)], 1 - ws):
                    c.start()

        ws = wcur[0]

        def compute(nrows):
            x = issue_rows(b + LOOKAHEAD, (b + LOOKAHEAD) % ROW_SLOTS, b + LOOKAHEAD < nused,
                           between=lambda: _unpack_rows(xbuf.at[slot, pl.ds(0, nrows)]))
            gu = _dot(x, wgu_f[ws]) + bgu_ref[pl.ds(e, 1), :]
            g = jnp.minimum(gu[:, :D_FF], SWIGLU_LIMIT)
            u = jnp.clip(gu[:, D_FF:], -SWIGLU_LIMIT, SWIGLU_LIMIT)
            act = (u + 1.0) * (g * jax.nn.sigmoid(SWIGLU_ALPHA * g))
            y = (_dot(act.astype(BF16), wd_f[ws])
                 + (bdn_ref[pl.ds(e, 1), :] + _anchor_zero_row(xbuf)))
            _pack_rows(y.astype(BF16).astype(F32), ys_ref.at[pl.ds(0, nrows)])
            if nrows < ROW_BLOCK:
                ys_ref[nrows:] = jnp.zeros((ROW_BLOCK - nrows, PACK_S, LANES), U32)

        for piece in range(ROW_BLOCK // TAIL_ROWS):
            top = (piece + 1) * TAIL_ROWS
            pl.when((rows > top - TAIL_ROWS) & (rows <= top))(functools.partial(compute, top))

    @pl.when(jnp.logical_not(active))
    def _():
        ys_ref[...] = jnp.zeros(ys_ref.shape, U32)


def _experts(plan, xsp, xss, w_gu, b_gu, w_down, b_down, *, n_tiles, n_prompt_tiles, rows):
    nblk = rows // ROW_BLOCK

    grid_spec = pltpu.PrefetchScalarGridSpec(
        num_scalar_prefetch=7,
        grid=(nblk,),
        in_specs=[pl.BlockSpec(memory_space=pl.ANY),
                  pl.BlockSpec(memory_space=pl.ANY),
                  pl.BlockSpec(memory_space=pl.ANY),
                  pl.BlockSpec((N_EXPERTS, 2 * D_FF), lambda b, *_: (0, 0)),
                  pl.BlockSpec(memory_space=pl.ANY),
                  pl.BlockSpec((N_EXPERTS, D_MODEL), lambda b, *_: (0, 0))],
        out_specs=pl.BlockSpec((ROW_BLOCK, PACK_S, LANES), lambda b, *_: (b, 0, 0)),
        scratch_shapes=[pltpu.VMEM((ROW_SLOTS + 1, ROW_BLOCK, PACK_S, LANES), U32),
                        pltpu.SemaphoreType.DMA((ROW_SLOTS,)),
                        pltpu.VMEM((2, D_MODEL, 2 * D_FF), F32), pltpu.VMEM((2, D_FF, D_MODEL), F32),
                        pltpu.SemaphoreType.DMA((2, 2)),
                        pltpu.SMEM((1,), I32), pltpu.SMEM((1,), I32)],
    )
    return pl.pallas_call(
        functools.partial(_expert_kernel, n_tiles=n_tiles, n_prompt_tiles=n_prompt_tiles),
        grid_spec=grid_spec,
        out_shape=jax.ShapeDtypeStruct((rows, PACK_S, LANES), U32),
        compiler_params=pltpu.CompilerParams(
            dimension_semantics=("arbitrary",), vmem_limit_bytes=VMEM_LIMIT),
        name="experts",
    )(plan['block_e'], plan['block_q'], plan['n_used'], plan['src_t'], plan['off_t'], plan['n_t'],
      plan['tot'], xsp, xss, w_gu, b_gu, w_down, b_down)


def _combine_kernel(ysrc_ref, sloc_ref, n_ref, col_ref, h_ref, ys_hbm, o_ref, buf, sems,
                    *, tm, tile0, tps):
    j = pl.program_id(0)
    nt = pl.num_programs(0)
    rt = TOP_K * tm

    def issue(step, slot, enabled):
        for u in range(tps):
            base = (tile0 + jnp.minimum(step, nt - 1) * tps + u) * N_EXPERTS
            for e in range(N_EXPERTS):
                n = n_ref[base + e]

                @pl.when(enabled & (n > 0))
                def _():
                    pltpu.make_async_copy(ys_hbm.at[pl.ds(ysrc_ref[base + e], n)],
                                          buf.at[slot, pl.ds(u * rt + sloc_ref[base + e], n)],
                                          sems.at[slot]).start()

    @pl.when(j == 0)
    def _():
        buf[ROW_SLOTS, 0:SUBLANES // PACK_S] = jnp.zeros((SUBLANES // PACK_S, PACK_S, LANES), U32)
        for ahead in range(LOOKAHEAD):
            issue(ahead, ahead, ahead < nt)

    slot = j % ROW_SLOTS
    pltpu.make_async_copy(ys_hbm.at[pl.ds(0, tps * rt)], buf.at[slot], sems.at[slot]).wait()
    ys = [_unpack_rows(buf.at[slot, pl.ds(u * rt, rt)]) for u in range(tps)]
    issue(j + LOOKAHEAD, (j + LOOKAHEAD) % ROW_SLOTS, j + LOOKAHEAD < nt)
    rio = lax.broadcasted_iota(I32, (tm, rt), 1)
    zero = _anchor_zero_row(buf)
    for u in range(tps):
        tok = slice(u * tm, (u + 1) * tm)
        col = col_ref[tok, :]
        gmat = jnp.zeros((tm, rt), F32)
        for k in range(TOP_K):
            gmat = gmat + jnp.where(rio == col[:, k:k + 1].astype(I32),
                                    col[:, TOP_K + k:TOP_K + k + 1], 0.0)
        o_ref[tok, :] = (h_ref[tok, :] + zero) + _dot(gmat.astype(BF16), ys[u])


def _combine(plan, col, h2d, ys, *, tm, tile0, tps):
    t = h2d.shape[0]
    rt = TOP_K * tm
    tok = tm * tps
    grid_spec = pltpu.PrefetchScalarGridSpec(
        num_scalar_prefetch=3,
        grid=(t // tok,),
        in_specs=[pl.BlockSpec((tok, LANES), lambda i, *_: (i, 0)),
                  pl.BlockSpec((tok, D_MODEL), lambda i, *_: (i, 0)),
                  pl.BlockSpec(memory_space=pl.ANY)],
        out_specs=pl.BlockSpec((tok, D_MODEL), lambda i, *_: (i, 0)),
        scratch_shapes=[pltpu.VMEM((ROW_SLOTS + 1, tps * rt, PACK_S, LANES), U32),
                        pltpu.SemaphoreType.DMA((ROW_SLOTS,))],
    )
    return pl.pallas_call(
        functools.partial(_combine_kernel, tm=tm, tile0=tile0, tps=tps),
        grid_spec=grid_spec,
        out_shape=jax.ShapeDtypeStruct((t, D_MODEL), F32),
        compiler_params=pltpu.CompilerParams(
            dimension_semantics=("arbitrary",), vmem_limit_bytes=VMEM_LIMIT),
        name="combine",
    )(plan['ysrc'], plan['sloc'], plan['n'], col, h2d, ys)


def kernel(x_prompt, x_sample, cache_k, cache_v, state_conv, norm1_g, w_in, q_norm_g, k_norm_g,
           sinks, conv_w, w_out, norm2_g, w_router, b_router, w_gu, b_gu, w_down, b_down):
    l = 0
    nbp, sp, _ = x_prompt.shape
    nbs, ss, _ = x_sample.shape
    tp, ts = nbp * sp, nbs * ss
    tmp, tms = TOKEN_TILE, ts
    ntp, nts = tp // tmp, ts // tms
    n_tiles = ntp + nts
    n_rows = (tp + ts) * TOP_K
    grp = jnp.arange(ATT_DIM) // HEAD_DIM
    w = {
        'g1': norm1_g[l][None, :],
        'w_in': w_in[l].astype(BF16),
        'qg': jnp.tile(q_norm_g[l], N_HEADS)[None, :],
        'kg': jnp.tile(k_norm_g[l], N_KV_HEADS)[None, :],
        'bd': (grp[:, None] == grp[None, :]).astype(BF16),
        'sinks': sinks[l][None, :],
        'conv_w': conv_w[l],
        'w_out': w_out[l].astype(BF16),
        'g2': norm2_g[l][None, :],
        'w_router_t': w_router[l].T,
        'b_router_t': b_router[l][:, None],
    }
    hp, xsp, colp, tcp, knp, vnp, unp = _mixer(x_prompt, None, w, tm=tmp, nsub=STREAMS_PER_STEP,
                                               st=tmp)
    cache = (cache_k[l].reshape(nbs, WINDOW, KV_DIM), cache_v[l].reshape(nbs, WINDOW, KV_DIM),
             state_conv[l])
    hs, xss, cols, tcs, kns, vns, uns = _mixer(x_sample, cache, w, tm=ss, nsub=nbs, st=tms)

    n = jnp.concatenate([tcp.reshape(ntp, N_EXPERTS), tcs.reshape(nts, N_EXPERTS)], axis=0)
    tile_base = jnp.concatenate([jnp.arange(ntp, dtype=I32) * (TOP_K * tmp),
                                 jnp.arange(nts, dtype=I32) * (TOP_K * tms)])
    sloc = jnp.cumsum(n, axis=1) - n
    off = jnp.cumsum(n, axis=0) - n
    tot = jnp.sum(n, axis=0)
    padded = (tot + ROW_BLOCK - 1) // ROW_BLOCK * ROW_BLOCK
    ends = jnp.cumsum(padded)
    base = ends - padded
    rows = (-(-n_rows // ROW_BLOCK) + N_EXPERTS) * ROW_BLOCK
    nblk = rows // ROW_BLOCK
    blk0 = jnp.arange(nblk, dtype=I32) * ROW_BLOCK
    block_e = jnp.minimum(jnp.sum((ends[None, :] <= blk0[:, None]).astype(I32), axis=1),
                          N_EXPERTS - 1)
    eq = (block_e[:, None] == jnp.arange(N_EXPERTS, dtype=I32)[None, :]).astype(I32)
    block_q = (blk0 - jnp.sum(eq * base[None, :], axis=1)) // ROW_BLOCK
    plan = {
        'block_e': block_e.astype(I32),
        'block_q': block_q.astype(I32),
        'n_used': (ends[-1:] // ROW_BLOCK).astype(I32),
        'src_t': (tile_base[:, None] + sloc).T.reshape(-1).astype(I32),
        'off_t': off.T.reshape(-1).astype(I32),
        'n_t': n.T.reshape(-1).astype(I32),
        'tot': tot.astype(I32),
        'ysrc': (base[None, :] + off).reshape(-1).astype(I32),
        'sloc': sloc.reshape(-1).astype(I32),
        'n': n.reshape(-1).astype(I32),
    }

    ys = _experts(plan, xsp.reshape(-1, PACK_S, LANES), xss.reshape(-1, PACK_S, LANES),
                  w_gu[l], b_gu[l], w_down[l], b_down[l],
                  n_tiles=n_tiles, n_prompt_tiles=ntp, rows=rows)
    yp = _combine(plan, colp.reshape(tp, LANES), hp.reshape(tp, D_MODEL), ys, tm=tmp, tile0=0,
                  tps=COMBINE_TILES)
    ysm = _combine(plan, cols.reshape(ts, LANES), hs.reshape(ts, D_MODEL), ys, tm=tms, tile0=ntp,
                   tps=1)

    kv5 = lambda a: a.reshape(1, a.shape[0], WINDOW, N_KV_HEADS, HEAD_DIM)
    return (yp.reshape(nbp, sp, D_MODEL), ysm.reshape(nbs, ss, D_MODEL),
            kv5(knp), kv5(vnp), unp[None], kv5(kns), kv5(vns), uns[None])
```

```python
import functools

import jax
import jax.numpy as jnp
from jax import lax
from jax.experimental import pallas as pl
from jax.experimental.pallas import tpu as pltpu

D_MODEL = 1024
CHUNK = 64
HEAD_DIM = 64
N_HEADS = 8
N_KV_HEADS = 2
Q_PER_KV = N_HEADS // N_KV_HEADS
ATT_DIM = N_HEADS * HEAD_DIM
KV_DIM = N_KV_HEADS * HEAD_DIM
GROUP_W = Q_PER_KV * HEAD_DIM
CONV_DIM = D_MODEL - ATT_DIM
CONV_W = 3
WINDOW = 128
IN_DIM = ATT_DIM + 2 * KV_DIM + 3 * CONV_DIM
N_EXPERTS = 32
TOP_K = 4
D_FF = D_MODEL
SWIGLU_LIMIT = 7.0
SWIGLU_ALPHA = 1.702
EPS = 1e-6
NEG = -1e30

LANES = 128
SUBLANES = 8
HALF = D_MODEL // 2
PACK_S = HALF // LANES
TOKEN_TILE = 256
STREAMS_PER_STEP = 4
COMBINE_TILES = 4
ROW_BLOCK = 1024
TAIL_ROWS = 256
INLINE_SEGMENTS = 40
LOOKAHEAD = 2
ROW_SLOTS = LOOKAHEAD + 1
VMEM_LIMIT = 56 * 1024 * 1024
MIXER_VMEM_LIMIT = 60 * 1024 * 1024
COMBINE_VMEM_LIMIT = MIXER_VMEM_LIMIT

F32 = jnp.float32
BF16 = jnp.bfloat16
U32 = jnp.uint32
I32 = jnp.int32


def _rms_rows(x, g):
    return x * lax.rsqrt(jnp.mean(x * x, axis=-1, keepdims=True) + EPS) * g


def _split_bf16(x):
    hi = x.astype(BF16)
    lo = (x - hi.astype(F32)).astype(BF16)
    return hi, lo


def _dot(a, b):
    return jnp.dot(a, b, preferred_element_type=F32)


def _dot_nt(a, b):
    return lax.dot_general(a, b, (((1,), (1,)), ((), ())), preferred_element_type=F32)


def _head_rms(t, bd, g):
    ssq = _dot((t * t).astype(BF16), bd)
    return t * lax.rsqrt(ssq * (1.0 / HEAD_DIM) + EPS) * g


def _pack_rows(vals, out_ref):
    r = vals.shape[0]
    bits = pltpu.bitcast(vals, U32)
    word = (bits[:, :HALF] >> 16) | (bits[:, HALF:] & jnp.uint32(0xFFFF0000))
    flat = out_ref.reshape(r * PACK_S, LANES)
    for s in range(PACK_S):
        flat[pl.ds(s, r, stride=PACK_S), :] = word[:, s * LANES:(s + 1) * LANES]


def _unpack_rows(packed_ref):
    r = packed_ref.shape[0]
    flat = packed_ref.reshape(r * PACK_S, LANES)
    lo, hi = [], []
    for s in range(PACK_S):
        w = flat[pl.ds(s, r, stride=PACK_S), :]
        lo.append(pltpu.bitcast(w << 16, F32))
        hi.append(pltpu.bitcast(w & jnp.uint32(0xFFFF0000), F32))
    return jnp.concatenate(lo + hi, axis=1).astype(BF16)


def _anchor_zero_row(staging_ref):
    z = staging_ref.at[ROW_SLOTS, pl.ds(0, SUBLANES // PACK_S)].reshape(SUBLANES, LANES)[...]
    return jnp.concatenate([pltpu.bitcast(z, F32)[0:1, :]] * (D_MODEL // LANES), axis=1)


def _spread_kv(t):
    assert N_KV_HEADS == 2 and KV_DIM == LANES
    swapped = pltpu.roll(t, HEAD_DIM, axis=1)
    low = lax.broadcasted_iota(I32, t.shape, 1) < HEAD_DIM
    pair0 = jnp.where(low, t, swapped)
    pair1 = jnp.where(low, swapped, t)
    reps = GROUP_W // LANES
    return jnp.concatenate([pair0] * reps + [pair1] * reps, axis=1).astype(BF16)


def _attention(i, qn, kall, vall, sink_ref, *, mask_history):
    tm = qn.shape[0]
    k4 = _spread_kv(kall)
    v4 = _spread_kv(vall)
    lane_grp = lax.broadcasted_iota(I32, (CHUNK, GROUP_W), 1) // HEAD_DIM
    row_grp = lax.broadcasted_iota(I32, (Q_PER_KV * CHUNK, 1), 0) // CHUNK
    span = WINDOW + CHUNK
    att_rows = []
    for c in range(tm // CHUNK):
        per_kv = []
        for hk in range(N_KV_HEADS):
            qc = qn[c * CHUNK:(c + 1) * CHUNK, hk * GROUP_W:(hk + 1) * GROUP_W]
            qs = jnp.concatenate([jnp.where(lane_grp == g, qc, 0.0) for g in range(Q_PER_KV)],
                                 axis=0).astype(BF16)
            kw = k4[c * CHUNK:c * CHUNK + span, hk * GROUP_W:(hk + 1) * GROUP_W]
            vw = v4[c * CHUNK:c * CHUNK + span, hk * GROUP_W:(hk + 1) * GROUP_W]
            s = _dot_nt(qs, kw) * (HEAD_DIM ** -0.5)
            if mask_history and c * CHUNK < WINDOW:
                kcol = lax.broadcasted_iota(I32, (1, span), 1)
                s = jnp.where(kcol >= jnp.where(i == 0, WINDOW - c * CHUNK, 0), s, NEG)
            sink = jnp.zeros((Q_PER_KV * CHUNK, 1), F32)
            for g in range(Q_PER_KV):
                sink = jnp.where(row_grp == g, sink_ref[0, hk * Q_PER_KV + g], sink)
            m = jnp.maximum(jnp.max(s, axis=-1, keepdims=True), sink)
            p = jnp.exp(s - m)
            den = jnp.sum(p, axis=-1, keepdims=True) + jnp.exp(sink - m)
            o = _dot(p.astype(BF16), vw) / den
            oc = jnp.zeros((CHUNK, GROUP_W), F32)
            for g in range(Q_PER_KV):
                oc = oc + jnp.where(lane_grp == g, o[g * CHUNK:(g + 1) * CHUNK, :], 0.0)
            per_kv.append(oc)
        att_rows.append(jnp.concatenate(per_kv, axis=1))
    return jnp.concatenate(att_rows, axis=0)


def _mixer_kernel(*refs, tm, nsub, st, has_cache):
    refs = list(refs)
    x_ref = refs.pop(0)
    if has_cache:
        ck_ref, cv_ref, st_ref = refs.pop(0), refs.pop(0), refs.pop(0)
    (g1_ref, win_ref, qg_ref, kg_ref, bd_ref, sink_ref, cw_ref, wout_ref, g2_ref,
     wrt_ref, brt_ref, upper_ref, lower_ref,
     h_ref, xs_ref, col_ref, tcnt_ref, knew_ref, vnew_ref, unew_ref, kprev, vprev, ubuf) = refs
    i = pl.program_id(1)
    ntok = nsub * tm

    @pl.when(i == 0)
    def _():
        ubuf[:, 0:SUBLANES, :] = jnp.zeros((nsub, SUBLANES, CONV_DIM), F32)
        if has_cache:
            kprev[...] = ck_ref[...]
            vprev[...] = cv_ref[...]
            ubuf[:, SUBLANES - (CONV_W - 1):SUBLANES, :] = st_ref[...]
        else:
            kprev[...] = jnp.zeros(kprev.shape, F32)
            vprev[...] = jnp.zeros(vprev.shape, F32)

    x = jnp.concatenate([x_ref[sub] for sub in range(nsub)], axis=0) if nsub > 1 else x_ref[0]
    xn = _rms_rows(x, g1_ref[...])
    proj = _dot(xn.astype(BF16), win_ref[...])
    o0 = ATT_DIM
    o1 = o0 + KV_DIM
    o2 = o1 + KV_DIM
    o3 = o2 + CONV_DIM
    o4 = o3 + CONV_DIM
    bd = bd_ref[...]
    qn = _head_rms(proj[:, :o0], bd, qg_ref[...])
    kn = _head_rms(proj[:, o0:o1], bd[:KV_DIM, :KV_DIM], kg_ref[...])
    v = proj[:, o1:o2]
    bg = proj[:, o2:o3]
    u = proj[:, o3:o4] * proj[:, o4:]

    cw = cw_ref[...]
    att_parts, cv_parts = [], []
    for sub in range(nsub):
        r0, r1 = sub * tm, (sub + 1) * tm
        kall = jnp.concatenate([kprev[sub], kn[r0:r1]], axis=0)
        vall = jnp.concatenate([vprev[sub], v[r0:r1]], axis=0)
        att_parts.append(_attention(i, qn[r0:r1], kall, vall, sink_ref,
                                    mask_history=not has_cache))
        kprev[sub] = kall[tm:, :]
        vprev[sub] = vall[tm:, :]
        knew_ref[sub] = kall[tm:, :]
        vnew_ref[sub] = vall[tm:, :]

        us = u[r0:r1]
        ubuf[sub, SUBLANES:SUBLANES + tm, :] = us
        first = SUBLANES - (CONV_W - 1)
        cvs = ubuf[sub, first:first + tm, :] * cw[0:1, :]
        for tap in range(1, CONV_W - 1):
            cvs = cvs + ubuf[sub, first + tap:first + tap + tm, :] * cw[tap:tap + 1, :]
        cv_parts.append(cvs + us * cw[CONV_W - 1:CONV_W, :])
        ubuf[sub, 0:SUBLANES, :] = ubuf[sub, tm:tm + SUBLANES, :]
        unew_ref[sub] = us[tm - (CONV_W - 1):, :]
    att = jnp.concatenate(att_parts, axis=0) if nsub > 1 else att_parts[0]
    cv = jnp.concatenate(cv_parts, axis=0) if nsub > 1 else cv_parts[0]

    mix = jnp.concatenate([att, bg * cv], axis=1).astype(BF16)
    h = x + _dot(mix, wout_ref[...])
    for sub in range(nsub):
        h_ref[sub] = h[sub * tm:(sub + 1) * tm]
    hn = _rms_rows(h, g2_ref[...])

    hn_hi, hn_lo = _split_bf16(hn)
    wr_hi, wr_lo = _split_bf16(wrt_ref[...])
    logits = (_dot_nt(wr_hi, hn_hi) + _dot_nt(wr_hi, hn_lo) + _dot_nt(wr_lo, hn_hi)) + brt_ref[...]
    eio = lax.broadcasted_iota(I32, (N_EXPERTS, ntok), 0)
    work = logits
    tops, sels = [], []
    for _ in range(TOP_K):
        mk = jnp.max(work, axis=0, keepdims=True)
        ik = jnp.min(jnp.where(work == mk, eio, N_EXPERTS), axis=0, keepdims=True)
        sel = eio == ik
        work = jnp.where(sel, -jnp.inf, work)
        tops.append(mk)
        sels.append(sel)
    es = [jnp.exp(t - tops[0]) for t in tops]
    esum = es[0] + es[1] + es[2] + es[3]

    onehot = jnp.zeros((N_EXPERTS, ntok), F32)
    for sel in sels:
        onehot = onehot + jnp.where(sel, 1.0, 0.0)
    onehot_b = onehot.astype(BF16)
    smaller = _dot(lower_ref[...], onehot_b)
    nst = ntok // st
    where_parts = []
    for k in range(nst):
        c0, c1 = k * st, (k + 1) * st
        before = _dot(onehot_b[:, c0:c1], upper_ref[...])
        start = jnp.sum(smaller[:, c0:c1], axis=1, keepdims=True)
        where_parts.append(before + start)
    where_to = jnp.concatenate(where_parts, axis=1) if nst > 1 else where_parts[0]
    pos = [jnp.sum(jnp.where(sel, where_to, 0.0), axis=0, keepdims=True) for sel in sels]
    rio = lax.broadcasted_iota(I32, (TOP_K * st, st), 0)
    for k in range(nst):
        c0, c1 = k * st, (k + 1) * st
        hit = rio == pos[0][:, c0:c1].astype(I32)
        for slot in range(1, TOP_K):
            hit = hit | (rio == pos[slot][:, c0:c1].astype(I32))
        perm = jnp.where(hit, 1.0, 0.0).astype(BF16)
        _pack_rows(_dot(perm, hn_hi[c0:c1]), xs_ref.at[k, 0])
        tcnt_ref[k, 0] = _dot_nt(jnp.ones((1, st), BF16), onehot_b[:, c0:c1]).astype(I32)

    rows8 = jnp.concatenate(pos + [e / esum for e in es], axis=0)
    sq = jnp.concatenate([rows8, jnp.zeros((LANES - 2 * TOP_K, ntok), F32)], axis=0)
    cols = jnp.transpose(sq)
    for sub in range(nsub):
        col_ref[sub] = cols[sub * tm:(sub + 1) * tm]


def _mixer(x, cache, w, *, tm, nsub, st):
    nb, s, _ = x.shape
    nt = s // tm
    nst = nsub * tm // st
    has_cache = cache is not None
    full = lambda shape: pl.BlockSpec(shape, lambda b, i: (0,) * len(shape))
    per_b = lambda shape: pl.BlockSpec((nsub,) + shape, lambda b, i: (b, 0, 0))
    tile = lambda last: pl.BlockSpec((nsub, tm, last), lambda b, i: (b, i, 0))
    kv_shape = (WINDOW, KV_DIM)
    kv_spec = per_b(kv_shape)

    in_specs = [tile(D_MODEL)]
    args = [x]
    if has_cache:
        in_specs += [kv_spec, kv_spec, per_b((CONV_W - 1, CONV_DIM))]
        args += list(cache)
    tio = jnp.arange(st)
    eio = jnp.arange(N_EXPERTS)
    upper = (tio[:, None] < tio[None, :]).astype(BF16)
    lower = (eio[None, :] < eio[:, None]).astype(BF16)
    in_specs += [
        full((1, D_MODEL)), full((D_MODEL, IN_DIM)), full((1, ATT_DIM)),
        full((1, KV_DIM)), full((ATT_DIM, ATT_DIM)),
        pl.BlockSpec(memory_space=pltpu.SMEM),
        full((CONV_W, CONV_DIM)), full((D_MODEL, D_MODEL)), full((1, D_MODEL)),
        full((N_EXPERTS, D_MODEL)), full((N_EXPERTS, 1)), full((st, st)),
        full((N_EXPERTS, N_EXPERTS)),
    ]
    args += [w['g1'], w['w_in'], w['qg'], w['kg'], w['bd'], w['sinks'], w['conv_w'],
             w['w_out'], w['g2'], w['w_router_t'], w['b_router_t'], upper, lower]
    rt = TOP_K * st
    nsort = nb // nsub * nst
    out_shape = [
        jax.ShapeDtypeStruct((nb, s, D_MODEL), F32),
        jax.ShapeDtypeStruct((nsort, nt, rt, PACK_S, LANES), U32),
        jax.ShapeDtypeStruct((nb, s, LANES), F32),
        jax.ShapeDtypeStruct((nsort, nt, 1, N_EXPERTS), I32),
        jax.ShapeDtypeStruct((nb,) + kv_shape, F32),
        jax.ShapeDtypeStruct((nb,) + kv_shape, F32),
        jax.ShapeDtypeStruct((nb, CONV_W - 1, CONV_DIM), F32),
    ]
    out_specs = [
        tile(D_MODEL),
        pl.BlockSpec((nst, 1, rt, PACK_S, LANES), lambda b, i: (b, i, 0, 0, 0)),
        tile(LANES),
        pl.BlockSpec((nst, 1, 1, N_EXPERTS), lambda b, i: (b, i, 0, 0)),
        kv_spec, kv_spec, per_b((CONV_W - 1, CONV_DIM)),
    ]
    scratch = [pltpu.VMEM((nsub, WINDOW, KV_DIM), F32), pltpu.VMEM((nsub, WINDOW, KV_DIM), F32),
               pltpu.VMEM((nsub, tm + SUBLANES, CONV_DIM), F32)]
    return pl.pallas_call(
        functools.partial(_mixer_kernel, tm=tm, nsub=nsub, st=st, has_cache=has_cache),
        grid=(nb // nsub, nt),
        in_specs=in_specs,
        out_specs=out_specs,
        out_shape=out_shape,
        scratch_shapes=scratch,
        compiler_params=pltpu.CompilerParams(
            dimension_semantics=("arbitrary", "arbitrary"), vmem_limit_bytes=MIXER_VMEM_LIMIT),
        name="mixer_sample" if has_cache else "mixer_prompt",
    )(*args)


def _expert_kernel(be_ref, bq_ref, nused_ref, src_ref, off_ref, n_ref, tot_ref,
                   xsp_hbm, xss_hbm, wgu_hbm, bgu_ref, wd_hbm, bdn_ref, ys_ref,
                   xbuf, sems, wgu_f, wd_f, wsems, ptr, wcur, *, n_tiles, n_prompt_tiles):
    b = pl.program_id(0)
    nused = nused_ref[0]

    def weight_copies(e, ws):
        return (pltpu.make_async_copy(wgu_hbm.at[e], wgu_f.at[ws], wsems.at[0, ws]),
                pltpu.make_async_copy(wd_hbm.at[e], wd_f.at[ws], wsems.at[1, ws]))

    def issue_rows(blk, slot, enabled, between=lambda: None):
        blk = jnp.minimum(blk, pl.num_programs(0) - 1)
        e = be_ref[blk]
        q = bq_ref[blk]
        lo_row = q * ROW_BLOCK
        hi_row = lo_row + ROW_BLOCK
        base = e * n_tiles

        def segment(j, j_limit, src_hbm, ok):
            jc = jnp.minimum(j, j_limit - 1)
            o = off_ref[base + jc]
            inside = ok & (j < j_limit) & (o < hi_row)
            lo = jnp.maximum(o, lo_row)
            ln = jnp.minimum(o + n_ref[base + jc], hi_row) - lo

            @pl.when(inside & (ln > 0))
            def _():
                pltpu.make_async_copy(src_hbm.at[pl.ds(src_ref[base + jc] + (lo - o), ln)],
                                      xbuf.at[slot, pl.ds(lo - lo_row, ln)], sems.at[slot]).start()
            return inside

        j0 = jnp.where(q == 0, 0, ptr[0])

        def cond(j):
            return (enabled & (j < n_prompt_tiles)
                    & (off_ref[base + jnp.minimum(j, n_prompt_tiles - 1)] < hi_row))

        def body(j):
            segment(j, n_prompt_tiles, xsp_hbm, True)
            return j + 1

        jtail = lax.while_loop(cond, body, j0 + INLINE_SEGMENTS)
        result = between()
        count = jnp.int32(0)
        for k in range(INLINE_SEGMENTS):
            count = count + segment(j0 + k, n_prompt_tiles, xsp_hbm, enabled).astype(I32)
        jend = jnp.where(count == INLINE_SEGMENTS, jtail, j0 + count)
        ptr[0] = jnp.where(enabled, jnp.maximum(jend - 1, 0), ptr[0])
        for j in range(n_prompt_tiles, n_tiles):
            segment(j, n_tiles, xss_hbm, enabled)
        return result

    @pl.when(b == 0)
    def _():
        xbuf[...] = jnp.zeros(xbuf.shape, U32)
        for ahead in range(LOOKAHEAD):
            issue_rows(ahead, ahead, ahead < nused)
        for c in weight_copies(be_ref[0], 0):
            c.start()
        wcur[0] = 1

    active = b < nused

    @pl.when(active)
    def _():
        slot = b % ROW_SLOTS
        e = be_ref[b]
        rows = jnp.minimum(tot_ref[e] - bq_ref[b] * ROW_BLOCK, ROW_BLOCK)
        pltpu.make_async_copy(xsp_hbm.at[pl.ds(0, rows)], xbuf.at[slot, pl.ds(0, rows)],
                              sems.at[slot]).wait()

        @pl.when(bq_ref[b] == 0)
        def _():
            ws = 1 - wcur[0]
            wcur[0] = ws
            for c in weight_copies(e, ws):
                c.wait()
            nxt = b + (tot_ref[e] + ROW_BLOCK - 1) // ROW_BLOCK

            @pl.when(nxt < nused)
            def _():
                for c in weight_copies(be_ref[jnp.minimum(nxt, pl.num_programs(0) - 1)], 1 - ws):
                    c.start()

        ws = wcur[0]

        def compute(nrows):
            x = issue_rows(b + LOOKAHEAD, (b + LOOKAHEAD) % ROW_SLOTS, b + LOOKAHEAD < nused,
                           between=lambda: _unpack_rows(xbuf.at[slot, pl.ds(0, nrows)]))
            gu = _dot(x, wgu_f[ws]) + bgu_ref[pl.ds(e, 1), :]
            g = jnp.minimum(gu[:, :D_FF], SWIGLU_LIMIT)
            u = jnp.clip(gu[:, D_FF:], -SWIGLU_LIMIT, SWIGLU_LIMIT)
            act = (u + 1.0) * (g * jax.nn.sigmoid(SWIGLU_ALPHA * g))
            y = (_dot(act.astype(BF16), wd_f[ws])
                 + (bdn_ref[pl.ds(e, 1), :] + _anchor_zero_row(xbuf)))
            _pack_rows(y.astype(BF16).astype(F32), ys_ref.at[pl.ds(0, nrows)])
            if nrows < ROW_BLOCK:
                ys_ref[nrows:] = jnp.zeros((ROW_BLOCK - nrows, PACK_S, LANES), U32)

        for piece in range(ROW_BLOCK // TAIL_ROWS):
            top = (piece + 1) * TAIL_ROWS
            pl.when((rows > top - TAIL_ROWS) & (rows <= top))(functools.partial(compute, top))

    @pl.when(jnp.logical_not(active))
    def _():
        ys_ref[...] = jnp.zeros(ys_ref.shape, U32)


def _experts(plan, xsp, xss, w_gu, b_gu, w_down, b_down, *, n_tiles, n_prompt_tiles, rows):
    nblk = rows // ROW_BLOCK

    grid_spec = pltpu.PrefetchScalarGridSpec(
        num_scalar_prefetch=7,
        grid=(nblk,),
        in_specs=[pl.BlockSpec(memory_space=pl.ANY),
                  pl.BlockSpec(memory_space=pl.ANY),
                  pl.BlockSpec(memory_space=pl.ANY),
                  pl.BlockSpec((N_EXPERTS, 2 * D_FF), lambda b, *_: (0, 0)),
                  pl.BlockSpec(memory_space=pl.ANY),
                  pl.BlockSpec((N_EXPERTS, D_MODEL), lambda b, *_: (0, 0))],
        out_specs=pl.BlockSpec((ROW_BLOCK, PACK_S, LANES), lambda b, *_: (b, 0, 0)),
        scratch_shapes=[pltpu.VMEM((ROW_SLOTS + 1, ROW_BLOCK, PACK_S, LANES), U32),
                        pltpu.SemaphoreType.DMA((ROW_SLOTS,)),
                        pltpu.VMEM((2, D_MODEL, 2 * D_FF), F32), pltpu.VMEM((2, D_FF, D_MODEL), F32),
                        pltpu.SemaphoreType.DMA((2, 2)),
                        pltpu.SMEM((1,), I32), pltpu.SMEM((1,), I32)],
    )
    return pl.pallas_call(
        functools.partial(_expert_kernel, n_tiles=n_tiles, n_prompt_tiles=n_prompt_tiles),
        grid_spec=grid_spec,
        out_shape=jax.ShapeDtypeStruct((rows, PACK_S, LANES), U32),
        compiler_params=pltpu.CompilerParams(
            dimension_semantics=("arbitrary",), vmem_limit_bytes=VMEM_LIMIT),
        name="experts",
    )(plan['block_e'], plan['block_q'], plan['n_used'], plan['src_t'], plan['off_t'], plan['n_t'],
      plan['tot'], xsp, xss, w_gu, b_gu, w_down, b_down)


def _combine_kernel(ysrc_ref, sloc_ref, n_ref, col_ref, h_ref, ys_hbm, o_ref, buf, sems,
                    *, tm, tile0, tps):
    j = pl.program_id(0)
    nt = pl.num_programs(0)
    rt = TOP_K * tm

    def issue(step, slot, enabled):
        for u in range(tps):
            base = (tile0 + jnp.minimum(step, nt - 1) * tps + u) * N_EXPERTS
            for e in range(N_EXPERTS):
                n = n_ref[base + e]

                @pl.when(enabled & (n > 0))
                def _():
                    pltpu.make_async_copy(ys_hbm.at[pl.ds(ysrc_ref[base + e], n)],
                                          buf.at[slot, pl.ds(u * rt + sloc_ref[base + e], n)],
                                          sems.at[slot]).start()

    @pl.when(j == 0)
    def _():
        buf[ROW_SLOTS, 0:SUBLANES // PACK_S] = jnp.zeros((SUBLANES // PACK_S, PACK_S, LANES), U32)
        for ahead in range(LOOKAHEAD):
            issue(ahead, ahead, ahead < nt)

    slot = j % ROW_SLOTS
    pltpu.make_async_copy(ys_hbm.at[pl.ds(0, tps * rt)], buf.at[slot], sems.at[slot]).wait()
    ys = [_unpack_rows(buf.at[slot, pl.ds(u * rt, rt)]) for u in range(tps)]
    issue(j + LOOKAHEAD, (j + LOOKAHEAD) % ROW_SLOTS, j + LOOKAHEAD < nt)
    rio = lax.broadcasted_iota(I32, (tm, rt), 1)
    zero = _anchor_zero_row(buf)
    for u in range(tps):
        tok = slice(u * tm, (u + 1) * tm)
        col = col_ref[tok, :]
        gmat = jnp.zeros((tm, rt), F32)
        for k in range(TOP_K):
            gmat = gmat + jnp.where(rio == col[:, k:k + 1].astype(I32),
                                    col[:, TOP_K + k:TOP_K + k + 1], 0.0)
        o_ref[tok, :] = (h_ref[tok, :] + zero) + _dot(gmat.astype(BF16), ys[u])


def _combine(plan, col, h2d, ys, *, tm, tile0, tps):
    t = h2d.shape[0]
    rt = TOP_K * tm
    tok = tm * tps
    grid_spec = pltpu.PrefetchScalarGridSpec(
        num_scalar_prefetch=3,
        grid=(t // tok,),
        in_specs=[pl.BlockSpec((tok, LANES), lambda i, *_: (i, 0)),
                  pl.BlockSpec((tok, D_MODEL), lambda i, *_: (i, 0)),
                  pl.BlockSpec(memory_space=pl.ANY)],
        out_specs=pl.BlockSpec((tok, D_MODEL), lambda i, *_: (i, 0)),
        scratch_shapes=[pltpu.VMEM((ROW_SLOTS + 1, tps * rt, PACK_S, LANES), U32),
                        pltpu.SemaphoreType.DMA((ROW_SLOTS,))],
    )
    return pl.pallas_call(
        functools.partial(_combine_kernel, tm=tm, tile0=tile0, tps=tps),
        grid_spec=grid_spec,
        out_shape=jax.ShapeDtypeStruct((t, D_MODEL), F32),
        compiler_params=pltpu.CompilerParams(
            dimension_semantics=("arbitrary",), vmem_limit_bytes=COMBINE_VMEM_LIMIT),
        name="combine",
    )(plan['ysrc'], plan['sloc'], plan['n'], col, h2d, ys)


def kernel(x_prompt, x_sample, cache_k, cache_v, state_conv, norm1_g, w_in, q_norm_g, k_norm_g,
           sinks, conv_w, w_out, norm2_g, w_router, b_router, w_gu, b_gu, w_down, b_down):
    l = 0
    nbp, sp, _ = x_prompt.shape
    nbs, ss, _ = x_sample.shape
    tp, ts = nbp * sp, nbs * ss
    tmp, tms = TOKEN_TILE, ts
    ntp, nts = tp // tmp, ts // tms
    n_tiles = ntp + nts
    n_rows = (tp + ts) * TOP_K
    grp = jnp.arange(ATT_DIM) // HEAD_DIM
    w = {
        'g1': norm1_g[l][None, :],
        'w_in': w_in[l].astype(BF16),
        'qg': jnp.tile(q_norm_g[l], N_HEADS)[None, :],
        'kg': jnp.tile(k_norm_g[l], N_KV_HEADS)[None, :],
        'bd': (grp[:, None] == grp[None, :]).astype(BF16),
        'sinks': sinks[l][None, :],
        'conv_w': conv_w[l],
        'w_out': w_out[l].astype(BF16),
        'g2': norm2_g[l][None, :],
        'w_router_t': w_router[l].T,
        'b_router_t': b_router[l][:, None],
    }
    hp, xsp, colp, tcp, knp, vnp, unp = _mixer(x_prompt, None, w, tm=tmp, nsub=STREAMS_PER_STEP,
                                               st=tmp)
    cache = (cache_k[l].reshape(nbs, WINDOW, KV_DIM), cache_v[l].reshape(nbs, WINDOW, KV_DIM),
             state_conv[l])
    hs, xss, cols, tcs, kns, vns, uns = _mixer(x_sample, cache, w, tm=ss, nsub=nbs, st=tms)

    n = jnp.concatenate([tcp.reshape(ntp, N_EXPERTS), tcs.reshape(nts, N_EXPERTS)], axis=0)
    tile_base = jnp.concatenate([jnp.arange(ntp, dtype=I32) * (TOP_K * tmp),
                                 jnp.arange(nts, dtype=I32) * (TOP_K * tms)])
    sloc = jnp.cumsum(n, axis=1) - n
    off = jnp.cumsum(n, axis=0) - n
    tot = jnp.sum(n, axis=0)
    padded = (tot + ROW_BLOCK - 1) // ROW_BLOCK * ROW_BLOCK
    ends = jnp.cumsum(padded)
    base = ends - padded
    rows = (-(-n_rows // ROW_BLOCK) + N_EXPERTS) * ROW_BLOCK
    nblk = rows // ROW_BLOCK
    blk0 = jnp.arange(nblk, dtype=I32) * ROW_BLOCK
    block_e = jnp.minimum(jnp.sum((ends[None, :] <= blk0[:, None]).astype(I32), axis=1),
                          N_EXPERTS - 1)
    eq = (block_e[:, None] == jnp.arange(N_EXPERTS, dtype=I32)[None, :]).astype(I32)
    block_q = (blk0 - jnp.sum(eq * base[None, :], axis=1)) // ROW_BLOCK
    plan = {
        'block_e': block_e.astype(I32),
        'block_q': block_q.astype(I32),
        'n_used': (ends[-1:] // ROW_BLOCK).astype(I32),
        'src_t': (tile_base[:, None] + sloc).T.reshape(-1).astype(I32),
        'off_t': off.T.reshape(-1).astype(I32),
        'n_t': n.T.reshape(-1).astype(I32),
        'tot': tot.astype(I32),
        'ysrc': (base[None, :] + off).reshape(-1).astype(I32),
        'sloc': sloc.reshape(-1).astype(I32),
        'n': n.reshape(-1).astype(I32),
    }

    ys = _experts(plan, xsp.reshape(-1, PACK_S, LANES), xss.reshape(-1, PACK_S, LANES),
                  w_gu[l], b_gu[l], w_down[l], b_down[l],
                  n_tiles=n_tiles, n_prompt_tiles=ntp, rows=rows)
    yp = _combine(plan, colp.reshape(tp, LANES), hp.reshape(tp, D_MODEL), ys, tm=tmp, tile0=0,
                  tps=COMBINE_TILES)
    ysm = _combine(plan, cols.reshape(ts, LANES), hs.reshape(ts, D_MODEL), ys, tm=tms, tile0=ntp,
                   tps=1)

    kv5 = lambda a: a.reshape(1, a.shape[0], WINDOW, N_KV_HEADS, HEAD_DIM)
    return (yp.reshape(nbp, sp, D_MODEL), ysm.reshape(nbs, ss, D_MODEL),
            kv5(knp), kv5(vnp), unp[None], kv5(kns), kv5(vns), uns[None])
```

```python
import functools

import jax
import jax.numpy as jnp
from jax import lax
from jax.experimental import pallas as pl
from jax.experimental.pallas import tpu as pltpu

D_MODEL = 1024
CHUNK = 64
HEAD_DIM = 64
N_HEADS = 8
N_KV_HEADS = 2
Q_PER_KV = N_HEADS // N_KV_HEADS
ATT_DIM = N_HEADS * HEAD_DIM
KV_DIM = N_KV_HEADS * HEAD_DIM
GROUP_W = Q_PER_KV * HEAD_DIM
CONV_DIM = D_MODEL - ATT_DIM
CONV_W = 3
WINDOW = 128
IN_DIM = ATT_DIM + 2 * KV_DIM + 3 * CONV_DIM
N_EXPERTS = 32
TOP_K = 4
D_FF = D_MODEL
SWIGLU_LIMIT = 7.0
SWIGLU_ALPHA = 1.702
EPS = 1e-6
NEG = -1e30

LANES = 128
SUBLANES = 8
HALF = D_MODEL // 2
PACK_S = HALF // LANES
TOKEN_TILE = 256
STREAMS_PER_STEP = 4
COMBINE_TILES = 2
ROW_BLOCK = 1024
TAIL_ROWS = 256
INLINE_SEGMENTS = 40
LOOKAHEAD = 2
ROW_SLOTS = LOOKAHEAD + 1
VMEM_LIMIT = 56 * 1024 * 1024
MIXER_VMEM_LIMIT = 60 * 1024 * 1024

F32 = jnp.float32
BF16 = jnp.bfloat16
U32 = jnp.uint32
I32 = jnp.int32


def _rms_rows(x, g):
    return x * lax.rsqrt(jnp.mean(x * x, axis=-1, keepdims=True) + EPS) * g


def _split_bf16(x):
    hi = x.astype(BF16)
    lo = (x - hi.astype(F32)).astype(BF16)
    return hi, lo


def _dot(a, b):
    return jnp.dot(a, b, preferred_element_type=F32)


def _dot_nt(a, b):
    return lax.dot_general(a, b, (((1,), (1,)), ((), ())), preferred_element_type=F32)


def _head_rms(t, bd, g):
    ssq = _dot((t * t).astype(BF16), bd)
    return t * lax.rsqrt(ssq * (1.0 / HEAD_DIM) + EPS) * g


def _pack_rows(vals, out_ref):
    r = vals.shape[0]
    bits = pltpu.bitcast(vals, U32)
    word = (bits[:, :HALF] >> 16) | (bits[:, HALF:] & jnp.uint32(0xFFFF0000))
    flat = out_ref.reshape(r * PACK_S, LANES)
    for s in range(PACK_S):
        flat[pl.ds(s, r, stride=PACK_S), :] = word[:, s * LANES:(s + 1) * LANES]


def _unpack_rows(packed_ref):
    r = packed_ref.shape[0]
    flat = packed_ref.reshape(r * PACK_S, LANES)
    lo, hi = [], []
    for s in range(PACK_S):
        w = flat[pl.ds(s, r, stride=PACK_S), :]
        lo.append(pltpu.bitcast(w << 16, F32))
        hi.append(pltpu.bitcast(w & jnp.uint32(0xFFFF0000), F32))
    return jnp.concatenate(lo + hi, axis=1).astype(BF16)


def _anchor_zero_row(staging_ref):
    z = staging_ref.at[ROW_SLOTS, pl.ds(0, SUBLANES // PACK_S)].reshape(SUBLANES, LANES)[...]
    return jnp.concatenate([pltpu.bitcast(z, F32)[0:1, :]] * (D_MODEL // LANES), axis=1)


def _spread_kv(t):
    assert N_KV_HEADS == 2 and KV_DIM == LANES
    swapped = pltpu.roll(t, HEAD_DIM, axis=1)
    low = lax.broadcasted_iota(I32, t.shape, 1) < HEAD_DIM
    pair0 = jnp.where(low, t, swapped)
    pair1 = jnp.where(low, swapped, t)
    reps = GROUP_W // LANES
    return jnp.concatenate([pair0] * reps + [pair1] * reps, axis=1).astype(BF16)


def _attention(i, qn, kall, vall, sink_ref, *, mask_history):
    tm = qn.shape[0]
    k4 = _spread_kv(kall)
    v4 = _spread_kv(vall)
    lane_grp = lax.broadcasted_iota(I32, (CHUNK, GROUP_W), 1) // HEAD_DIM
    row_grp = lax.broadcasted_iota(I32, (Q_PER_KV * CHUNK, 1), 0) // CHUNK
    span = WINDOW + CHUNK
    att_rows = []
    for c in range(tm // CHUNK):
        per_kv = []
        for hk in range(N_KV_HEADS):
            qc = qn[c * CHUNK:(c + 1) * CHUNK, hk * GROUP_W:(hk + 1) * GROUP_W]
            qs = jnp.concatenate([jnp.where(lane_grp == g, qc, 0.0) for g in range(Q_PER_KV)],
                                 axis=0).astype(BF16)
            kw = k4[c * CHUNK:c * CHUNK + span, hk * GROUP_W:(hk + 1) * GROUP_W]
            vw = v4[c * CHUNK:c * CHUNK + span, hk * GROUP_W:(hk + 1) * GROUP_W]
            s = _dot_nt(qs, kw) * (HEAD_DIM ** -0.5)
            if mask_history and c * CHUNK < WINDOW:
                kcol = lax.broadcasted_iota(I32, (1, span), 1)
                s = jnp.where(kcol >= jnp.where(i == 0, WINDOW - c * CHUNK, 0), s, NEG)
            sink = jnp.zeros((Q_PER_KV * CHUNK, 1), F32)
            for g in range(Q_PER_KV):
                sink = jnp.where(row_grp == g, sink_ref[0, hk * Q_PER_KV + g], sink)
            m = jnp.maximum(jnp.max(s, axis=-1, keepdims=True), sink)
            p = jnp.exp(s - m)
            den = jnp.sum(p, axis=-1, keepdims=True) + jnp.exp(sink - m)
            o = _dot(p.astype(BF16), vw) / den
            oc = jnp.zeros((CHUNK, GROUP_W), F32)
            for g in range(Q_PER_KV):
                oc = oc + jnp.where(lane_grp == g, o[g * CHUNK:(g + 1) * CHUNK, :], 0.0)
            per_kv.append(oc)
        att_rows.append(jnp.concatenate(per_kv, axis=1))
    return jnp.concatenate(att_rows, axis=0)


def _mixer_kernel(*refs, tm, nsub, st, has_cache):
    refs = list(refs)
    x_ref = refs.pop(0)
    if has_cache:
        ck_ref, cv_ref, st_ref = refs.pop(0), refs.pop(0), refs.pop(0)
    (g1_ref, win_ref, qg_ref, kg_ref, bd_ref, sink_ref, cw_ref, wout_ref, g2_ref,
     wrt_ref, brt_ref, upper_ref, lower_ref,
     h_ref, xs_ref, col_ref, tcnt_ref, knew_ref, vnew_ref, unew_ref, kprev, vprev, ubuf) = refs
    i = pl.program_id(1)
    ntok = nsub * tm

    @pl.when(i == 0)
    def _():
        ubuf[:, 0:SUBLANES, :] = jnp.zeros((nsub, SUBLANES, CONV_DIM), F32)
        if has_cache:
            kprev[...] = ck_ref[...]
            vprev[...] = cv_ref[...]
            ubuf[:, SUBLANES - (CONV_W - 1):SUBLANES, :] = st_ref[...]
        else:
            kprev[...] = jnp.zeros(kprev.shape, F32)
            vprev[...] = jnp.zeros(vprev.shape, F32)

    x = jnp.concatenate([x_ref[sub] for sub in range(nsub)], axis=0) if nsub > 1 else x_ref[0]
    xn = _rms_rows(x, g1_ref[...])
    proj = _dot(xn.astype(BF16), win_ref[...])
    o0 = ATT_DIM
    o1 = o0 + KV_DIM
    o2 = o1 + KV_DIM
    o3 = o2 + CONV_DIM
    o4 = o3 + CONV_DIM
    bd = bd_ref[...]
    qn = _head_rms(proj[:, :o0], bd, qg_ref[...])
    kn = _head_rms(proj[:, o0:o1], bd[:KV_DIM, :KV_DIM], kg_ref[...])
    v = proj[:, o1:o2]
    bg = proj[:, o2:o3]
    u = proj[:, o3:o4] * proj[:, o4:]

    cw = cw_ref[...]
    att_parts, cv_parts = [], []
    for sub in range(nsub):
        r0, r1 = sub * tm, (sub + 1) * tm
        kall = jnp.concatenate([kprev[sub], kn[r0:r1]], axis=0)
        vall = jnp.concatenate([vprev[sub], v[r0:r1]], axis=0)
        att_parts.append(_attention(i, qn[r0:r1], kall, vall, sink_ref,
                                    mask_history=not has_cache))
        kprev[sub] = kall[tm:, :]
        vprev[sub] = vall[tm:, :]
        knew_ref[sub] = kall[tm:, :]
        vnew_ref[sub] = vall[tm:, :]

        us = u[r0:r1]
        ubuf[sub, SUBLANES:SUBLANES + tm, :] = us
        first = SUBLANES - (CONV_W - 1)
        cvs = ubuf[sub, first:first + tm, :] * cw[0:1, :]
        for tap in range(1, CONV_W - 1):
            cvs = cvs + ubuf[sub, first + tap:first + tap + tm, :] * cw[tap:tap + 1, :]
        cv_parts.append(cvs + us * cw[CONV_W - 1:CONV_W, :])
        ubuf[sub, 0:SUBLANES, :] = ubuf[sub, tm:tm + SUBLANES, :]
        unew_ref[sub] = us[tm - (CONV_W - 1):, :]
    att = jnp.concatenate(att_parts, axis=0) if nsub > 1 else att_parts[0]
    cv = jnp.concatenate(cv_parts, axis=0) if nsub > 1 else cv_parts[0]

    mix = jnp.concatenate([att, bg * cv], axis=1).astype(BF16)
    h = x + _dot(mix, wout_ref[...])
    for sub in range(nsub):
        h_ref[sub] = h[sub * tm:(sub + 1) * tm]
    hn = _rms_rows(h, g2_ref[...])

    hn_hi, hn_lo = _split_bf16(hn)
    wr_hi, wr_lo = _split_bf16(wrt_ref[...])
    logits = (_dot_nt(wr_hi, hn_hi) + _dot_nt(wr_hi, hn_lo) + _dot_nt(wr_lo, hn_hi)) + brt_ref[...]
    eio = lax.broadcasted_iota(I32, (N_EXPERTS, ntok), 0)
    work = logits
    tops, sels = [], []
    for _ in range(TOP_K):
        mk = jnp.max(work, axis=0, keepdims=True)
        ik = jnp.min(jnp.where(work == mk, eio, N_EXPERTS), axis=0, keepdims=True)
        sel = eio == ik
        work = jnp.where(sel, -jnp.inf, work)
        tops.append(mk)
        sels.append(sel)
    es = [jnp.exp(t - tops[0]) for t in tops]
    esum = es[0] + es[1] + es[2] + es[3]

    onehot = jnp.zeros((N_EXPERTS, ntok), F32)
    for sel in sels:
        onehot = onehot + jnp.where(sel, 1.0, 0.0)
    onehot_b = onehot.astype(BF16)
    smaller = _dot(lower_ref[...], onehot_b)
    nst = ntok // st
    where_parts = []
    for k in range(nst):
        c0, c1 = k * st, (k + 1) * st
        before = _dot(onehot_b[:, c0:c1], upper_ref[...])
        start = jnp.sum(smaller[:, c0:c1], axis=1, keepdims=True)
        where_parts.append(before + start)
    where_to = jnp.concatenate(where_parts, axis=1) if nst > 1 else where_parts[0]
    pos = [jnp.sum(jnp.where(sel, where_to, 0.0), axis=0, keepdims=True) for sel in sels]
    rio = lax.broadcasted_iota(I32, (TOP_K * st, st), 0)
    for k in range(nst):
        c0, c1 = k * st, (k + 1) * st
        hit = rio == pos[0][:, c0:c1].astype(I32)
        for slot in range(1, TOP_K):
            hit = hit | (rio == pos[slot][:, c0:c1].astype(I32))
        perm = jnp.where(hit, 1.0, 0.0).astype(BF16)
        _pack_rows(_dot(perm, hn_hi[c0:c1]), xs_ref.at[k, 0])
        tcnt_ref[k, 0] = _dot_nt(jnp.ones((1, st), BF16), onehot_b[:, c0:c1]).astype(I32)

    rows8 = jnp.concatenate(pos + [e / esum for e in es], axis=0)
    sq = jnp.concatenate([rows8, jnp.zeros((LANES - 2 * TOP_K, ntok), F32)], axis=0)
    cols = jnp.transpose(sq)
    for sub in range(nsub):
        col_ref[sub] = cols[sub * tm:(sub + 1) * tm]


def _mixer(x, cache, w, *, tm, nsub, st):
    nb, s, _ = x.shape
    nt = s // tm
    nst = nsub * tm // st
    has_cache = cache is not None
    full = lambda shape: pl.BlockSpec(shape, lambda b, i: (0,) * len(shape))
    per_b = lambda shape: pl.BlockSpec((nsub,) + shape, lambda b, i: (b, 0, 0))
    tile = lambda last: pl.BlockSpec((nsub, tm, last), lambda b, i: (b, i, 0))
    kv_shape = (WINDOW, KV_DIM)
    kv_spec = per_b(kv_shape)

    in_specs = [tile(D_MODEL)]
    args = [x]
    if has_cache:
        in_specs += [kv_spec, kv_spec, per_b((CONV_W - 1, CONV_DIM))]
        args += list(cache)
    tio = jnp.arange(st)
    eio = jnp.arange(N_EXPERTS)
    upper = (tio[:, None] < tio[None, :]).astype(BF16)
    lower = (eio[None, :] < eio[:, None]).astype(BF16)
    in_specs += [
        full((1, D_MODEL)), full((D_MODEL, IN_DIM)), full((1, ATT_DIM)),
        full((1, KV_DIM)), full((ATT_DIM, ATT_DIM)),
        pl.BlockSpec(memory_space=pltpu.SMEM),
        full((CONV_W, CONV_DIM)), full((D_MODEL, D_MODEL)), full((1, D_MODEL)),
        full((N_EXPERTS, D_MODEL)), full((N_EXPERTS, 1)), full((st, st)),
        full((N_EXPERTS, N_EXPERTS)),
    ]
    args += [w['g1'], w['w_in'], w['qg'], w['kg'], w['bd'], w['sinks'], w['conv_w'],
             w['w_out'], w['g2'], w['w_router_t'], w['b_router_t'], upper, lower]
    fusible = [a is w['w_in'] or a is w['w_out'] for a in args]
    rt = TOP_K * st
    nsort = nb // nsub * nst
    out_shape = [
        jax.ShapeDtypeStruct((nb, s, D_MODEL), F32),
        jax.ShapeDtypeStruct((nsort, nt, rt, PACK_S, LANES), U32),
        jax.ShapeDtypeStruct((nb, s, LANES), F32),
        jax.ShapeDtypeStruct((nsort, nt, 1, N_EXPERTS), I32),
        jax.ShapeDtypeStruct((nb,) + kv_shape, F32),
        jax.ShapeDtypeStruct((nb,) + kv_shape, F32),
        jax.ShapeDtypeStruct((nb, CONV_W - 1, CONV_DIM), F32),
    ]
    out_specs = [
        tile(D_MODEL),
        pl.BlockSpec((nst, 1, rt, PACK_S, LANES), lambda b, i: (b, i, 0, 0, 0)),
        tile(LANES),
        pl.BlockSpec((nst, 1, 1, N_EXPERTS), lambda b, i: (b, i, 0, 0)),
        kv_spec, kv_spec, per_b((CONV_W - 1, CONV_DIM)),
    ]
    scratch = [pltpu.VMEM((nsub, WINDOW, KV_DIM), F32), pltpu.VMEM((nsub, WINDOW, KV_DIM), F32),
               pltpu.VMEM((nsub, tm + SUBLANES, CONV_DIM), F32)]
    return pl.pallas_call(
        functools.partial(_mixer_kernel, tm=tm, nsub=nsub, st=st, has_cache=has_cache),
        grid=(nb // nsub, nt),
        in_specs=in_specs,
        out_specs=out_specs,
        out_shape=out_shape,
        scratch_shapes=scratch,
        compiler_params=pltpu.CompilerParams(
            dimension_semantics=("arbitrary", "arbitrary"), vmem_limit_bytes=MIXER_VMEM_LIMIT,
            allow_input_fusion=fusible),
        name="mixer_sample" if has_cache else "mixer_prompt",
    )(*args)


def _expert_kernel(be_ref, bq_ref, nused_ref, src_ref, off_ref, n_ref, tot_ref,
                   xsp_hbm, xss_hbm, wgu_hbm, bgu_ref, wd_hbm, bdn_ref, ys_ref,
                   xbuf, sems, wgu_f, wd_f, wsems, ptr, wcur, *, n_tiles, n_prompt_tiles):
    b = pl.program_id(0)
    nused = nused_ref[0]

    def weight_copies(e, ws):
        return (pltpu.make_async_copy(wgu_hbm.at[e], wgu_f.at[ws], wsems.at[0, ws]),
                pltpu.make_async_copy(wd_hbm.at[e], wd_f.at[ws], wsems.at[1, ws]))

    def issue_rows(blk, slot, enabled, between=lambda: None):
        blk = jnp.minimum(blk, pl.num_programs(0) - 1)
        e = be_ref[blk]
        q = bq_ref[blk]
        lo_row = q * ROW_BLOCK
        hi_row = lo_row + ROW_BLOCK
        base = e * n_tiles

        def segment(j, j_limit, src_hbm, ok):
            jc = jnp.minimum(j, j_limit - 1)
            o = off_ref[base + jc]
            inside = ok & (j < j_limit) & (o < hi_row)
            lo = jnp.maximum(o, lo_row)
            ln = jnp.minimum(o + n_ref[base + jc], hi_row) - lo

            @pl.when(inside & (ln > 0))
            def _():
                pltpu.make_async_copy(src_hbm.at[pl.ds(src_ref[base + jc] + (lo - o), ln)],
                                      xbuf.at[slot, pl.ds(lo - lo_row, ln)], sems.at[slot]).start()
            return inside

        j0 = jnp.where(q == 0, 0, ptr[0])

        def cond(j):
            return (enabled & (j < n_prompt_tiles)
                    & (off_ref[base + jnp.minimum(j, n_prompt_tiles - 1)] < hi_row))

        def body(j):
            segment(j, n_prompt_tiles, xsp_hbm, True)
            return j + 1

        jtail = lax.while_loop(cond, body, j0 + INLINE_SEGMENTS)
        result = between()
        count = jnp.int32(0)
        for k in range(INLINE_SEGMENTS):
            count = count + segment(j0 + k, n_prompt_tiles, xsp_hbm, enabled).astype(I32)
        jend = jnp.where(count == INLINE_SEGMENTS, jtail, j0 + count)
        ptr[0] = jnp.where(enabled, jnp.maximum(jend - 1, 0), ptr[0])
        for j in range(n_prompt_tiles, n_tiles):
            segment(j, n_tiles, xss_hbm, enabled)
        return result

    @pl.when(b == 0)
    def _():
        xbuf[...] = jnp.zeros(xbuf.shape, U32)
        for ahead in range(LOOKAHEAD):
            issue_rows(ahead, ahead, ahead < nused)
        for c in weight_copies(be_ref[0], 0):
            c.start()
        wcur[0] = 1

    active = b < nused

    @pl.when(active)
    def _():
        slot = b % ROW_SLOTS
        e = be_ref[b]
        rows = jnp.minimum(tot_ref[e] - bq_ref[b] * ROW_BLOCK, ROW_BLOCK)
        pltpu.make_async_copy(xsp_hbm.at[pl.ds(0, rows)], xbuf.at[slot, pl.ds(0, rows)],
                              sems.at[slot]).wait()

        @pl.when(bq_ref[b] == 0)
        def _():
            ws = 1 - wcur[0]
            wcur[0] = ws
            for c in weight_copies(e, ws):
                c.wait()
            nxt = b + (tot_ref[e] + ROW_BLOCK - 1) // ROW_BLOCK

            @pl.when(nxt < nused)
            def _():
                for c in weight_copies(be_ref[jnp.minimum(nxt, pl.num_programs(0) - 1)], 1 - ws):
                    c.start()

        ws = wcur[0]

        def compute(nrows):
            x = issue_rows(b + LOOKAHEAD, (b + LOOKAHEAD) % ROW_SLOTS, b + LOOKAHEAD < nused,
                           between=lambda: _unpack_rows(xbuf.at[slot, pl.ds(0, nrows)]))
            gu = _dot(x, wgu_f[ws]) + bgu_ref[pl.ds(e, 1), :]
            g = jnp.minimum(gu[:, :D_FF], SWIGLU_LIMIT)
            u = jnp.clip(gu[:, D_FF:], -SWIGLU_LIMIT, SWIGLU_LIMIT)
            act = (u + 1.0) * (g * jax.nn.sigmoid(SWIGLU_ALPHA * g))
            y = (_dot(act.astype(BF16), wd_f[ws])
                 + (bdn_ref[pl.ds(e, 1), :] + _anchor_zero_row(xbuf)))
            _pack_rows(y.astype(BF16).astype(F32), ys_ref.at[pl.ds(0, nrows)])
            if nrows < ROW_BLOCK:
                ys_ref[nrows:] = jnp.zeros((ROW_BLOCK - nrows, PACK_S, LANES), U32)

        for piece in range(ROW_BLOCK // TAIL_ROWS):
            top = (piece + 1) * TAIL_ROWS
            pl.when((rows > top - TAIL_ROWS) & (rows <= top))(functools.partial(compute, top))

    @pl.when(jnp.logical_not(active))
    def _():
        ys_ref[...] = jnp.zeros(ys_ref.shape, U32)


def _experts(plan, xsp, xss, w_gu, b_gu, w_down, b_down, *, n_tiles, n_prompt_tiles, rows):
    nblk = rows // ROW_BLOCK

    grid_spec = pltpu.PrefetchScalarGridSpec(
        num_scalar_prefetch=7,
        grid=(nblk,),
        in_specs=[pl.BlockSpec(memory_space=pl.ANY),
                  pl.BlockSpec(memory_space=pl.ANY),
                  pl.BlockSpec(memory_space=pl.ANY),
                  pl.BlockSpec((N_EXPERTS, 2 * D_FF), lambda b, *_: (0, 0)),
                  pl.BlockSpec(memory_space=pl.ANY),
                  pl.BlockSpec((N_EXPERTS, D_MODEL), lambda b, *_: (0, 0))],
        out_specs=pl.BlockSpec((ROW_BLOCK, PACK_S, LANES), lambda b, *_: (b, 0, 0)),
        scratch_shapes=[pltpu.VMEM((ROW_SLOTS + 1, ROW_BLOCK, PACK_S, LANES), U32),
                        pltpu.SemaphoreType.DMA((ROW_SLOTS,)),
                        pltpu.VMEM((2, D_MODEL, 2 * D_FF), F32), pltpu.VMEM((2, D_FF, D_MODEL), F32),
                        pltpu.SemaphoreType.DMA((2, 2)),
                        pltpu.SMEM((1,), I32), pltpu.SMEM((1,), I32)],
    )
    return pl.pallas_call(
        functools.partial(_expert_kernel, n_tiles=n_tiles, n_prompt_tiles=n_prompt_tiles),
        grid_spec=grid_spec,
        out_shape=jax.ShapeDtypeStruct((rows, PACK_S, LANES), U32),
        compiler_params=pltpu.CompilerParams(
            dimension_semantics=("arbitrary",), vmem_limit_bytes=VMEM_LIMIT),
        name="experts",
    )(plan['block_e'], plan['block_q'], plan['n_used'], plan['src_t'], plan['off_t'], plan['n_t'],
      plan['tot'], xsp, xss, w_gu, b_gu, w_down, b_down)


def _combine_kernel(ysrc_ref, sloc_ref, n_ref, col_ref, h_ref, ys_hbm, o_ref, buf, sems,
                    *, tm, tile0, tps):
    j = pl.program_id(0)
    nt = pl.num_programs(0)
    rt = TOP_K * tm

    def issue(step, slot, enabled):
        for u in range(tps):
            base = (tile0 + jnp.minimum(step, nt - 1) * tps + u) * N_EXPERTS
            for e in range(N_EXPERTS):
                n = n_ref[base + e]

                @pl.when(enabled & (n > 0))
                def _():
                    pltpu.make_async_copy(ys_hbm.at[pl.ds(ysrc_ref[base + e], n)],
                                          buf.at[slot, pl.ds(u * rt + sloc_ref[base + e], n)],
                                          sems.at[slot]).start()

    @pl.when(j == 0)
    def _():
        buf[ROW_SLOTS, 0:SUBLANES // PACK_S] = jnp.zeros((SUBLANES // PACK_S, PACK_S, LANES), U32)
        for ahead in range(LOOKAHEAD):
            issue(ahead, ahead, ahead < nt)

    slot = j % ROW_SLOTS
    pltpu.make_async_copy(ys_hbm.at[pl.ds(0, tps * rt)], buf.at[slot], sems.at[slot]).wait()
    ys = [_unpack_rows(buf.at[slot, pl.ds(u * rt, rt)]) for u in range(tps)]
    issue(j + LOOKAHEAD, (j + LOOKAHEAD) % ROW_SLOTS, j + LOOKAHEAD < nt)
    rio = lax.broadcasted_iota(I32, (tm, rt), 1)
    zero = _anchor_zero_row(buf)
    for u in range(tps):
        tok = slice(u * tm, (u + 1) * tm)
        col = col_ref[tok, :]
        gmat = jnp.zeros((tm, rt), F32)
        for k in range(TOP_K):
            gmat = gmat + jnp.where(rio == col[:, k:k + 1].astype(I32),
                                    col[:, TOP_K + k:TOP_K + k + 1], 0.0)
        o_ref[tok, :] = (h_ref[tok, :] + zero) + _dot(gmat.astype(BF16), ys[u])


def _combine(plan, col, h2d, ys, *, tm, tile0, tps):
    t = h2d.shape[0]
    rt = TOP_K * tm
    tok = tm * tps
    grid_spec = pltpu.PrefetchScalarGridSpec(
        num_scalar_prefetch=3,
        grid=(t // tok,),
        in_specs=[pl.BlockSpec((tok, LANES), lambda i, *_: (i, 0)),
                  pl.BlockSpec((tok, D_MODEL), lambda i, *_: (i, 0)),
                  pl.BlockSpec(memory_space=pl.ANY)],
        out_specs=pl.BlockSpec((tok, D_MODEL), lambda i, *_: (i, 0)),
        scratch_shapes=[pltpu.VMEM((ROW_SLOTS + 1, tps * rt, PACK_S, LANES), U32),
                        pltpu.SemaphoreType.DMA((ROW_SLOTS,))],
    )
    return pl.pallas_call(
        functools.partial(_combine_kernel, tm=tm, tile0=tile0, tps=tps),
        grid_spec=grid_spec,
        out_shape=jax.ShapeDtypeStruct((t, D_MODEL), F32),
        compiler_params=pltpu.CompilerParams(
            dimension_semantics=("arbitrary",), vmem_limit_bytes=VMEM_LIMIT),
        name="combine",
    )(plan['ysrc'], plan['sloc'], plan['n'], col, h2d, ys)


def kernel(x_prompt, x_sample, cache_k, cache_v, state_conv, norm1_g, w_in, q_norm_g, k_norm_g,
           sinks, conv_w, w_out, norm2_g, w_router, b_router, w_gu, b_gu, w_down, b_down):
    l = 0
    nbp, sp, _ = x_prompt.shape
    nbs, ss, _ = x_sample.shape
    tp, ts = nbp * sp, nbs * ss
    tmp, tms = TOKEN_TILE, ts
    ntp, nts = tp // tmp, ts // tms
    n_tiles = ntp + nts
    n_rows = (tp + ts) * TOP_K
    grp = jnp.arange(ATT_DIM) // HEAD_DIM
    w = {
        'g1': norm1_g[l][None, :],
        'w_in': w_in[l].astype(BF16),
        'qg': jnp.tile(q_norm_g[l], N_HEADS)[None, :],
        'kg': jnp.tile(k_norm_g[l], N_KV_HEADS)[None, :],
        'bd': (grp[:, None] == grp[None, :]).astype(BF16),
        'sinks': sinks[l][None, :],
        'conv_w': conv_w[l],
        'w_out': w_out[l].astype(BF16),
        'g2': norm2_g[l][None, :],
        'w_router_t': w_router[l].T,
        'b_router_t': b_router[l][:, None],
    }
    hp, xsp, colp, tcp, knp, vnp, unp = _mixer(x_prompt, None, w, tm=tmp, nsub=STREAMS_PER_STEP,
                                               st=tmp)
    cache = (cache_k[l].reshape(nbs, WINDOW, KV_DIM), cache_v[l].reshape(nbs, WINDOW, KV_DIM),
             state_conv[l])
    hs, xss, cols, tcs, kns, vns, uns = _mixer(x_sample, cache, w, tm=ss, nsub=nbs, st=tms)

    n = jnp.concatenate([tcp.reshape(ntp, N_EXPERTS), tcs.reshape(nts, N_EXPERTS)], axis=0)
    tile_base = jnp.concatenate([jnp.arange(ntp, dtype=I32) * (TOP_K * tmp),
                                 jnp.arange(nts, dtype=I32) * (TOP_K * tms)])
    sloc = jnp.cumsum(n, axis=1) - n
    off = jnp.cumsum(n, axis=0) - n
    tot = jnp.sum(n, axis=0)
    padded = (tot + ROW_BLOCK - 1) // ROW_BLOCK * ROW_BLOCK
    ends = jnp.cumsum(padded)
    base = ends - padded
    rows = (-(-n_rows // ROW_BLOCK) + N_EXPERTS) * ROW_BLOCK
    nblk = rows // ROW_BLOCK
    blk0 = jnp.arange(nblk, dtype=I32) * ROW_BLOCK
    block_e = jnp.minimum(jnp.sum((ends[None, :] <= blk0[:, None]).astype(I32), axis=1),
                          N_EXPERTS - 1)
    eq = (block_e[:, None] == jnp.arange(N_EXPERTS, dtype=I32)[None, :]).astype(I32)
    block_q = (blk0 - jnp.sum(eq * base[None, :], axis=1)) // ROW_BLOCK
    plan = {
        'block_e': block_e.astype(I32),
        'block_q': block_q.astype(I32),
        'n_used': (ends[-1:] // ROW_BLOCK).astype(I32),
        'src_t': (tile_base[:, None] + sloc).T.reshape(-1).astype(I32),
        'off_t': off.T.reshape(-1).astype(I32),
        'n_t': n.T.reshape(-1).astype(I32),
        'tot': tot.astype(I32),
        'ysrc': (base[None, :] + off).reshape(-1).astype(I32),
        'sloc': sloc.reshape(-1).astype(I32),
        'n': n.reshape(-1).astype(I32),
    }

    ys = _experts(plan, xsp.reshape(-1, PACK_S, LANES), xss.reshape(-1, PACK_S, LANES),
                  w_gu[l], b_gu[l], w_down[l], b_down[l],
                  n_tiles=n_tiles, n_prompt_tiles=ntp, rows=rows)
    yp = _combine(plan, colp.reshape(tp, LANES), hp.reshape(tp, D_MODEL), ys, tm=tmp, tile0=0,
                  tps=COMBINE_TILES)
    ysm = _combine(plan, cols.reshape(ts, LANES), hs.reshape(ts, D_MODEL), ys, tm=tms, tile0=ntp,
                   tps=1)

    kv5 = lambda a: a.reshape(1, a.shape[0], WINDOW, N_KV_HEADS, HEAD_DIM)
    return (yp.reshape(nbp, sp, D_MODEL), ysm.reshape(nbs, ss, D_MODEL),
            kv5(knp), kv5(vnp), unp[None], kv5(kns), kv5(vns), uns[None])
```
